```python
import jax, jax.numpy as jnp
from jax import lax
import numpy as np

D_MODEL = 2048
BATCH = 2
SEQ = 8192
DEPTH = 2

GRID_W = 64
CTX_LEN = 256
N_MOD = 9
FFN_DIM = 5632
HEAD_DIM = 128
N_FOURIER_GROUPS = 4
FOURIER_GROUP_DIM = 128
FOURIER_WIDTH = N_FOURIER_GROUPS * FOURIER_GROUP_DIM
N_Q_HEADS = (D_MODEL - FOURIER_WIDTH) // HEAD_DIM
N_KV_HEADS = 4
Q_GROUP = N_Q_HEADS // N_KV_HEADS
ATTN_Q_WIDTH = N_Q_HEADS * HEAD_DIM
ATTN_KV_WIDTH = N_KV_HEADS * HEAD_DIM
AB_IN_WIDTH = FOURIER_WIDTH + ATTN_Q_WIDTH + 2 * ATTN_KV_WIDTH
AB_OUT_WIDTH = FOURIER_WIDTH + ATTN_Q_WIDTH
ATTN_SCALE = HEAD_DIM ** -0.5
Q_BLOCK = 128
ROPE_AXIS_DIM = HEAD_DIM // 2
ROPE_THETA = 10000.0
HGRN_WIDTH = D_MODEL
HGRN_HEAD_DIM = 128
HGRN_HEADS = HGRN_WIDTH // HGRN_HEAD_DIM
HGRN_IN_WIDTH = 5 * HGRN_WIDTH
HGRN_CHUNK = 64
N_EVEN = (DEPTH + 1) // 2
N_ODD = DEPTH // 2
EPS = 1e-6

kernel_name = 'hybrid_fourier_gqa_hgrn2_macaron'


def rms_norm(x, gain):
    xf = x.astype(jnp.float32)
    y = xf * lax.rsqrt(jnp.mean(xf * xf, axis=-1, keepdims=True) + EPS)
    return (y * gain.astype(jnp.float32)).astype(x.dtype)


def ada_pre(h, gain, shift, scale):
    return rms_norm(h, gain) * (1 + scale) + shift


def swiglu(h, w_in, w_out):
    a, b = jnp.split(h @ w_in, 2, axis=-1)
    return (jax.nn.silu(a) * b) @ w_out


def axial_rope(n_tokens):
    rows = n_tokens // GRID_W
    row_id = jnp.repeat(jnp.arange(rows, dtype=jnp.float32), GRID_W)
    col_id = jnp.tile(jnp.arange(GRID_W, dtype=jnp.float32), rows)
    inv_freq = ROPE_THETA ** (-jnp.arange(0, ROPE_AXIS_DIM, 2, dtype=jnp.float32) / ROPE_AXIS_DIM)
    ang = jnp.concatenate([row_id[:, None] * inv_freq, col_id[:, None] * inv_freq], axis=-1)
    return jnp.cos(ang), jnp.sin(ang)


def apply_axial_rope(x, cos, sin):
    B, L, H, _ = x.shape
    nf = ROPE_AXIS_DIM // 2
    xf = x.astype(jnp.float32).reshape(B, L, H, 2, 2, nf)
    c = cos.reshape(L, 1, 2, nf)
    s = sin.reshape(L, 1, 2, nf)
    x1, x2 = xf[..., 0, :], xf[..., 1, :]
    y = jnp.stack([x1 * c - x2 * s, x1 * s + x2 * c], axis=-2)
    return y.reshape(B, L, H, HEAD_DIM).astype(x.dtype)


def fourier_mix(f):
    B, n, _ = f.shape
    g = f.astype(jnp.float32).reshape(B, n, N_FOURIER_GROUPS, FOURIER_GROUP_DIM)
    y = jnp.fft.fft2(g, axes=(1, 3), norm='ortho').real
    return y.reshape(B, n, FOURIER_WIDTH).astype(f.dtype)


def gqa_attend(q, k, v):
    B, n = q.shape[:2]
    qg = q.reshape(B, n, N_KV_HEADS, Q_GROUP, HEAD_DIM)
    s = jnp.einsum('bqkgd,bskd->bkgqs', qg, k, preferred_element_type=jnp.float32) * ATTN_SCALE
    p = jax.nn.softmax(s, axis=-1).astype(v.dtype)
    o = jnp.einsum('bkgqs,bskd->bqkgd', p, v)
    return o.reshape(B, n, N_Q_HEADS * HEAD_DIM)


def split_ab(p, qk_norm):
    B, n, _ = p.shape
    f, q, k, v = jnp.split(p, [FOURIER_WIDTH, FOURIER_WIDTH + ATTN_Q_WIDTH,
                               FOURIER_WIDTH + ATTN_Q_WIDTH + ATTN_KV_WIDTH], axis=-1)
    q = rms_norm(q.reshape(B, n, N_Q_HEADS, HEAD_DIM), qk_norm[0])
    k = rms_norm(k.reshape(B, n, N_KV_HEADS, HEAD_DIM), qk_norm[1])
    v = v.reshape(B, n, N_KV_HEADS, HEAD_DIM)
    return f, q, k, v


def fourier_gqa_mixer(a_lat, a_ctx, w_in, qk_norm, w_out, need_ctx_out):
    B, L, _ = a_lat.shape
    f_lat, q_lat, k_lat, v_lat = split_ab(a_lat @ w_in, qk_norm)
    f_ctx, q_ctx, k_ctx, v_ctx = split_ab(a_ctx @ w_in, qk_norm)
    cos, sin = axial_rope(L)
    q_lat = apply_axial_rope(q_lat, cos, sin)
    k_lat = apply_axial_rope(k_lat, cos, sin)
    k_all = jnp.concatenate([k_ctx, k_lat], axis=1)
    v_all = jnp.concatenate([v_ctx, v_lat], axis=1)
    n_blocks = L // Q_BLOCK
    q_blocks = q_lat.reshape(B, n_blocks, Q_BLOCK, N_Q_HEADS, HEAD_DIM).transpose(1, 0, 2, 3, 4)
    attn_lat = lax.map(lambda qb: gqa_attend(qb, k_all, v_all), q_blocks)
    attn_lat = attn_lat.transpose(1, 0, 2, 3).reshape(B, L, ATTN_Q_WIDTH)
    y_lat = jnp.concatenate([fourier_mix(f_lat), attn_lat], axis=-1) @ w_out
    y_ctx = None
    if need_ctx_out:
        attn_ctx = gqa_attend(q_ctx, k_ctx, v_ctx)
        y_ctx = jnp.concatenate([fourier_mix(f_ctx), attn_ctx], axis=-1) @ w_out
    return y_lat, y_ctx


def chunked_gated_scan(q, k, v, log_f, s0):
    B, n, H, _ = q.shape
    nc = n // HGRN_CHUNK

    def chunks(t):
        return t.reshape(B, nc, HGRN_CHUNK, H, t.shape[-1]).transpose(1, 0, 3, 2, 4)

    causal = jnp.tril(jnp.ones((HGRN_CHUNK, HGRN_CHUNK), dtype=bool))[:, :, None]

    def step(S, xs):
        qc, kc, vc, gc = xs
        b = jnp.cumsum(gc, axis=2)
        o_inter = jnp.einsum('bhtk,bhkv->bhtv', qc * jnp.exp(b), S)
        diff = b[:, :, :, None, :] - b[:, :, None, :, :]
        decay = jnp.where(causal, jnp.exp(jnp.where(causal, diff, 0.0)), 0.0)
        a = jnp.einsum('bhtk,bhsk,bhtsk->bhts', qc, kc, decay)
        o = o_inter + jnp.einsum('bhts,bhsv->bhtv', a, vc)
        b_last = b[:, :, -1:, :]
        S_new = jnp.exp(b_last[:, :, 0, :])[..., None] * S + jnp.einsum('bhsk,bhsv->bhkv', kc * jnp.exp(b_last - b), vc)
        return S_new, o

    s_fin, o = lax.scan(step, s0, (chunks(q), chunks(k), chunks(v), chunks(log_f)))
    o = o.transpose(1, 0, 3, 2, 4).reshape(B, n, H, v.shape[-1])
    return o, s_fin


def bidirectional_scan(q, v, k_fw, lf_fw, k_bw, lf_bw, s_fw0, s_bw0):
    o_fw, s_fw = chunked_gated_scan(q, k_fw, v, lf_fw, s_fw0)
    rev = lambda t: jnp.flip(t, axis=1)
    o_bw, s_bw = chunked_gated_scan(rev(q), rev(k_bw), rev(v), rev(lf_bw), s_bw0)
    return o_fw + rev(o_bw), s_fw, s_bw


def hgrn2_project(a, w_in, lower_bound):
    B, n, _ = a.shape
    q, f_fw, f_bw, i, g = jnp.split(a @ w_in, 5, axis=-1)
    heads = lambda t: t.astype(jnp.float32).reshape(B, n, HGRN_HEADS, HGRN_HEAD_DIM)

    def forget(fl):
        fl = fl.astype(jnp.float32)
        log_f = jnp.logaddexp(jnp.log(lower_bound), jnp.log1p(-lower_bound) + jax.nn.log_sigmoid(fl))
        key = (1.0 - lower_bound) * jax.nn.sigmoid(-fl)
        return heads(key), heads(log_f)

    k_fw, lf_fw = forget(f_fw)
    k_bw, lf_bw = forget(f_bw)
    return heads(jax.nn.silu(q)), heads(i), g, k_fw, lf_fw, k_bw, lf_bw


def hgrn2_mixer(a_lat, a_ctx, w_in, lower_bound, o_norm, w_out, need_ctx_out):
    B = a_lat.shape[0]
    s_zero = jnp.zeros((B, HGRN_HEADS, HGRN_HEAD_DIM, HGRN_HEAD_DIM), jnp.float32)
    q_c, i_c, g_c, k_fc, lf_fc, k_bc, lf_bc = hgrn2_project(a_ctx, w_in, lower_bound)
    o_c, s_fw, s_bw = bidirectional_scan(q_c, i_c, k_fc, lf_fc, k_bc, lf_bc, s_zero, s_zero)
    q_l, i_l, g_l, k_fl, lf_fl, k_bl, lf_bl = hgrn2_project(a_lat, w_in, lower_bound)
    o_l, _, _ = bidirectional_scan(q_l, i_l, k_fl, lf_fl, k_bl, lf_bl, s_fw, s_bw)

    def readout(o, g):
        Bo, n = o.shape[:2]
        o = rms_norm(o, o_norm).reshape(Bo, n, HGRN_WIDTH).astype(g.dtype)
        return (o * jax.nn.sigmoid(g)) @ w_out

    y_lat = readout(o_l, g_l)
    y_ctx = readout(o_c, g_c) if need_ctx_out else None
    return y_lat, y_ctx


def setup_inputs(seed: int = 0) -> dict:
    key = jax.random.key(seed)
    ks = jax.random.split(key, 17)
    d = D_MODEL
    nrm = lambda k, shape, std: std * jax.random.normal(k, shape, jnp.float32)
    return {
        'x': nrm(ks[0], (BATCH, SEQ, d), 1.0),
        'c': nrm(ks[1], (BATCH, d), 1.0),
        'ctx': nrm(ks[2], (BATCH, CTX_LEN, d), 1.0),
        'c_ctx': nrm(ks[3], (d,), 1.0),
        'w_mod': nrm(ks[4], (DEPTH, d, N_MOD * d), 0.5 * d ** -0.5),
        'b_mod': nrm(ks[5], (DEPTH, N_MOD * d), 0.01),
        'norm_gains': 1.0 + nrm(ks[6], (DEPTH, 3, d), 0.02),
        'ffn_w_in': nrm(ks[7], (DEPTH, 2, d, 2 * FFN_DIM), d ** -0.5),
        'ffn_w_out': nrm(ks[8], (DEPTH, 2, FFN_DIM, d), FFN_DIM ** -0.5),
        'ab_w_in': nrm(ks[9], (N_EVEN, d, AB_IN_WIDTH), d ** -0.5),
        'qk_norm': 1.0 + nrm(ks[10], (N_EVEN, 2, HEAD_DIM), 0.02),
        'ab_w_out': nrm(ks[11], (N_EVEN, AB_OUT_WIDTH, d), AB_OUT_WIDTH ** -0.5),
        'hgrn_w_in': nrm(ks[12], (N_ODD, d, HGRN_IN_WIDTH), d ** -0.5),
        'hgrn_lb_logits': nrm(ks[13], (DEPTH, HGRN_WIDTH), 0.1),
        'hgrn_o_norm': 1.0 + nrm(ks[14], (N_ODD, HGRN_HEAD_DIM), 0.02),
        'hgrn_w_out': nrm(ks[15], (N_ODD, HGRN_WIDTH, d), HGRN_WIDTH ** -0.5),
        'final_norm': 1.0 + nrm(ks[16], (d,), 0.02),
    }


def reference(x, c, ctx, c_ctx, w_mod, b_mod, norm_gains, ffn_w_in, ffn_w_out, ab_w_in, qk_norm, ab_w_out,
              hgrn_w_in, hgrn_lb_logits, hgrn_o_norm, hgrn_w_out, final_norm):
    lb_cum = jnp.cumsum(jax.nn.softmax(hgrn_lb_logits.astype(jnp.float32), axis=0), axis=0)
    lower_bounds = lb_cum - lb_cum[0]
    h_lat, h_ctx = x, ctx
    for layer in range(DEPTH):
        last = layer == DEPTH - 1
        m_lat = jnp.split((jax.nn.silu(c) @ w_mod[layer] + b_mod[layer])[:, None, :], N_MOD, axis=-1)
        m_ctx = jnp.split((jax.nn.silu(c_ctx) @ w_mod[layer] + b_mod[layer])[None, None, :], N_MOD, axis=-1)
        g0, g1, g2 = norm_gains[layer, 0], norm_gains[layer, 1], norm_gains[layer, 2]
        h_lat = h_lat + 0.5 * m_lat[2] * swiglu(ada_pre(h_lat, g0, m_lat[0], m_lat[1]), ffn_w_in[layer, 0], ffn_w_out[layer, 0])
        h_ctx = h_ctx + 0.5 * m_ctx[2] * swiglu(ada_pre(h_ctx, g0, m_ctx[0], m_ctx[1]), ffn_w_in[layer, 0], ffn_w_out[layer, 0])
        a_lat = ada_pre(h_lat, g1, m_lat[3], m_lat[4])
        a_ctx = ada_pre(h_ctx, g1, m_ctx[3], m_ctx[4])
        if layer % 2 == 0:
            e = layer // 2
            y_lat, y_ctx = fourier_gqa_mixer(a_lat, a_ctx, ab_w_in[e], qk_norm[e], ab_w_out[e], not last)
        else:
            o = layer // 2
            y_lat, y_ctx = hgrn2_mixer(a_lat, a_ctx, hgrn_w_in[o], lower_bounds[layer], hgrn_o_norm[o], hgrn_w_out[o], not last)
        h_lat = h_lat + m_lat[5] * y_lat
        h_lat = h_lat + 0.5 * m_lat[8] * swiglu(ada_pre(h_lat, g2, m_lat[6], m_lat[7]), ffn_w_in[layer, 1], ffn_w_out[layer, 1])
        if not last:
            h_ctx = h_ctx + m_ctx[5] * y_ctx
            h_ctx = h_ctx + 0.5 * m_ctx[8] * swiglu(ada_pre(h_ctx, g2, m_ctx[6], m_ctx[7]), ffn_w_in[layer, 1], ffn_w_out[layer, 1])
    return rms_norm(h_lat, final_norm)
```

```python
import functools
import math

import jax
import jax.numpy as jnp
import numpy as np
from jax import lax
from jax.experimental import pallas as pl
from jax.experimental.pallas import tpu as pltpu

F32 = jnp.float32
BF16 = jnp.bfloat16

EPS = 1e-6
N_MOD = 9
HEAD_DIM = 128
N_KV_HEADS = 4
FOURIER_WIDTH = 512
FOURIER_GROUP_DIM = 128
GRID_W = 64
ROPE_THETA = 10000.0
ROPE_AXIS_DIM = HEAD_DIM // 2
ATTN_SCALE = HEAD_DIM ** -0.5

LANES = 128
SUBLANES = 8
VMEM_LIMIT_BYTES = 56 * 1024 * 1024

ROW_TILE = 512
READOUT_ROW_TILE = 256
FFN_TILE = 512
PROJ_COL_TILE = 512
MOD_COL_TILE = 1024
ATTN_Q_TILE = 256
ATTN_KV_TILE = 512
SCAN_CHUNK = 128
SCAN_HEADS_PER_BLOCK = 4
FFT_B = 128
FFT_B_BLOCK = 8
FFT_COL_TILE = 2048


def _cparams(*sem):
    return pltpu.CompilerParams(dimension_semantics=sem, vmem_limit_bytes=VMEM_LIMIT_BYTES)


def _dot(a, b):
    return jnp.dot(a, b, preferred_element_type=F32)


def _dot_hi(a, b):
    return jnp.dot(a, b, preferred_element_type=F32, precision=lax.Precision.HIGHEST)


def _dot_nt(a, b):
    return lax.dot_general(a, b, (((1,), (1,)), ((), ())), preferred_element_type=F32)


def _dot_tn(a, b):
    return lax.dot_general(a, b, (((0,), (0,)), ((), ())), preferred_element_type=F32)


def _silu(x):
    return x * jax.nn.sigmoid(x)


def _rms(x, gain):
    return x * lax.rsqrt(jnp.mean(x * x, axis=-1, keepdims=True) + EPS) * gain


def _ada(h, gain, shift, scale):
    return _rms(h, gain) * (1.0 + scale) + shift


def _mod_kernel(c_ref, w_ref, b_ref, o_ref):
    a = _silu(c_ref[...]).astype(BF16)
    o_ref[...] = _dot(a, w_ref[...].astype(BF16)) + b_ref[...]


def _modulation(c_rows, w_mod, b_mod):
    depth, d, nd = w_mod.shape
    tn = MOD_COL_TILE
    return pl.pallas_call(
        _mod_kernel,
        out_shape=jax.ShapeDtypeStruct((depth, SUBLANES, nd), F32),
        grid=(depth, nd // tn),
        in_specs=[
            pl.BlockSpec((SUBLANES, d), lambda l, j: (0, 0)),
            pl.BlockSpec((None, d, tn), lambda l, j: (l, 0, j)),
            pl.BlockSpec((None, 1, tn), lambda l, j: (l, 0, j)),
        ],
        out_specs=pl.BlockSpec((None, SUBLANES, tn), lambda l, j: (l, 0, j)),
        compiler_params=_cparams("parallel", "arbitrary"),
        name="modulation",
    )(c_rows, w_mod, b_mod.reshape(depth, 1, nd))


class _Rows:
    def __init__(self, batch, seq, ctx_len, tile):
        assert seq % tile == 0 and (batch * ctx_len) % tile == 0
        self.batch, self.seq, self.ctx_len, self.tile = batch, seq, ctx_len, tile
        self.n_lat = batch * seq
        self.n = self.n_lat + batch * ctx_len
        self.lat_tiles = self.n_lat // tile
        self.tiles = self.n // tile
        self.tiles_per_batch = seq // tile

    def group(self, i):
        return jnp.where(i < self.lat_tiles, 1 + i // self.tiles_per_batch, 0)


def _mod_spec(rows, sub):
    d = None
    return lambda dm: pl.BlockSpec((None, None, 3, dm), lambda i, *_: (rows.group(i), sub, 0, 0))


def _ffn_kernel(h_ref, mod_ref, gain_ref, wa_ref, wb_ref, wo_ref, fin_ref, o_ref, xn_ref, acc_ref, *, final):
    j = pl.program_id(1)

    @pl.when(j == 0)
    def _():
        xn = _ada(h_ref[...], gain_ref[...], mod_ref[0:1, :], mod_ref[1:2, :])
        xn_ref[...] = xn.astype(BF16)
        acc_ref[...] = jnp.zeros_like(acc_ref)

    xn = xn_ref[...]
    a = _dot(xn, wa_ref[...])
    b = _dot(xn, wb_ref[...])
    g = (_silu(a) * b).astype(BF16)
    acc_ref[...] += _dot(g, wo_ref[...])

    @pl.when(j == pl.num_programs(1) - 1)
    def _():
        out = h_ref[...] + 0.5 * mod_ref[2:3, :] * acc_ref[...]
        if final:
            out = _rms(out, fin_ref[...])
        o_ref[...] = out


def _ffn(h, mods, sub, gain, w_in, w_out, fin, rows, n_tiles, final):
    d = h.shape[1]
    f = w_out.shape[0]
    tm, tf = rows.tile, FFN_TILE
    nf = f // tf
    return pl.pallas_call(
        functools.partial(_ffn_kernel, final=final),
        out_shape=jax.ShapeDtypeStruct((n_tiles * tm, d), F32),
        grid=(n_tiles, nf),
        in_specs=[
            pl.BlockSpec((tm, d), lambda i, j: (i, 0)),
            pl.BlockSpec((None, None, 3, d), lambda i, j: (rows.group(i), sub, 0, 0)),
            pl.BlockSpec((1, d), lambda i, j: (0, 0)),
            pl.BlockSpec((d, tf), lambda i, j: (0, j)),
            pl.BlockSpec((d, tf), lambda i, j: (0, nf + j)),
            pl.BlockSpec((tf, d), lambda i, j: (j, 0)),
            pl.BlockSpec((1, d), lambda i, j: (0, 0)),
        ],
        out_specs=pl.BlockSpec((tm, d), lambda i, j: (i, 0)),
        scratch_shapes=[pltpu.VMEM((tm, d), BF16), pltpu.VMEM((tm, d), F32)],
        compiler_params=_cparams("parallel", "arbitrary"),
        name="ffn",
    )(h, mods, gain, w_in, w_in, w_out, fin)


def _rope_tables(seq, tile):
    t = np.arange(seq)
    inv_freq = ROPE_THETA ** (-np.arange(0, ROPE_AXIS_DIM, 2, dtype=np.float64) / ROPE_AXIS_DIM)
    ang = np.concatenate([(t // GRID_W)[:, None] * inv_freq, (t % GRID_W)[:, None] * inv_freq], axis=-1)
    nf = ROPE_AXIS_DIM // 2
    cos = np.cos(ang).reshape(seq, 2, 1, nf)
    sin = np.sin(ang).reshape(seq, 2, 1, nf)
    zero = np.zeros_like(sin)
    c_full = np.broadcast_to(cos, (seq, 2, 2, nf)).reshape(seq, HEAD_DIM)
    s_up = np.concatenate([-sin, zero], axis=2).reshape(seq, HEAD_DIM)
    s_dn = np.concatenate([zero, sin], axis=2).reshape(seq, HEAD_DIM)
    lat = np.concatenate([c_full, s_up, s_dn], axis=1)
    ident = np.concatenate([np.ones((tile, HEAD_DIM)), np.zeros((tile, 2 * HEAD_DIM))], axis=1)
    return jnp.asarray(np.concatenate([lat, ident], axis=0), dtype=F32)


def _norm_rope_heads(acc, gain, rope):
    nf = ROPE_AXIS_DIM // 2
    c, s_up, s_dn = rope[:, :HEAD_DIM], rope[:, HEAD_DIM:2 * HEAD_DIM], rope[:, 2 * HEAD_DIM:]
    heads = []
    for hh in range(acc.shape[1] // HEAD_DIM):
        y = _rms(acc[:, hh * HEAD_DIM:(hh + 1) * HEAD_DIM], gain)
        y = y * c + pltpu.roll(y, HEAD_DIM - nf, 1) * s_up + pltpu.roll(y, nf, 1) * s_dn
        heads.append(y)
    return jnp.concatenate(heads, axis=1).astype(BF16)


def _ab_proj_kernel(h_ref, mod_ref, gain_ref, w_ref, qkn_ref, rope_ref, f_ref, q_ref, k_ref, v_ref, xn_ref,
                    *, nq):
    j = pl.program_id(1)

    @pl.when(j == 0)
    def _():
        xn_ref[...] = _ada(h_ref[...], gain_ref[...], mod_ref[0:1, :], mod_ref[1:2, :]).astype(BF16)

    acc = _dot(xn_ref[...], w_ref[...])

    @pl.when(j == 0)
    def _():
        f_ref[...] = acc

    @pl.when(jnp.logical_and(j >= 1, j <= nq))
    def _():
        q_ref[...] = _norm_rope_heads(acc, qkn_ref[0:1, :], rope_ref[...])

    @pl.when(j == nq + 1)
    def _():
        k_ref[...] = _norm_rope_heads(acc, qkn_ref[1:2, :], rope_ref[...])

    @pl.when(j == nq + 2)
    def _():
        v_ref[...] = acc.astype(BF16)


def _ab_proj(h, mods, gain, w_in, qk_norm, rope, rows):
    d = h.shape[1]
    tm, tn = rows.tile, PROJ_COL_TILE
    assert FOURIER_WIDTH == tn and N_KV_HEADS * HEAD_DIM == tn
    q_width = w_in.shape[1] - FOURIER_WIDTH - 2 * N_KV_HEADS * HEAD_DIM
    nq = q_width // tn
    n = rows.n
    rope_blk = lambda i, j: (jnp.where(i < rows.lat_tiles, i % rows.tiles_per_batch, rows.tiles_per_batch), 0)
    return pl.pallas_call(
        functools.partial(_ab_proj_kernel, nq=nq),
        out_shape=(
            jax.ShapeDtypeStruct((n, FOURIER_WIDTH), F32),
            jax.ShapeDtypeStruct((n, q_width), BF16),
            jax.ShapeDtypeStruct((n, tn), BF16),
            jax.ShapeDtypeStruct((n, tn), BF16),
        ),
        grid=(rows.tiles, nq + 3),
        in_specs=[
            pl.BlockSpec((tm, d), lambda i, j: (i, 0)),
            pl.BlockSpec((None, None, 3, d), lambda i, j: (rows.group(i), 1, 0, 0)),
            pl.BlockSpec((1, d), lambda i, j: (0, 0)),
            pl.BlockSpec((d, tn), lambda i, j: (0, j)),
            pl.BlockSpec((2, HEAD_DIM), lambda i, j: (0, 0)),
            pl.BlockSpec((tm, 3 * HEAD_DIM), rope_blk),
        ],
        out_specs=(
            pl.BlockSpec((tm, tn), lambda i, j: (i, 0)),
            pl.BlockSpec((tm, tn), lambda i, j: (i, jnp.clip(j - 1, 0, nq - 1))),
            pl.BlockSpec((tm, tn), lambda i, j: (i, 0)),
            pl.BlockSpec((tm, tn), lambda i, j: (i, 0)),
        ),
        scratch_shapes=[pltpu.VMEM((tm, d), BF16)],
        compiler_params=_cparams("parallel", "arbitrary"),
        name="ab_proj",
    )(h, mods, gain, w_in, qk_norm, rope)


def _attn_kernel(*refs, seg_lens, kv_tile, group):
    q_ref = refs[0]
    kv_refs = refs[1:1 + 2 * len(seg_lens)]
    o_ref = refs[1 + 2 * len(seg_lens)]
    m_ref, l_ref, acc_ref = refs[2 + 2 * len(seg_lens):]
    tq = q_ref.shape[0]
    q = q_ref[...]
    qs = jnp.concatenate([q[:, g * HEAD_DIM:(g + 1) * HEAD_DIM] for g in range(group)], axis=0)

    m_ref[...] = jnp.full_like(m_ref, -jnp.inf)
    l_ref[...] = jnp.zeros_like(l_ref)
    acc_ref[...] = jnp.zeros_like(acc_ref)

    def update(k, v):
        s = _dot_nt(qs, k) * ATTN_SCALE
        m_old = m_ref[...]
        m_new = jnp.maximum(m_old, jnp.max(s, axis=-1, keepdims=True))
        p = jnp.exp(s - m_new)
        alpha = jnp.exp(m_old - m_new)
        l_ref[...] = alpha * l_ref[...] + jnp.sum(p, axis=-1, keepdims=True)
        acc_ref[...] = alpha * acc_ref[...] + _dot(p.astype(BF16), v)
        m_ref[...] = m_new

    for si, seg in enumerate(seg_lens):
        k_ref, v_ref = kv_refs[2 * si], kv_refs[2 * si + 1]
        if seg <= kv_tile:
            update(k_ref[...], v_ref[...])
        else:
            def body(c, carry, k_ref=k_ref, v_ref=v_ref):
                start = pl.multiple_of(c * kv_tile, kv_tile)
                update(k_ref[pl.ds(start, kv_tile), :], v_ref[pl.ds(start, kv_tile), :])
                return carry
            lax.fori_loop(0, seg // kv_tile, body, 0)

    out = acc_ref[...] / l_ref[...]
    o_ref[...] = jnp.concatenate([out[g * tq:(g + 1) * tq, :] for g in range(group)], axis=1).astype(BF16)


def _attention(q_all, k_all, v_all, rows, latent):
    b, seq, lc = rows.batch, rows.seq, rows.ctx_len
    group = q_all.shape[1] // (N_KV_HEADS * HEAD_DIM)
    gw = group * HEAD_DIM
    ctx_blk0 = rows.n_lat // lc
    ctx_spec = pl.BlockSpec((lc, HEAD_DIM), lambda bi, hi, i: (ctx_blk0 + bi, hi))
    if latent:
        tq = ATTN_Q_TILE
        nqt = seq // tq
        q_spec = pl.BlockSpec((tq, gw), lambda bi, hi, i: (bi * nqt + i, hi))
        lat_spec = pl.BlockSpec((seq, HEAD_DIM), lambda bi, hi, i: (bi, hi))
        in_specs = [q_spec, ctx_spec, ctx_spec, lat_spec, lat_spec]
        args = (q_all, k_all, v_all, k_all, v_all)
        seg_lens = (lc, seq)
        n_out = rows.n_lat
    else:
        tq = lc
        nqt = 1
        q_spec = pl.BlockSpec((tq, gw), lambda bi, hi, i: (ctx_blk0 + bi, hi))
        in_specs = [q_spec, ctx_spec, ctx_spec]
        args = (q_all, k_all, v_all)
        seg_lens = (lc,)
        n_out = b * lc
    out_spec = pl.BlockSpec((tq, gw), lambda bi, hi, i: (bi * nqt + i, hi))
    return pl.pallas_call(
        functools.partial(_attn_kernel, seg_lens=seg_lens, kv_tile=ATTN_KV_TILE, group=group),
        out_shape=jax.ShapeDtypeStruct((n_out, q_all.shape[1]), BF16),
        grid=(b, N_KV_HEADS, nqt),
        in_specs=in_specs,
        out_specs=out_spec,
        scratch_shapes=[
            pltpu.VMEM((group * tq, 1), F32),
            pltpu.VMEM((group * tq, 1), F32),
            pltpu.VMEM((group * tq, HEAD_DIM), F32),
        ],
        compiler_params=_cparams("parallel", "parallel", "arbitrary"),
        name="attention_lat" if latent else "attention_ctx",
    )(*args)


def _dft_cs(n):
    idx = np.arange(n)
    ang = 2.0 * np.pi * ((idx[:, None] * idx[None, :]) % n) / n
    return np.cos(ang), np.sin(ang)


def _fft1_kernel(x_ref, fa_ref, tw_ref, o_ref, *, a):
    w = o_ref.shape[2]
    for r in range(o_ref.shape[0]):
        z = _dot_hi(fa_ref[...], x_ref[:, r * w:(r + 1) * w])
        zr, zi = z[:a], z[a:]
        tc = jnp.concatenate([tw_ref[r, 0]] * (w // LANES), axis=1)
        ts = jnp.concatenate([tw_ref[r, 1]] * (w // LANES), axis=1)
        o_ref[r, :a, :] = zr * tc - zi * ts
        o_ref[r, a:, :] = zr * ts + zi * tc


def _fft2_kernel(zr_ref, zi_ref, m2_ref, mc_ref, o_ref, *, scale):
    z = jnp.concatenate([zr_ref[...], zi_ref[...]], axis=0)
    v = _dot_hi(m2_ref[...], z)
    vr, vi = v[:FFT_B], v[FFT_B:]
    outs = []
    for g in range(o_ref.shape[1] // LANES):
        u = jnp.concatenate([vr[:, g * LANES:(g + 1) * LANES], vi[:, g * LANES:(g + 1) * LANES]], axis=1)
        outs.append(_dot_hi(u, mc_ref[...]))
    o_ref[...] = jnp.concatenate(outs, axis=1) * scale


def _fourier_latent(f_all, rows):
    b, seq = rows.batch, rows.seq
    w = FOURIER_WIDTH
    a = seq // FFT_B
    assert a % SUBLANES == 0 and FOURIER_GROUP_DIM == LANES
    ca, sa = _dft_cs(a)
    fa = jnp.asarray(np.concatenate([ca, sa], axis=0), dtype=F32)
    p1b = (np.arange(a)[None, :] * np.arange(FFT_B)[:, None]) % seq
    ang = 2.0 * np.pi * p1b / seq
    tw = np.stack([np.cos(ang), np.sin(ang)], axis=1)[..., None]
    tw = jnp.asarray(np.broadcast_to(tw, (FFT_B, 2, a, LANES)), dtype=F32)
    cb, sb = _dft_cs(FFT_B)
    m2 = jnp.asarray(np.block([[cb, -sb], [sb, cb]]), dtype=F32)
    cc, sc = _dft_cs(FOURIER_GROUP_DIM)
    mc = jnp.asarray(np.concatenate([cc, -sc], axis=0), dtype=F32)

    blk = FFT_B_BLOCK
    x2 = f_all.reshape(rows.n // FFT_B, FFT_B * w)
    z = pl.pallas_call(
        functools.partial(_fft1_kernel, a=a),
        out_shape=jax.ShapeDtypeStruct((b, FFT_B, 2 * a, w), F32),
        grid=(b, FFT_B // blk),
        in_specs=[
            pl.BlockSpec((a, blk * w), lambda bi, j: (bi, j)),
            pl.BlockSpec((2 * a, a), lambda bi, j: (0, 0)),
            pl.BlockSpec((blk, 2, a, LANES), lambda bi, j: (j, 0, 0, 0)),
        ],
        out_specs=pl.BlockSpec((None, blk, 2 * a, w), lambda bi, j: (bi, j, 0, 0)),
        compiler_params=_cparams("parallel", "parallel"),
        name="fourier_stage1",
    )(x2, fa, tw)

    tc = min(FFT_COL_TILE, a * w)
    ncol = (a * w) // tc
    z2 = z.reshape(b, FFT_B, 2 * a * w)
    y = pl.pallas_call(
        functools.partial(_fft2_kernel, scale=1.0 / math.sqrt(seq * FOURIER_GROUP_DIM)),
        out_shape=jax.ShapeDtypeStruct((b, FFT_B, a * w), F32),
        grid=(b, ncol),
        in_specs=[
            pl.BlockSpec((None, FFT_B, tc), lambda bi, j: (bi, 0, j)),
            pl.BlockSpec((None, FFT_B, tc), lambda bi, j: (bi, 0, ncol + j)),
            pl.BlockSpec((2 * FFT_B, 2 * FFT_B), lambda bi, j: (0, 0)),
            pl.BlockSpec((2 * LANES, LANES), lambda bi, j: (0, 0)),
        ],
        out_specs=pl.BlockSpec((None, FFT_B, tc), lambda bi, j: (bi, 0, j)),
        compiler_params=_cparams("parallel", "parallel"),
        name="fourier_stage2",
    )(z2, z2, m2, mc)
    return y.reshape(b * seq, w)


def _dft_ctx_kernel(x_ref, cn_ref, sn_ref, cc_ref, sc_ref, o_ref, *, scale):
    x = x_ref[...]
    outs = []
    for g in range(x.shape[1] // LANES):
        xg = x[:, g * LANES:(g + 1) * LANES]
        outs.append(_dot_hi(cn_ref[...], _dot_hi(xg, cc_ref[...])) - _dot_hi(sn_ref[...], _dot_hi(xg, sc_ref[...])))
    o_ref[...] = jnp.concatenate(outs, axis=1) * scale


def _fourier_ctx(f_all, rows):
    b, lc = rows.batch, rows.ctx_len
    w = FOURIER_WIDTH
    cn, sn = _dft_cs(lc)
    cc, sc = _dft_cs(FOURIER_GROUP_DIM)
    blk0 = rows.n_lat // lc
    mat = lambda m: pl.BlockSpec(m.shape, lambda bi: (0, 0))
    consts = [jnp.asarray(m, dtype=F32) for m in (cn, sn, cc, sc)]
    return pl.pallas_call(
        functools.partial(_dft_ctx_kernel, scale=1.0 / math.sqrt(lc * FOURIER_GROUP_DIM)),
        out_shape=jax.ShapeDtypeStruct((b * lc, w), F32),
        grid=(b,),
        in_specs=[pl.BlockSpec((lc, w), lambda bi: (blk0 + bi, 0))] + [mat(m) for m in consts],
        out_specs=pl.BlockSpec((lc, w), lambda bi: (bi, 0)),
        compiler_params=_cparams("parallel"),
        name="fourier_ctx",
    )(f_all, *consts)


def _ab_out_kernel(x1_ref, x2_ref, w_ref, h_ref, mod_ref, o_ref):
    x = jnp.concatenate([x1_ref[...].astype(BF16), x2_ref[...]], axis=1)
    o_ref[...] = h_ref[...] + mod_ref[2:3, :] * _dot(x, w_ref[...])


def _ab_out(x1, x2, w_out, h, mods, rows):
    d = h.shape[1]
    tm = rows.tile
    return pl.pallas_call(
        _ab_out_kernel,
        out_shape=jax.ShapeDtypeStruct((rows.n, d), F32),
        grid=(rows.tiles,),
        in_specs=[
            pl.BlockSpec((tm, x1.shape[1]), lambda i: (i, 0)),
            pl.BlockSpec((tm, x2.shape[1]), lambda i: (i, 0)),
            pl.BlockSpec(w_out.shape, lambda i: (0, 0)),
            pl.BlockSpec((tm, d), lambda i: (i, 0)),
            pl.BlockSpec((None, None, 3, d), lambda i: (rows.group(i), 1, 0, 0)),
        ],
        out_specs=pl.BlockSpec((tm, d), lambda i: (i, 0)),
        compiler_params=_cparams("parallel"),
        name="ab_out",
    )(x1, x2, w_out, h, mods)


def _hgrn_out_kernel(ofw_ref, obw_ref, g_ref, gain_ref, w_ref, h_ref, mod_ref, o_ref):
    o = ofw_ref[...] + obw_ref[...]
    parts = []
    for hh in range(o.shape[1] // HEAD_DIM):
        oh = o[:, hh * HEAD_DIM:(hh + 1) * HEAD_DIM]
        parts.append(oh * lax.rsqrt(jnp.mean(oh * oh, axis=-1, keepdims=True) + EPS))
    on = jnp.concatenate(parts, axis=1) * gain_ref[...]
    y = (on * jax.nn.sigmoid(g_ref[...])).astype(BF16)
    o_ref[...] = h_ref[...] + mod_ref[2:3, :] * _dot(y, w_ref[...])


def _hgrn_out(o_fw, o_bw, p, gain, w_out, h, mods, rows_r, n_tiles):
    d = h.shape[1]
    tm = rows_r.tile
    g_blk = (p.shape[1] - d) // d
    return pl.pallas_call(
        _hgrn_out_kernel,
        out_shape=jax.ShapeDtypeStruct((n_tiles * tm, d), F32),
        grid=(n_tiles,),
        in_specs=[
            pl.BlockSpec((tm, d), lambda i: (i, 0)),
            pl.BlockSpec((tm, d), lambda i: (i, 0)),
            pl.BlockSpec((tm, d), lambda i: (i, g_blk)),
            pl.BlockSpec((1, d), lambda i: (0, 0)),
            pl.BlockSpec(w_out.shape, lambda i: (0, 0)),
            pl.BlockSpec((tm, d), lambda i: (i, 0)),
            pl.BlockSpec((None, None, 3, d), lambda i: (rows_r.group(i), 1, 0, 0)),
        ],
        out_specs=pl.BlockSpec((tm, d), lambda i: (i, 0)),
        compiler_params=_cparams("parallel"),
        name="hgrn_out",
    )(o_fw, o_bw, p, gain, w_out, h, mods)


def _hgrn_proj_kernel(h_ref, mod_ref, gain_ref, w_ref, o_ref, xn_ref, *, n_silu):
    j = pl.program_id(1)

    @pl.when(j == 0)
    def _():
        xn_ref[...] = _ada(h_ref[...], gain_ref[...], mod_ref[0:1, :], mod_ref[1:2, :]).astype(BF16)

    acc = _dot(xn_ref[...], w_ref[...])
    o_ref[...] = jnp.where(j < n_silu, _silu(acc), acc)


def _hgrn_proj(h, mods, gain, w_in, rows):
    d = h.shape[1]
    tm, tn = rows.tile, PROJ_COL_TILE
    nw = w_in.shape[1]
    return pl.pallas_call(
        functools.partial(_hgrn_proj_kernel, n_silu=d // tn),
        out_shape=jax.ShapeDtypeStruct((rows.n, nw), F32),
        grid=(rows.tiles, nw // tn),
        in_specs=[
            pl.BlockSpec((tm, d), lambda i, j: (i, 0)),
            pl.BlockSpec((None, None, 3, d), lambda i, j: (rows.group(i), 1, 0, 0)),
            pl.BlockSpec((1, d), lambda i, j: (0, 0)),
            pl.BlockSpec((d, tn), lambda i, j: (0, j)),
        ],
        out_specs=pl.BlockSpec((tm, tn), lambda i, j: (i, j)),
        scratch_shapes=[pltpu.VMEM((tm, d), BF16)],
        compiler_params=_cparams("parallel", "arbitrary"),
        name="hgrn_proj",
    )(h, mods, gain, w_in)


def _scan_levels(chunk):
    return [chunk >> (i + 1) for i in range(int(math.log2(chunk)))]


def _scan_consts(chunk, reverse):
    t = np.arange(chunk)[:, None]
    s = np.arange(chunk)[None, :]
    tri = (s >= t) if reverse else (s <= t)
    masks = []
    for h in _scan_levels(chunk):
        same = (t // (2 * h)) == (s // (2 * h))
        t_up = (t // h) % 2 == 1
        s_up = (s // h) % 2 == 1
        pair = (~t_up & s_up) if reverse else (t_up & ~s_up)
        masks.append(same & pair)
    masks.append(t == s)
    return jnp.asarray(tri, dtype=BF16), jnp.asarray(np.stack(masks), dtype=F32)


def _seg_bcast(x, h, reverse):
    c, w = x.shape
    off = h if reverse else h - 1
    if 2 * h >= 2 * SUBLANES:
        pieces = [jnp.broadcast_to(x[g * 2 * h + off:g * 2 * h + off + 1, :], (2 * h, w)) for g in range(c // (2 * h))]
        return pieces[0] if len(pieces) == 1 else jnp.concatenate(pieces, axis=0)
    x3 = x.reshape(c // SUBLANES, SUBLANES, w)
    sub = lax.broadcasted_iota(jnp.int32, x3.shape, 1)
    y = None
    for g in range(SUBLANES // (2 * h)):
        piece = jnp.broadcast_to(x3[:, g * 2 * h + off:g * 2 * h + off + 1, :], x3.shape)
        y = piece if y is None else jnp.where(sub >= g * 2 * h, piece, y)
    return y.reshape(c, w)


def _scan_kernel(q_ref, f_ref, v_ref, lb_ref, tri_ref, mask_ref, o_ref, st_ref, *, reverse, heads):
    @pl.when(pl.program_id(2) == 0)
    def _():
        st_ref[...] = jnp.zeros_like(st_ref)

    chunk = q_ref.shape[0]
    levels = _scan_levels(chunk)
    tri = tri_ref[...]
    for hh in range(heads):
        sl = slice(hh * HEAD_DIM, (hh + 1) * HEAD_DIM)
        q = q_ref[:, sl]
        fl = f_ref[:, sl]
        v = v_ref[:, sl].astype(BF16)
        lb = lb_ref[:, sl]
        key = (1.0 - lb) * jax.nn.sigmoid(-fl)
        log_sig = jnp.minimum(fl, 0.0) - jnp.log1p(jnp.exp(-jnp.abs(fl)))
        x1 = jnp.log(lb)
        x2 = jnp.log1p(-lb) + log_sig
        delta = x1 - x2
        lf = jnp.where(jnp.isnan(delta), x1 + x2, jnp.maximum(x1, x2) + jnp.log1p(jnp.exp(-jnp.abs(delta))))
        l1 = lf.astype(BF16)
        r1 = lf - l1.astype(F32)
        l2 = r1.astype(BF16)
        l3 = (r1 - l2.astype(F32)).astype(BF16)
        cs = _dot(tri, jnp.concatenate([l1, l2, l3], axis=1))
        b = cs[:, :HEAD_DIM] + cs[:, HEAD_DIM:2 * HEAD_DIM] + cs[:, 2 * HEAD_DIM:]
        total = b[0:1, :] if reverse else b[chunk - 1:chunk, :]

        a = mask_ref[len(levels)] * _dot_nt(q.astype(BF16), key.astype(BF16))
        for li, h in enumerate(levels):
            e = jnp.exp(-jnp.abs(b - _seg_bcast(b, h, reverse)))
            a = a + mask_ref[li] * _dot_nt((q * e).astype(BF16), (key * e).astype(BF16))

        st = st_ref[hh]
        o = _dot_nt((q * jnp.exp(b)).astype(BF16), st.astype(BF16)) + _dot(a.astype(BF16), v)
        o_ref[:, sl] = o
        kd = (key * jnp.exp(total - b)).astype(BF16)
        st_ref[hh] = st * jnp.exp(total) + _dot_tn(v, kd)


def _hgrn_scan(p, lower_bound, rows, reverse):
    b, seq, lc = rows.batch, rows.seq, rows.ctx_len
    d = lower_bound.shape[0]
    c, hb = SCAN_CHUNK, SCAN_HEADS_PER_BLOCK
    wb = hb * HEAD_DIM
    ncb = d // wb
    nctx, nlat = lc // c, seq // c
    ctx0 = rows.n_lat // c
    f_blk = (2 if reverse else 1) * ncb
    v_blk = 3 * ncb

    def row(bi, s):
        if reverse:
            return jnp.where(s < nctx, ctx0 + bi * nctx + (nctx - 1 - s), bi * nlat + (nlat - 1 - (s - nctx)))
        return jnp.where(s < nctx, ctx0 + bi * nctx + s, bi * nlat + (s - nctx))

    tri, masks = _scan_consts(c, reverse)
    return pl.pallas_call(
        functools.partial(_scan_kernel, reverse=reverse, heads=hb),
        out_shape=jax.ShapeDtypeStruct((rows.n, d), F32),
        grid=(b, ncb, nctx + nlat),
        in_specs=[
            pl.BlockSpec((c, wb), lambda bi, hi, s: (row(bi, s), hi)),
            pl.BlockSpec((c, wb), lambda bi, hi, s: (row(bi, s), f_blk + hi)),
            pl.BlockSpec((c, wb), lambda bi, hi, s: (row(bi, s), v_blk + hi)),
            pl.BlockSpec((1, wb), lambda bi, hi, s: (0, hi)),
            pl.BlockSpec(tri.shape, lambda bi, hi, s: (0, 0)),
            pl.BlockSpec(masks.shape, lambda bi, hi, s: (0, 0, 0)),
        ],
        out_specs=pl.BlockSpec((c, wb), lambda bi, hi, s: (row(bi, s), hi)),
        scratch_shapes=[pltpu.VMEM((hb, HEAD_DIM, HEAD_DIM), F32)],
        compiler_params=_cparams("parallel", "parallel", "arbitrary"),
        name="hgrn_scan_bw" if reverse else "hgrn_scan_fw",
    )(p, p, p, lower_bound.reshape(1, d), tri, masks)


def kernel(x, c, ctx, c_ctx, w_mod, b_mod, norm_gains, ffn_w_in, ffn_w_out, ab_w_in, qk_norm, ab_w_out,
           hgrn_w_in, hgrn_lb_logits, hgrn_o_norm, hgrn_w_out, final_norm):
    batch, seq, d = x.shape
    lc = ctx.shape[1]
    depth = w_mod.shape[0]
    rows = _Rows(batch, seq, lc, ROW_TILE)
    rows_r = _Rows(batch, seq, lc, READOUT_ROW_TILE)
    assert seq % GRID_W == 0 and batch + 1 <= SUBLANES

    c_rows = jnp.concatenate([c_ctx[None, :], c, jnp.zeros((SUBLANES - 1 - batch, d), F32)], axis=0)
    mods_all = _modulation(c_rows, w_mod, b_mod).reshape(depth, SUBLANES, 3, 3, d)

    lb_cum = jnp.cumsum(jax.nn.softmax(hgrn_lb_logits.astype(F32), axis=0), axis=0)
    lower_bounds = lb_cum - lb_cum[0]

    ffn_w_in_b = ffn_w_in.astype(BF16)
    ffn_w_out_b = ffn_w_out.astype(BF16)
    fin = final_norm.reshape(1, d)
    rope = _rope_tables(seq, ROW_TILE)

    h = jnp.concatenate([x.reshape(batch * seq, d), ctx.reshape(batch * lc, d)], axis=0)
    for layer in range(depth):
        last = layer == depth - 1
        mods = mods_all[layer]
        gains = norm_gains[layer].reshape(3, 1, d)
        h = _ffn(h, mods, 0, gains[0], ffn_w_in_b[layer, 0], ffn_w_out_b[layer, 0], fin, rows, rows.tiles, False)
        if layer % 2 == 0:
            e = layer // 2
            f_all, q_all, k_all, v_all = _ab_proj(h, mods, gains[1], ab_w_in[e].astype(BF16), qk_norm[e], rope, rows)
            attn = jnp.concatenate([_attention(q_all, k_all, v_all, rows, True),
                                    _attention(q_all, k_all, v_all, rows, False)], axis=0)
            four = jnp.concatenate([_fourier_latent(f_all, rows), _fourier_ctx(f_all, rows)], axis=0)
            h = _ab_out(four, attn, ab_w_out[e].astype(BF16), h, mods, rows)
        else:
            o = layer // 2
            p = _hgrn_proj(h, mods, gains[1], hgrn_w_in[o].astype(BF16), rows)
            o_fw = _hgrn_scan(p, lower_bounds[layer], rows, False)
            o_bw = _hgrn_scan(p, lower_bounds[layer], rows, True)
            gain_o = jnp.tile(hgrn_o_norm[o], d // HEAD_DIM).reshape(1, d)
            n_t = rows_r.lat_tiles if last else rows_r.tiles
            h = _hgrn_out(o_fw, o_bw, p, gain_o, hgrn_w_out[o].astype(BF16), h, mods, rows_r, n_t)
        n_t = rows.lat_tiles if last else rows.tiles
        h = _ffn(h, mods, 2, gains[2], ffn_w_in_b[layer, 1], ffn_w_out_b[layer, 1], fin, rows, n_t, last)
    return h[:batch * seq].reshape(batch, seq, d)
```

```python
import functools
import math

import jax
import jax.numpy as jnp
import numpy as np
from jax import lax
from jax.experimental import pallas as pl
from jax.experimental.pallas import tpu as pltpu

F32 = jnp.float32
BF16 = jnp.bfloat16

EPS = 1e-6
N_MOD = 9
HEAD_DIM = 128
N_KV_HEADS = 4
FOURIER_WIDTH = 512
FOURIER_GROUP_DIM = 128
GRID_W = 64
ROPE_THETA = 10000.0
ROPE_AXIS_DIM = HEAD_DIM // 2
ATTN_SCALE = HEAD_DIM ** -0.5
LOG2_E = math.log2(math.e)

LANES = 128
SUBLANES = 8
VMEM_LIMIT_BYTES = 56 * 1024 * 1024

ROW_TILE = 512
READOUT_ROW_TILE = 256
FFN_TILE = 512
PROJ_COL_TILE = 2048
MOD_COL_TILE = 1024
ATTN_Q_TILE = 256
SCAN_CHUNK = 128
SCAN_HEADS_PER_BLOCK = 4
SCAN_HEAD_GROUP = 4
FFT_B = 128
FFT_B_BLOCK = 8
FFT_COL_TILE = 2048


def _cparams(*sem):
    return pltpu.CompilerParams(dimension_semantics=sem, vmem_limit_bytes=VMEM_LIMIT_BYTES)


def _dot(a, b):
    return jnp.dot(a, b, preferred_element_type=F32)


def _dot_hi(a, b):
    return jnp.dot(a, b, preferred_element_type=F32, precision=lax.Precision.HIGHEST)


def _dot_nt(a, b):
    return lax.dot_general(a, b, (((1,), (1,)), ((), ())), preferred_element_type=F32)


def _dot_tn(a, b):
    return lax.dot_general(a, b, (((0,), (0,)), ((), ())), preferred_element_type=F32)


def _silu(x):
    return x * jax.nn.sigmoid(x)


def _rms(x, gain):
    return x * lax.rsqrt(jnp.mean(x * x, axis=-1, keepdims=True) + EPS) * gain


def _ada(h, gain, shift, scale):
    return _rms(h, gain) * (1.0 + scale) + shift


def _mod_kernel(c_ref, w_ref, b_ref, o_ref):
    a = _silu(c_ref[...]).astype(BF16)
    o_ref[...] = _dot(a, w_ref[...].astype(BF16)) + b_ref[...]


def _modulation(c_rows, w_mod, b_mod):
    depth, d, nd = w_mod.shape
    tn = MOD_COL_TILE
    return pl.pallas_call(
        _mod_kernel,
        out_shape=jax.ShapeDtypeStruct((depth, SUBLANES, nd), F32),
        grid=(depth, nd // tn),
        in_specs=[
            pl.BlockSpec((SUBLANES, d), lambda l, j: (0, 0)),
            pl.BlockSpec((None, d, tn), lambda l, j: (l, 0, j)),
            pl.BlockSpec((None, 1, tn), lambda l, j: (l, 0, j)),
        ],
        out_specs=pl.BlockSpec((None, SUBLANES, tn), lambda l, j: (l, 0, j)),
        compiler_params=_cparams("parallel", "arbitrary"),
        name="modulation",
    )(c_rows, w_mod, b_mod.reshape(depth, 1, nd))


class _Rows:
    def __init__(self, batch, seq, ctx_len, tile):
        assert seq % tile == 0 and (batch * ctx_len) % tile == 0
        self.batch, self.seq, self.ctx_len, self.tile = batch, seq, ctx_len, tile
        self.n_lat = batch * seq
        self.n = self.n_lat + batch * ctx_len
        self.lat_tiles = self.n_lat // tile
        self.tiles = self.n // tile
        self.tiles_per_batch = seq // tile

    def group(self, i):
        return jnp.where(i < self.lat_tiles, 1 + i // self.tiles_per_batch, 0)


def _ffn_kernel(h_ref, mod_ref, gain_ref, wa_ref, wb_ref, wo_ref, fin_ref, o_ref, xn_ref, acc_ref, *, final):
    j = pl.program_id(1)

    @pl.when(j == 0)
    def _():
        xn = _ada(h_ref[...], gain_ref[...], mod_ref[0:1, :], mod_ref[1:2, :])
        xn_ref[...] = xn.astype(BF16)
        acc_ref[...] = jnp.zeros_like(acc_ref)

    xn = xn_ref[...]
    a = _dot(xn, wa_ref[...])
    b = _dot(xn, wb_ref[...])
    g = (_silu(a) * b).astype(BF16)
    acc_ref[...] += _dot(g, wo_ref[...])

    @pl.when(j == pl.num_programs(1) - 1)
    def _():
        out = h_ref[...] + 0.5 * mod_ref[2:3, :] * acc_ref[...]
        if final:
            out = _rms(out, fin_ref[...])
        o_ref[...] = out


def _ffn(h, mods, sub, gain, w_in, w_out, fin, rows, n_tiles, final):
    d = h.shape[1]
    f = w_out.shape[0]
    tm, tf = rows.tile, FFN_TILE
    nf = f // tf
    return pl.pallas_call(
        functools.partial(_ffn_kernel, final=final),
        out_shape=jax.ShapeDtypeStruct((n_tiles * tm, d), F32),
        grid=(n_tiles, nf),
        in_specs=[
            pl.BlockSpec((tm, d), lambda i, j: (i, 0)),
            pl.BlockSpec((None, None, 3, d), lambda i, j: (rows.group(i), sub, 0, 0)),
            pl.BlockSpec((1, d), lambda i, j: (0, 0)),
            pl.BlockSpec((d, tf), lambda i, j: (0, j)),
            pl.BlockSpec((d, tf), lambda i, j: (0, nf + j)),
            pl.BlockSpec((tf, d), lambda i, j: (j, 0)),
            pl.BlockSpec((1, d), lambda i, j: (0, 0)),
        ],
        out_specs=pl.BlockSpec((tm, d), lambda i, j: (i, 0)),
        scratch_shapes=[pltpu.VMEM((tm, d), BF16), pltpu.VMEM((tm, d), F32)],
        compiler_params=_cparams("parallel", "arbitrary"),
        name="ffn",
    )(h, mods, gain, w_in, w_in, w_out, fin)


def _rope_tables(seq, tile):
    t = np.arange(seq)
    inv_freq = ROPE_THETA ** (-np.arange(0, ROPE_AXIS_DIM, 2, dtype=np.float64) / ROPE_AXIS_DIM)
    ang = np.concatenate([(t // GRID_W)[:, None] * inv_freq, (t % GRID_W)[:, None] * inv_freq], axis=-1)
    nf = ROPE_AXIS_DIM // 2
    cos = np.cos(ang).reshape(seq, 2, 1, nf)
    sin = np.sin(ang).reshape(seq, 2, 1, nf)
    zero = np.zeros_like(sin)
    c_full = np.broadcast_to(cos, (seq, 2, 2, nf)).reshape(seq, HEAD_DIM)
    s_up = np.concatenate([-sin, zero], axis=2).reshape(seq, HEAD_DIM)
    s_dn = np.concatenate([zero, sin], axis=2).reshape(seq, HEAD_DIM)
    lat = np.concatenate([c_full, s_up, s_dn], axis=1)
    ident = np.concatenate([np.ones((tile, HEAD_DIM)), np.zeros((tile, 2 * HEAD_DIM))], axis=1)
    return jnp.asarray(np.concatenate([lat, ident], axis=0), dtype=F32)


def _norm_rope_heads(acc, gain, rope):
    nf = ROPE_AXIS_DIM // 2
    c, s_up, s_dn = rope[:, :HEAD_DIM], rope[:, HEAD_DIM:2 * HEAD_DIM], rope[:, 2 * HEAD_DIM:]
    heads = []
    for hh in range(acc.shape[1] // HEAD_DIM):
        y = _rms(acc[:, hh * HEAD_DIM:(hh + 1) * HEAD_DIM], gain)
        y = y * c + pltpu.roll(y, HEAD_DIM - nf, 1) * s_up + pltpu.roll(y, nf, 1) * s_dn
        heads.append(y)
    return jnp.concatenate(heads, axis=1).astype(BF16)


def _ab_proj_kernel(h_ref, mod_ref, gain_ref, w_ref, qkn_ref, rope_ref, f_ref, q_ref, k_ref, vt_ref):
    xn = _ada(h_ref[...], gain_ref[...], mod_ref[0:1, :], mod_ref[1:2, :]).astype(BF16)
    c0 = f_ref.shape[1]
    c1 = c0 + q_ref.shape[1]
    c2 = c1 + k_ref.shape[1]
    f_ref[...] = _dot(xn, w_ref[:, :c0])
    q_ref[...] = _norm_rope_heads(_dot(xn, w_ref[:, c0:c1]), qkn_ref[0:1, :], rope_ref[...])
    k_ref[...] = _norm_rope_heads(_dot(xn, w_ref[:, c1:c2]), qkn_ref[1:2, :], rope_ref[...])
    v = _dot(xn, w_ref[:, c2:])
    for hh in range(N_KV_HEADS):
        vt_ref[hh] = v[:, hh * HEAD_DIM:(hh + 1) * HEAD_DIM].T.astype(BF16)


def _ab_proj(h, mods, gain, w_in, qk_norm, rope, rows):
    d = h.shape[1]
    tm = rows.tile
    kv_width = N_KV_HEADS * HEAD_DIM
    q_width = w_in.shape[1] - FOURIER_WIDTH - 2 * kv_width
    n = rows.n
    rope_blk = lambda i: (jnp.where(i < rows.lat_tiles, i % rows.tiles_per_batch, rows.tiles_per_batch), 0)
    return pl.pallas_call(
        _ab_proj_kernel,
        out_shape=(
            jax.ShapeDtypeStruct((n, FOURIER_WIDTH), F32),
            jax.ShapeDtypeStruct((n, q_width), BF16),
            jax.ShapeDtypeStruct((n, kv_width), BF16),
            jax.ShapeDtypeStruct((N_KV_HEADS, rows.tiles, HEAD_DIM, tm), BF16),
        ),
        grid=(rows.tiles,),
        in_specs=[
            pl.BlockSpec((tm, d), lambda i: (i, 0)),
            pl.BlockSpec((None, None, 3, d), lambda i: (rows.group(i), 1, 0, 0)),
            pl.BlockSpec((1, d), lambda i: (0, 0)),
            pl.BlockSpec(w_in.shape, lambda i: (0, 0)),
            pl.BlockSpec((2, HEAD_DIM), lambda i: (0, 0)),
            pl.BlockSpec((tm, 3 * HEAD_DIM), rope_blk),
        ],
        out_specs=(
            pl.BlockSpec((tm, FOURIER_WIDTH), lambda i: (i, 0)),
            pl.BlockSpec((tm, q_width), lambda i: (i, 0)),
            pl.BlockSpec((tm, kv_width), lambda i: (i, 0)),
            pl.BlockSpec((N_KV_HEADS, None, HEAD_DIM, tm), lambda i: (0, i, 0, 0)),
        ),
        compiler_params=_cparams("parallel"),
        name="ab_proj",
    )(h, mods, gain, w_in, qk_norm, rope)


def _attn_kernel(*refs, lat_chunks, group):
    if lat_chunks:
        q_ref, kc_ref, vtc_ref, kl_ref, vtl_ref, o_ref, acc_ref = refs
    else:
        q_ref, kc_ref, vtc_ref, o_ref, acc_ref = refs
    tq = q_ref.shape[0]
    q = q_ref[...]
    qs = jnp.concatenate([q[:, g * HEAD_DIM:(g + 1) * HEAD_DIM] for g in range(group)], axis=0)
    nq = group * tq
    acc_ref[...] = jnp.zeros_like(acc_ref)

    def update(k, vt, m, l):
        s = _dot_nt(k, qs) * (ATTN_SCALE * LOG2_E)
        m_new = jnp.maximum(m, jnp.max(s, axis=0, keepdims=True))
        p = jnp.exp2(s - m_new)
        alpha = jnp.exp2(m - m_new)
        l_new = alpha * l + jnp.sum(p, axis=0, keepdims=True)
        acc_ref[...] = alpha * acc_ref[...] + _dot(vt, p.astype(BF16))
        return m_new, l_new

    m = jnp.full((1, nq), -jnp.inf, F32)
    l = jnp.zeros((1, nq), F32)
    m, l = update(kc_ref[...], vtc_ref[...], m, l)
    if lat_chunks:
        tk = vtl_ref.shape[2]

        def body(c, ml):
            start = pl.multiple_of(c * tk, tk)
            return update(kl_ref[pl.ds(start, tk), :], vtl_ref[c], *ml)

        m, l = lax.fori_loop(0, lat_chunks, body, (m, l), unroll=2)

    out = (acc_ref[...] / l).T
    o_ref[...] = jnp.concatenate([out[g * tq:(g + 1) * tq, :] for g in range(group)], axis=1).astype(BF16)


def _attention(q_all, k_all, vt_all, rows, latent):
    b, seq, lc, tile = rows.batch, rows.seq, rows.ctx_len, rows.tile
    group = q_all.shape[1] // (N_KV_HEADS * HEAD_DIM)
    gw = group * HEAD_DIM
    assert tile % lc == 0 and seq % tile == 0
    ctx_blk0 = rows.n_lat // lc
    kc_spec = pl.BlockSpec((lc, HEAD_DIM), lambda bi, hi, i: (ctx_blk0 + bi, hi))
    vtc_spec = pl.BlockSpec(
        (None, None, HEAD_DIM, lc),
        lambda bi, hi, i: (hi, (rows.n_lat + bi * lc) // tile, 0, ((rows.n_lat + bi * lc) % tile) // lc))
    if latent:
        tq = ATTN_Q_TILE
        nqt = seq // tq
        lat_chunks = seq // tile
        q_spec = pl.BlockSpec((tq, gw), lambda bi, hi, i: (bi * nqt + i, hi))
        kl_spec = pl.BlockSpec((seq, HEAD_DIM), lambda bi, hi, i: (bi, hi))
        vtl_spec = pl.BlockSpec((None, lat_chunks, HEAD_DIM, tile), lambda bi, hi, i: (hi, bi, 0, 0))
        in_specs = [q_spec, kc_spec, vtc_spec, kl_spec, vtl_spec]
        args = (q_all, k_all, vt_all, k_all, vt_all)
        n_out = rows.n_lat
    else:
        tq = lc
        nqt = 1
        lat_chunks = 0
        q_spec = pl.BlockSpec((tq, gw), lambda bi, hi, i: (ctx_blk0 + bi, hi))
        in_specs = [q_spec, kc_spec, vtc_spec]
        args = (q_all, k_all, vt_all)
        n_out = b * lc
    out_spec = pl.BlockSpec((tq, gw), lambda bi, hi, i: (bi * nqt + i, hi))
    return pl.pallas_call(
        functools.partial(_attn_kernel, lat_chunks=lat_chunks, group=group),
        out_shape=jax.ShapeDtypeStruct((n_out, q_all.shape[1]), BF16),
        grid=(b, N_KV_HEADS, nqt),
        in_specs=in_specs,
        out_specs=out_spec,
        scratch_shapes=[pltpu.VMEM((HEAD_DIM, group * tq), F32)],
        compiler_params=_cparams("parallel", "parallel", "arbitrary"),
        name="attention_lat" if latent else "attention_ctx",
    )(*args)


def _dft_cs(n):
    idx = np.arange(n)
    ang = 2.0 * np.pi * ((idx[:, None] * idx[None, :]) % n) / n
    return np.cos(ang), np.sin(ang)


def _fft1_kernel(x_ref, fa_ref, tw_ref, o_ref, *, a):
    w = o_ref.shape[2]
    for r in range(o_ref.shape[0]):
        z = _dot_hi(fa_ref[...], x_ref[:, r * w:(r + 1) * w])
        zr, zi = z[:a], z[a:]
        tc = jnp.concatenate([tw_ref[r, 0]] * (w // LANES), axis=1)
        ts = jnp.concatenate([tw_ref[r, 1]] * (w // LANES), axis=1)
        o_ref[r, :a, :] = zr * tc - zi * ts
        o_ref[r, a:, :] = zr * ts + zi * tc


def _fft2_kernel(zr_ref, zi_ref, m2_ref, mc_ref, o_ref, *, scale):
    z = jnp.concatenate([zr_ref[...], zi_ref[...]], axis=0)
    v = _dot_hi(m2_ref[...], z)
    vr, vi = v[:FFT_B], v[FFT_B:]
    outs = []
    for g in range(o_ref.shape[1] // LANES):
        u = jnp.concatenate([vr[:, g * LANES:(g + 1) * LANES], vi[:, g * LANES:(g + 1) * LANES]], axis=1)
        outs.append(_dot_hi(u, mc_ref[...]))
    o_ref[...] = jnp.concatenate(outs, axis=1) * scale


def _fourier_latent(f_all, rows):
    b, seq = rows.batch, rows.seq
    w = FOURIER_WIDTH
    a = seq // FFT_B
    assert a % SUBLANES == 0 and FOURIER_GROUP_DIM == LANES
    ca, sa = _dft_cs(a)
    fa = jnp.asarray(np.concatenate([ca, sa], axis=0), dtype=F32)
    p1b = (np.arange(a)[None, :] * np.arange(FFT_B)[:, None]) % seq
    ang = 2.0 * np.pi * p1b / seq
    tw = np.stack([np.cos(ang), np.sin(ang)], axis=1)[..., None]
    tw = jnp.asarray(np.broadcast_to(tw, (FFT_B, 2, a, LANES)), dtype=F32)
    cb, sb = _dft_cs(FFT_B)
    m2 = jnp.asarray(np.block([[cb, -sb], [sb, cb]]), dtype=F32)
    cc, sc = _dft_cs(FOURIER_GROUP_DIM)
    mc = jnp.asarray(np.concatenate([cc, -sc], axis=0), dtype=F32)

    blk = FFT_B_BLOCK
    x2 = f_all.reshape(rows.n // FFT_B, FFT_B * w)
    z = pl.pallas_call(
        functools.partial(_fft1_kernel, a=a),
        out_shape=jax.ShapeDtypeStruct((b, FFT_B, 2 * a, w), F32),
        grid=(b, FFT_B // blk),
        in_specs=[
            pl.BlockSpec((a, blk * w), lambda bi, j: (bi, j)),
            pl.BlockSpec((2 * a, a), lambda bi, j: (0, 0)),
            pl.BlockSpec((blk, 2, a, LANES), lambda bi, j: (j, 0, 0, 0)),
        ],
        out_specs=pl.BlockSpec((None, blk, 2 * a, w), lambda bi, j: (bi, j, 0, 0)),
        compiler_params=_cparams("parallel", "parallel"),
        name="fourier_stage1",
    )(x2, fa, tw)

    tc = min(FFT_COL_TILE, a * w)
    ncol = (a * w) // tc
    z2 = z.reshape(b, FFT_B, 2 * a * w)
    y = pl.pallas_call(
        functools.partial(_fft2_kernel, scale=1.0 / math.sqrt(seq * FOURIER_GROUP_DIM)),
        out_shape=jax.ShapeDtypeStruct((b, FFT_B, a * w), F32),
        grid=(b, ncol),
        in_specs=[
            pl.BlockSpec((None, FFT_B, tc), lambda bi, j: (bi, 0, j)),
            pl.BlockSpec((None, FFT_B, tc), lambda bi, j: (bi, 0, ncol + j)),
            pl.BlockSpec((2 * FFT_B, 2 * FFT_B), lambda bi, j: (0, 0)),
            pl.BlockSpec((2 * LANES, LANES), lambda bi, j: (0, 0)),
        ],
        out_specs=pl.BlockSpec((None, FFT_B, tc), lambda bi, j: (bi, 0, j)),
        compiler_params=_cparams("parallel", "parallel"),
        name="fourier_stage2",
    )(z2, z2, m2, mc)
    return y.reshape(b * seq, w)


def _dft_ctx_kernel(x_ref, cn_ref, sn_ref, cc_ref, sc_ref, o_ref, *, scale):
    x = x_ref[...]
    outs = []
    for g in range(x.shape[1] // LANES):
        xg = x[:, g * LANES:(g + 1) * LANES]
        outs.append(_dot_hi(cn_ref[...], _dot_hi(xg, cc_ref[...])) - _dot_hi(sn_ref[...], _dot_hi(xg, sc_ref[...])))
    o_ref[...] = jnp.concatenate(outs, axis=1) * scale


def _fourier_ctx(f_all, rows):
    b, lc = rows.batch, rows.ctx_len
    w = FOURIER_WIDTH
    cn, sn = _dft_cs(lc)
    cc, sc = _dft_cs(FOURIER_GROUP_DIM)
    blk0 = rows.n_lat // lc
    mat = lambda m: pl.BlockSpec(m.shape, lambda bi: (0, 0))
    consts = [jnp.asarray(m, dtype=F32) for m in (cn, sn, cc, sc)]
    return pl.pallas_call(
        functools.partial(_dft_ctx_kernel, scale=1.0 / math.sqrt(lc * FOURIER_GROUP_DIM)),
        out_shape=jax.ShapeDtypeStruct((b * lc, w), F32),
        grid=(b,),
        in_specs=[pl.BlockSpec((lc, w), lambda bi: (blk0 + bi, 0))] + [mat(m) for m in consts],
        out_specs=pl.BlockSpec((lc, w), lambda bi: (bi, 0)),
        compiler_params=_cparams("parallel"),
        name="fourier_ctx",
    )(f_all, *consts)


def _ab_out_kernel(x1_ref, x2_ref, w_ref, h_ref, mod_ref, o_ref):
    x = jnp.concatenate([x1_ref[...].astype(BF16), x2_ref[...]], axis=1)
    o_ref[...] = h_ref[...] + mod_ref[2:3, :] * _dot(x, w_ref[...])


def _ab_out(x1, x2, w_out, h, mods, rows):
    d = h.shape[1]
    tm = rows.tile
    return pl.pallas_call(
        _ab_out_kernel,
        out_shape=jax.ShapeDtypeStruct((rows.n, d), F32),
        grid=(rows.tiles,),
        in_specs=[
            pl.BlockSpec((tm, x1.shape[1]), lambda i: (i, 0)),
            pl.BlockSpec((tm, x2.shape[1]), lambda i: (i, 0)),
            pl.BlockSpec(w_out.shape, lambda i: (0, 0)),
            pl.BlockSpec((tm, d), lambda i: (i, 0)),
            pl.BlockSpec((None, None, 3, d), lambda i: (rows.group(i), 1, 0, 0)),
        ],
        out_specs=pl.BlockSpec((tm, d), lambda i: (i, 0)),
        compiler_params=_cparams("parallel"),
        name="ab_out",
    )(x1, x2, w_out, h, mods)


def _hgrn_out_kernel(ofw_ref, obw_ref, g_ref, gain_ref, w_ref, h_ref, mod_ref, o_ref):
    o = ofw_ref[...] + obw_ref[...]
    parts = []
    for hh in range(o.shape[1] // HEAD_DIM):
        oh = o[:, hh * HEAD_DIM:(hh + 1) * HEAD_DIM]
        parts.append(oh * lax.rsqrt(jnp.mean(oh * oh, axis=-1, keepdims=True) + EPS))
    on = jnp.concatenate(parts, axis=1) * gain_ref[...]
    y = (on * jax.nn.sigmoid(g_ref[...])).astype(BF16)
    o_ref[...] = h_ref[...] + mod_ref[2:3, :] * _dot(y, w_ref[...])


def _hgrn_out(o_fw, o_bw, p, gain, w_out, h, mods, rows_r, n_tiles):
    d = h.shape[1]
    tm = rows_r.tile
    g_blk = (p.shape[1] - d) // d
    return pl.pallas_call(
        _hgrn_out_kernel,
        out_shape=jax.ShapeDtypeStruct((n_tiles * tm, d), F32),
        grid=(n_tiles,),
        in_specs=[
            pl.BlockSpec((tm, d), lambda i: (i, 0)),
            pl.BlockSpec((tm, d), lambda i: (i, 0)),
            pl.BlockSpec((tm, d), lambda i: (i, g_blk)),
            pl.BlockSpec((1, d), lambda i: (0, 0)),
            pl.BlockSpec(w_out.shape, lambda i: (0, 0)),
            pl.BlockSpec((tm, d), lambda i: (i, 0)),
            pl.BlockSpec((None, None, 3, d), lambda i: (rows_r.group(i), 1, 0, 0)),
        ],
        out_specs=pl.BlockSpec((tm, d), lambda i: (i, 0)),
        compiler_params=_cparams("parallel"),
        name="hgrn_out",
    )(o_fw, o_bw, p, gain, w_out, h, mods)


def _hgrn_proj_kernel(h_ref, mod_ref, gain_ref, w_ref, o_ref, xn_ref, *, n_silu):
    j = pl.program_id(1)

    @pl.when(j == 0)
    def _():
        xn_ref[...] = _ada(h_ref[...], gain_ref[...], mod_ref[0:1, :], mod_ref[1:2, :]).astype(BF16)

    acc = _dot(xn_ref[...], w_ref[...])
    o_ref[...] = jnp.where(j < n_silu, _silu(acc), acc)


def _hgrn_proj(h, mods, gain, w_in, rows):
    d = h.shape[1]
    tm, tn = rows.tile, PROJ_COL_TILE
    nw = w_in.shape[1]
    return pl.pallas_call(
        functools.partial(_hgrn_proj_kernel, n_silu=d // tn),
        out_shape=jax.ShapeDtypeStruct((rows.n, nw), F32),
        grid=(rows.tiles, nw // tn),
        in_specs=[
            pl.BlockSpec((tm, d), lambda i, j: (i, 0)),
            pl.BlockSpec((None, None, 3, d), lambda i, j: (rows.group(i), 1, 0, 0)),
            pl.BlockSpec((1, d), lambda i, j: (0, 0)),
            pl.BlockSpec((d, tn), lambda i, j: (0, j)),
        ],
        out_specs=pl.BlockSpec((tm, tn), lambda i, j: (i, j)),
        scratch_shapes=[pltpu.VMEM((tm, d), BF16)],
        compiler_params=_cparams("parallel", "arbitrary"),
        name="hgrn_proj",
    )(h, mods, gain, w_in)


def _scan_levels(chunk):
    return [chunk >> (i + 1) for i in range(int(math.log2(chunk)))]


def _scan_consts(chunk, reverse):
    t = np.arange(chunk)[:, None]
    s = np.arange(chunk)[None, :]
    tri = (s >= t) if reverse else (s <= t)
    masks = []
    for h in _scan_levels(chunk):
        same = (t // (2 * h)) == (s // (2 * h))
        t_up = (t // h) % 2 == 1
        s_up = (s // h) % 2 == 1
        pair = (~t_up & s_up) if reverse else (t_up & ~s_up)
        masks.append(same & pair)
    masks.append(t == s)
    return jnp.asarray(tri, dtype=BF16), jnp.asarray(np.stack(masks), dtype=F32)


def _seg_bcast(x, h, reverse):
    c, w = x.shape
    off = h if reverse else h - 1
    if 2 * h >= 2 * SUBLANES:
        pieces = [jnp.broadcast_to(x[g * 2 * h + off:g * 2 * h + off + 1, :], (2 * h, w)) for g in range(c // (2 * h))]
        return pieces[0] if len(pieces) == 1 else jnp.concatenate(pieces, axis=0)
    x3 = x.reshape(c // SUBLANES, SUBLANES, w)
    sub = lax.broadcasted_iota(jnp.int32, x3.shape, 1)
    y = None
    for g in range(SUBLANES // (2 * h)):
        piece = jnp.broadcast_to(x3[:, g * 2 * h + off:g * 2 * h + off + 1, :], x3.shape)
        y = piece if y is None else jnp.where(sub >= g * 2 * h, piece, y)
    return y.reshape(c, w)


def _scan_group(q, fl, lb, v, tri, mask_ref, o_ref, st_ref, h0, reverse):
    chunk, width = q.shape
    levels = _scan_levels(chunk)
    key = (1.0 - lb) * jax.nn.sigmoid(-fl)
    log_sig = jnp.minimum(fl, 0.0) - jnp.log(1.0 + jnp.exp(-jnp.abs(fl)))
    x1 = jnp.log(lb)
    x2 = jnp.log1p(-lb) + log_sig
    delta = x1 - x2
    lf = jnp.where(jnp.isnan(delta), x1 + x2, jnp.maximum(x1, x2) + jnp.log(1.0 + jnp.exp(-jnp.abs(delta))))
    l1 = lf.astype(BF16)
    r1 = lf - l1.astype(F32)
    l2 = r1.astype(BF16)
    l3 = (r1 - l2.astype(F32)).astype(BF16)
    cs = _dot(tri, jnp.concatenate([l1, l2, l3], axis=1))
    b = cs[:, :width] + cs[:, width:2 * width] + cs[:, 2 * width:]
    total = b[0:1, :] if reverse else b[chunk - 1:chunk, :]

    q_in = (q * jnp.exp(b)).astype(BF16)
    k_out = (key * jnp.exp(total - b)).astype(BF16)
    decay = jnp.exp(total)
    n_heads = width // HEAD_DIM
    lanes = [slice(hh * HEAD_DIM, (hh + 1) * HEAD_DIM) for hh in range(n_heads)]

    def pair_scores(li, ql, kl):
        return [mask_ref[li] * _dot_nt(ql[:, sl], kl[:, sl]) for sl in lanes]

    a = pair_scores(len(levels), q.astype(BF16), key.astype(BF16))
    for li, h in enumerate(levels):
        e = jnp.exp(-jnp.abs(b - _seg_bcast(b, h, reverse)))
        a = [x + y for x, y in zip(a, pair_scores(li, (q * e).astype(BF16), (key * e).astype(BF16)))]

    for hh, sl in enumerate(lanes):
        osl = slice((h0 + hh) * HEAD_DIM, (h0 + hh + 1) * HEAD_DIM)
        vh = v[:, sl].astype(BF16)
        st = st_ref[h0 + hh]
        o_ref[:, osl] = _dot_nt(q_in[:, sl], st.astype(BF16)) + _dot(a[hh].astype(BF16), vh)
        st_ref[h0 + hh] = st * decay[:, sl] + _dot_tn(vh, k_out[:, sl])


def _scan_kernel(q_ref, f_ref, v_ref, lb_ref, tri_ref, mask_ref, o_ref, st_ref, *, reverse, heads, group):
    @pl.when(pl.program_id(2) == 0)
    def _():
        st_ref[...] = jnp.zeros_like(st_ref)

    tri = tri_ref[...]
    for h0 in range(0, heads, group):
        sl = slice(h0 * HEAD_DIM, (h0 + group) * HEAD_DIM)
        _scan_group(q_ref[:, sl], f_ref[:, sl], lb_ref[:, sl], v_ref[:, sl], tri, mask_ref, o_ref, st_ref, h0, reverse)


def _hgrn_scan(p, lower_bound, rows, reverse):
    b, seq, lc = rows.batch, rows.seq, rows.ctx_len
    d = lower_bound.shape[0]
    c, hb = SCAN_CHUNK, SCAN_HEADS_PER_BLOCK
    wb = hb * HEAD_DIM
    ncb = d // wb
    nctx, nlat = lc // c, seq // c
    ctx0 = rows.n_lat // c
    f_blk = (2 if reverse else 1) * ncb
    v_blk = 3 * ncb

    def row(bi, s):
        if reverse:
            return jnp.where(s < nctx, ctx0 + bi * nctx + (nctx - 1 - s), bi * nlat + (nlat - 1 - (s - nctx)))
        return jnp.where(s < nctx, ctx0 + bi * nctx + s, bi * nlat + (s - nctx))

    tri, masks = _scan_consts(c, reverse)
    return pl.pallas_call(
        functools.partial(_scan_kernel, reverse=reverse, heads=hb, group=SCAN_HEAD_GROUP),
        out_shape=jax.ShapeDtypeStruct((rows.n, d), F32),
        grid=(b, ncb, nctx + nlat),
        in_specs=[
            pl.BlockSpec((c, wb), lambda bi, hi, s: (row(bi, s), hi)),
            pl.BlockSpec((c, wb), lambda bi, hi, s: (row(bi, s), f_blk + hi)),
            pl.BlockSpec((c, wb), lambda bi, hi, s: (row(bi, s), v_blk + hi)),
            pl.BlockSpec((1, wb), lambda bi, hi, s: (0, hi)),
            pl.BlockSpec(tri.shape, lambda bi, hi, s: (0, 0)),
            pl.BlockSpec(masks.shape, lambda bi, hi, s: (0, 0, 0)),
        ],
        out_specs=pl.BlockSpec((c, wb), lambda bi, hi, s: (row(bi, s), hi)),
        scratch_shapes=[pltpu.VMEM((hb, HEAD_DIM, HEAD_DIM), F32)],
        compiler_params=_cparams("parallel", "parallel", "arbitrary"),
        name="hgrn_scan_bw" if reverse else "hgrn_scan_fw",
    )(p, p, p, lower_bound.reshape(1, d), tri, masks)


def kernel(x, c, ctx, c_ctx, w_mod, b_mod, norm_gains, ffn_w_in, ffn_w_out, ab_w_in, qk_norm, ab_w_out,
           hgrn_w_in, hgrn_lb_logits, hgrn_o_norm, hgrn_w_out, final_norm):
    batch, seq, d = x.shape
    lc = ctx.shape[1]
    depth = w_mod.shape[0]
    rows = _Rows(batch, seq, lc, ROW_TILE)
    rows_r = _Rows(batch, seq, lc, READOUT_ROW_TILE)
    assert seq % GRID_W == 0 and batch + 1 <= SUBLANES

    c_rows = jnp.concatenate([c_ctx[None, :], c, jnp.zeros((SUBLANES - 1 - batch, d), F32)], axis=0)
    mods_all = _modulation(c_rows, w_mod, b_mod).reshape(depth, SUBLANES, 3, 3, d)

    lb_cum = jnp.cumsum(jax.nn.softmax(hgrn_lb_logits.astype(F32), axis=0), axis=0)
    lower_bounds = lb_cum - lb_cum[0]

    ffn_w_in_b = ffn_w_in.astype(BF16)
    ffn_w_out_b = ffn_w_out.astype(BF16)
    fin = final_norm.reshape(1, d)
    rope = _rope_tables(seq, ROW_TILE)

    h = jnp.concatenate([x.reshape(batch * seq, d), ctx.reshape(batch * lc, d)], axis=0)
    for layer in range(depth):
        last = layer == depth - 1
        mods = mods_all[layer]
        gains = norm_gains[layer].reshape(3, 1, d)
        h = _ffn(h, mods, 0, gains[0], ffn_w_in_b[layer, 0], ffn_w_out_b[layer, 0], fin, rows, rows.tiles, False)
        if layer % 2 == 0:
            e = layer // 2
            f_all, q_all, k_all, vt_all = _ab_proj(h, mods, gains[1], ab_w_in[e].astype(BF16), qk_norm[e], rope, rows)
            attn = jnp.concatenate([_attention(q_all, k_all, vt_all, rows, True),
                                    _attention(q_all, k_all, vt_all, rows, False)], axis=0)
            four = jnp.concatenate([_fourier_latent(f_all, rows), _fourier_ctx(f_all, rows)], axis=0)
            h = _ab_out(four, attn, ab_w_out[e].astype(BF16), h, mods, rows)
        else:
            o = layer // 2
            p = _hgrn_proj(h, mods, gains[1], hgrn_w_in[o].astype(BF16), rows)
            o_fw = _hgrn_scan(p, lower_bounds[layer], rows, False)
            o_bw = _hgrn_scan(p, lower_bounds[layer], rows, True)
            gain_o = jnp.tile(hgrn_o_norm[o], d // HEAD_DIM).reshape(1, d)
            n_t = rows_r.lat_tiles if last else rows_r.tiles
            h = _hgrn_out(o_fw, o_bw, p, gain_o, hgrn_w_out[o].astype(BF16), h, mods, rows_r, n_t)
        n_t = rows.lat_tiles if last else rows.tiles
        h = _ffn(h, mods, 2, gains[2], ffn_w_in_b[layer, 1], ffn_w_out_b[layer, 1], fin, rows, n_t, last)
    return h[:batch * seq].reshape(batch, seq, d)
```

```python
import functools
import math

import jax
import jax.numpy as jnp
import numpy as np
from jax import lax
from jax.experimental import pallas as pl
from jax.experimental.pallas import tpu as pltpu

F32 = jnp.float32
BF16 = jnp.bfloat16

EPS = 1e-6
N_MOD = 9
HEAD_DIM = 128
N_KV_HEADS = 4
FOURIER_WIDTH = 512
FOURIER_GROUP_DIM = 128
GRID_W = 64
ROPE_THETA = 10000.0
ROPE_AXIS_DIM = HEAD_DIM // 2
ATTN_SCALE = HEAD_DIM ** -0.5
LOG2_E = math.log2(math.e)

LANES = 128
SUBLANES = 8
BF16_SUBLANES = 16
VT_ROWS = HEAD_DIM + BF16_SUBLANES
VMEM_LIMIT_BYTES = 56 * 1024 * 1024

ROW_TILE = 512
READOUT_ROW_TILE = 256
FFN_TILE = 512
PROJ_COL_TILE = 2048
MOD_COL_TILE = 1024
ATTN_Q_TILE = 256
SCAN_CHUNK = 128
SCAN_HEADS_PER_BLOCK = 4
SCAN_HEAD_GROUP = 4
FFT_B = 128
FFT_B_BLOCK = 8
FFT_COL_TILE = 2048


def _cparams(*sem):
    return pltpu.CompilerParams(dimension_semantics=sem, vmem_limit_bytes=VMEM_LIMIT_BYTES)


def _dot(a, b):
    return jnp.dot(a, b, preferred_element_type=F32)


def _dot_hi(a, b):
    return jnp.dot(a, b, preferred_element_type=F32, precision=lax.Precision.HIGHEST)


def _dot_nt(a, b):
    return lax.dot_general(a, b, (((1,), (1,)), ((), ())), preferred_element_type=F32)


def _dot_tn(a, b):
    return lax.dot_general(a, b, (((0,), (0,)), ((), ())), preferred_element_type=F32)


def _silu(x):
    return x * jax.nn.sigmoid(x)


def _rms(x, gain):
    return x * lax.rsqrt(jnp.mean(x * x, axis=-1, keepdims=True) + EPS) * gain


def _ada(h, gain, shift, scale):
    return _rms(h, gain) * (1.0 + scale) + shift


def _mod_kernel(c_ref, w_ref, b_ref, o_ref):
    a = _silu(c_ref[...]).astype(BF16)
    o_ref[...] = _dot(a, w_ref[...].astype(BF16)) + b_ref[...]


def _modulation(c_rows, w_mod, b_mod):
    depth, d, nd = w_mod.shape
    tn = MOD_COL_TILE
    return pl.pallas_call(
        _mod_kernel,
        out_shape=jax.ShapeDtypeStruct((depth, SUBLANES, nd), F32),
        grid=(depth, nd // tn),
        in_specs=[
            pl.BlockSpec((SUBLANES, d), lambda l, j: (0, 0)),
            pl.BlockSpec((None, d, tn), lambda l, j: (l, 0, j)),
            pl.BlockSpec((None, 1, tn), lambda l, j: (l, 0, j)),
        ],
        out_specs=pl.BlockSpec((None, SUBLANES, tn), lambda l, j: (l, 0, j)),
        compiler_params=_cparams("parallel", "arbitrary"),
        name="modulation",
    )(c_rows, w_mod, b_mod.reshape(depth, 1, nd))


class _Rows:
    def __init__(self, batch, seq, ctx_len, tile):
        assert seq % tile == 0 and (batch * ctx_len) % tile == 0
        self.batch, self.seq, self.ctx_len, self.tile = batch, seq, ctx_len, tile
        self.n_lat = batch * seq
        self.n = self.n_lat + batch * ctx_len
        self.lat_tiles = self.n_lat // tile
        self.tiles = self.n // tile
        self.tiles_per_batch = seq // tile

    def group(self, i):
        return jnp.where(i < self.lat_tiles, 1 + i // self.tiles_per_batch, 0)


def _ffn_kernel(*refs, final, split_tiles):
    if split_tiles is None:
        h_ref, mod_ref, gain_ref, wa_ref, wb_ref, wo_ref, fin_ref, o_ref, xn_ref, acc_ref = refs
        load_h = lambda: h_ref[...]
    else:
        hl_ref, hc_ref, mod_ref, gain_ref, wa_ref, wb_ref, wo_ref, fin_ref, o_ref, xn_ref, acc_ref = refs
        load_h = lambda: jnp.where(pl.program_id(0) < split_tiles, hl_ref[...], hc_ref[...])
    j = pl.program_id(1)

    @pl.when(j == 0)
    def _():
        xn = _ada(load_h(), gain_ref[...], mod_ref[0:1, :], mod_ref[1:2, :])
        xn_ref[...] = xn.astype(BF16)
        acc_ref[...] = jnp.zeros_like(acc_ref)

    xn = xn_ref[...]
    a = _dot(xn, wa_ref[...])
    b = _dot(xn, wb_ref[...])
    g = (_silu(a) * b).astype(BF16)
    acc_ref[...] += _dot(g, wo_ref[...])

    @pl.when(j == pl.num_programs(1) - 1)
    def _():
        out = load_h() + 0.5 * mod_ref[2:3, :] * acc_ref[...]
        if final:
            out = _rms(out, fin_ref[...])
        o_ref[...] = out


def _ffn(h, mods, sub, gain, w_in, w_out, layer, which, fin, rows, n_tiles, final):
    f, d = w_out.shape[2:]
    tm, tf = rows.tile, FFN_TILE
    nf = f // tf
    row_spec = pl.BlockSpec((tm, d), lambda i, j: (i, 0))
    if isinstance(h, tuple):
        lt = rows.lat_tiles
        h_args = h
        h_specs = [pl.BlockSpec((tm, d), lambda i, j: (jnp.minimum(i, lt - 1), 0)),
                   pl.BlockSpec((tm, d), lambda i, j: (jnp.maximum(i - lt, 0), 0))]
        split_tiles = lt
    else:
        h_args, h_specs, split_tiles = (h,), [row_spec], None
    return pl.pallas_call(
        functools.partial(_ffn_kernel, final=final, split_tiles=split_tiles),
        out_shape=jax.ShapeDtypeStruct((n_tiles * tm, d), F32),
        grid=(n_tiles, nf),
        in_specs=h_specs + [
            pl.BlockSpec((None, None, 3, d), lambda i, j: (rows.group(i), sub, 0, 0)),
            pl.BlockSpec((1, d), lambda i, j: (0, 0)),
            pl.BlockSpec((None, None, d, tf), lambda i, j: (layer, which, 0, j)),
            pl.BlockSpec((None, None, d, tf), lambda i, j: (layer, which, 0, nf + j)),
            pl.BlockSpec((None, None, tf, d), lambda i, j: (layer, which, j, 0)),
            pl.BlockSpec((1, d), lambda i, j: (0, 0)),
        ],
        out_specs=row_spec,
        scratch_shapes=[pltpu.VMEM((tm, d), BF16), pltpu.VMEM((tm, d), F32)],
        compiler_params=_cparams("parallel", "arbitrary"),
        name="ffn",
    )(*h_args, mods, gain, w_in, w_in, w_out, fin)


def _rope_tables(seq, tile):
    t = np.arange(seq)
    inv_freq = ROPE_THETA ** (-np.arange(0, ROPE_AXIS_DIM, 2, dtype=np.float64) / ROPE_AXIS_DIM)
    ang = np.concatenate([(t // GRID_W)[:, None] * inv_freq, (t % GRID_W)[:, None] * inv_freq], axis=-1)
    nf = ROPE_AXIS_DIM // 2
    cos = np.cos(ang).reshape(seq, 2, 1, nf)
    sin = np.sin(ang).reshape(seq, 2, 1, nf)
    zero = np.zeros_like(sin)
    c_full = np.broadcast_to(cos, (seq, 2, 2, nf)).reshape(seq, HEAD_DIM)
    s_up = np.concatenate([-sin, zero], axis=2).reshape(seq, HEAD_DIM)
    s_dn = np.concatenate([zero, sin], axis=2).reshape(seq, HEAD_DIM)
    lat = np.concatenate([c_full, s_up, s_dn], axis=1)
    ident = np.concatenate([np.ones((tile, HEAD_DIM)), np.zeros((tile, 2 * HEAD_DIM))], axis=1)
    return jnp.asarray(np.concatenate([lat, ident], axis=0), dtype=F32)


def _norm_rope_heads(acc, gain, rope, post_scale=None):
    nf = ROPE_AXIS_DIM // 2
    c, s_up, s_dn = rope[:, :HEAD_DIM], rope[:, HEAD_DIM:2 * HEAD_DIM], rope[:, 2 * HEAD_DIM:]
    heads = []
    for hh in range(acc.shape[1] // HEAD_DIM):
        y = _rms(acc[:, hh * HEAD_DIM:(hh + 1) * HEAD_DIM], gain)
        y = y * c + pltpu.roll(y, HEAD_DIM - nf, 1) * s_up + pltpu.roll(y, nf, 1) * s_dn
        heads.append(y if post_scale is None else y * post_scale)
    return jnp.concatenate(heads, axis=1).astype(BF16)


def _ab_proj_kernel(h_ref, mod_ref, gain_ref, w_ref, qkn_ref, rope_ref, f_ref, q_ref, k_ref, vt_ref):
    xn = _ada(h_ref[...], gain_ref[...], mod_ref[0:1, :], mod_ref[1:2, :]).astype(BF16)
    c0 = f_ref.shape[1]
    c1 = c0 + q_ref.shape[1]
    c2 = c1 + k_ref.shape[1]
    f_ref[...] = _dot(xn, w_ref[:, :c0])
    q_ref[...] = _norm_rope_heads(_dot(xn, w_ref[:, c0:c1]), qkn_ref[0:1, :], rope_ref[...], ATTN_SCALE * LOG2_E)
    k_ref[...] = _norm_rope_heads(_dot(xn, w_ref[:, c1:c2]), qkn_ref[1:2, :], rope_ref[...])
    v = _dot(xn, w_ref[:, c2:])
    ones = jnp.ones((vt_ref.shape[1] - HEAD_DIM, vt_ref.shape[2]), BF16)
    for hh in range(N_KV_HEADS):
        vt_ref[hh, :HEAD_DIM, :] = v[:, hh * HEAD_DIM:(hh + 1) * HEAD_DIM].T.astype(BF16)
        vt_ref[hh, HEAD_DIM:, :] = ones


def _ab_proj(h, mods, gain, w_in, qk_norm, rope, rows):
    d = h.shape[1]
    tm = rows.tile
    kv_width = N_KV_HEADS * HEAD_DIM
    q_width = w_in.shape[1] - FOURIER_WIDTH - 2 * kv_width
    n = rows.n
    rope_blk = lambda i: (jnp.where(i < rows.lat_tiles, i % rows.tiles_per_batch, rows.tiles_per_batch), 0)
    return pl.pallas_call(
        _ab_proj_kernel,
        out_shape=(
            jax.ShapeDtypeStruct((n, FOURIER_WIDTH), F32),
            jax.ShapeDtypeStruct((n, q_width), BF16),
            jax.ShapeDtypeStruct((n, kv_width), BF16),
            jax.ShapeDtypeStruct((N_KV_HEADS, rows.tiles, VT_ROWS, tm), BF16),
        ),
        grid=(rows.tiles,),
        in_specs=[
            pl.BlockSpec((tm, d), lambda i: (i, 0)),
            pl.BlockSpec((None, None, 3, d), lambda i: (rows.group(i), 1, 0, 0)),
            pl.BlockSpec((1, d), lambda i: (0, 0)),
            pl.BlockSpec(w_in.shape, lambda i: (0, 0)),
            pl.BlockSpec((2, HEAD_DIM), lambda i: (0, 0)),
            pl.BlockSpec((tm, 3 * HEAD_DIM), rope_blk),
        ],
        out_specs=(
            pl.BlockSpec((tm, FOURIER_WIDTH), lambda i: (i, 0)),
            pl.BlockSpec((tm, q_width), lambda i: (i, 0)),
            pl.BlockSpec((tm, kv_width), lambda i: (i, 0)),
            pl.BlockSpec((N_KV_HEADS, None, VT_ROWS, tm), lambda i: (0, i, 0, 0)),
        ),
        compiler_params=_cparams("parallel"),
        name="ab_proj",
    )(h, mods, gain, w_in, qk_norm, rope)


def _attn_kernel(*refs, lat_chunks, group):
    if lat_chunks:
        q_ref, kc_ref, vtc_ref, kl_ref, vtl_ref, o_ref, acc_ref, s_ref = refs
    else:
        q_ref, kc_ref, vtc_ref, _, o_ref, acc_ref = refs
    tq = q_ref.shape[0]
    q = q_ref[...]
    qs = jnp.concatenate([q[:, g * HEAD_DIM:(g + 1) * HEAD_DIM] for g in range(group)], axis=0)
    nq = group * tq
    acc_ref[...] = jnp.zeros_like(acc_ref)

    def scores(k):
        return _dot_nt(k, qs)

    def update(s, vt, m):
        m_new = jnp.maximum(m, jnp.max(s, axis=0, keepdims=True))
        p = jnp.exp2(s - m_new).astype(BF16)
        acc_ref[...] = jnp.exp2(m - m_new) * acc_ref[...] + _dot(vt, p)
        return m_new

    m = update(scores(kc_ref[...]), vtc_ref[...], jnp.full((1, nq), -jnp.inf, F32))
    if lat_chunks:
        tk = vtl_ref.shape[2]

        def lat_scores(c):
            return scores(kl_ref[pl.ds(pl.multiple_of(c * tk, tk), tk), :])

        assert lat_chunks % 2 == 0
        s_ref[0] = lat_scores(0)

        def body(i, m):
            c = 2 * i
            s_ref[1] = lat_scores(c + 1)
            m = update(s_ref[0], vtl_ref[c], m)
            s_ref[0] = lat_scores(jnp.minimum(c + 2, lat_chunks - 1))
            return update(s_ref[1], vtl_ref[c + 1], m)

        m = lax.fori_loop(0, lat_chunks // 2, body, m)

    out = (acc_ref[:HEAD_DIM, :] / acc_ref[HEAD_DIM:HEAD_DIM + 1, :]).T
    o_ref[...] = jnp.concatenate([out[g * tq:(g + 1) * tq, :] for g in range(group)], axis=1).astype(BF16)


def _attention(q_all, k_all, vt_all, rows, lat_out=None):
    b, seq, lc, tile = rows.batch, rows.seq, rows.ctx_len, rows.tile
    group = q_all.shape[1] // (N_KV_HEADS * HEAD_DIM)
    gw = group * HEAD_DIM
    assert tile % lc == 0 and seq % tile == 0
    ctx_blk0 = rows.n_lat // lc
    kc_spec = pl.BlockSpec((lc, HEAD_DIM), lambda bi, hi, i: (ctx_blk0 + bi, hi))
    vtc_spec = pl.BlockSpec(
        (None, None, VT_ROWS, lc),
        lambda bi, hi, i: (hi, (rows.n_lat + bi * lc) // tile, 0, ((rows.n_lat + bi * lc) % tile) // lc))
    if lat_out is None:
        tq = ATTN_Q_TILE
        nqt = seq // tq
        lat_chunks = seq // tile
        q_spec = pl.BlockSpec((tq, gw), lambda bi, hi, i: (bi * nqt + i, hi))
        kl_spec = pl.BlockSpec((seq, HEAD_DIM), lambda bi, hi, i: (bi, hi))
        vtl_spec = pl.BlockSpec((None, lat_chunks, VT_ROWS, tile), lambda bi, hi, i: (hi, bi, 0, 0))
        in_specs = [q_spec, kc_spec, vtc_spec, kl_spec, vtl_spec]
        args = (q_all, k_all, vt_all, k_all, vt_all)
        aliases = {}
    else:
        tq = lc
        nqt = 1
        lat_chunks = 0
        q_spec = pl.BlockSpec((tq, gw), lambda bi, hi, i: (ctx_blk0 + bi, hi))
        in_specs = [q_spec, kc_spec, vtc_spec, pl.BlockSpec(memory_space=pl.ANY)]
        args = (q_all, k_all, vt_all, lat_out)
        aliases = {3: 0}
    return pl.pallas_call(
        functools.partial(_attn_kernel, lat_chunks=lat_chunks, group=group),
        out_shape=jax.ShapeDtypeStruct((rows.n, q_all.shape[1]), BF16),
        grid=(b, N_KV_HEADS, nqt),
        in_specs=in_specs,
        out_specs=q_spec,
        scratch_shapes=[pltpu.VMEM((VT_ROWS, group * tq), F32)]
        + ([pltpu.VMEM((2, tile, group * tq), F32)] if lat_chunks else []),
        input_output_aliases=aliases,
        compiler_params=_cparams("parallel", "parallel", "arbitrary"),
        name="attention_ctx" if lat_chunks == 0 else "attention_lat",
    )(*args)


def _dft_cs(n):
    idx = np.arange(n)
    ang = 2.0 * np.pi * ((idx[:, None] * idx[None, :]) % n) / n
    return np.cos(ang), np.sin(ang)


def _fft1_kernel(x_ref, fa_ref, tw_ref, o_ref, *, a):
    w = o_ref.shape[2]
    for r in range(o_ref.shape[0]):
        z = _dot_hi(fa_ref[...], x_ref[:, r * w:(r + 1) * w])
        zr, zi = z[:a], z[a:]
        tc = jnp.concatenate([tw_ref[r, 0]] * (w // LANES), axis=1)
        ts = jnp.concatenate([tw_ref[r, 1]] * (w // LANES), axis=1)
        o_ref[r, :a, :] = zr * tc - zi * ts
        o_ref[r, a:, :] = zr * ts + zi * tc


def _fft2_kernel(zr_ref, zi_ref, m2_ref, mc_ref, o_ref, *, scale):
    z = jnp.concatenate([zr_ref[...], zi_ref[...]], axis=0)
    v = _dot_hi(m2_ref[...], z)
    vr, vi = v[:FFT_B], v[FFT_B:]
    outs = []
    for g in range(o_ref.shape[1] // LANES):
        u = jnp.concatenate([vr[:, g * LANES:(g + 1) * LANES], vi[:, g * LANES:(g + 1) * LANES]], axis=1)
        outs.append(_dot_hi(u, mc_ref[...]))
    o_ref[...] = jnp.concatenate(outs, axis=1) * scale


def _fourier_latent(f_all, rows):
    b, seq = rows.batch, rows.seq
    w = FOURIER_WIDTH
    a = seq // FFT_B
    assert a % SUBLANES == 0 and FOURIER_GROUP_DIM == LANES
    ca, sa = _dft_cs(a)
    fa = jnp.asarray(np.concatenate([ca, sa], axis=0), dtype=F32)
    p1b = (np.arange(a)[None, :] * np.arange(FFT_B)[:, None]) % seq
    ang = 2.0 * np.pi * p1b / seq
    tw = np.stack([np.cos(ang), np.sin(ang)], axis=1)[..., None]
    tw = jnp.asarray(np.broadcast_to(tw, (FFT_B, 2, a, LANES)), dtype=F32)
    cb, sb = _dft_cs(FFT_B)
    m2 = jnp.asarray(np.block([[cb, -sb], [sb, cb]]), dtype=F32)
    cc, sc = _dft_cs(FOURIER_GROUP_DIM)
    mc = jnp.asarray(np.concatenate([cc, -sc], axis=0), dtype=F32)

    blk = FFT_B_BLOCK
    x2 = f_all.reshape(rows.n // FFT_B, FFT_B * w)
    z = pl.pallas_call(
        functools.partial(_fft1_kernel, a=a),
        out_shape=jax.ShapeDtypeStruct((b, FFT_B, 2 * a, w), F32),
        grid=(b, FFT_B // blk),
        in_specs=[
            pl.BlockSpec((a, blk * w), lambda bi, j: (bi, j)),
            pl.BlockSpec((2 * a, a), lambda bi, j: (0, 0)),
            pl.BlockSpec((blk, 2, a, LANES), lambda bi, j: (j, 0, 0, 0)),
        ],
        out_specs=pl.BlockSpec((None, blk, 2 * a, w), lambda bi, j: (bi, j, 0, 0)),
        compiler_params=_cparams("parallel", "parallel"),
        name="fourier_stage1",
    )(x2, fa, tw)

    tc = min(FFT_COL_TILE, a * w)
    ncol = (a * w) // tc
    z2 = z.reshape(b, FFT_B, 2 * a * w)
    y = pl.pallas_call(
        functools.partial(_fft2_kernel, scale=1.0 / math.sqrt(seq * FOURIER_GROUP_DIM)),
        out_shape=jax.ShapeDtypeStruct((b, FFT_B, a * w), F32),
        grid=(b, ncol),
        in_specs=[
            pl.BlockSpec((None, FFT_B, tc), lambda bi, j: (bi, 0, j)),
            pl.BlockSpec((None, FFT_B, tc), lambda bi, j: (bi, 0, ncol + j)),
            pl.BlockSpec((2 * FFT_B, 2 * FFT_B), lambda bi, j: (0, 0)),
            pl.BlockSpec((2 * LANES, LANES), lambda bi, j: (0, 0)),
        ],
        out_specs=pl.BlockSpec((None, FFT_B, tc), lambda bi, j: (bi, 0, j)),
        compiler_params=_cparams("parallel", "parallel"),
        name="fourier_stage2",
    )(z2, z2, m2, mc)
    return y.reshape(b * seq, w)


def _dft_ctx_kernel(x_ref, cn_ref, sn_ref, cc_ref, sc_ref, o_ref, *, scale):
    x = x_ref[...]
    outs = []
    for g in range(x.shape[1] // LANES):
        xg = x[:, g * LANES:(g + 1) * LANES]
        outs.append(_dot_hi(cn_ref[...], _dot_hi(xg, cc_ref[...])) - _dot_hi(sn_ref[...], _dot_hi(xg, sc_ref[...])))
    o_ref[...] = jnp.concatenate(outs, axis=1) * scale


def _fourier_ctx(f_all, rows):
    b, lc = rows.batch, rows.ctx_len
    w = FOURIER_WIDTH
    cn, sn = _dft_cs(lc)
    cc, sc = _dft_cs(FOURIER_GROUP_DIM)
    blk0 = rows.n_lat // lc
    mat = lambda m: pl.BlockSpec(m.shape, lambda bi: (0, 0))
    consts = [jnp.asarray(m, dtype=F32) for m in (cn, sn, cc, sc)]
    return pl.pallas_call(
        functools.partial(_dft_ctx_kernel, scale=1.0 / math.sqrt(lc * FOURIER_GROUP_DIM)),
        out_shape=jax.ShapeDtypeStruct((b * lc, w), F32),
        grid=(b,),
        in_specs=[pl.BlockSpec((lc, w), lambda bi: (blk0 + bi, 0))] + [mat(m) for m in consts],
        out_specs=pl.BlockSpec((lc, w), lambda bi: (bi, 0)),
        compiler_params=_cparams("parallel"),
        name="fourier_ctx",
    )(f_all, *consts)


def _ab_out_kernel(x1l_ref, x1c_ref, x2_ref, w_ref, h_ref, mod_ref, o_ref, *, lat_tiles):
    x1 = jnp.where(pl.program_id(0) < lat_tiles, x1l_ref[...], x1c_ref[...])
    x = jnp.concatenate([x1.astype(BF16), x2_ref[...]], axis=1)
    o_ref[...] = h_ref[...] + mod_ref[2:3, :] * _dot(x, w_ref[...])


def _ab_out(x1_lat, x1_ctx, x2, w_out, h, mods, rows):
    d = h.shape[1]
    tm = rows.tile
    lt = rows.lat_tiles
    return pl.pallas_call(
        functools.partial(_ab_out_kernel, lat_tiles=lt),
        out_shape=jax.ShapeDtypeStruct((rows.n, d), F32),
        grid=(rows.tiles,),
        in_specs=[
            pl.BlockSpec((tm, x1_lat.shape[1]), lambda i: (jnp.minimum(i, lt - 1), 0)),
            pl.BlockSpec((tm, x1_ctx.shape[1]), lambda i: (jnp.maximum(i - lt, 0), 0)),
            pl.BlockSpec((tm, x2.shape[1]), lambda i: (i, 0)),
            pl.BlockSpec(w_out.shape, lambda i: (0, 0)),
            pl.BlockSpec((tm, d), lambda i: (i, 0)),
            pl.BlockSpec((None, None, 3, d), lambda i: (rows.group(i), 1, 0, 0)),
        ],
        out_specs=pl.BlockSpec((tm, d), lambda i: (i, 0)),
        compiler_params=_cparams("parallel"),
        name="ab_out",
    )(x1_lat, x1_ctx, x2, w_out, h, mods)


def _hgrn_out_kernel(ofw_ref, obw_ref, g_ref, gain_ref, w_ref, h_ref, mod_ref, o_ref):
    o = ofw_ref[...] + obw_ref[...]
    parts = []
    for hh in range(o.shape[1] // HEAD_DIM):
        oh = o[:, hh * HEAD_DIM:(hh + 1) * HEAD_DIM]
        parts.append(oh * lax.rsqrt(jnp.mean(oh * oh, axis=-1, keepdims=True) + EPS))
    on = jnp.concatenate(parts, axis=1) * gain_ref[...]
    y = (on * jax.nn.sigmoid(g_ref[...])).astype(BF16)
    o_ref[...] = h_ref[...] + mod_ref[2:3, :] * _dot(y, w_ref[...])


def _hgrn_out(o_fw, o_bw, p, gain, w_out, h, mods, rows_r, n_tiles):
    d = h.shape[1]
    tm = rows_r.tile
    g_blk = (p.shape[1] - d) // d
    return pl.pallas_call(
        _hgrn_out_kernel,
        out_shape=jax.ShapeDtypeStruct((n_tiles * tm, d), F32),
        grid=(n_tiles,),
        in_specs=[
            pl.BlockSpec((tm, d), lambda i: (i, 0)),
            pl.BlockSpec((tm, d), lambda i: (i, 0)),
            pl.BlockSpec((tm, d), lambda i: (i, g_blk)),
            pl.BlockSpec((1, d), lambda i: (0, 0)),
            pl.BlockSpec(w_out.shape, lambda i: (0, 0)),
            pl.BlockSpec((tm, d), lambda i: (i, 0)),
            pl.BlockSpec((None, None, 3, d), lambda i: (rows_r.group(i), 1, 0, 0)),
        ],
        out_specs=pl.BlockSpec((tm, d), lambda i: (i, 0)),
        compiler_params=_cparams("parallel"),
        name="hgrn_out",
    )(o_fw, o_bw, p, gain, w_out, h, mods)


def _hgrn_proj_kernel(h_ref, mod_ref, gain_ref, w_ref, o_ref, xn_ref, *, n_silu):
    j = pl.program_id(1)

    @pl.when(j == 0)
    def _():
        xn_ref[...] = _ada(h_ref[...], gain_ref[...], mod_ref[0:1, :], mod_ref[1:2, :]).astype(BF16)

    acc = _dot(xn_ref[...], w_ref[...])
    o_ref[...] = jnp.where(j < n_silu, _silu(acc), acc)


def _hgrn_proj(h, mods, gain, w_in, rows):
    d = h.shape[1]
    tm, tn = rows.tile, PROJ_COL_TILE
    nw = w_in.shape[1]
    return pl.pallas_call(
        functools.partial(_hgrn_proj_kernel, n_silu=d // tn),
        out_shape=jax.ShapeDtypeStruct((rows.n, nw), F32),
        grid=(rows.tiles, nw // tn),
        in_specs=[
            pl.BlockSpec((tm, d), lambda i, j: (i, 0)),
            pl.BlockSpec((None, None, 3, d), lambda i, j: (rows.group(i), 1, 0, 0)),
            pl.BlockSpec((1, d), lambda i, j: (0, 0)),
            pl.BlockSpec((d, tn), lambda i, j: (0, j)),
        ],
        out_specs=pl.BlockSpec((tm, tn), lambda i, j: (i, j)),
        scratch_shapes=[pltpu.VMEM((tm, d), BF16)],
        compiler_params=_cparams("parallel", "arbitrary"),
        name="hgrn_proj",
    )(h, mods, gain, w_in)


def _scan_levels(chunk):
    return [chunk >> (i + 1) for i in range(int(math.log2(chunk)))]


def _scan_consts(chunk, reverse):
    t = np.arange(chunk)[:, None]
    s = np.arange(chunk)[None, :]
    tri = (s >= t) if reverse else (s <= t)
    masks = []
    for h in _scan_levels(chunk):
        same = (t // (2 * h)) == (s // (2 * h))
        t_up = (t // h) % 2 == 1
        s_up = (s // h) % 2 == 1
        pair = (~t_up & s_up) if reverse else (t_up & ~s_up)
        masks.append(same & pair)
    masks.append(t == s)
    return jnp.asarray(tri, dtype=BF16), jnp.asarray(np.stack(masks), dtype=F32)


def _seg_bcast(x, h, reverse):
    c, w = x.shape
    off = h if reverse else h - 1
    if 2 * h >= 2 * SUBLANES:
        pieces = [jnp.broadcast_to(x[g * 2 * h + off:g * 2 * h + off + 1, :], (2 * h, w)) for g in range(c // (2 * h))]
        return pieces[0] if len(pieces) == 1 else jnp.concatenate(pieces, axis=0)
    x3 = x.reshape(c // SUBLANES, SUBLANES, w)
    sub = lax.broadcasted_iota(jnp.int32, x3.shape, 1)
    y = None
    for g in range(SUBLANES // (2 * h)):
        piece = jnp.broadcast_to(x3[:, g * 2 * h + off:g * 2 * h + off + 1, :], x3.shape)
        y = piece if y is None else jnp.where(sub >= g * 2 * h, piece, y)
    return y.reshape(c, w)


def _scan_group(q, fl, lb, v, tri, mask_ref, o_ref, st_ref, h0, reverse):
    chunk, width = q.shape
    levels = _scan_levels(chunk)
    key = (1.0 - lb) * jax.nn.sigmoid(-fl)
    log_sig = jnp.minimum(fl, 0.0) - jnp.log(1.0 + jnp.exp(-jnp.abs(fl)))
    x1 = jnp.log(lb)
    x2 = jnp.log1p(-lb) + log_sig
    delta = x1 - x2
    lf = jnp.where(jnp.isnan(delta), x1 + x2, jnp.maximum(x1, x2) + jnp.log(1.0 + jnp.exp(-jnp.abs(delta))))
    l1 = lf.astype(BF16)
    r1 = lf - l1.astype(F32)
    l2 = r1.astype(BF16)
    l3 = (r1 - l2.astype(F32)).astype(BF16)
    cs = _dot(tri, jnp.concatenate([l1, l2, l3], axis=1))
    b = cs[:, :width] + cs[:, width:2 * width] + cs[:, 2 * width:]
    total = b[0:1, :] if reverse else b[chunk - 1:chunk, :]

    q_in = (q * jnp.exp(b)).astype(BF16)
    k_out = (key * jnp.exp(total - b)).astype(BF16)
    decay = jnp.exp(total)
    n_heads = width // HEAD_DIM
    lanes = [slice(hh * HEAD_DIM, (hh + 1) * HEAD_DIM) for hh in range(n_heads)]

    def pair_scores(li, ql, kl):
        return [mask_ref[li] * _dot_nt(ql[:, sl], kl[:, sl]) for sl in lanes]

    a = pair_scores(len(levels), q.astype(BF16), key.astype(BF16))
    for li, h in enumerate(levels):
        e = jnp.exp(-jnp.abs(b - _seg_bcast(b, h, reverse)))
        a = [x + y for x, y in zip(a, pair_scores(li, (q * e).astype(BF16), (key * e).astype(BF16)))]

    for hh, sl in enumerate(lanes):
        osl = slice((h0 + hh) * HEAD_DIM, (h0 + hh + 1) * HEAD_DIM)
        vh = v[:, sl].astype(BF16)
        st = st_ref[h0 + hh]
        o_ref[:, osl] = _dot_nt(q_in[:, sl], st.astype(BF16)) + _dot(a[hh].astype(BF16), vh)
        st_ref[h0 + hh] = st * decay[:, sl] + _dot_tn(vh, k_out[:, sl])


def _scan_kernel(q_ref, f_ref, v_ref, lb_ref, tri_ref, mask_ref, o_ref, st_ref, *, reverse, heads, group):
    @pl.when(pl.program_id(2) == 0)
    def _():
        st_ref[...] = jnp.zeros_like(st_ref)

    tri = tri_ref[...]
    for h0 in range(0, heads, group):
        sl = slice(h0 * HEAD_DIM, (h0 + group) * HEAD_DIM)
        _scan_group(q_ref[:, sl], f_ref[:, sl], lb_ref[:, sl], v_ref[:, sl], tri, mask_ref, o_ref, st_ref, h0, reverse)


def _hgrn_scan(p, lower_bound, rows, reverse):
    b, seq, lc = rows.batch, rows.seq, rows.ctx_len
    d = lower_bound.shape[0]
    c, hb = SCAN_CHUNK, SCAN_HEADS_PER_BLOCK
    wb = hb * HEAD_DIM
    ncb = d // wb
    nctx, nlat = lc // c, seq // c
    ctx0 = rows.n_lat // c
    f_blk = (2 if reverse else 1) * ncb
    v_blk = 3 * ncb

    def row(bi, s):
        if reverse:
            return jnp.where(s < nctx, ctx0 + bi * nctx + (nctx - 1 - s), bi * nlat + (nlat - 1 - (s - nctx)))
        return jnp.where(s < nctx, ctx0 + bi * nctx + s, bi * nlat + (s - nctx))

    tri, masks = _scan_consts(c, reverse)
    return pl.pallas_call(
        functools.partial(_scan_kernel, reverse=reverse, heads=hb, group=SCAN_HEAD_GROUP),
        out_shape=jax.ShapeDtypeStruct((rows.n, d), F32),
        grid=(b, ncb, nctx + nlat),
        in_specs=[
            pl.BlockSpec((c, wb), lambda bi, hi, s: (row(bi, s), hi)),
            pl.BlockSpec((c, wb), lambda bi, hi, s: (row(bi, s), f_blk + hi)),
            pl.BlockSpec((c, wb), lambda bi, hi, s: (row(bi, s), v_blk + hi)),
            pl.BlockSpec((1, wb), lambda bi, hi, s: (0, hi)),
            pl.BlockSpec(tri.shape, lambda bi, hi, s: (0, 0)),
            pl.BlockSpec(masks.shape, lambda bi, hi, s: (0, 0, 0)),
        ],
        out_specs=pl.BlockSpec((c, wb), lambda bi, hi, s: (row(bi, s), hi)),
        scratch_shapes=[pltpu.VMEM((hb, HEAD_DIM, HEAD_DIM), F32)],
        compiler_params=_cparams("parallel", "parallel", "arbitrary"),
        name="hgrn_scan_bw" if reverse else "hgrn_scan_fw",
    )(p, p, p, lower_bound.reshape(1, d), tri, masks)


def kernel(x, c, ctx, c_ctx, w_mod, b_mod, norm_gains, ffn_w_in, ffn_w_out, ab_w_in, qk_norm, ab_w_out,
           hgrn_w_in, hgrn_lb_logits, hgrn_o_norm, hgrn_w_out, final_norm):
    batch, seq, d = x.shape
    lc = ctx.shape[1]
    depth = w_mod.shape[0]
    rows = _Rows(batch, seq, lc, ROW_TILE)
    rows_r = _Rows(batch, seq, lc, READOUT_ROW_TILE)
    assert seq % GRID_W == 0 and batch + 1 <= SUBLANES

    c_rows = jnp.concatenate([c_ctx[None, :], c, jnp.zeros((SUBLANES - 1 - batch, d), F32)], axis=0)
    mods_all = _modulation(c_rows, w_mod, b_mod).reshape(depth, SUBLANES, 3, 3, d)

    lb_cum = jnp.cumsum(jax.nn.softmax(hgrn_lb_logits.astype(F32), axis=0), axis=0)
    lower_bounds = lb_cum - lb_cum[0]

    ffn_w_in_b = ffn_w_in.astype(BF16)
    ffn_w_out_b = ffn_w_out.astype(BF16)
    fin = final_norm.reshape(1, d)
    rope = _rope_tables(seq, ROW_TILE)

    h = (x.reshape(batch * seq, d), ctx.reshape(batch * lc, d))
    for layer in range(depth):
        last = layer == depth - 1
        mods = mods_all[layer]
        gains = norm_gains[layer].reshape(3, 1, d)
        h = _ffn(h, mods, 0, gains[0], ffn_w_in_b, ffn_w_out_b, layer, 0, fin, rows, rows.tiles, False)
        if layer % 2 == 0:
            e = layer // 2
            f_all, q_all, k_all, vt_all = _ab_proj(h, mods, gains[1], ab_w_in[e].astype(BF16), qk_norm[e], rope, rows)
            attn = _attention(q_all, k_all, vt_all, rows)
            attn = _attention(q_all, k_all, vt_all, rows, lat_out=attn)
            h = _ab_out(_fourier_latent(f_all, rows), _fourier_ctx(f_all, rows), attn,
                        ab_w_out[e].astype(BF16), h, mods, rows)
        else:
            o = layer // 2
            p = _hgrn_proj(h, mods, gains[1], hgrn_w_in[o].astype(BF16), rows)
            o_fw = _hgrn_scan(p, lower_bounds[layer], rows, False)
            o_bw = _hgrn_scan(p, lower_bounds[layer], rows, True)
            gain_o = jnp.tile(hgrn_o_norm[o], d // HEAD_DIM).reshape(1, d)
            n_t = rows_r.lat_tiles if last else rows_r.tiles
            h = _hgrn_out(o_fw, o_bw, p, gain_o, hgrn_w_out[o].astype(BF16), h, mods, rows_r, n_t)
        n_t = rows.lat_tiles if last else rows.tiles
        h = _ffn(h, mods, 2, gains[2], ffn_w_in_b, ffn_w_out_b, layer, 1, fin, rows, n_t, last)
    return h[:batch * seq].reshape(batch, seq, d)
```

```python
import functools
import math

import jax
import jax.numpy as jnp
import numpy as np
from jax import lax
from jax.experimental import pallas as pl
from jax.experimental.pallas import tpu as pltpu

F32 = jnp.float32
BF16 = jnp.bfloat16

EPS = 1e-6
N_MOD = 9
HEAD_DIM = 128
N_KV_HEADS = 4
FOURIER_WIDTH = 512
FOURIER_GROUP_DIM = 128
GRID_W = 64
ROPE_THETA = 10000.0
ROPE_AXIS_DIM = HEAD_DIM // 2
ATTN_SCALE = HEAD_DIM ** -0.5
LOG2_E = math.log2(math.e)

LANES = 128
SUBLANES = 8
BF16_SUBLANES = 16
VT_ROWS = HEAD_DIM + BF16_SUBLANES
VMEM_LIMIT_BYTES = 56 * 1024 * 1024

ROW_TILE = 512
READOUT_ROW_TILE = 256
FFN_TILE = 512
PROJ_COL_TILE = 2048
MOD_COL_TILE = 1024
ATTN_Q_TILE = 256
SCAN_CHUNK = 128
SCAN_HEADS_PER_BLOCK = 4
SCAN_CHUNKS_PER_STEP = 2
FFT_B = 128
FFT_B_BLOCK = 8
FFT_COL_TILE = 2048


def _cparams(*sem):
    return pltpu.CompilerParams(dimension_semantics=sem, vmem_limit_bytes=VMEM_LIMIT_BYTES)


def _dot(a, b):
    return jnp.dot(a, b, preferred_element_type=F32)


def _dot_hi(a, b):
    return jnp.dot(a, b, preferred_element_type=F32, precision=lax.Precision.HIGHEST)


def _dot_nt(a, b):
    return lax.dot_general(a, b, (((1,), (1,)), ((), ())), preferred_element_type=F32)


def _dot_tn(a, b):
    return lax.dot_general(a, b, (((0,), (0,)), ((), ())), preferred_element_type=F32)


def _silu(x):
    return x * jax.nn.sigmoid(x)


def _rms(x, gain):
    return x * lax.rsqrt(jnp.mean(x * x, axis=-1, keepdims=True) + EPS) * gain


def _ada(h, gain, shift, scale):
    return _rms(h, gain) * (1.0 + scale) + shift


def _mod_kernel(c_ref, w_ref, b_ref, o_ref):
    a = _silu(c_ref[...]).astype(BF16)
    o_ref[...] = _dot(a, w_ref[...].astype(BF16)) + b_ref[...]


def _modulation(c_rows, w_mod, b_mod):
    depth, d, nd = w_mod.shape
    tn = MOD_COL_TILE
    return pl.pallas_call(
        _mod_kernel,
        out_shape=jax.ShapeDtypeStruct((depth, SUBLANES, nd), F32),
        grid=(depth, nd // tn),
        in_specs=[
            pl.BlockSpec((SUBLANES, d), lambda l, j: (0, 0)),
            pl.BlockSpec((None, d, tn), lambda l, j: (l, 0, j)),
            pl.BlockSpec((None, 1, tn), lambda l, j: (l, 0, j)),
        ],
        out_specs=pl.BlockSpec((None, SUBLANES, tn), lambda l, j: (l, 0, j)),
        compiler_params=_cparams("parallel", "arbitrary"),
        name="modulation",
    )(c_rows, w_mod, b_mod.reshape(depth, 1, nd))


class _Rows:
    def __init__(self, batch, seq, ctx_len, tile):
        assert seq % tile == 0 and (batch * ctx_len) % tile == 0
        self.batch, self.seq, self.ctx_len, self.tile = batch, seq, ctx_len, tile
        self.n_lat = batch * seq
        self.n = self.n_lat + batch * ctx_len
        self.lat_tiles = self.n_lat // tile
        self.tiles = self.n // tile
        self.tiles_per_batch = seq // tile

    def group(self, i):
        return jnp.where(i < self.lat_tiles, 1 + i // self.tiles_per_batch, 0)


def _ffn_kernel(*refs, final, split_tiles):
    if split_tiles is None:
        h_ref, mod_ref, gain_ref, wa_ref, wb_ref, wo_ref, fin_ref, o_ref, xn_ref, acc_ref = refs
        load_h = lambda: h_ref[...]
    else:
        hl_ref, hc_ref, mod_ref, gain_ref, wa_ref, wb_ref, wo_ref, fin_ref, o_ref, xn_ref, acc_ref = refs
        load_h = lambda: jnp.where(pl.program_id(0) < split_tiles, hl_ref[...], hc_ref[...])
    j = pl.program_id(1)

    @pl.when(j == 0)
    def _():
        xn = _ada(load_h(), gain_ref[...], mod_ref[0:1, :], mod_ref[1:2, :])
        xn_ref[...] = xn.astype(BF16)
        acc_ref[...] = jnp.zeros_like(acc_ref)

    xn = xn_ref[...]
    a = _dot(xn, wa_ref[...])
    b = _dot(xn, wb_ref[...])
    g = (_silu(a) * b).astype(BF16)
    acc_ref[...] += _dot(g, wo_ref[...])

    @pl.when(j == pl.num_programs(1) - 1)
    def _():
        out = load_h() + 0.5 * mod_ref[2:3, :] * acc_ref[...]
        if final:
            out = _rms(out, fin_ref[...])
        o_ref[...] = out


def _ffn(h, mods, sub, gain, w_in, w_out, layer, which, fin, rows, n_tiles, final):
    f, d = w_out.shape[2:]
    tm, tf = rows.tile, FFN_TILE
    nf = f // tf
    row_spec = pl.BlockSpec((tm, d), lambda i, j: (i, 0))
    if isinstance(h, tuple):
        lt = rows.lat_tiles
        h_args = h
        h_specs = [pl.BlockSpec((tm, d), lambda i, j: (jnp.minimum(i, lt - 1), 0)),
                   pl.BlockSpec((tm, d), lambda i, j: (jnp.maximum(i - lt, 0), 0))]
        split_tiles = lt
    else:
        h_args, h_specs, split_tiles = (h,), [row_spec], None
    return pl.pallas_call(
        functools.partial(_ffn_kernel, final=final, split_tiles=split_tiles),
        out_shape=jax.ShapeDtypeStruct((n_tiles * tm, d), F32),
        grid=(n_tiles, nf),
        in_specs=h_specs + [
            pl.BlockSpec((None, None, 3, d), lambda i, j: (rows.group(i), sub, 0, 0)),
            pl.BlockSpec((1, d), lambda i, j: (0, 0)),
            pl.BlockSpec((None, None, d, tf), lambda i, j: (layer, which, 0, j)),
            pl.BlockSpec((None, None, d, tf), lambda i, j: (layer, which, 0, nf + j)),
            pl.BlockSpec((None, None, tf, d), lambda i, j: (layer, which, j, 0)),
            pl.BlockSpec((1, d), lambda i, j: (0, 0)),
        ],
        out_specs=row_spec,
        scratch_shapes=[pltpu.VMEM((tm, d), BF16), pltpu.VMEM((tm, d), F32)],
        compiler_params=_cparams("parallel", "arbitrary"),
        name="ffn",
    )(*h_args, mods, gain, w_in, w_in, w_out, fin)


def _rope_tables(seq, tile):
    t = np.arange(seq)
    inv_freq = ROPE_THETA ** (-np.arange(0, ROPE_AXIS_DIM, 2, dtype=np.float64) / ROPE_AXIS_DIM)
    ang = np.concatenate([(t // GRID_W)[:, None] * inv_freq, (t % GRID_W)[:, None] * inv_freq], axis=-1)
    nf = ROPE_AXIS_DIM // 2
    cos = np.cos(ang).reshape(seq, 2, 1, nf)
    sin = np.sin(ang).reshape(seq, 2, 1, nf)
    zero = np.zeros_like(sin)
    c_full = np.broadcast_to(cos, (seq, 2, 2, nf)).reshape(seq, HEAD_DIM)
    s_up = np.concatenate([-sin, zero], axis=2).reshape(seq, HEAD_DIM)
    s_dn = np.concatenate([zero, sin], axis=2).reshape(seq, HEAD_DIM)
    lat = np.concatenate([c_full, s_up, s_dn], axis=1)
    ident = np.concatenate([np.ones((tile, HEAD_DIM)), np.zeros((tile, 2 * HEAD_DIM))], axis=1)
    return jnp.asarray(np.concatenate([lat, ident], axis=0), dtype=F32)


def _norm_rope_heads(acc, gain, rope, post_scale=None):
    nf = ROPE_AXIS_DIM // 2
    c, s_up, s_dn = rope[:, :HEAD_DIM], rope[:, HEAD_DIM:2 * HEAD_DIM], rope[:, 2 * HEAD_DIM:]
    heads = []
    for hh in range(acc.shape[1] // HEAD_DIM):
        y = _rms(acc[:, hh * HEAD_DIM:(hh + 1) * HEAD_DIM], gain)
        y = y * c + pltpu.roll(y, HEAD_DIM - nf, 1) * s_up + pltpu.roll(y, nf, 1) * s_dn
        heads.append(y if post_scale is None else y * post_scale)
    return jnp.concatenate(heads, axis=1).astype(BF16)


def _ab_proj_kernel(h_ref, mod_ref, gain_ref, w_ref, qkn_ref, rope_ref, f_ref, q_ref, k_ref, vt_ref):
    xn = _ada(h_ref[...], gain_ref[...], mod_ref[0:1, :], mod_ref[1:2, :]).astype(BF16)
    c0 = f_ref.shape[1]
    c1 = c0 + q_ref.shape[1]
    c2 = c1 + k_ref.shape[1]
    f_ref[...] = _dot(xn, w_ref[:, :c0])
    q_ref[...] = _norm_rope_heads(_dot(xn, w_ref[:, c0:c1]), qkn_ref[0:1, :], rope_ref[...], ATTN_SCALE * LOG2_E)
    k_ref[...] = _norm_rope_heads(_dot(xn, w_ref[:, c1:c2]), qkn_ref[1:2, :], rope_ref[...])
    v = _dot(xn, w_ref[:, c2:])
    ones = jnp.ones((vt_ref.shape[1] - HEAD_DIM, vt_ref.shape[2]), BF16)
    for hh in range(N_KV_HEADS):
        vt_ref[hh, :HEAD_DIM, :] = v[:, hh * HEAD_DIM:(hh + 1) * HEAD_DIM].T.astype(BF16)
        vt_ref[hh, HEAD_DIM:, :] = ones


def _ab_proj(h, mods, gain, w_in, qk_norm, rope, rows):
    d = h.shape[1]
    tm = rows.tile
    kv_width = N_KV_HEADS * HEAD_DIM
    q_width = w_in.shape[1] - FOURIER_WIDTH - 2 * kv_width
    n = rows.n
    rope_blk = lambda i: (jnp.where(i < rows.lat_tiles, i % rows.tiles_per_batch, rows.tiles_per_batch), 0)
    return pl.pallas_call(
        _ab_proj_kernel,
        out_shape=(
            jax.ShapeDtypeStruct((n, FOURIER_WIDTH), F32),
            jax.ShapeDtypeStruct((n, q_width), BF16),
            jax.ShapeDtypeStruct((n, kv_width), BF16),
            jax.ShapeDtypeStruct((N_KV_HEADS, rows.tiles, VT_ROWS, tm), BF16),
        ),
        grid=(rows.tiles,),
        in_specs=[
            pl.BlockSpec((tm, d), lambda i: (i, 0)),
            pl.BlockSpec((None, None, 3, d), lambda i: (rows.group(i), 1, 0, 0)),
            pl.BlockSpec((1, d), lambda i: (0, 0)),
            pl.BlockSpec(w_in.shape, lambda i: (0, 0)),
            pl.BlockSpec((2, HEAD_DIM), lambda i: (0, 0)),
            pl.BlockSpec((tm, 3 * HEAD_DIM), rope_blk),
        ],
        out_specs=(
            pl.BlockSpec((tm, FOURIER_WIDTH), lambda i: (i, 0)),
            pl.BlockSpec((tm, q_width), lambda i: (i, 0)),
            pl.BlockSpec((tm, kv_width), lambda i: (i, 0)),
            pl.BlockSpec((N_KV_HEADS, None, VT_ROWS, tm), lambda i: (0, i, 0, 0)),
        ),
        compiler_params=_cparams("parallel"),
        name="ab_proj",
    )(h, mods, gain, w_in, qk_norm, rope)


def _attn_kernel(*refs, lat_chunks, group):
    if lat_chunks:
        q_ref, kc_ref, vtc_ref, kl_ref, vtl_ref, o_ref, acc_ref, s_ref = refs
    else:
        q_ref, kc_ref, vtc_ref, _, o_ref, acc_ref = refs
    tq = q_ref.shape[0]
    q = q_ref[...]
    qs = jnp.concatenate([q[:, g * HEAD_DIM:(g + 1) * HEAD_DIM] for g in range(group)], axis=0)
    nq = group * tq
    acc_ref[...] = jnp.zeros_like(acc_ref)

    def scores(k):
        return _dot_nt(k, qs)

    def update(s, vt, m):
        m_new = jnp.maximum(m, jnp.max(s, axis=0, keepdims=True))
        p = jnp.exp2(s - m_new).astype(BF16)
        acc_ref[...] = jnp.exp2(m - m_new) * acc_ref[...] + _dot(vt, p)
        return m_new

    m = update(scores(kc_ref[...]), vtc_ref[...], jnp.full((1, nq), -jnp.inf, F32))
    if lat_chunks:
        tk = vtl_ref.shape[2]

        def lat_scores(c):
            return scores(kl_ref[pl.ds(pl.multiple_of(c * tk, tk), tk), :])

        assert lat_chunks % 2 == 0
        s_ref[0] = lat_scores(0)

        def body(i, m):
            c = 2 * i
            s_ref[1] = lat_scores(c + 1)
            m = update(s_ref[0], vtl_ref[c], m)
            s_ref[0] = lat_scores(jnp.minimum(c + 2, lat_chunks - 1))
            return update(s_ref[1], vtl_ref[c + 1], m)

        m = lax.fori_loop(0, lat_chunks // 2, body, m)

    out = (acc_ref[:HEAD_DIM, :] / acc_ref[HEAD_DIM:HEAD_DIM + 1, :]).T
    o_ref[...] = jnp.concatenate([out[g * tq:(g + 1) * tq, :] for g in range(group)], axis=1).astype(BF16)


def _attention(q_all, k_all, vt_all, rows, lat_out=None):
    b, seq, lc, tile = rows.batch, rows.seq, rows.ctx_len, rows.tile
    group = q_all.shape[1] // (N_KV_HEADS * HEAD_DIM)
    gw = group * HEAD_DIM
    assert tile % lc == 0 and seq % tile == 0
    ctx_blk0 = rows.n_lat // lc
    kc_spec = pl.BlockSpec((lc, HEAD_DIM), lambda bi, hi, i: (ctx_blk0 + bi, hi))
    vtc_spec = pl.BlockSpec(
        (None, None, VT_ROWS, lc),
        lambda bi, hi, i: (hi, (rows.n_lat + bi * lc) // tile, 0, ((rows.n_lat + bi * lc) % tile) // lc))
    if lat_out is None:
        tq = ATTN_Q_TILE
        nqt = seq // tq
        lat_chunks = seq // tile
        q_spec = pl.BlockSpec((tq, gw), lambda bi, hi, i: (bi * nqt + i, hi))
        kl_spec = pl.BlockSpec((seq, HEAD_DIM), lambda bi, hi, i: (bi, hi))
        vtl_spec = pl.BlockSpec((None, lat_chunks, VT_ROWS, tile), lambda bi, hi, i: (hi, bi, 0, 0))
        in_specs = [q_spec, kc_spec, vtc_spec, kl_spec, vtl_spec]
        args = (q_all, k_all, vt_all, k_all, vt_all)
        aliases = {}
    else:
        tq = lc
        nqt = 1
        lat_chunks = 0
        q_spec = pl.BlockSpec((tq, gw), lambda bi, hi, i: (ctx_blk0 + bi, hi))
        in_specs = [q_spec, kc_spec, vtc_spec, pl.BlockSpec(memory_space=pl.ANY)]
        args = (q_all, k_all, vt_all, lat_out)
        aliases = {3: 0}
    return pl.pallas_call(
        functools.partial(_attn_kernel, lat_chunks=lat_chunks, group=group),
        out_shape=jax.ShapeDtypeStruct((rows.n, q_all.shape[1]), BF16),
        grid=(b, N_KV_HEADS, nqt),
        in_specs=in_specs,
        out_specs=q_spec,
        scratch_shapes=[pltpu.VMEM((VT_ROWS, group * tq), F32)]
        + ([pltpu.VMEM((2, tile, group * tq), F32)] if lat_chunks else []),
        input_output_aliases=aliases,
        compiler_params=_cparams("parallel", "parallel", "arbitrary"),
        name="attention_ctx" if lat_chunks == 0 else "attention_lat",
    )(*args)


def _dft_cs(n):
    idx = np.arange(n)
    ang = 2.0 * np.pi * ((idx[:, None] * idx[None, :]) % n) / n
    return np.cos(ang), np.sin(ang)


def _fft1_kernel(x_ref, fa_ref, tw_ref, o_ref, *, a):
    w = o_ref.shape[2]
    for r in range(o_ref.shape[0]):
        z = _dot_hi(fa_ref[...], x_ref[:, r * w:(r + 1) * w])
        zr, zi = z[:a], z[a:]
        tc = jnp.concatenate([tw_ref[r, 0]] * (w // LANES), axis=1)
        ts = jnp.concatenate([tw_ref[r, 1]] * (w // LANES), axis=1)
        o_ref[r, :a, :] = zr * tc - zi * ts
        o_ref[r, a:, :] = zr * ts + zi * tc


def _fft2_kernel(zr_ref, zi_ref, m2_ref, mc_ref, o_ref, *, scale):
    z = jnp.concatenate([zr_ref[...], zi_ref[...]], axis=0)
    v = _dot_hi(m2_ref[...], z)
    vr, vi = v[:FFT_B], v[FFT_B:]
    outs = []
    for g in range(o_ref.shape[1] // LANES):
        u = jnp.concatenate([vr[:, g * LANES:(g + 1) * LANES], vi[:, g * LANES:(g + 1) * LANES]], axis=1)
        outs.append(_dot_hi(u, mc_ref[...]))
    o_ref[...] = jnp.concatenate(outs, axis=1) * scale


def _fourier_latent(f_all, rows):
    b, seq = rows.batch, rows.seq
    w = FOURIER_WIDTH
    a = seq // FFT_B
    assert a % SUBLANES == 0 and FOURIER_GROUP_DIM == LANES
    ca, sa = _dft_cs(a)
    fa = jnp.asarray(np.concatenate([ca, sa], axis=0), dtype=F32)
    p1b = (np.arange(a)[None, :] * np.arange(FFT_B)[:, None]) % seq
    ang = 2.0 * np.pi * p1b / seq
    tw = np.stack([np.cos(ang), np.sin(ang)], axis=1)[..., None]
    tw = jnp.asarray(np.broadcast_to(tw, (FFT_B, 2, a, LANES)), dtype=F32)
    cb, sb = _dft_cs(FFT_B)
    m2 = jnp.asarray(np.block([[cb, -sb], [sb, cb]]), dtype=F32)
    cc, sc = _dft_cs(FOURIER_GROUP_DIM)
    mc = jnp.asarray(np.concatenate([cc, -sc], axis=0), dtype=F32)

    blk = FFT_B_BLOCK
    x2 = f_all.reshape(rows.n // FFT_B, FFT_B * w)
    z = pl.pallas_call(
        functools.partial(_fft1_kernel, a=a),
        out_shape=jax.ShapeDtypeStruct((b, FFT_B, 2 * a, w), F32),
        grid=(b, FFT_B // blk),
        in_specs=[
            pl.BlockSpec((a, blk * w), lambda bi, j: (bi, j)),
            pl.BlockSpec((2 * a, a), lambda bi, j: (0, 0)),
            pl.BlockSpec((blk, 2, a, LANES), lambda bi, j: (j, 0, 0, 0)),
        ],
        out_specs=pl.BlockSpec((None, blk, 2 * a, w), lambda bi, j: (bi, j, 0, 0)),
        compiler_params=_cparams("parallel", "parallel"),
        name="fourier_stage1",
    )(x2, fa, tw)

    tc = min(FFT_COL_TILE, a * w)
    ncol = (a * w) // tc
    z2 = z.reshape(b, FFT_B, 2 * a * w)
    y = pl.pallas_call(
        functools.partial(_fft2_kernel, scale=1.0 / math.sqrt(seq * FOURIER_GROUP_DIM)),
        out_shape=jax.ShapeDtypeStruct((b, FFT_B, a * w), F32),
        grid=(b, ncol),
        in_specs=[
            pl.BlockSpec((None, FFT_B, tc), lambda bi, j: (bi, 0, j)),
            pl.BlockSpec((None, FFT_B, tc), lambda bi, j: (bi, 0, ncol + j)),
            pl.BlockSpec((2 * FFT_B, 2 * FFT_B), lambda bi, j: (0, 0)),
            pl.BlockSpec((2 * LANES, LANES), lambda bi, j: (0, 0)),
        ],
        out_specs=pl.BlockSpec((None, FFT_B, tc), lambda bi, j: (bi, 0, j)),
        compiler_params=_cparams("parallel", "parallel"),
        name="fourier_stage2",
    )(z2, z2, m2, mc)
    return y.reshape(b * seq, w)


def _dft_ctx_kernel(x_ref, cn_ref, sn_ref, cc_ref, sc_ref, o_ref, *, scale):
    x = x_ref[...]
    outs = []
    for g in range(x.shape[1] // LANES):
        xg = x[:, g * LANES:(g + 1) * LANES]
        outs.append(_dot_hi(cn_ref[...], _dot_hi(xg, cc_ref[...])) - _dot_hi(sn_ref[...], _dot_hi(xg, sc_ref[...])))
    o_ref[...] = jnp.concatenate(outs, axis=1) * scale


def _fourier_ctx(f_all, rows):
    b, lc = rows.batch, rows.ctx_len
    w = FOURIER_WIDTH
    cn, sn = _dft_cs(lc)
    cc, sc = _dft_cs(FOURIER_GROUP_DIM)
    blk0 = rows.n_lat // lc
    mat = lambda m: pl.BlockSpec(m.shape, lambda bi: (0, 0))
    consts = [jnp.asarray(m, dtype=F32) for m in (cn, sn, cc, sc)]
    return pl.pallas_call(
        functools.partial(_dft_ctx_kernel, scale=1.0 / math.sqrt(lc * FOURIER_GROUP_DIM)),
        out_shape=jax.ShapeDtypeStruct((b * lc, w), F32),
        grid=(b,),
        in_specs=[pl.BlockSpec((lc, w), lambda bi: (blk0 + bi, 0))] + [mat(m) for m in consts],
        out_specs=pl.BlockSpec((lc, w), lambda bi: (bi, 0)),
        compiler_params=_cparams("parallel"),
        name="fourier_ctx",
    )(f_all, *consts)


def _ab_out_kernel(x1l_ref, x1c_ref, x2_ref, w_ref, h_ref, mod_ref, o_ref, *, lat_tiles):
    x1 = jnp.where(pl.program_id(0) < lat_tiles, x1l_ref[...], x1c_ref[...])
    x = jnp.concatenate([x1.astype(BF16), x2_ref[...]], axis=1)
    o_ref[...] = h_ref[...] + mod_ref[2:3, :] * _dot(x, w_ref[...])


def _ab_out(x1_lat, x1_ctx, x2, w_out, h, mods, rows):
    d = h.shape[1]
    tm = rows.tile
    lt = rows.lat_tiles
    return pl.pallas_call(
        functools.partial(_ab_out_kernel, lat_tiles=lt),
        out_shape=jax.ShapeDtypeStruct((rows.n, d), F32),
        grid=(rows.tiles,),
        in_specs=[
            pl.BlockSpec((tm, x1_lat.shape[1]), lambda i: (jnp.minimum(i, lt - 1), 0)),
            pl.BlockSpec((tm, x1_ctx.shape[1]), lambda i: (jnp.maximum(i - lt, 0), 0)),
            pl.BlockSpec((tm, x2.shape[1]), lambda i: (i, 0)),
            pl.BlockSpec(w_out.shape, lambda i: (0, 0)),
            pl.BlockSpec((tm, d), lambda i: (i, 0)),
            pl.BlockSpec((None, None, 3, d), lambda i: (rows.group(i), 1, 0, 0)),
        ],
        out_specs=pl.BlockSpec((tm, d), lambda i: (i, 0)),
        compiler_params=_cparams("parallel"),
        name="ab_out",
    )(x1_lat, x1_ctx, x2, w_out, h, mods)


def _hgrn_out_kernel(ofw_ref, obw_ref, g_ref, gain_ref, w_ref, h_ref, mod_ref, o_ref):
    o = ofw_ref[...] + obw_ref[...]
    parts = []
    for hh in range(o.shape[1] // HEAD_DIM):
        oh = o[:, hh * HEAD_DIM:(hh + 1) * HEAD_DIM]
        parts.append(oh * lax.rsqrt(jnp.mean(oh * oh, axis=-1, keepdims=True) + EPS))
    on = jnp.concatenate(parts, axis=1) * gain_ref[...]
    y = (on * jax.nn.sigmoid(g_ref[...])).astype(BF16)
    o_ref[...] = h_ref[...] + mod_ref[2:3, :] * _dot(y, w_ref[...])


def _hgrn_out(o_fw, o_bw, p, gain, w_out, h, mods, rows_r, n_tiles):
    d = h.shape[1]
    tm = rows_r.tile
    g_blk = (p.shape[1] - d) // d
    return pl.pallas_call(
        _hgrn_out_kernel,
        out_shape=jax.ShapeDtypeStruct((n_tiles * tm, d), F32),
        grid=(n_tiles,),
        in_specs=[
            pl.BlockSpec((tm, d), lambda i: (i, 0)),
            pl.BlockSpec((tm, d), lambda i: (i, 0)),
            pl.BlockSpec((tm, d), lambda i: (i, g_blk)),
            pl.BlockSpec((1, d), lambda i: (0, 0)),
            pl.BlockSpec(w_out.shape, lambda i: (0, 0)),
            pl.BlockSpec((tm, d), lambda i: (i, 0)),
            pl.BlockSpec((None, None, 3, d), lambda i: (rows_r.group(i), 1, 0, 0)),
        ],
        out_specs=pl.BlockSpec((tm, d), lambda i: (i, 0)),
        compiler_params=_cparams("parallel"),
        name="hgrn_out",
    )(o_fw, o_bw, p, gain, w_out, h, mods)


def _hgrn_proj_kernel(h_ref, mod_ref, gain_ref, w_ref, o_ref, xn_ref, *, n_silu):
    j = pl.program_id(1)

    @pl.when(j == 0)
    def _():
        xn_ref[...] = _ada(h_ref[...], gain_ref[...], mod_ref[0:1, :], mod_ref[1:2, :]).astype(BF16)

    acc = _dot(xn_ref[...], w_ref[...])
    o_ref[...] = jnp.where(j < n_silu, _silu(acc), acc)


def _hgrn_proj(h, mods, gain, w_in, rows):
    d = h.shape[1]
    tm, tn = rows.tile, PROJ_COL_TILE
    nw = w_in.shape[1]
    return pl.pallas_call(
        functools.partial(_hgrn_proj_kernel, n_silu=d // tn),
        out_shape=jax.ShapeDtypeStruct((rows.n, nw), F32),
        grid=(rows.tiles, nw // tn),
        in_specs=[
            pl.BlockSpec((tm, d), lambda i, j: (i, 0)),
            pl.BlockSpec((None, None, 3, d), lambda i, j: (rows.group(i), 1, 0, 0)),
            pl.BlockSpec((1, d), lambda i, j: (0, 0)),
            pl.BlockSpec((d, tn), lambda i, j: (0, j)),
        ],
        out_specs=pl.BlockSpec((tm, tn), lambda i, j: (i, j)),
        scratch_shapes=[pltpu.VMEM((tm, d), BF16)],
        compiler_params=_cparams("parallel", "arbitrary"),
        name="hgrn_proj",
    )(h, mods, gain, w_in)


def _scan_levels(chunk):
    return [chunk >> (i + 1) for i in range(int(math.log2(chunk)))]


def _scan_consts(chunk, reverse):
    t = np.arange(chunk)[:, None]
    s = np.arange(chunk)[None, :]
    tri = (s >= t) if reverse else (s <= t)
    wide, narrow = [], []
    for h in _scan_levels(chunk):
        same = (t // (2 * h)) == (s // (2 * h))
        t_up = (t // h) % 2 == 1
        s_up = (s // h) % 2 == 1
        mask = same & ((~t_up & s_up) if reverse else (t_up & ~s_up))
        if h >= SUBLANES:
            wide.append(mask[np.nonzero(~t_up[:, 0] if reverse else t_up[:, 0])[0]])
        else:
            narrow.append(mask)
    narrow.append(t == s)
    return (jnp.asarray(tri, dtype=BF16), jnp.asarray(np.stack(wide), dtype=F32),
            jnp.asarray(np.stack(narrow), dtype=F32))


def _seg_bcast(x, h, reverse):
    c, w = x.shape
    off = h if reverse else h - 1
    if 2 * h >= 2 * SUBLANES:
        pieces = [jnp.broadcast_to(x[g * 2 * h + off:g * 2 * h + off + 1, :], (2 * h, w)) for g in range(c // (2 * h))]
        return pieces[0] if len(pieces) == 1 else jnp.concatenate(pieces, axis=0)
    x3 = x.reshape(c // SUBLANES, SUBLANES, w)
    sub = lax.broadcasted_iota(jnp.int32, x3.shape, 1)
    y = None
    for g in range(SUBLANES // (2 * h)):
        piece = jnp.broadcast_to(x3[:, g * 2 * h + off:g * 2 * h + off + 1, :], x3.shape)
        y = piece if y is None else jnp.where(sub >= g * 2 * h, piece, y)
    return y.reshape(c, w)


def _scan_kernel(q_ref, f_ref, v_ref, lb_ref, tri_ref, wmask_ref, nmask_ref, o_ref, st_ref, *, reverse, chunk):
    @pl.when(pl.program_id(2) == 0)
    def _():
        st_ref[...] = jnp.zeros_like(st_ref)

    rows, width = q_ref.shape
    levels = _scan_levels(chunk)
    spans = [slice(ci * chunk, (ci + 1) * chunk) for ci in range(rows // chunk)]
    lanes = [slice(hh * HEAD_DIM, (hh + 1) * HEAD_DIM) for hh in range(width // HEAD_DIM)]
    q = q_ref[...]
    fl = f_ref[...]
    lb = lb_ref[...]
    key = (1.0 - lb) * jax.nn.sigmoid(-fl)
    log_sig = jnp.minimum(fl, 0.0) - jnp.log(1.0 + jnp.exp(-jnp.abs(fl)))
    x1 = jnp.log(lb)
    x2 = jnp.log1p(-lb) + log_sig
    delta = x1 - x2
    lf = jnp.where(jnp.isnan(delta), x1 + x2, jnp.maximum(x1, x2) + jnp.log(1.0 + jnp.exp(-jnp.abs(delta))))
    lf = lf * LOG2_E
    l1 = lf.astype(BF16)
    r1 = lf - l1.astype(F32)
    l2 = r1.astype(BF16)
    l3 = (r1 - l2.astype(F32)).astype(BF16)
    parts = jnp.concatenate([l1, l2, l3], axis=1)
    tri = tri_ref[...]
    cs = jnp.concatenate([_dot(tri, parts[sp]) for sp in spans], axis=0)
    b = cs[:, :width] + cs[:, width:2 * width] + cs[:, 2 * width:]
    totals = [b[sp.start:sp.start + 1, :] if reverse else b[sp.stop - 1:sp.stop, :] for sp in spans]
    total_rows = jnp.concatenate([jnp.broadcast_to(t, (chunk, width)) for t in totals], axis=0)

    q_in = (q * jnp.exp2(b)).astype(BF16)
    k_out = (key * jnp.exp2(total_rows - b)).astype(BF16)
    v = v_ref[...].astype(BF16)

    order = list(range(len(spans)))[::-1] if reverse else list(range(len(spans)))
    kv = [[_dot_tn(v[sp, sl], k_out[sp, sl]) for sl in lanes] for sp in spans]
    o_state = [[None] * len(lanes) for _ in spans]
    for hh, sl in enumerate(lanes):
        st = st_ref[hh]
        for ci in order:
            o_state[ci][hh] = _dot_nt(q_in[spans[ci], sl], st.astype(BF16))
            st = st * jnp.exp2(totals[ci][:, sl]) + kv[ci][hh]
        st_ref[hh] = st

    n_blk = chunk // SUBLANES
    wide_levels = [h for h in levels if h >= SUBLANES]
    narrow_levels = [h for h in levels if h < SUBLANES]

    def narrow_scores(li, ql, kl):
        out = []
        for sp in spans:
            row = []
            for sl in lanes:
                s = nmask_ref[li] * _dot_nt(ql[sp, sl], kl[sp, sl])
                row.append([s[k * SUBLANES:(k + 1) * SUBLANES, :] for k in range(n_blk)])
            out.append(row)
        return out

    a = narrow_scores(len(narrow_levels), q.astype(BF16), key.astype(BF16))
    for li, h in enumerate(narrow_levels):
        e = jnp.exp2(-jnp.abs(b - _seg_bcast(b, h, reverse)))
        new = narrow_scores(li, (q * e).astype(BF16), (key * e).astype(BF16))
        a = [[[x + y for x, y in zip(xb, yb)] for xb, yb in zip(xa, ya)] for xa, ya in zip(a, new)]

    for li, h in enumerate(wide_levels):
        q_parts, k_parts = [], []
        for g in range(rows // (2 * h)):
            lo = slice(g * 2 * h, g * 2 * h + h)
            hi = slice(g * 2 * h + h, (g + 1) * 2 * h)
            r = g * 2 * h + (h if reverse else h - 1)
            b_ref = jnp.broadcast_to(b[r:r + 1, :], (h, width))
            q_half, k_half = (lo, hi) if reverse else (hi, lo)
            q_parts.append(q[q_half] * jnp.exp2(b[q_half] - b_ref))
            k_part = key[k_half] * jnp.exp2(b_ref - b[k_half])
            zeros = jnp.zeros((h, width), F32)
            k_parts += [zeros, k_part] if reverse else [k_part, zeros]
        q_sel = jnp.concatenate(q_parts, axis=0).astype(BF16)
        k_hat = jnp.concatenate(k_parts, axis=0).astype(BF16)
        half = chunk // 2
        for ci, sp in enumerate(spans):
            for hh, sl in enumerate(lanes):
                s = wmask_ref[li] * _dot_nt(q_sel[ci * half:(ci + 1) * half, sl], k_hat[sp, sl])
                for j in range(chunk // (2 * h)):
                    first = (j * 2 * h + (0 if reverse else h)) // SUBLANES
                    for k in range(h // SUBLANES):
                        r0 = j * h + k * SUBLANES
                        a[ci][hh][first + k] = a[ci][hh][first + k] + s[r0:r0 + SUBLANES, :]

    for ci, sp in enumerate(spans):
        for hh, sl in enumerate(lanes):
            pairs = jnp.concatenate(a[ci][hh], axis=0).astype(BF16)
            o_ref[sp, sl] = o_state[ci][hh] + _dot(pairs, v[sp, sl])


def _hgrn_scan(p, lower_bound, rows, reverse):
    b, seq, lc = rows.batch, rows.seq, rows.ctx_len
    d = lower_bound.shape[0]
    c, hb = SCAN_CHUNK, SCAN_HEADS_PER_BLOCK
    r = c * SCAN_CHUNKS_PER_STEP
    wb = hb * HEAD_DIM
    ncb = d // wb
    assert lc % r == 0 and seq % r == 0
    nctx, nlat = lc // r, seq // r
    ctx0 = rows.n_lat // r
    f_blk = (2 if reverse else 1) * ncb
    v_blk = 3 * ncb

    def row(bi, s):
        if reverse:
            return jnp.where(s < nctx, ctx0 + bi * nctx + (nctx - 1 - s), bi * nlat + (nlat - 1 - (s - nctx)))
        return jnp.where(s < nctx, ctx0 + bi * nctx + s, bi * nlat + (s - nctx))

    tri, wmasks, nmasks = _scan_consts(c, reverse)
    return pl.pallas_call(
        functools.partial(_scan_kernel, reverse=reverse, chunk=c),
        out_shape=jax.ShapeDtypeStruct((rows.n, d), F32),
        grid=(b, ncb, nctx + nlat),
        in_specs=[
            pl.BlockSpec((r, wb), lambda bi, hi, s: (row(bi, s), hi)),
            pl.BlockSpec((r, wb), lambda bi, hi, s: (row(bi, s), f_blk + hi)),
            pl.BlockSpec((r, wb), lambda bi, hi, s: (row(bi, s), v_blk + hi)),
            pl.BlockSpec((1, wb), lambda bi, hi, s: (0, hi)),
            pl.BlockSpec(tri.shape, lambda bi, hi, s: (0, 0)),
            pl.BlockSpec(wmasks.shape, lambda bi, hi, s: (0, 0, 0)),
            pl.BlockSpec(nmasks.shape, lambda bi, hi, s: (0, 0, 0)),
        ],
        out_specs=pl.BlockSpec((r, wb), lambda bi, hi, s: (row(bi, s), hi)),
        scratch_shapes=[pltpu.VMEM((hb, HEAD_DIM, HEAD_DIM), F32)],
        compiler_params=_cparams("parallel", "parallel", "arbitrary"),
        name="hgrn_scan_bw" if reverse else "hgrn_scan_fw",
    )(p, p, p, lower_bound.reshape(1, d), tri, wmasks, nmasks)


def kernel(x, c, ctx, c_ctx, w_mod, b_mod, norm_gains, ffn_w_in, ffn_w_out, ab_w_in, qk_norm, ab_w_out,
           hgrn_w_in, hgrn_lb_logits, hgrn_o_norm, hgrn_w_out, final_norm):
    batch, seq, d = x.shape
    lc = ctx.shape[1]
    depth = w_mod.shape[0]
    rows = _Rows(batch, seq, lc, ROW_TILE)
    rows_r = _Rows(batch, seq, lc, READOUT_ROW_TILE)
    assert seq % GRID_W == 0 and batch + 1 <= SUBLANES

    c_rows = jnp.concatenate([c_ctx[None, :], c, jnp.zeros((SUBLANES - 1 - batch, d), F32)], axis=0)
    mods_all = _modulation(c_rows, w_mod, b_mod).reshape(depth, SUBLANES, 3, 3, d)

    lb_cum = jnp.cumsum(jax.nn.softmax(hgrn_lb_logits.astype(F32), axis=0), axis=0)
    lower_bounds = lb_cum - lb_cum[0]

    ffn_w_in_b = ffn_w_in.astype(BF16)
    ffn_w_out_b = ffn_w_out.astype(BF16)
    fin = final_norm.reshape(1, d)
    rope = _rope_tables(seq, ROW_TILE)

    h = (x.reshape(batch * seq, d), ctx.reshape(batch * lc, d))
    for layer in range(depth):
        last = layer == depth - 1
        mods = mods_all[layer]
        gains = norm_gains[layer].reshape(3, 1, d)
        h = _ffn(h, mods, 0, gains[0], ffn_w_in_b, ffn_w_out_b, layer, 0, fin, rows, rows.tiles, False)
        if layer % 2 == 0:
            e = layer // 2
            f_all, q_all, k_all, vt_all = _ab_proj(h, mods, gains[1], ab_w_in[e].astype(BF16), qk_norm[e], rope, rows)
            attn = _attention(q_all, k_all, vt_all, rows)
            attn = _attention(q_all, k_all, vt_all, rows, lat_out=attn)
            h = _ab_out(_fourier_latent(f_all, rows), _fourier_ctx(f_all, rows), attn,
                        ab_w_out[e].astype(BF16), h, mods, rows)
        else:
            o = layer // 2
            p = _hgrn_proj(h, mods, gains[1], hgrn_w_in[o].astype(BF16), rows)
            o_fw = _hgrn_scan(p, lower_bounds[layer], rows, False)
            o_bw = _hgrn_scan(p, lower_bounds[layer], rows, True)
            gain_o = jnp.tile(hgrn_o_norm[o], d // HEAD_DIM).reshape(1, d)
            n_t = rows_r.lat_tiles if last else rows_r.tiles
            h = _hgrn_out(o_fw, o_bw, p, gain_o, hgrn_w_out[o].astype(BF16), h, mods, rows_r, n_t)
        n_t = rows.lat_tiles if last else rows.tiles
        h = _ffn(h, mods, 2, gains[2], ffn_w_in_b, ffn_w_out_b, layer, 1, fin, rows, n_t, last)
    return h[:batch * seq].reshape(batch, seq, d)
```

```python
import functools
import math

import jax
import jax.numpy as jnp
import numpy as np
from jax import lax
from jax.experimental import pallas as pl
from jax.experimental.pallas import tpu as pltpu

F32 = jnp.float32
BF16 = jnp.bfloat16

EPS = 1e-6
N_MOD = 9
HEAD_DIM = 128
N_KV_HEADS = 4
FOURIER_WIDTH = 512
FOURIER_GROUP_DIM = 128
GRID_W = 64
ROPE_THETA = 10000.0
ROPE_AXIS_DIM = HEAD_DIM // 2
ATTN_SCALE = HEAD_DIM ** -0.5
LOG2_E = math.log2(math.e)

LANES = 128
SUBLANES = 8
BF16_SUBLANES = 16
VT_ROWS = HEAD_DIM + BF16_SUBLANES
VMEM_LIMIT_BYTES = 56 * 1024 * 1024

ROW_TILE = 512
READOUT_ROW_TILE = 256
FFN_TILE = 512
PROJ_COL_TILE = 2048
MOD_COL_TILE = 1024
ATTN_Q_TILE = 512
SCAN_CHUNK = 128
SCAN_HEADS_PER_BLOCK = 4
SCAN_CHUNKS_PER_STEP = 2
FFT_B = 128
FFT_B_BLOCK = 8
FFT_COL_TILE = 2048


def _cparams(*sem):
    return pltpu.CompilerParams(dimension_semantics=sem, vmem_limit_bytes=VMEM_LIMIT_BYTES)


def _dot(a, b):
    return jnp.dot(a, b, preferred_element_type=F32)


def _dot_hi(a, b):
    return jnp.dot(a, b, preferred_element_type=F32, precision=lax.Precision.HIGHEST)


def _dot_nt(a, b):
    return lax.dot_general(a, b, (((1,), (1,)), ((), ())), preferred_element_type=F32)


def _dot_tn(a, b):
    return lax.dot_general(a, b, (((0,), (0,)), ((), ())), preferred_element_type=F32)


def _silu(x):
    return x * jax.nn.sigmoid(x)


def _rms(x, gain):
    return x * lax.rsqrt(jnp.mean(x * x, axis=-1, keepdims=True) + EPS) * gain


def _ada(h, gain, shift, scale):
    return _rms(h, gain) * (1.0 + scale) + shift


def _mod_kernel(c_ref, w_ref, b_ref, o_ref):
    a = _silu(c_ref[...]).astype(BF16)
    o_ref[...] = _dot(a, w_ref[...].astype(BF16)) + b_ref[...]


def _modulation(c_rows, w_mod, b_mod):
    depth, d, nd = w_mod.shape
    tn = MOD_COL_TILE
    return pl.pallas_call(
        _mod_kernel,
        out_shape=jax.ShapeDtypeStruct((depth, SUBLANES, nd), F32),
        grid=(depth, nd // tn),
        in_specs=[
            pl.BlockSpec((SUBLANES, d), lambda l, j: (0, 0)),
            pl.BlockSpec((None, d, tn), lambda l, j: (l, 0, j)),
            pl.BlockSpec((None, 1, tn), lambda l, j: (l, 0, j)),
        ],
        out_specs=pl.BlockSpec((None, SUBLANES, tn), lambda l, j: (l, 0, j)),
        compiler_params=_cparams("parallel", "arbitrary"),
        name="modulation",
    )(c_rows, w_mod, b_mod.reshape(depth, 1, nd))


class _Rows:
    def __init__(self, batch, seq, ctx_len, tile):
        assert seq % tile == 0 and (batch * ctx_len) % tile == 0
        self.batch, self.seq, self.ctx_len, self.tile = batch, seq, ctx_len, tile
        self.n_lat = batch * seq
        self.n = self.n_lat + batch * ctx_len
        self.lat_tiles = self.n_lat // tile
        self.tiles = self.n // tile
        self.tiles_per_batch = seq // tile

    def group(self, i):
        return jnp.where(i < self.lat_tiles, 1 + i // self.tiles_per_batch, 0)


def _ffn_kernel(*refs, final, split_tiles):
    if split_tiles is None:
        h_ref, mod_ref, gain_ref, wa_ref, wb_ref, wo_ref, fin_ref, o_ref, xn_ref, acc_ref = refs
        load_h = lambda: h_ref[...]
    else:
        hl_ref, hc_ref, mod_ref, gain_ref, wa_ref, wb_ref, wo_ref, fin_ref, o_ref, xn_ref, acc_ref = refs
        load_h = lambda: jnp.where(pl.program_id(0) < split_tiles, hl_ref[...], hc_ref[...])
    i = pl.program_id(0)
    j = pl.program_id(1)

    @pl.when(j == 0)
    def _():
        xn_ref[...] = _ada(load_h(), gain_ref[...], mod_ref[0:1, :], mod_ref[1:2, :]).astype(BF16)

    @pl.when(jnp.logical_and(i == 0, j == 0))
    def _():
        acc_ref[...] = jnp.zeros_like(acc_ref)

    xn = xn_ref[...]
    a = _dot(xn, wa_ref[...])
    b = _dot(xn, wb_ref[...])
    g = (_silu(a) * b).astype(BF16)
    acc_ref[...] = jnp.where(j == 0, 0.0, acc_ref[...]) + _dot(g, wo_ref[...])

    @pl.when(j == pl.num_programs(1) - 1)
    def _():
        out = load_h() + 0.5 * mod_ref[2:3, :] * acc_ref[...]
        if final:
            out = _rms(out, fin_ref[...])
        o_ref[...] = out


def _ffn(h, mods, sub, gain, w_in, w_out, layer, which, fin, rows, n_tiles, final):
    f, d = w_out.shape[2:]
    tm, tf = rows.tile, FFN_TILE
    nf = f // tf
    if isinstance(h, tuple):
        lt = rows.lat_tiles
        row_specs = [pl.BlockSpec((tm, d), lambda i, j: (jnp.minimum(i, lt - 1), 0)),
                     pl.BlockSpec((tm, d), lambda i, j: (jnp.maximum(i - lt, 0), 0))]
        row_args, split_tiles = h, lt
    else:
        row_specs = [pl.BlockSpec((tm, d), lambda i, j: (i, 0))]
        row_args, split_tiles = (h,), None
    return pl.pallas_call(
        functools.partial(_ffn_kernel, final=final, split_tiles=split_tiles),
        out_shape=jax.ShapeDtypeStruct((n_tiles * tm, d), F32),
        grid=(n_tiles, nf),
        in_specs=row_specs + [
            pl.BlockSpec((None, None, 3, d), lambda i, j: (rows.group(i), sub, 0, 0)),
            pl.BlockSpec((1, d), lambda i, j: (0, 0)),
            pl.BlockSpec((None, None, d, tf), lambda i, j: (layer, which, 0, j)),
            pl.BlockSpec((None, None, d, tf), lambda i, j: (layer, which, 0, nf + j)),
            pl.BlockSpec((None, None, tf, d), lambda i, j: (layer, which, j, 0)),
            pl.BlockSpec((1, d), lambda i, j: (0, 0)),
        ],
        out_specs=pl.BlockSpec((tm, d), lambda i, j: (i, 0)),
        scratch_shapes=[pltpu.VMEM((tm, d), BF16), pltpu.VMEM((tm, d), F32)],
        compiler_params=_cparams("arbitrary", "arbitrary"),
        name="ffn",
    )(*row_args, mods, gain, w_in, w_in, w_out, fin)


def _rope_tables(seq, tile):
    t = np.arange(seq)
    inv_freq = ROPE_THETA ** (-np.arange(0, ROPE_AXIS_DIM, 2, dtype=np.float64) / ROPE_AXIS_DIM)
    ang = np.concatenate([(t // GRID_W)[:, None] * inv_freq, (t % GRID_W)[:, None] * inv_freq], axis=-1)
    nf = ROPE_AXIS_DIM // 2
    cos = np.cos(ang).reshape(seq, 2, 1, nf)
    sin = np.sin(ang).reshape(seq, 2, 1, nf)
    zero = np.zeros_like(sin)
    c_full = np.broadcast_to(cos, (seq, 2, 2, nf)).reshape(seq, HEAD_DIM)
    s_up = np.concatenate([-sin, zero], axis=2).reshape(seq, HEAD_DIM)
    s_dn = np.concatenate([zero, sin], axis=2).reshape(seq, HEAD_DIM)
    lat = np.concatenate([c_full, s_up, s_dn], axis=1)
    ident = np.concatenate([np.ones((tile, HEAD_DIM)), np.zeros((tile, 2 * HEAD_DIM))], axis=1)
    return jnp.asarray(np.concatenate([lat, ident], axis=0), dtype=F32)


def _norm_rope_heads(acc, gain, rope, post_scale=None):
    nf = ROPE_AXIS_DIM // 2
    c, s_up, s_dn = rope[:, :HEAD_DIM], rope[:, HEAD_DIM:2 * HEAD_DIM], rope[:, 2 * HEAD_DIM:]
    heads = []
    for hh in range(acc.shape[1] // HEAD_DIM):
        y = _rms(acc[:, hh * HEAD_DIM:(hh + 1) * HEAD_DIM], gain)
        y = y * c + pltpu.roll(y, HEAD_DIM - nf, 1) * s_up + pltpu.roll(y, nf, 1) * s_dn
        heads.append(y if post_scale is None else y * post_scale)
    return jnp.concatenate(heads, axis=1).astype(BF16)


def _ab_proj_kernel(h_ref, mod_ref, gain_ref, w_ref, qkn_ref, rope_ref, f_ref, q_ref, k_ref, vt_ref):
    xn = _ada(h_ref[...], gain_ref[...], mod_ref[0:1, :], mod_ref[1:2, :]).astype(BF16)
    c0 = f_ref.shape[1]
    c1 = c0 + q_ref.shape[1]
    c2 = c1 + k_ref.shape[1]
    f_ref[...] = _dot(xn, w_ref[:, :c0])
    q_ref[...] = _norm_rope_heads(_dot(xn, w_ref[:, c0:c1]), qkn_ref[0:1, :], rope_ref[...], ATTN_SCALE * LOG2_E)
    k_ref[...] = _norm_rope_heads(_dot(xn, w_ref[:, c1:c2]), qkn_ref[1:2, :], rope_ref[...])
    v = _dot(xn, w_ref[:, c2:])
    ones = jnp.ones((vt_ref.shape[1] - HEAD_DIM, vt_ref.shape[2]), BF16)
    for hh in range(N_KV_HEADS):
        vt_ref[hh, :HEAD_DIM, :] = v[:, hh * HEAD_DIM:(hh + 1) * HEAD_DIM].T.astype(BF16)
        vt_ref[hh, HEAD_DIM:, :] = ones


def _ab_proj(h, mods, gain, w_in, qk_norm, rope, rows):
    d = h.shape[1]
    tm = rows.tile
    kv_width = N_KV_HEADS * HEAD_DIM
    q_width = w_in.shape[1] - FOURIER_WIDTH - 2 * kv_width
    n = rows.n
    rope_blk = lambda i: (jnp.where(i < rows.lat_tiles, i % rows.tiles_per_batch, rows.tiles_per_batch), 0)
    return pl.pallas_call(
        _ab_proj_kernel,
        out_shape=(
            jax.ShapeDtypeStruct((n, FOURIER_WIDTH), F32),
            jax.ShapeDtypeStruct((n, q_width), BF16),
            jax.ShapeDtypeStruct((n, kv_width), BF16),
            jax.ShapeDtypeStruct((N_KV_HEADS, rows.tiles, VT_ROWS, tm), BF16),
        ),
        grid=(rows.tiles,),
        in_specs=[
            pl.BlockSpec((tm, d), lambda i: (i, 0)),
            pl.BlockSpec((None, None, 3, d), lambda i: (rows.group(i), 1, 0, 0)),
            pl.BlockSpec((1, d), lambda i: (0, 0)),
            pl.BlockSpec(w_in.shape, lambda i: (0, 0)),
            pl.BlockSpec((2, HEAD_DIM), lambda i: (0, 0)),
            pl.BlockSpec((tm, 3 * HEAD_DIM), rope_blk),
        ],
        out_specs=(
            pl.BlockSpec((tm, FOURIER_WIDTH), lambda i: (i, 0)),
            pl.BlockSpec((tm, q_width), lambda i: (i, 0)),
            pl.BlockSpec((tm, kv_width), lambda i: (i, 0)),
            pl.BlockSpec((N_KV_HEADS, None, VT_ROWS, tm), lambda i: (0, i, 0, 0)),
        ),
        compiler_params=_cparams("parallel"),
        name="ab_proj",
    )(h, mods, gain, w_in, qk_norm, rope)


def _attn_kernel(*refs, lat_chunks, group):
    if lat_chunks:
        q_ref, kc_ref, vtc_ref, kl_ref, vtl_ref, o_ref, acc_ref, s_ref = refs
    else:
        q_ref, kc_ref, vtc_ref, _, o_ref, acc_ref = refs
    tq = q_ref.shape[0]
    q = q_ref[...]
    qs = jnp.concatenate([q[:, g * HEAD_DIM:(g + 1) * HEAD_DIM] for g in range(group)], axis=0)
    nq = group * tq
    acc_ref[...] = jnp.zeros_like(acc_ref)

    def scores(k):
        return _dot_nt(k, qs)

    def update(s, vt, m):
        m_new = jnp.maximum(m, jnp.max(s, axis=0, keepdims=True))
        p = jnp.exp2(s - m_new).astype(BF16)
        acc_ref[...] = jnp.exp2(m - m_new) * acc_ref[...] + _dot(vt, p)
        return m_new

    m = update(scores(kc_ref[...]), vtc_ref[...], jnp.full((1, nq), -jnp.inf, F32))
    if lat_chunks:
        tk = vtl_ref.shape[2]

        def lat_scores(c):
            return scores(kl_ref[pl.ds(pl.multiple_of(c * tk, tk), tk), :])

        assert lat_chunks % 2 == 0
        s_ref[0] = lat_scores(0)

        def body(i, m):
            c = 2 * i
            s_ref[1] = lat_scores(c + 1)
            m = update(s_ref[0], vtl_ref[c], m)
            s_ref[0] = lat_scores(jnp.minimum(c + 2, lat_chunks - 1))
            return update(s_ref[1], vtl_ref[c + 1], m)

        m = lax.fori_loop(0, lat_chunks // 2, body, m)

    out = (acc_ref[:HEAD_DIM, :] / acc_ref[HEAD_DIM:HEAD_DIM + 1, :]).T
    o_ref[...] = jnp.concatenate([out[g * tq:(g + 1) * tq, :] for g in range(group)], axis=1).astype(BF16)


def _attention(q_all, k_all, vt_all, rows, lat_out=None):
    b, seq, lc, tile = rows.batch, rows.seq, rows.ctx_len, rows.tile
    group = q_all.shape[1] // (N_KV_HEADS * HEAD_DIM)
    gw = group * HEAD_DIM
    assert tile % lc == 0 and seq % tile == 0
    ctx_blk0 = rows.n_lat // lc
    kc_spec = pl.BlockSpec((lc, HEAD_DIM), lambda bi, hi, i: (ctx_blk0 + bi, hi))
    vtc_spec = pl.BlockSpec(
        (None, None, VT_ROWS, lc),
        lambda bi, hi, i: (hi, (rows.n_lat + bi * lc) // tile, 0, ((rows.n_lat + bi * lc) % tile) // lc))
    if lat_out is None:
        tq = ATTN_Q_TILE
        nqt = seq // tq
        lat_chunks = seq // tile
        q_spec = pl.BlockSpec((tq, gw), lambda bi, hi, i: (bi * nqt + i, hi))
        kl_spec = pl.BlockSpec((seq, HEAD_DIM), lambda bi, hi, i: (bi, hi))
        vtl_spec = pl.BlockSpec((None, lat_chunks, VT_ROWS, tile), lambda bi, hi, i: (hi, bi, 0, 0))
        in_specs = [q_spec, kc_spec, vtc_spec, kl_spec, vtl_spec]
        args = (q_all, k_all, vt_all, k_all, vt_all)
        aliases = {}
    else:
        tq = lc
        nqt = 1
        lat_chunks = 0
        q_spec = pl.BlockSpec((tq, gw), lambda bi, hi, i: (ctx_blk0 + bi, hi))
        in_specs = [q_spec, kc_spec, vtc_spec, pl.BlockSpec(memory_space=pl.ANY)]
        args = (q_all, k_all, vt_all, lat_out)
        aliases = {3: 0}
    return pl.pallas_call(
        functools.partial(_attn_kernel, lat_chunks=lat_chunks, group=group),
        out_shape=jax.ShapeDtypeStruct((rows.n, q_all.shape[1]), BF16),
        grid=(b, N_KV_HEADS, nqt),
        in_specs=in_specs,
        out_specs=q_spec,
        scratch_shapes=[pltpu.VMEM((VT_ROWS, group * tq), F32)]
        + ([pltpu.VMEM((2, tile, group * tq), F32)] if lat_chunks else []),
        input_output_aliases=aliases,
        compiler_params=_cparams("parallel", "parallel", "arbitrary"),
        name="attention_ctx" if lat_chunks == 0 else "attention_lat",
    )(*args)


def _dft_cs(n):
    idx = np.arange(n)
    ang = 2.0 * np.pi * ((idx[:, None] * idx[None, :]) % n) / n
    return np.cos(ang), np.sin(ang)


def _fft1_kernel(x_ref, fa_ref, tw_ref, o_ref, *, a):
    w = o_ref.shape[2]
    for r in range(o_ref.shape[0]):
        z = _dot_hi(fa_ref[...], x_ref[:, r * w:(r + 1) * w])
        zr, zi = z[:a], z[a:]
        tc = jnp.concatenate([tw_ref[r, 0]] * (w // LANES), axis=1)
        ts = jnp.concatenate([tw_ref[r, 1]] * (w // LANES), axis=1)
        o_ref[r, :a, :] = zr * tc - zi * ts
        o_ref[r, a:, :] = zr * ts + zi * tc


def _fft2_kernel(zr_ref, zi_ref, m2_ref, mc_ref, o_ref, *, scale):
    z = jnp.concatenate([zr_ref[...], zi_ref[...]], axis=0)
    v = _dot_hi(m2_ref[...], z)
    vr, vi = v[:FFT_B], v[FFT_B:]
    outs = []
    for g in range(o_ref.shape[1] // LANES):
        u = jnp.concatenate([vr[:, g * LANES:(g + 1) * LANES], vi[:, g * LANES:(g + 1) * LANES]], axis=1)
        outs.append(_dot_hi(u, mc_ref[...]))
    o_ref[...] = jnp.concatenate(outs, axis=1) * scale


def _fourier_latent(f_all, rows):
    b, seq = rows.batch, rows.seq
    w = FOURIER_WIDTH
    a = seq // FFT_B
    assert a % SUBLANES == 0 and FOURIER_GROUP_DIM == LANES
    ca, sa = _dft_cs(a)
    fa = jnp.asarray(np.concatenate([ca, sa], axis=0), dtype=F32)
    p1b = (np.arange(a)[None, :] * np.arange(FFT_B)[:, None]) % seq
    ang = 2.0 * np.pi * p1b / seq
    tw = np.stack([np.cos(ang), np.sin(ang)], axis=1)[..., None]
    tw = jnp.asarray(np.broadcast_to(tw, (FFT_B, 2, a, LANES)), dtype=F32)
    cb, sb = _dft_cs(FFT_B)
    m2 = jnp.asarray(np.block([[cb, -sb], [sb, cb]]), dtype=F32)
    cc, sc = _dft_cs(FOURIER_GROUP_DIM)
    mc = jnp.asarray(np.concatenate([cc, -sc], axis=0), dtype=F32)

    blk = FFT_B_BLOCK
    x2 = f_all.reshape(rows.n // FFT_B, FFT_B * w)
    z = pl.pallas_call(
        functools.partial(_fft1_kernel, a=a),
        out_shape=jax.ShapeDtypeStruct((b, FFT_B, 2 * a, w), F32),
        grid=(b, FFT_B // blk),
        in_specs=[
            pl.BlockSpec((a, blk * w), lambda bi, j: (bi, j)),
            pl.BlockSpec((2 * a, a), lambda bi, j: (0, 0)),
            pl.BlockSpec((blk, 2, a, LANES), lambda bi, j: (j, 0, 0, 0)),
        ],
        out_specs=pl.BlockSpec((None, blk, 2 * a, w), lambda bi, j: (bi, j, 0, 0)),
        compiler_params=_cparams("parallel", "parallel"),
        name="fourier_stage1",
    )(x2, fa, tw)

    tc = min(FFT_COL_TILE, a * w)
    ncol = (a * w) // tc
    z2 = z.reshape(b, FFT_B, 2 * a * w)
    y = pl.pallas_call(
        functools.partial(_fft2_kernel, scale=1.0 / math.sqrt(seq * FOURIER_GROUP_DIM)),
        out_shape=jax.ShapeDtypeStruct((b, FFT_B, a * w), F32),
        grid=(b, ncol),
        in_specs=[
            pl.BlockSpec((None, FFT_B, tc), lambda bi, j: (bi, 0, j)),
            pl.BlockSpec((None, FFT_B, tc), lambda bi, j: (bi, 0, ncol + j)),
            pl.BlockSpec((2 * FFT_B, 2 * FFT_B), lambda bi, j: (0, 0)),
            pl.BlockSpec((2 * LANES, LANES), lambda bi, j: (0, 0)),
        ],
        out_specs=pl.BlockSpec((None, FFT_B, tc), lambda bi, j: (bi, 0, j)),
        compiler_params=_cparams("parallel", "parallel"),
        name="fourier_stage2",
    )(z2, z2, m2, mc)
    return y.reshape(b * seq, w)


def _dft_ctx_kernel(x_ref, cn_ref, sn_ref, cc_ref, sc_ref, o_ref, *, scale):
    x = x_ref[...]
    outs = []
    for g in range(x.shape[1] // LANES):
        xg = x[:, g * LANES:(g + 1) * LANES]
        outs.append(_dot_hi(cn_ref[...], _dot_hi(xg, cc_ref[...])) - _dot_hi(sn_ref[...], _dot_hi(xg, sc_ref[...])))
    o_ref[...] = jnp.concatenate(outs, axis=1) * scale


def _fourier_ctx(f_all, rows):
    b, lc = rows.batch, rows.ctx_len
    w = FOURIER_WIDTH
    cn, sn = _dft_cs(lc)
    cc, sc = _dft_cs(FOURIER_GROUP_DIM)
    blk0 = rows.n_lat // lc
    mat = lambda m: pl.BlockSpec(m.shape, lambda bi: (0, 0))
    consts = [jnp.asarray(m, dtype=F32) for m in (cn, sn, cc, sc)]
    return pl.pallas_call(
        functools.partial(_dft_ctx_kernel, scale=1.0 / math.sqrt(lc * FOURIER_GROUP_DIM)),
        out_shape=jax.ShapeDtypeStruct((b * lc, w), F32),
        grid=(b,),
        in_specs=[pl.BlockSpec((lc, w), lambda bi: (blk0 + bi, 0))] + [mat(m) for m in consts],
        out_specs=pl.BlockSpec((lc, w), lambda bi: (bi, 0)),
        compiler_params=_cparams("parallel"),
        name="fourier_ctx",
    )(f_all, *consts)


def _ab_out_kernel(x1l_ref, x1c_ref, x2_ref, w_ref, h_ref, mod_ref, o_ref, *, lat_tiles):
    x1 = jnp.where(pl.program_id(0) < lat_tiles, x1l_ref[...], x1c_ref[...])
    x = jnp.concatenate([x1.astype(BF16), x2_ref[...]], axis=1)
    o_ref[...] = h_ref[...] + mod_ref[2:3, :] * _dot(x, w_ref[...])


def _ab_out(x1_lat, x1_ctx, x2, w_out, h, mods, rows):
    d = h.shape[1]
    tm = rows.tile
    lt = rows.lat_tiles
    return pl.pallas_call(
        functools.partial(_ab_out_kernel, lat_tiles=lt),
        out_shape=jax.ShapeDtypeStruct((rows.n, d), F32),
        grid=(rows.tiles,),
        in_specs=[
            pl.BlockSpec((tm, x1_lat.shape[1]), lambda i: (jnp.minimum(i, lt - 1), 0)),
            pl.BlockSpec((tm, x1_ctx.shape[1]), lambda i: (jnp.maximum(i - lt, 0), 0)),
            pl.BlockSpec((tm, x2.shape[1]), lambda i: (i, 0)),
            pl.BlockSpec(w_out.shape, lambda i: (0, 0)),
            pl.BlockSpec((tm, d), lambda i: (i, 0)),
            pl.BlockSpec((None, None, 3, d), lambda i: (rows.group(i), 1, 0, 0)),
        ],
        out_specs=pl.BlockSpec((tm, d), lambda i: (i, 0)),
        compiler_params=_cparams("parallel"),
        name="ab_out",
    )(x1_lat, x1_ctx, x2, w_out, h, mods)


def _hgrn_out_kernel(ofw_ref, obw_ref, g_ref, gain_ref, w_ref, h_ref, mod_ref, o_ref):
    o = ofw_ref[...] + obw_ref[...]
    parts = []
    for hh in range(o.shape[1] // HEAD_DIM):
        oh = o[:, hh * HEAD_DIM:(hh + 1) * HEAD_DIM]
        parts.append(oh * lax.rsqrt(jnp.mean(oh * oh, axis=-1, keepdims=True) + EPS))
    on = jnp.concatenate(parts, axis=1) * gain_ref[...]
    y = (on * jax.nn.sigmoid(g_ref[...])).astype(BF16)
    o_ref[...] = h_ref[...] + mod_ref[2:3, :] * _dot(y, w_ref[...])


def _hgrn_out(o_fw, o_bw, p, gain, w_out, h, mods, rows_r, n_tiles):
    d = h.shape[1]
    tm = rows_r.tile
    g_blk = (p.shape[1] - d) // d
    return pl.pallas_call(
        _hgrn_out_kernel,
        out_shape=jax.ShapeDtypeStruct((n_tiles * tm, d), F32),
        grid=(n_tiles,),
        in_specs=[
            pl.BlockSpec((tm, d), lambda i: (i, 0)),
            pl.BlockSpec((tm, d), lambda i: (i, 0)),
            pl.BlockSpec((tm, d), lambda i: (i, g_blk)),
            pl.BlockSpec((1, d), lambda i: (0, 0)),
            pl.BlockSpec(w_out.shape, lambda i: (0, 0)),
            pl.BlockSpec((tm, d), lambda i: (i, 0)),
            pl.BlockSpec((None, None, 3, d), lambda i: (rows_r.group(i), 1, 0, 0)),
        ],
        out_specs=pl.BlockSpec((tm, d), lambda i: (i, 0)),
        compiler_params=_cparams("parallel"),
        name="hgrn_out",
    )(o_fw, o_bw, p, gain, w_out, h, mods)


def _hgrn_proj_kernel(h_ref, mod_ref, gain_ref, w_ref, o_ref, xn_ref, *, n_silu):
    j = pl.program_id(1)

    @pl.when(j == 0)
    def _():
        xn_ref[...] = _ada(h_ref[...], gain_ref[...], mod_ref[0:1, :], mod_ref[1:2, :]).astype(BF16)

    acc = _dot(xn_ref[...], w_ref[...])
    o_ref[...] = jnp.where(j < n_silu, _silu(acc), acc)


def _hgrn_proj(h, mods, gain, w_in, rows):
    d = h.shape[1]
    tm, tn = rows.tile, PROJ_COL_TILE
    nw = w_in.shape[1]
    return pl.pallas_call(
        functools.partial(_hgrn_proj_kernel, n_silu=d // tn),
        out_shape=jax.ShapeDtypeStruct((rows.n, nw), F32),
        grid=(rows.tiles, nw // tn),
        in_specs=[
            pl.BlockSpec((tm, d), lambda i, j: (i, 0)),
            pl.BlockSpec((None, None, 3, d), lambda i, j: (rows.group(i), 1, 0, 0)),
            pl.BlockSpec((1, d), lambda i, j: (0, 0)),
            pl.BlockSpec((d, tn), lambda i, j: (0, j)),
        ],
        out_specs=pl.BlockSpec((tm, tn), lambda i, j: (i, j)),
        scratch_shapes=[pltpu.VMEM((tm, d), BF16)],
        compiler_params=_cparams("parallel", "arbitrary"),
        name="hgrn_proj",
    )(h, mods, gain, w_in)


def _scan_levels(chunk):
    return [chunk >> (i + 1) for i in range(int(math.log2(chunk)))]


def _scan_consts(chunk, reverse):
    t = np.arange(chunk)[:, None]
    s = np.arange(chunk)[None, :]
    tri = (s >= t) if reverse else (s <= t)
    wide, narrow = [], []
    for h in _scan_levels(chunk):
        same = (t // (2 * h)) == (s // (2 * h))
        t_up = (t // h) % 2 == 1
        s_up = (s // h) % 2 == 1
        mask = same & ((~t_up & s_up) if reverse else (t_up & ~s_up))
        if h >= SUBLANES:
            wide.append(mask[np.nonzero(~t_up[:, 0] if reverse else t_up[:, 0])[0]])
        else:
            narrow.append(mask)
    narrow.append(t == s)
    return (jnp.asarray(tri, dtype=BF16), jnp.asarray(np.stack(wide), dtype=F32),
            jnp.asarray(np.stack(narrow), dtype=F32))


def _seg_bcast(x, h, reverse):
    c, w = x.shape
    off = h if reverse else h - 1
    if 2 * h >= 2 * SUBLANES:
        pieces = [jnp.broadcast_to(x[g * 2 * h + off:g * 2 * h + off + 1, :], (2 * h, w)) for g in range(c // (2 * h))]
        return pieces[0] if len(pieces) == 1 else jnp.concatenate(pieces, axis=0)
    x3 = x.reshape(c // SUBLANES, SUBLANES, w)
    sub = lax.broadcasted_iota(jnp.int32, x3.shape, 1)
    y = None
    for g in range(SUBLANES // (2 * h)):
        piece = jnp.broadcast_to(x3[:, g * 2 * h + off:g * 2 * h + off + 1, :], x3.shape)
        y = piece if y is None else jnp.where(sub >= g * 2 * h, piece, y)
    return y.reshape(c, w)


def _scan_kernel(q_ref, f_ref, v_ref, lb_ref, tri_ref, wmask_ref, nmask_ref, o_ref, st_ref, *, reverse, chunk):
    @pl.when(pl.program_id(2) == 0)
    def _():
        st_ref[...] = jnp.zeros_like(st_ref)

    rows, width = q_ref.shape
    levels = _scan_levels(chunk)
    spans = [slice(ci * chunk, (ci + 1) * chunk) for ci in range(rows // chunk)]
    lanes = [slice(hh * HEAD_DIM, (hh + 1) * HEAD_DIM) for hh in range(width // HEAD_DIM)]
    q = q_ref[...]
    fl = f_ref[...]
    lb = lb_ref[...]
    key = (1.0 - lb) * jax.nn.sigmoid(-fl)
    log_sig = jnp.minimum(fl, 0.0) - jnp.log(1.0 + jnp.exp(-jnp.abs(fl)))
    x1 = jnp.log(lb)
    x2 = jnp.log1p(-lb) + log_sig
    delta = x1 - x2
    lf = jnp.where(jnp.isnan(delta), x1 + x2, jnp.maximum(x1, x2) + jnp.log(1.0 + jnp.exp(-jnp.abs(delta))))
    lf = lf * LOG2_E
    l1 = lf.astype(BF16)
    r1 = lf - l1.astype(F32)
    l2 = r1.astype(BF16)
    l3 = (r1 - l2.astype(F32)).astype(BF16)
    parts = jnp.concatenate([l1, l2, l3], axis=1)
    tri = tri_ref[...]
    cs = jnp.concatenate([_dot(tri, parts[sp]) for sp in spans], axis=0)
    b = cs[:, :width] + cs[:, width:2 * width] + cs[:, 2 * width:]
    totals = [b[sp.start:sp.start + 1, :] if reverse else b[sp.stop - 1:sp.stop, :] for sp in spans]
    total_rows = jnp.concatenate([jnp.broadcast_to(t, (chunk, width)) for t in totals], axis=0)

    q_in = (q * jnp.exp2(b)).astype(BF16)
    k_out = (key * jnp.exp2(total_rows - b)).astype(BF16)
    v = v_ref[...].astype(BF16)

    order = list(range(len(spans)))[::-1] if reverse else list(range(len(spans)))
    kv = [[_dot_tn(v[sp, sl], k_out[sp, sl]) for sl in lanes] for sp in spans]
    o_state = [[None] * len(lanes) for _ in spans]
    for hh, sl in enumerate(lanes):
        st = st_ref[hh]
        for ci in order:
            o_state[ci][hh] = _dot_nt(q_in[spans[ci], sl], st.astype(BF16))
            st = st * jnp.exp2(totals[ci][:, sl]) + kv[ci][hh]
        st_ref[hh] = st

    n_blk = chunk // SUBLANES
    wide_levels = [h for h in levels if h >= SUBLANES]
    narrow_levels = [h for h in levels if h < SUBLANES]

    def narrow_scores(li, ql, kl):
        out = []
        for sp in spans:
            row = []
            for sl in lanes:
                s = nmask_ref[li] * _dot_nt(ql[sp, sl], kl[sp, sl])
                row.append([s[k * SUBLANES:(k + 1) * SUBLANES, :] for k in range(n_blk)])
            out.append(row)
        return out

    a = narrow_scores(len(narrow_levels), q.astype(BF16), key.astype(BF16))
    for li, h in enumerate(narrow_levels):
        e = jnp.exp2(-jnp.abs(b - _seg_bcast(b, h, reverse)))
        new = narrow_scores(li, (q * e).astype(BF16), (key * e).astype(BF16))
        a = [[[x + y for x, y in zip(xb, yb)] for xb, yb in zip(xa, ya)] for xa, ya in zip(a, new)]

    for li, h in enumerate(wide_levels):
        q_parts, k_parts = [], []
        for g in range(rows // (2 * h)):
            lo = slice(g * 2 * h, g * 2 * h + h)
            hi = slice(g * 2 * h + h, (g + 1) * 2 * h)
            r = g * 2 * h + (h if reverse else h - 1)
            b_ref = jnp.broadcast_to(b[r:r + 1, :], (h, width))
            q_half, k_half = (lo, hi) if reverse else (hi, lo)
            q_parts.append(q[q_half] * jnp.exp2(b[q_half] - b_ref))
            k_part = key[k_half] * jnp.exp2(b_ref - b[k_half])
            zeros = jnp.zeros((h, width), F32)
            k_parts += [zeros, k_part] if reverse else [k_part, zeros]
        q_sel = jnp.concatenate(q_parts, axis=0).astype(BF16)
        k_hat = jnp.concatenate(k_parts, axis=0).astype(BF16)
        half = chunk // 2
        for ci, sp in enumerate(spans):
            for hh, sl in enumerate(lanes):
                s = wmask_ref[li] * _dot_nt(q_sel[ci * half:(ci + 1) * half, sl], k_hat[sp, sl])
                for j in range(chunk // (2 * h)):
                    first = (j * 2 * h + (0 if reverse else h)) // SUBLANES
                    for k in range(h // SUBLANES):
                        r0 = j * h + k * SUBLANES
                        a[ci][hh][first + k] = a[ci][hh][first + k] + s[r0:r0 + SUBLANES, :]

    for ci, sp in enumerate(spans):
        for hh, sl in enumerate(lanes):
            pairs = jnp.concatenate(a[ci][hh], axis=0).astype(BF16)
            o_ref[sp, sl] = o_state[ci][hh] + _dot(pairs, v[sp, sl])


def _hgrn_scan(p, lower_bound, rows, reverse):
    b, seq, lc = rows.batch, rows.seq, rows.ctx_len
    d = lower_bound.shape[0]
    c, hb = SCAN_CHUNK, SCAN_HEADS_PER_BLOCK
    r = c * SCAN_CHUNKS_PER_STEP
    wb = hb * HEAD_DIM
    ncb = d // wb
    assert lc % r == 0 and seq % r == 0
    nctx, nlat = lc // r, seq // r
    ctx0 = rows.n_lat // r
    f_blk = (2 if reverse else 1) * ncb
    v_blk = 3 * ncb

    def row(bi, s):
        if reverse:
            return jnp.where(s < nctx, ctx0 + bi * nctx + (nctx - 1 - s), bi * nlat + (nlat - 1 - (s - nctx)))
        return jnp.where(s < nctx, ctx0 + bi * nctx + s, bi * nlat + (s - nctx))

    tri, wmasks, nmasks = _scan_consts(c, reverse)
    return pl.pallas_call(
        functools.partial(_scan_kernel, reverse=reverse, chunk=c),
        out_shape=jax.ShapeDtypeStruct((rows.n, d), F32),
        grid=(b, ncb, nctx + nlat),
        in_specs=[
            pl.BlockSpec((r, wb), lambda bi, hi, s: (row(bi, s), hi)),
            pl.BlockSpec((r, wb), lambda bi, hi, s: (row(bi, s), f_blk + hi)),
            pl.BlockSpec((r, wb), lambda bi, hi, s: (row(bi, s), v_blk + hi)),
            pl.BlockSpec((1, wb), lambda bi, hi, s: (0, hi)),
            pl.BlockSpec(tri.shape, lambda bi, hi, s: (0, 0)),
            pl.BlockSpec(wmasks.shape, lambda bi, hi, s: (0, 0, 0)),
            pl.BlockSpec(nmasks.shape, lambda bi, hi, s: (0, 0, 0)),
        ],
        out_specs=pl.BlockSpec((r, wb), lambda bi, hi, s: (row(bi, s), hi)),
        scratch_shapes=[pltpu.VMEM((hb, HEAD_DIM, HEAD_DIM), F32)],
        compiler_params=_cparams("parallel", "parallel", "arbitrary"),
        name="hgrn_scan_bw" if reverse else "hgrn_scan_fw",
    )(p, p, p, lower_bound.reshape(1, d), tri, wmasks, nmasks)


def kernel(x, c, ctx, c_ctx, w_mod, b_mod, norm_gains, ffn_w_in, ffn_w_out, ab_w_in, qk_norm, ab_w_out,
           hgrn_w_in, hgrn_lb_logits, hgrn_o_norm, hgrn_w_out, final_norm):
    batch, seq, d = x.shape
    lc = ctx.shape[1]
    depth = w_mod.shape[0]
    rows = _Rows(batch, seq, lc, ROW_TILE)
    rows_r = _Rows(batch, seq, lc, READOUT_ROW_TILE)
    assert seq % GRID_W == 0 and batch + 1 <= SUBLANES

    c_rows = jnp.concatenate([c_ctx[None, :], c, jnp.zeros((SUBLANES - 1 - batch, d), F32)], axis=0)
    mods_all = _modulation(c_rows, w_mod, b_mod).reshape(depth, SUBLANES, 3, 3, d)

    lb_cum = jnp.cumsum(jax.nn.softmax(hgrn_lb_logits.astype(F32), axis=0), axis=0)
    lower_bounds = lb_cum - lb_cum[0]

    ffn_w_in_b = ffn_w_in.astype(BF16)
    ffn_w_out_b = ffn_w_out.astype(BF16)
    fin = final_norm.reshape(1, d)
    rope = _rope_tables(seq, ROW_TILE)

    h = (x.reshape(batch * seq, d), ctx.reshape(batch * lc, d))
    for layer in range(depth):
        last = layer == depth - 1
        mods = mods_all[layer]
        gains = norm_gains[layer].reshape(3, 1, d)
        h = _ffn(h, mods, 0, gains[0], ffn_w_in_b, ffn_w_out_b, layer, 0, fin, rows, rows.tiles, False)
        if layer % 2 == 0:
            e = layer // 2
            f_all, q_all, k_all, vt_all = _ab_proj(h, mods, gains[1], ab_w_in[e].astype(BF16), qk_norm[e], rope, rows)
            attn = _attention(q_all, k_all, vt_all, rows)
            attn = _attention(q_all, k_all, vt_all, rows, lat_out=attn)
            h = _ab_out(_fourier_latent(f_all, rows), _fourier_ctx(f_all, rows), attn,
                        ab_w_out[e].astype(BF16), h, mods, rows)
        else:
            o = layer // 2
            p = _hgrn_proj(h, mods, gains[1], hgrn_w_in[o].astype(BF16), rows)
            o_fw = _hgrn_scan(p, lower_bounds[layer], rows, False)
            o_bw = _hgrn_scan(p, lower_bounds[layer], rows, True)
            gain_o = jnp.tile(hgrn_o_norm[o], d // HEAD_DIM).reshape(1, d)
            n_t = rows_r.lat_tiles if last else rows_r.tiles
            h = _hgrn_out(o_fw, o_bw, p, gain_o, hgrn_w_out[o].astype(BF16), h, mods, rows_r, n_t)
        n_t = rows.lat_tiles if last else rows.tiles
        h = _ffn(h, mods, 2, gains[2], ffn_w_in_b, ffn_w_out_b, layer, 1, fin, rows, n_t, last)
    return h[:batch * seq].reshape(batch, seq, d)
```

```python
import functools
import math

import jax
import jax.numpy as jnp
import numpy as np
from jax import lax
from jax.experimental import pallas as pl
from jax.experimental.pallas import tpu as pltpu

F32 = jnp.float32
BF16 = jnp.bfloat16

EPS = 1e-6
N_MOD = 9
HEAD_DIM = 128
N_KV_HEADS = 4
FOURIER_WIDTH = 512
FOURIER_GROUP_DIM = 128
GRID_W = 64
ROPE_THETA = 10000.0
ROPE_AXIS_DIM = HEAD_DIM // 2
ATTN_SCALE = HEAD_DIM ** -0.5
LOG2_E = math.log2(math.e)

LANES = 128
SUBLANES = 8
BF16_SUBLANES = 16
VT_ROWS = HEAD_DIM + BF16_SUBLANES
VMEM_LIMIT_BYTES = 56 * 1024 * 1024

ROW_TILE = 512
READOUT_ROW_TILE = 256
FFN_TILE = 512
PROJ_COL_TILE = 2048
MOD_COL_TILE = 1024
ATTN_Q_TILE = 1024
SCAN_CHUNK = 128
SCAN_HEADS_PER_BLOCK = 8
SCAN_CHUNKS_PER_STEP = 2
FFT_B = 128
FFT_B_BLOCK = 8
FFT_COL_TILE = 2048


def _cparams(*sem):
    return pltpu.CompilerParams(dimension_semantics=sem, vmem_limit_bytes=VMEM_LIMIT_BYTES)


def _dot(a, b):
    return jnp.dot(a, b, preferred_element_type=F32)


def _dot_hi(a, b):
    return jnp.dot(a, b, preferred_element_type=F32, precision=lax.Precision.HIGHEST)


def _dot_nt(a, b):
    return lax.dot_general(a, b, (((1,), (1,)), ((), ())), preferred_element_type=F32)


def _dot_tn(a, b):
    return lax.dot_general(a, b, (((0,), (0,)), ((), ())), preferred_element_type=F32)


def _silu(x):
    return x * jax.nn.sigmoid(x)


def _rms(x, gain):
    return x * lax.rsqrt(jnp.mean(x * x, axis=-1, keepdims=True) + EPS) * gain


def _ada(h, gain, shift, scale):
    return _rms(h, gain) * (1.0 + scale) + shift


def _mod_kernel(c_ref, w_ref, b_ref, o_ref):
    a = _silu(c_ref[...]).astype(BF16)
    o_ref[...] = _dot(a, w_ref[...].astype(BF16)) + b_ref[...]


def _modulation(c_rows, w_mod, b_mod):
    depth, d, nd = w_mod.shape
    tn = MOD_COL_TILE
    return pl.pallas_call(
        _mod_kernel,
        out_shape=jax.ShapeDtypeStruct((depth, SUBLANES, nd), F32),
        grid=(depth, nd // tn),
        in_specs=[
            pl.BlockSpec((SUBLANES, d), lambda l, j: (0, 0)),
            pl.BlockSpec((None, d, tn), lambda l, j: (l, 0, j)),
            pl.BlockSpec((None, 1, tn), lambda l, j: (l, 0, j)),
        ],
        out_specs=pl.BlockSpec((None, SUBLANES, tn), lambda l, j: (l, 0, j)),
        compiler_params=_cparams("parallel", "arbitrary"),
        name="modulation",
    )(c_rows, w_mod, b_mod.reshape(depth, 1, nd))


class _Rows:
    def __init__(self, batch, seq, ctx_len, tile):
        assert seq % tile == 0 and (batch * ctx_len) % tile == 0
        self.batch, self.seq, self.ctx_len, self.tile = batch, seq, ctx_len, tile
        self.n_lat = batch * seq
        self.n = self.n_lat + batch * ctx_len
        self.lat_tiles = self.n_lat // tile
        self.tiles = self.n // tile
        self.tiles_per_batch = seq // tile

    def group(self, i):
        return jnp.where(i < self.lat_tiles, 1 + i // self.tiles_per_batch, 0)


def _ffn_kernel(*refs, final, split_tiles):
    if split_tiles is None:
        h_ref, mod_ref, gain_ref, wa_ref, wb_ref, wo_ref, fin_ref, o_ref, xn_ref, acc_ref = refs
        load_h = lambda: h_ref[...]
    else:
        hl_ref, hc_ref, mod_ref, gain_ref, wa_ref, wb_ref, wo_ref, fin_ref, o_ref, xn_ref, acc_ref = refs
        load_h = lambda: jnp.where(pl.program_id(0) < split_tiles, hl_ref[...], hc_ref[...])
    i = pl.program_id(0)
    j = pl.program_id(1)

    @pl.when(j == 0)
    def _():
        xn_ref[...] = _ada(load_h(), gain_ref[...], mod_ref[0:1, :], mod_ref[1:2, :]).astype(BF16)

    @pl.when(jnp.logical_and(i == 0, j == 0))
    def _():
        acc_ref[...] = jnp.zeros_like(acc_ref)

    xn = xn_ref[...]
    a = _dot(xn, wa_ref[...])
    b = _dot(xn, wb_ref[...])
    g = (_silu(a) * b).astype(BF16)
    acc_ref[...] = jnp.where(j == 0, 0.0, acc_ref[...]) + _dot(g, wo_ref[...])

    @pl.when(j == pl.num_programs(1) - 1)
    def _():
        out = load_h() + 0.5 * mod_ref[2:3, :] * acc_ref[...]
        if final:
            out = _rms(out, fin_ref[...])
        o_ref[...] = out


def _ffn(h, mods, sub, gain, w_in, w_out, layer, which, fin, rows, n_tiles, final):
    f, d = w_out.shape[2:]
    tm, tf = rows.tile, FFN_TILE
    nf = f // tf
    if isinstance(h, tuple):
        lt = rows.lat_tiles
        row_specs = [pl.BlockSpec((tm, d), lambda i, j: (jnp.minimum(i, lt - 1), 0)),
                     pl.BlockSpec((tm, d), lambda i, j: (jnp.maximum(i - lt, 0), 0))]
        row_args, split_tiles = h, lt
    else:
        row_specs = [pl.BlockSpec((tm, d), lambda i, j: (i, 0))]
        row_args, split_tiles = (h,), None
    return pl.pallas_call(
        functools.partial(_ffn_kernel, final=final, split_tiles=split_tiles),
        out_shape=jax.ShapeDtypeStruct((n_tiles * tm, d), F32),
        grid=(n_tiles, nf),
        in_specs=row_specs + [
            pl.BlockSpec((None, None, 3, d), lambda i, j: (rows.group(i), sub, 0, 0)),
            pl.BlockSpec((1, d), lambda i, j: (0, 0)),
            pl.BlockSpec((None, None, d, tf), lambda i, j: (layer, which, 0, j)),
            pl.BlockSpec((None, None, d, tf), lambda i, j: (layer, which, 0, nf + j)),
            pl.BlockSpec((None, None, tf, d), lambda i, j: (layer, which, j, 0)),
            pl.BlockSpec((1, d), lambda i, j: (0, 0)),
        ],
        out_specs=pl.BlockSpec((tm, d), lambda i, j: (i, 0)),
        scratch_shapes=[pltpu.VMEM((tm, d), BF16), pltpu.VMEM((tm, d), F32)],
        compiler_params=_cparams("arbitrary", "arbitrary"),
        name="ffn",
    )(*row_args, mods, gain, w_in, w_in, w_out, fin)


def _rope_tables(seq, tile):
    t = np.arange(seq)
    inv_freq = ROPE_THETA ** (-np.arange(0, ROPE_AXIS_DIM, 2, dtype=np.float64) / ROPE_AXIS_DIM)
    ang = np.concatenate([(t // GRID_W)[:, None] * inv_freq, (t % GRID_W)[:, None] * inv_freq], axis=-1)
    nf = ROPE_AXIS_DIM // 2
    cos = np.cos(ang).reshape(seq, 2, 1, nf)
    sin = np.sin(ang).reshape(seq, 2, 1, nf)
    zero = np.zeros_like(sin)
    c_full = np.broadcast_to(cos, (seq, 2, 2, nf)).reshape(seq, HEAD_DIM)
    s_up = np.concatenate([-sin, zero], axis=2).reshape(seq, HEAD_DIM)
    s_dn = np.concatenate([zero, sin], axis=2).reshape(seq, HEAD_DIM)
    lat = np.concatenate([c_full, s_up, s_dn], axis=1)
    ident = np.concatenate([np.ones((tile, HEAD_DIM)), np.zeros((tile, 2 * HEAD_DIM))], axis=1)
    return jnp.asarray(np.concatenate([lat, ident], axis=0), dtype=F32)


def _norm_rope_heads(acc, gain, rope, post_scale=None):
    nf = ROPE_AXIS_DIM // 2
    c, s_up, s_dn = rope[:, :HEAD_DIM], rope[:, HEAD_DIM:2 * HEAD_DIM], rope[:, 2 * HEAD_DIM:]
    heads = []
    for hh in range(acc.shape[1] // HEAD_DIM):
        y = _rms(acc[:, hh * HEAD_DIM:(hh + 1) * HEAD_DIM], gain)
        y = y * c + pltpu.roll(y, HEAD_DIM - nf, 1) * s_up + pltpu.roll(y, nf, 1) * s_dn
        heads.append(y if post_scale is None else y * post_scale)
    return jnp.concatenate(heads, axis=1).astype(BF16)


def _ab_proj_kernel(h_ref, mod_ref, gain_ref, w_ref, qkn_ref, rope_ref, f_ref, q_ref, k_ref, vt_ref):
    xn = _ada(h_ref[...], gain_ref[...], mod_ref[0:1, :], mod_ref[1:2, :]).astype(BF16)
    c0 = f_ref.shape[1]
    c1 = c0 + q_ref.shape[1]
    c2 = c1 + k_ref.shape[1]
    f_ref[...] = _dot(xn, w_ref[:, :c0])
    q_ref[...] = _norm_rope_heads(_dot(xn, w_ref[:, c0:c1]), qkn_ref[0:1, :], rope_ref[...], ATTN_SCALE * LOG2_E)
    k_ref[...] = _norm_rope_heads(_dot(xn, w_ref[:, c1:c2]), qkn_ref[1:2, :], rope_ref[...])
    v = _dot(xn, w_ref[:, c2:])
    ones = jnp.ones((vt_ref.shape[1] - HEAD_DIM, vt_ref.shape[2]), BF16)
    for hh in range(N_KV_HEADS):
        vt_ref[hh, :HEAD_DIM, :] = v[:, hh * HEAD_DIM:(hh + 1) * HEAD_DIM].T.astype(BF16)
        vt_ref[hh, HEAD_DIM:, :] = ones


def _ab_proj(h, mods, gain, w_in, qk_norm, rope, rows):
    d = h.shape[1]
    tm = rows.tile
    kv_width = N_KV_HEADS * HEAD_DIM
    q_width = w_in.shape[1] - FOURIER_WIDTH - 2 * kv_width
    n = rows.n
    rope_blk = lambda i: (jnp.where(i < rows.lat_tiles, i % rows.tiles_per_batch, rows.tiles_per_batch), 0)
    return pl.pallas_call(
        _ab_proj_kernel,
        out_shape=(
            jax.ShapeDtypeStruct((n, FOURIER_WIDTH), F32),
            jax.ShapeDtypeStruct((n, q_width), BF16),
            jax.ShapeDtypeStruct((n, kv_width), BF16),
            jax.ShapeDtypeStruct((N_KV_HEADS, rows.tiles, VT_ROWS, tm), BF16),
        ),
        grid=(rows.tiles,),
        in_specs=[
            pl.BlockSpec((tm, d), lambda i: (i, 0)),
            pl.BlockSpec((None, None, 3, d), lambda i: (rows.group(i), 1, 0, 0)),
            pl.BlockSpec((1, d), lambda i: (0, 0)),
            pl.BlockSpec(w_in.shape, lambda i: (0, 0)),
            pl.BlockSpec((2, HEAD_DIM), lambda i: (0, 0)),
            pl.BlockSpec((tm, 3 * HEAD_DIM), rope_blk),
        ],
        out_specs=(
            pl.BlockSpec((tm, FOURIER_WIDTH), lambda i: (i, 0)),
            pl.BlockSpec((tm, q_width), lambda i: (i, 0)),
            pl.BlockSpec((tm, kv_width), lambda i: (i, 0)),
            pl.BlockSpec((N_KV_HEADS, None, VT_ROWS, tm), lambda i: (0, i, 0, 0)),
        ),
        compiler_params=_cparams("parallel"),
        name="ab_proj",
    )(h, mods, gain, w_in, qk_norm, rope)


def _attn_kernel(*refs, lat_chunks, group):
    if lat_chunks:
        q_ref, kc_ref, vtc_ref, kl_ref, vtl_ref, o_ref, acc_ref, s_ref = refs
    else:
        q_ref, kc_ref, vtc_ref, _, o_ref, acc_ref = refs
    tq = q_ref.shape[0]
    q = q_ref[...]
    qs = jnp.concatenate([q[:, g * HEAD_DIM:(g + 1) * HEAD_DIM] for g in range(group)], axis=0)
    nq = group * tq
    acc_ref[...] = jnp.zeros_like(acc_ref)

    def scores(k):
        return _dot_nt(k, qs)

    def update(s, vt, m):
        m_new = jnp.maximum(m, jnp.max(s, axis=0, keepdims=True))
        p = jnp.exp2(s - m_new).astype(BF16)
        acc_ref[...] = jnp.exp2(m - m_new) * acc_ref[...] + _dot(vt, p)
        return m_new

    m = update(scores(kc_ref[...]), vtc_ref[...], jnp.full((1, nq), -jnp.inf, F32))
    if lat_chunks:
        tk = vtl_ref.shape[2]

        def lat_scores(c):
            return scores(kl_ref[pl.ds(pl.multiple_of(c * tk, tk), tk), :])

        assert lat_chunks % 2 == 0
        s_ref[0] = lat_scores(0)

        def body(i, m):
            c = 2 * i
            s_ref[1] = lat_scores(c + 1)
            m = update(s_ref[0], vtl_ref[c], m)
            s_ref[0] = lat_scores(jnp.minimum(c + 2, lat_chunks - 1))
            return update(s_ref[1], vtl_ref[c + 1], m)

        m = lax.fori_loop(0, lat_chunks // 2, body, m)

    out = (acc_ref[:HEAD_DIM, :] / acc_ref[HEAD_DIM:HEAD_DIM + 1, :]).T
    o_ref[...] = jnp.concatenate([out[g * tq:(g + 1) * tq, :] for g in range(group)], axis=1).astype(BF16)


def _attention(q_all, k_all, vt_all, rows, lat_out=None):
    b, seq, lc, tile = rows.batch, rows.seq, rows.ctx_len, rows.tile
    group = q_all.shape[1] // (N_KV_HEADS * HEAD_DIM)
    gw = group * HEAD_DIM
    assert tile % lc == 0 and seq % tile == 0
    ctx_blk0 = rows.n_lat // lc
    kc_spec = pl.BlockSpec((lc, HEAD_DIM), lambda bi, hi, i: (ctx_blk0 + bi, hi))
    vtc_spec = pl.BlockSpec(
        (None, None, VT_ROWS, lc),
        lambda bi, hi, i: (hi, (rows.n_lat + bi * lc) // tile, 0, ((rows.n_lat + bi * lc) % tile) // lc))
    if lat_out is None:
        tq = ATTN_Q_TILE
        nqt = seq // tq
        lat_chunks = seq // tile
        q_spec = pl.BlockSpec((tq, gw), lambda bi, hi, i: (bi * nqt + i, hi))
        kl_spec = pl.BlockSpec((seq, HEAD_DIM), lambda bi, hi, i: (bi, hi))
        vtl_spec = pl.BlockSpec((None, lat_chunks, VT_ROWS, tile), lambda bi, hi, i: (hi, bi, 0, 0))
        in_specs = [q_spec, kc_spec, vtc_spec, kl_spec, vtl_spec]
        args = (q_all, k_all, vt_all, k_all, vt_all)
        aliases = {}
    else:
        tq = lc
        nqt = 1
        lat_chunks = 0
        q_spec = pl.BlockSpec((tq, gw), lambda bi, hi, i: (ctx_blk0 + bi, hi))
        in_specs = [q_spec, kc_spec, vtc_spec, pl.BlockSpec(memory_space=pl.ANY)]
        args = (q_all, k_all, vt_all, lat_out)
        aliases = {3: 0}
    return pl.pallas_call(
        functools.partial(_attn_kernel, lat_chunks=lat_chunks, group=group),
        out_shape=jax.ShapeDtypeStruct((rows.n, q_all.shape[1]), BF16),
        grid=(b, N_KV_HEADS, nqt),
        in_specs=in_specs,
        out_specs=q_spec,
        scratch_shapes=[pltpu.VMEM((VT_ROWS, group * tq), F32)]
        + ([pltpu.VMEM((2, tile, group * tq), F32)] if lat_chunks else []),
        input_output_aliases=aliases,
        compiler_params=_cparams("parallel", "parallel", "arbitrary"),
        name="attention_ctx" if lat_chunks == 0 else "attention_lat",
    )(*args)


def _dft_cs(n):
    idx = np.arange(n)
    ang = 2.0 * np.pi * ((idx[:, None] * idx[None, :]) % n) / n
    return np.cos(ang), np.sin(ang)


def _fft1_kernel(x_ref, fa_ref, tw_ref, o_ref, *, a):
    w = o_ref.shape[2]
    for r in range(o_ref.shape[0]):
        z = _dot_hi(fa_ref[...], x_ref[:, r * w:(r + 1) * w])
        zr, zi = z[:a], z[a:]
        tc = jnp.concatenate([tw_ref[r, 0]] * (w // LANES), axis=1)
        ts = jnp.concatenate([tw_ref[r, 1]] * (w // LANES), axis=1)
        o_ref[r, :a, :] = zr * tc - zi * ts
        o_ref[r, a:, :] = zr * ts + zi * tc


def _fft2_kernel(zr_ref, zi_ref, m2_ref, mc_ref, o_ref, *, scale):
    z = jnp.concatenate([zr_ref[...], zi_ref[...]], axis=0)
    v = _dot_hi(m2_ref[...], z)
    vr, vi = v[:FFT_B], v[FFT_B:]
    outs = []
    for g in range(o_ref.shape[1] // LANES):
        u = jnp.concatenate([vr[:, g * LANES:(g + 1) * LANES], vi[:, g * LANES:(g + 1) * LANES]], axis=1)
        outs.append(_dot_hi(u, mc_ref[...]))
    o_ref[...] = jnp.concatenate(outs, axis=1) * scale


def _fourier_latent(f_all, rows):
    b, seq = rows.batch, rows.seq
    w = FOURIER_WIDTH
    a = seq // FFT_B
    assert a % SUBLANES == 0 and FOURIER_GROUP_DIM == LANES
    ca, sa = _dft_cs(a)
    fa = jnp.asarray(np.concatenate([ca, sa], axis=0), dtype=F32)
    p1b = (np.arange(a)[None, :] * np.arange(FFT_B)[:, None]) % seq
    ang = 2.0 * np.pi * p1b / seq
    tw = np.stack([np.cos(ang), np.sin(ang)], axis=1)[..., None]
    tw = jnp.asarray(np.broadcast_to(tw, (FFT_B, 2, a, LANES)), dtype=F32)
    cb, sb = _dft_cs(FFT_B)
    m2 = jnp.asarray(np.block([[cb, -sb], [sb, cb]]), dtype=F32)
    cc, sc = _dft_cs(FOURIER_GROUP_DIM)
    mc = jnp.asarray(np.concatenate([cc, -sc], axis=0), dtype=F32)

    blk = FFT_B_BLOCK
    x2 = f_all.reshape(rows.n // FFT_B, FFT_B * w)
    z = pl.pallas_call(
        functools.partial(_fft1_kernel, a=a),
        out_shape=jax.ShapeDtypeStruct((b, FFT_B, 2 * a, w), F32),
        grid=(b, FFT_B // blk),
        in_specs=[
            pl.BlockSpec((a, blk * w), lambda bi, j: (bi, j)),
            pl.BlockSpec((2 * a, a), lambda bi, j: (0, 0)),
            pl.BlockSpec((blk, 2, a, LANES), lambda bi, j: (j, 0, 0, 0)),
        ],
        out_specs=pl.BlockSpec((None, blk, 2 * a, w), lambda bi, j: (bi, j, 0, 0)),
        compiler_params=_cparams("parallel", "parallel"),
        name="fourier_stage1",
    )(x2, fa, tw)

    tc = min(FFT_COL_TILE, a * w)
    ncol = (a * w) // tc
    z2 = z.reshape(b, FFT_B, 2 * a * w)
    y = pl.pallas_call(
        functools.partial(_fft2_kernel, scale=1.0 / math.sqrt(seq * FOURIER_GROUP_DIM)),
        out_shape=jax.ShapeDtypeStruct((b, FFT_B, a * w), F32),
        grid=(b, ncol),
        in_specs=[
            pl.BlockSpec((None, FFT_B, tc), lambda bi, j: (bi, 0, j)),
            pl.BlockSpec((None, FFT_B, tc), lambda bi, j: (bi, 0, ncol + j)),
            pl.BlockSpec((2 * FFT_B, 2 * FFT_B), lambda bi, j: (0, 0)),
            pl.BlockSpec((2 * LANES, LANES), lambda bi, j: (0, 0)),
        ],
        out_specs=pl.BlockSpec((None, FFT_B, tc), lambda bi, j: (bi, 0, j)),
        compiler_params=_cparams("parallel", "parallel"),
        name="fourier_stage2",
    )(z2, z2, m2, mc)
    return y.reshape(b * seq, w)


def _dft_ctx_kernel(x_ref, cn_ref, sn_ref, cc_ref, sc_ref, o_ref, *, scale):
    x = x_ref[...]
    outs = []
    for g in range(x.shape[1] // LANES):
        xg = x[:, g * LANES:(g + 1) * LANES]
        outs.append(_dot_hi(cn_ref[...], _dot_hi(xg, cc_ref[...])) - _dot_hi(sn_ref[...], _dot_hi(xg, sc_ref[...])))
    o_ref[...] = jnp.concatenate(outs, axis=1) * scale


def _fourier_ctx(f_all, rows):
    b, lc = rows.batch, rows.ctx_len
    w = FOURIER_WIDTH
    cn, sn = _dft_cs(lc)
    cc, sc = _dft_cs(FOURIER_GROUP_DIM)
    blk0 = rows.n_lat // lc
    mat = lambda m: pl.BlockSpec(m.shape, lambda bi: (0, 0))
    consts = [jnp.asarray(m, dtype=F32) for m in (cn, sn, cc, sc)]
    return pl.pallas_call(
        functools.partial(_dft_ctx_kernel, scale=1.0 / math.sqrt(lc * FOURIER_GROUP_DIM)),
        out_shape=jax.ShapeDtypeStruct((b * lc, w), F32),
        grid=(b,),
        in_specs=[pl.BlockSpec((lc, w), lambda bi: (blk0 + bi, 0))] + [mat(m) for m in consts],
        out_specs=pl.BlockSpec((lc, w), lambda bi: (bi, 0)),
        compiler_params=_cparams("parallel"),
        name="fourier_ctx",
    )(f_all, *consts)


def _ab_out_kernel(x1l_ref, x1c_ref, x2_ref, w_ref, h_ref, mod_ref, o_ref, *, lat_tiles):
    x1 = jnp.where(pl.program_id(0) < lat_tiles, x1l_ref[...], x1c_ref[...])
    x = jnp.concatenate([x1.astype(BF16), x2_ref[...]], axis=1)
    o_ref[...] = h_ref[...] + mod_ref[2:3, :] * _dot(x, w_ref[...])


def _ab_out(x1_lat, x1_ctx, x2, w_out, h, mods, rows):
    d = h.shape[1]
    tm = rows.tile
    lt = rows.lat_tiles
    return pl.pallas_call(
        functools.partial(_ab_out_kernel, lat_tiles=lt),
        out_shape=jax.ShapeDtypeStruct((rows.n, d), F32),
        grid=(rows.tiles,),
        in_specs=[
            pl.BlockSpec((tm, x1_lat.shape[1]), lambda i: (jnp.minimum(i, lt - 1), 0)),
            pl.BlockSpec((tm, x1_ctx.shape[1]), lambda i: (jnp.maximum(i - lt, 0), 0)),
            pl.BlockSpec((tm, x2.shape[1]), lambda i: (i, 0)),
            pl.BlockSpec(w_out.shape, lambda i: (0, 0)),
            pl.BlockSpec((tm, d), lambda i: (i, 0)),
            pl.BlockSpec((None, None, 3, d), lambda i: (rows.group(i), 1, 0, 0)),
        ],
        out_specs=pl.BlockSpec((tm, d), lambda i: (i, 0)),
        compiler_params=_cparams("parallel"),
        name="ab_out",
    )(x1_lat, x1_ctx, x2, w_out, h, mods)


def _hgrn_out_kernel(ofw_ref, obw_ref, g_ref, gain_ref, w_ref, h_ref, mod_ref, o_ref):
    o = ofw_ref[...] + obw_ref[...]
    parts = []
    for hh in range(o.shape[1] // HEAD_DIM):
        oh = o[:, hh * HEAD_DIM:(hh + 1) * HEAD_DIM]
        parts.append(oh * lax.rsqrt(jnp.mean(oh * oh, axis=-1, keepdims=True) + EPS))
    on = jnp.concatenate(parts, axis=1) * gain_ref[...]
    y = (on * jax.nn.sigmoid(g_ref[...])).astype(BF16)
    o_ref[...] = h_ref[...] + mod_ref[2:3, :] * _dot(y, w_ref[...])


def _hgrn_out(o_fw, o_bw, p, gain, w_out, h, mods, rows_r, n_tiles):
    d = h.shape[1]
    tm = rows_r.tile
    g_blk = (p.shape[1] - d) // d
    return pl.pallas_call(
        _hgrn_out_kernel,
        out_shape=jax.ShapeDtypeStruct((n_tiles * tm, d), F32),
        grid=(n_tiles,),
        in_specs=[
            pl.BlockSpec((tm, d), lambda i: (i, 0)),
            pl.BlockSpec((tm, d), lambda i: (i, 0)),
            pl.BlockSpec((tm, d), lambda i: (i, g_blk)),
            pl.BlockSpec((1, d), lambda i: (0, 0)),
            pl.BlockSpec(w_out.shape, lambda i: (0, 0)),
            pl.BlockSpec((tm, d), lambda i: (i, 0)),
            pl.BlockSpec((None, None, 3, d), lambda i: (rows_r.group(i), 1, 0, 0)),
        ],
        out_specs=pl.BlockSpec((tm, d), lambda i: (i, 0)),
        compiler_params=_cparams("parallel"),
        name="hgrn_out",
    )(o_fw, o_bw, p, gain, w_out, h, mods)


def _hgrn_proj_kernel(h_ref, mod_ref, gain_ref, w_ref, o_ref, xn_ref, *, n_silu):
    j = pl.program_id(1)

    @pl.when(j == 0)
    def _():
        xn_ref[...] = _ada(h_ref[...], gain_ref[...], mod_ref[0:1, :], mod_ref[1:2, :]).astype(BF16)

    acc = _dot(xn_ref[...], w_ref[...])
    o_ref[...] = jnp.where(j < n_silu, _silu(acc), acc)


def _hgrn_proj(h, mods, gain, w_in, rows):
    d = h.shape[1]
    tm, tn = rows.tile, PROJ_COL_TILE
    nw = w_in.shape[1]
    return pl.pallas_call(
        functools.partial(_hgrn_proj_kernel, n_silu=d // tn),
        out_shape=jax.ShapeDtypeStruct((rows.n, nw), F32),
        grid=(rows.tiles, nw // tn),
        in_specs=[
            pl.BlockSpec((tm, d), lambda i, j: (i, 0)),
            pl.BlockSpec((None, None, 3, d), lambda i, j: (rows.group(i), 1, 0, 0)),
            pl.BlockSpec((1, d), lambda i, j: (0, 0)),
            pl.BlockSpec((d, tn), lambda i, j: (0, j)),
        ],
        out_specs=pl.BlockSpec((tm, tn), lambda i, j: (i, j)),
        scratch_shapes=[pltpu.VMEM((tm, d), BF16)],
        compiler_params=_cparams("parallel", "arbitrary"),
        name="hgrn_proj",
    )(h, mods, gain, w_in)


def _scan_levels(chunk):
    return [chunk >> (i + 1) for i in range(int(math.log2(chunk)))]


def _scan_consts(chunk, reverse):
    t = np.arange(chunk)[:, None]
    s = np.arange(chunk)[None, :]
    tri = (s >= t) if reverse else (s <= t)
    wide, narrow = [], []
    for h in _scan_levels(chunk):
        same = (t // (2 * h)) == (s // (2 * h))
        t_up = (t // h) % 2 == 1
        s_up = (s // h) % 2 == 1
        mask = same & ((~t_up & s_up) if reverse else (t_up & ~s_up))
        if h >= SUBLANES:
            wide.append(mask[np.nonzero(~t_up[:, 0] if reverse else t_up[:, 0])[0]])
        else:
            narrow.append(mask)
    narrow.append(t == s)
    return (jnp.asarray(tri, dtype=BF16), jnp.asarray(np.stack(wide), dtype=F32),
            jnp.asarray(np.stack(narrow), dtype=F32))


def _seg_bcast(x, h, reverse):
    c, w = x.shape
    off = h if reverse else h - 1
    if 2 * h >= 2 * SUBLANES:
        pieces = [jnp.broadcast_to(x[g * 2 * h + off:g * 2 * h + off + 1, :], (2 * h, w)) for g in range(c // (2 * h))]
        return pieces[0] if len(pieces) == 1 else jnp.concatenate(pieces, axis=0)
    x3 = x.reshape(c // SUBLANES, SUBLANES, w)
    sub = lax.broadcasted_iota(jnp.int32, x3.shape, 1)
    y = None
    for g in range(SUBLANES // (2 * h)):
        piece = jnp.broadcast_to(x3[:, g * 2 * h + off:g * 2 * h + off + 1, :], x3.shape)
        y = piece if y is None else jnp.where(sub >= g * 2 * h, piece, y)
    return y.reshape(c, w)


def _scan_kernel(q_ref, f_ref, v_ref, lb_ref, tri_ref, wmask_ref, nmask_ref, o_ref, st_ref, *, reverse, chunk):
    @pl.when(pl.program_id(2) == 0)
    def _():
        st_ref[...] = jnp.zeros_like(st_ref)

    rows, width = q_ref.shape
    levels = _scan_levels(chunk)
    spans = [slice(ci * chunk, (ci + 1) * chunk) for ci in range(rows // chunk)]
    lanes = [slice(hh * HEAD_DIM, (hh + 1) * HEAD_DIM) for hh in range(width // HEAD_DIM)]
    q = q_ref[...]
    fl = f_ref[...]
    lb = lb_ref[...]
    key = (1.0 - lb) * jax.nn.sigmoid(-fl)
    log_sig = jnp.minimum(fl, 0.0) - jnp.log(1.0 + jnp.exp(-jnp.abs(fl)))
    x1 = jnp.log(lb)
    x2 = jnp.log1p(-lb) + log_sig
    delta = x1 - x2
    lf = jnp.where(jnp.isnan(delta), x1 + x2, jnp.maximum(x1, x2) + jnp.log(1.0 + jnp.exp(-jnp.abs(delta))))
    lf = lf * LOG2_E
    l1 = lf.astype(BF16)
    r1 = lf - l1.astype(F32)
    l2 = r1.astype(BF16)
    l3 = (r1 - l2.astype(F32)).astype(BF16)
    parts = jnp.concatenate([l1, l2, l3], axis=1)
    tri = tri_ref[...]
    cs = jnp.concatenate([_dot(tri, parts[sp]) for sp in spans], axis=0)
    b = cs[:, :width] + cs[:, width:2 * width] + cs[:, 2 * width:]
    totals = [b[sp.start:sp.start + 1, :] if reverse else b[sp.stop - 1:sp.stop, :] for sp in spans]
    total_rows = jnp.concatenate([jnp.broadcast_to(t, (chunk, width)) for t in totals], axis=0)

    q_in = (q * jnp.exp2(b)).astype(BF16)
    k_out = (key * jnp.exp2(total_rows - b)).astype(BF16)
    v = v_ref[...].astype(BF16)

    order = list(range(len(spans)))[::-1] if reverse else list(range(len(spans)))
    kv = [[_dot_tn(v[sp, sl], k_out[sp, sl]) for sl in lanes] for sp in spans]
    o_state = [[None] * len(lanes) for _ in spans]
    for hh, sl in enumerate(lanes):
        st = st_ref[hh]
        for ci in order:
            o_state[ci][hh] = _dot_nt(q_in[spans[ci], sl], st.astype(BF16))
            st = st * jnp.exp2(totals[ci][:, sl]) + kv[ci][hh]
        st_ref[hh] = st

    n_blk = chunk // SUBLANES
    wide_levels = [h for h in levels if h >= SUBLANES]
    narrow_levels = [h for h in levels if h < SUBLANES]

    def narrow_scores(li, ql, kl):
        out = []
        for sp in spans:
            row = []
            for sl in lanes:
                s = nmask_ref[li] * _dot_nt(ql[sp, sl], kl[sp, sl])
                row.append([s[k * SUBLANES:(k + 1) * SUBLANES, :] for k in range(n_blk)])
            out.append(row)
        return out

    a = narrow_scores(len(narrow_levels), q.astype(BF16), key.astype(BF16))
    for li, h in enumerate(narrow_levels):
        e = jnp.exp2(-jnp.abs(b - _seg_bcast(b, h, reverse)))
        new = narrow_scores(li, (q * e).astype(BF16), (key * e).astype(BF16))
        a = [[[x + y for x, y in zip(xb, yb)] for xb, yb in zip(xa, ya)] for xa, ya in zip(a, new)]

    for li, h in enumerate(wide_levels):
        q_parts, k_parts = [], []
        for g in range(rows // (2 * h)):
            lo = slice(g * 2 * h, g * 2 * h + h)
            hi = slice(g * 2 * h + h, (g + 1) * 2 * h)
            r = g * 2 * h + (h if reverse else h - 1)
            b_ref = jnp.broadcast_to(b[r:r + 1, :], (h, width))
            q_half, k_half = (lo, hi) if reverse else (hi, lo)
            q_parts.append(q[q_half] * jnp.exp2(b[q_half] - b_ref))
            k_part = key[k_half] * jnp.exp2(b_ref - b[k_half])
            zeros = jnp.zeros((h, width), F32)
            k_parts += [zeros, k_part] if reverse else [k_part, zeros]
        q_sel = jnp.concatenate(q_parts, axis=0).astype(BF16)
        k_hat = jnp.concatenate(k_parts, axis=0).astype(BF16)
        half = chunk // 2
        for ci, sp in enumerate(spans):
            for hh, sl in enumerate(lanes):
                s = wmask_ref[li] * _dot_nt(q_sel[ci * half:(ci + 1) * half, sl], k_hat[sp, sl])
                for j in range(chunk // (2 * h)):
                    first = (j * 2 * h + (0 if reverse else h)) // SUBLANES
                    for k in range(h // SUBLANES):
                        r0 = j * h + k * SUBLANES
                        a[ci][hh][first + k] = a[ci][hh][first + k] + s[r0:r0 + SUBLANES, :]

    for ci, sp in enumerate(spans):
        for hh, sl in enumerate(lanes):
            pairs = jnp.concatenate(a[ci][hh], axis=0).astype(BF16)
            o_ref[sp, sl] = o_state[ci][hh] + _dot(pairs, v[sp, sl])


def _hgrn_scan(p, lower_bound, rows, reverse):
    b, seq, lc = rows.batch, rows.seq, rows.ctx_len
    d = lower_bound.shape[0]
    c, hb = SCAN_CHUNK, SCAN_HEADS_PER_BLOCK
    r = c * SCAN_CHUNKS_PER_STEP
    wb = hb * HEAD_DIM
    ncb = d // wb
    assert lc % r == 0 and seq % r == 0
    nctx, nlat = lc // r, seq // r
    ctx0 = rows.n_lat // r
    f_blk = (2 if reverse else 1) * ncb
    v_blk = 3 * ncb

    def row(bi, s):
        if reverse:
            return jnp.where(s < nctx, ctx0 + bi * nctx + (nctx - 1 - s), bi * nlat + (nlat - 1 - (s - nctx)))
        return jnp.where(s < nctx, ctx0 + bi * nctx + s, bi * nlat + (s - nctx))

    tri, wmasks, nmasks = _scan_consts(c, reverse)
    return pl.pallas_call(
        functools.partial(_scan_kernel, reverse=reverse, chunk=c),
        out_shape=jax.ShapeDtypeStruct((rows.n, d), F32),
        grid=(b, ncb, nctx + nlat),
        in_specs=[
            pl.BlockSpec((r, wb), lambda bi, hi, s: (row(bi, s), hi)),
            pl.BlockSpec((r, wb), lambda bi, hi, s: (row(bi, s), f_blk + hi)),
            pl.BlockSpec((r, wb), lambda bi, hi, s: (row(bi, s), v_blk + hi)),
            pl.BlockSpec((1, wb), lambda bi, hi, s: (0, hi)),
            pl.BlockSpec(tri.shape, lambda bi, hi, s: (0, 0)),
            pl.BlockSpec(wmasks.shape, lambda bi, hi, s: (0, 0, 0)),
            pl.BlockSpec(nmasks.shape, lambda bi, hi, s: (0, 0, 0)),
        ],
        out_specs=pl.BlockSpec((r, wb), lambda bi, hi, s: (row(bi, s), hi)),
        scratch_shapes=[pltpu.VMEM((hb, HEAD_DIM, HEAD_DIM), F32)],
        compiler_params=_cparams("parallel", "parallel", "arbitrary"),
        name="hgrn_scan_bw" if reverse else "hgrn_scan_fw",
    )(p, p, p, lower_bound.reshape(1, d), tri, wmasks, nmasks)


def kernel(x, c, ctx, c_ctx, w_mod, b_mod, norm_gains, ffn_w_in, ffn_w_out, ab_w_in, qk_norm, ab_w_out,
           hgrn_w_in, hgrn_lb_logits, hgrn_o_norm, hgrn_w_out, final_norm):
    batch, seq, d = x.shape
    lc = ctx.shape[1]
    depth = w_mod.shape[0]
    rows = _Rows(batch, seq, lc, ROW_TILE)
    rows_r = _Rows(batch, seq, lc, READOUT_ROW_TILE)
    assert seq % GRID_W == 0 and batch + 1 <= SUBLANES

    c_rows = jnp.concatenate([c_ctx[None, :], c, jnp.zeros((SUBLANES - 1 - batch, d), F32)], axis=0)
    mods_all = _modulation(c_rows, w_mod, b_mod).reshape(depth, SUBLANES, 3, 3, d)

    lb_cum = jnp.cumsum(jax.nn.softmax(hgrn_lb_logits.astype(F32), axis=0), axis=0)
    lower_bounds = lb_cum - lb_cum[0]

    ffn_w_in_b = ffn_w_in.astype(BF16)
    ffn_w_out_b = ffn_w_out.astype(BF16)
    fin = final_norm.reshape(1, d)
    rope = _rope_tables(seq, ROW_TILE)

    h = (x.reshape(batch * seq, d), ctx.reshape(batch * lc, d))
    for layer in range(depth):
        last = layer == depth - 1
        mods = mods_all[layer]
        gains = norm_gains[layer].reshape(3, 1, d)
        h = _ffn(h, mods, 0, gains[0], ffn_w_in_b, ffn_w_out_b, layer, 0, fin, rows, rows.tiles, False)
        if layer % 2 == 0:
            e = layer // 2
            f_all, q_all, k_all, vt_all = _ab_proj(h, mods, gains[1], ab_w_in[e].astype(BF16), qk_norm[e], rope, rows)
            attn = _attention(q_all, k_all, vt_all, rows)
            attn = _attention(q_all, k_all, vt_all, rows, lat_out=attn)
            h = _ab_out(_fourier_latent(f_all, rows), _fourier_ctx(f_all, rows), attn,
                        ab_w_out[e].astype(BF16), h, mods, rows)
        else:
            o = layer // 2
            p = _hgrn_proj(h, mods, gains[1], hgrn_w_in[o].astype(BF16), rows)
            o_fw = _hgrn_scan(p, lower_bounds[layer], rows, False)
            o_bw = _hgrn_scan(p, lower_bounds[layer], rows, True)
            gain_o = jnp.tile(hgrn_o_norm[o], d // HEAD_DIM).reshape(1, d)
            n_t = rows_r.lat_tiles if last else rows_r.tiles
            h = _hgrn_out(o_fw, o_bw, p, gain_o, hgrn_w_out[o].astype(BF16), h, mods, rows_r, n_t)
        n_t = rows.lat_tiles if last else rows.tiles
        h = _ffn(h, mods, 2, gains[2], ffn_w_in_b, ffn_w_out_b, layer, 1, fin, rows, n_t, last)
    return h[:batch * seq].reshape(batch, seq, d)
```

```python
import functools
import math

import jax
import jax.numpy as jnp
import numpy as np
from jax import lax
from jax.experimental import pallas as pl
from jax.experimental.pallas import tpu as pltpu

F32 = jnp.float32
BF16 = jnp.bfloat16

EPS = 1e-6
N_MOD = 9
HEAD_DIM = 128
N_KV_HEADS = 4
FOURIER_WIDTH = 512
FOURIER_GROUP_DIM = 128
GRID_W = 64
ROPE_THETA = 10000.0
ROPE_AXIS_DIM = HEAD_DIM // 2
ATTN_SCALE = HEAD_DIM ** -0.5
LOG2_E = math.log2(math.e)

LANES = 128
SUBLANES = 8
BF16_SUBLANES = 16
VT_ROWS = HEAD_DIM + BF16_SUBLANES
VMEM_LIMIT_BYTES = 56 * 1024 * 1024

ROW_TILE = 512
READOUT_ROW_TILE = 256
FFN_TILE = 512
PROJ_COL_TILE = 2048
MOD_COL_TILE = 1024
ATTN_Q_TILE = 1024
SCAN_CHUNK = 128
SCAN_HEADS_PER_BLOCK = 8
SCAN_CHUNKS_PER_STEP = 2
FFT_B = 128
FFT_B_BLOCK = 8
FFT_COL_TILE = 2048


def _cparams(*sem):
    return pltpu.CompilerParams(dimension_semantics=sem, vmem_limit_bytes=VMEM_LIMIT_BYTES)


def _dot(a, b):
    return jnp.dot(a, b, preferred_element_type=F32)


def _dot_hi(a, b):
    return jnp.dot(a, b, preferred_element_type=F32, precision=lax.Precision.HIGHEST)


def _split_bf16(x):
    hi = x.astype(BF16)
    return hi, (x - hi.astype(F32)).astype(BF16)


def _dot_split(a, b):
    (ah, al), (bh, bl) = a, b
    return _dot(ah, bh) + _dot(al, bh) + _dot(ah, bl)


def _dot_nt(a, b):
    return lax.dot_general(a, b, (((1,), (1,)), ((), ())), preferred_element_type=F32)


def _dot_tn(a, b):
    return lax.dot_general(a, b, (((0,), (0,)), ((), ())), preferred_element_type=F32)


def _silu(x):
    return x * jax.nn.sigmoid(x)


def _rms(x, gain):
    return x * lax.rsqrt(jnp.mean(x * x, axis=-1, keepdims=True) + EPS) * gain


def _ada(h, gain, shift, scale):
    return _rms(h, gain) * (1.0 + scale) + shift


def _mod_kernel(c_ref, w_ref, b_ref, o_ref):
    a = _silu(c_ref[...]).astype(BF16)
    o_ref[...] = _dot(a, w_ref[...].astype(BF16)) + b_ref[...]


def _modulation(c_rows, w_mod, b_mod):
    depth, d, nd = w_mod.shape
    tn = MOD_COL_TILE
    return pl.pallas_call(
        _mod_kernel,
        out_shape=jax.ShapeDtypeStruct((depth, SUBLANES, nd), F32),
        grid=(depth, nd // tn),
        in_specs=[
            pl.BlockSpec((SUBLANES, d), lambda l, j: (0, 0)),
            pl.BlockSpec((None, d, tn), lambda l, j: (l, 0, j)),
            pl.BlockSpec((None, 1, tn), lambda l, j: (l, 0, j)),
        ],
        out_specs=pl.BlockSpec((None, SUBLANES, tn), lambda l, j: (l, 0, j)),
        compiler_params=_cparams("parallel", "arbitrary"),
        name="modulation",
    )(c_rows, w_mod, b_mod.reshape(depth, 1, nd))


class _Rows:
    def __init__(self, batch, seq, ctx_len, tile):
        assert seq % tile == 0 and (batch * ctx_len) % tile == 0
        self.batch, self.seq, self.ctx_len, self.tile = batch, seq, ctx_len, tile
        self.n_lat = batch * seq
        self.n = self.n_lat + batch * ctx_len
        self.lat_tiles = self.n_lat // tile
        self.tiles = self.n // tile
        self.tiles_per_batch = seq // tile

    def group(self, i):
        return jnp.where(i < self.lat_tiles, 1 + i // self.tiles_per_batch, 0)


def _ffn_kernel(*refs, final, split_tiles):
    if split_tiles is None:
        h_ref, mod_ref, gain_ref, wa_ref, wb_ref, wo_ref, fin_ref, o_ref, xn_ref, acc_ref = refs
        load_h = lambda: h_ref[...]
    else:
        hl_ref, hc_ref, mod_ref, gain_ref, wa_ref, wb_ref, wo_ref, fin_ref, o_ref, xn_ref, acc_ref = refs
        load_h = lambda: jnp.where(pl.program_id(0) < split_tiles, hl_ref[...], hc_ref[...])
    i = pl.program_id(0)
    j = pl.program_id(1)

    @pl.when(j == 0)
    def _():
        xn_ref[...] = _ada(load_h(), gain_ref[...], mod_ref[0:1, :], mod_ref[1:2, :]).astype(BF16)

    @pl.when(jnp.logical_and(i == 0, j == 0))
    def _():
        acc_ref[...] = jnp.zeros_like(acc_ref)

    xn = xn_ref[...]
    a = _dot(xn, wa_ref[...])
    b = _dot(xn, wb_ref[...])
    g = (_silu(a) * b).astype(BF16)
    acc_ref[...] = jnp.where(j == 0, 0.0, acc_ref[...]) + _dot(g, wo_ref[...])

    @pl.when(j == pl.num_programs(1) - 1)
    def _():
        out = load_h() + 0.5 * mod_ref[2:3, :] * acc_ref[...]
        if final:
            out = _rms(out, fin_ref[...])
        o_ref[...] = out


def _ffn(h, mods, sub, gain, w_in, w_out, layer, which, fin, rows, n_tiles, final):
    f, d = w_out.shape[2:]
    tm, tf = rows.tile, FFN_TILE
    nf = f // tf
    if isinstance(h, tuple):
        lt = rows.lat_tiles
        row_specs = [pl.BlockSpec((tm, d), lambda i, j: (jnp.minimum(i, lt - 1), 0)),
                     pl.BlockSpec((tm, d), lambda i, j: (jnp.maximum(i - lt, 0), 0))]
        row_args, split_tiles = h, lt
    else:
        row_specs = [pl.BlockSpec((tm, d), lambda i, j: (i, 0))]
        row_args, split_tiles = (h,), None
    return pl.pallas_call(
        functools.partial(_ffn_kernel, final=final, split_tiles=split_tiles),
        out_shape=jax.ShapeDtypeStruct((n_tiles * tm, d), F32),
        grid=(n_tiles, nf),
        in_specs=row_specs + [
            pl.BlockSpec((None, None, 3, d), lambda i, j: (rows.group(i), sub, 0, 0)),
            pl.BlockSpec((1, d), lambda i, j: (0, 0)),
            pl.BlockSpec((None, None, d, tf), lambda i, j: (layer, which, 0, j)),
            pl.BlockSpec((None, None, d, tf), lambda i, j: (layer, which, 0, nf + j)),
            pl.BlockSpec((None, None, tf, d), lambda i, j: (layer, which, j, 0)),
            pl.BlockSpec((1, d), lambda i, j: (0, 0)),
        ],
        out_specs=pl.BlockSpec((tm, d), lambda i, j: (i, 0)),
        scratch_shapes=[pltpu.VMEM((tm, d), BF16), pltpu.VMEM((tm, d), F32)],
        compiler_params=_cparams("arbitrary", "arbitrary"),
        name="ffn",
    )(*row_args, mods, gain, w_in, w_in, w_out, fin)


def _rope_tables(seq, tile):
    t = np.arange(seq)
    inv_freq = ROPE_THETA ** (-np.arange(0, ROPE_AXIS_DIM, 2, dtype=np.float64) / ROPE_AXIS_DIM)
    ang = np.concatenate([(t // GRID_W)[:, None] * inv_freq, (t % GRID_W)[:, None] * inv_freq], axis=-1)
    nf = ROPE_AXIS_DIM // 2
    cos = np.cos(ang).reshape(seq, 2, 1, nf)
    sin = np.sin(ang).reshape(seq, 2, 1, nf)
    zero = np.zeros_like(sin)
    c_full = np.broadcast_to(cos, (seq, 2, 2, nf)).reshape(seq, HEAD_DIM)
    s_up = np.concatenate([-sin, zero], axis=2).reshape(seq, HEAD_DIM)
    s_dn = np.concatenate([zero, sin], axis=2).reshape(seq, HEAD_DIM)
    lat = np.concatenate([c_full, s_up, s_dn], axis=1)
    ident = np.concatenate([np.ones((tile, HEAD_DIM)), np.zeros((tile, 2 * HEAD_DIM))], axis=1)
    return jnp.asarray(np.concatenate([lat, ident], axis=0), dtype=F32)


def _norm_rope_heads(acc, gain, rope, post_scale=None):
    nf = ROPE_AXIS_DIM // 2
    c, s_up, s_dn = rope[:, :HEAD_DIM], rope[:, HEAD_DIM:2 * HEAD_DIM], rope[:, 2 * HEAD_DIM:]
    heads = []
    for hh in range(acc.shape[1] // HEAD_DIM):
        y = _rms(acc[:, hh * HEAD_DIM:(hh + 1) * HEAD_DIM], gain)
        y = y * c + pltpu.roll(y, HEAD_DIM - nf, 1) * s_up + pltpu.roll(y, nf, 1) * s_dn
        heads.append(y if post_scale is None else y * post_scale)
    return jnp.concatenate(heads, axis=1).astype(BF16)


def _ab_proj_kernel(h_ref, mod_ref, gain_ref, w_ref, qkn_ref, rope_ref, f_ref, q_ref, k_ref, vt_ref):
    xn = _ada(h_ref[...], gain_ref[...], mod_ref[0:1, :], mod_ref[1:2, :]).astype(BF16)
    c0 = f_ref.shape[1]
    c1 = c0 + q_ref.shape[1]
    c2 = c1 + k_ref.shape[1]
    q_ref[...] = _norm_rope_heads(_dot(xn, w_ref[:, c0:c1]), qkn_ref[0:1, :], rope_ref[...], ATTN_SCALE * LOG2_E)
    k_ref[...] = _norm_rope_heads(_dot(xn, w_ref[:, c1:c2]), qkn_ref[1:2, :], rope_ref[...])
    v = _dot(xn, w_ref[:, c2:])
    ones = jnp.ones((vt_ref.shape[1] - HEAD_DIM, vt_ref.shape[2]), BF16)
    for hh in range(N_KV_HEADS):
        vt_ref[hh, :HEAD_DIM, :] = v[:, hh * HEAD_DIM:(hh + 1) * HEAD_DIM].T.astype(BF16)
        vt_ref[hh, HEAD_DIM:, :] = ones
    f_ref[...] = _dot(xn, w_ref[:, :c0])


def _ab_proj(h, mods, gain, w_in, qk_norm, rope, rows):
    d = h.shape[1]
    tm = rows.tile
    kv_width = N_KV_HEADS * HEAD_DIM
    q_width = w_in.shape[1] - FOURIER_WIDTH - 2 * kv_width
    n = rows.n
    rope_blk = lambda i: (jnp.where(i < rows.lat_tiles, i % rows.tiles_per_batch, rows.tiles_per_batch), 0)
    return pl.pallas_call(
        _ab_proj_kernel,
        out_shape=(
            jax.ShapeDtypeStruct((n, FOURIER_WIDTH), F32),
            jax.ShapeDtypeStruct((n, q_width), BF16),
            jax.ShapeDtypeStruct((n, kv_width), BF16),
            jax.ShapeDtypeStruct((N_KV_HEADS, rows.tiles, VT_ROWS, tm), BF16),
        ),
        grid=(rows.tiles,),
        in_specs=[
            pl.BlockSpec((tm, d), lambda i: (i, 0)),
            pl.BlockSpec((None, None, 3, d), lambda i: (rows.group(i), 1, 0, 0)),
            pl.BlockSpec((1, d), lambda i: (0, 0)),
            pl.BlockSpec(w_in.shape, lambda i: (0, 0)),
            pl.BlockSpec((2, HEAD_DIM), lambda i: (0, 0)),
            pl.BlockSpec((tm, 3 * HEAD_DIM), rope_blk),
        ],
        out_specs=(
            pl.BlockSpec((tm, FOURIER_WIDTH), lambda i: (i, 0)),
            pl.BlockSpec((tm, q_width), lambda i: (i, 0)),
            pl.BlockSpec((tm, kv_width), lambda i: (i, 0)),
            pl.BlockSpec((N_KV_HEADS, None, VT_ROWS, tm), lambda i: (0, i, 0, 0)),
        ),
        compiler_params=_cparams("parallel"),
        name="ab_proj",
    )(h, mods, gain, w_in, qk_norm, rope)


def _attn_kernel(*refs, lat_chunks, group):
    if lat_chunks:
        q_ref, kc_ref, vtc_ref, kl_ref, vtl_ref, o_ref, acc_ref, s_ref = refs
    else:
        q_ref, kc_ref, vtc_ref, _, o_ref, acc_ref = refs
    tq = q_ref.shape[0]
    q = q_ref[...]
    qs = jnp.concatenate([q[:, g * HEAD_DIM:(g + 1) * HEAD_DIM] for g in range(group)], axis=0)
    nq = group * tq
    acc_ref[...] = jnp.zeros_like(acc_ref)

    def scores(k):
        return _dot_nt(k, qs)

    def update(s, vt, m):
        m_new = jnp.maximum(m, jnp.max(s, axis=0, keepdims=True))
        p = jnp.exp2(s - m_new).astype(BF16)
        acc_ref[...] = jnp.exp2(m - m_new) * acc_ref[...] + _dot(vt, p)
        return m_new

    m = update(scores(kc_ref[...]), vtc_ref[...], jnp.full((1, nq), -jnp.inf, F32))
    if lat_chunks:
        tk = vtl_ref.shape[2]

        def lat_scores(c):
            return scores(kl_ref[pl.ds(pl.multiple_of(c * tk, tk), tk), :])

        assert lat_chunks % 2 == 0
        s_ref[0] = lat_scores(0)

        def body(i, m):
            c = 2 * i
            s_ref[1] = lat_scores(c + 1)
            m = update(s_ref[0], vtl_ref[c], m)
            s_ref[0] = lat_scores(jnp.minimum(c + 2, lat_chunks - 1))
            return update(s_ref[1], vtl_ref[c + 1], m)

        m = lax.fori_loop(0, lat_chunks // 2, body, m)

    out = (acc_ref[:HEAD_DIM, :] / acc_ref[HEAD_DIM:HEAD_DIM + 1, :]).T
    o_ref[...] = jnp.concatenate([out[g * tq:(g + 1) * tq, :] for g in range(group)], axis=1).astype(BF16)


def _attention(q_all, k_all, vt_all, rows, lat_out=None):
    b, seq, lc, tile = rows.batch, rows.seq, rows.ctx_len, rows.tile
    group = q_all.shape[1] // (N_KV_HEADS * HEAD_DIM)
    gw = group * HEAD_DIM
    assert tile % lc == 0 and seq % tile == 0
    ctx_blk0 = rows.n_lat // lc
    kc_spec = pl.BlockSpec((lc, HEAD_DIM), lambda bi, hi, i: (ctx_blk0 + bi, hi))
    vtc_spec = pl.BlockSpec(
        (None, None, VT_ROWS, lc),
        lambda bi, hi, i: (hi, (rows.n_lat + bi * lc) // tile, 0, ((rows.n_lat + bi * lc) % tile) // lc))
    if lat_out is None:
        tq = ATTN_Q_TILE
        nqt = seq // tq
        lat_chunks = seq // tile
        q_spec = pl.BlockSpec((tq, gw), lambda bi, hi, i: (bi * nqt + i, hi))
        kl_spec = pl.BlockSpec((seq, HEAD_DIM), lambda bi, hi, i: (bi, hi))
        vtl_spec = pl.BlockSpec((None, lat_chunks, VT_ROWS, tile), lambda bi, hi, i: (hi, bi, 0, 0))
        in_specs = [q_spec, kc_spec, vtc_spec, kl_spec, vtl_spec]
        args = (q_all, k_all, vt_all, k_all, vt_all)
        aliases = {}
    else:
        tq = lc
        nqt = 1
        lat_chunks = 0
        q_spec = pl.BlockSpec((tq, gw), lambda bi, hi, i: (ctx_blk0 + bi, hi))
        in_specs = [q_spec, kc_spec, vtc_spec, pl.BlockSpec(memory_space=pl.ANY)]
        args = (q_all, k_all, vt_all, lat_out)
        aliases = {3: 0}
    return pl.pallas_call(
        functools.partial(_attn_kernel, lat_chunks=lat_chunks, group=group),
        out_shape=jax.ShapeDtypeStruct((rows.n, q_all.shape[1]), BF16),
        grid=(b, N_KV_HEADS, nqt),
        in_specs=in_specs,
        out_specs=q_spec,
        scratch_shapes=[pltpu.VMEM((VT_ROWS, group * tq), F32)]
        + ([pltpu.VMEM((2, tile, group * tq), F32)] if lat_chunks else []),
        input_output_aliases=aliases,
        compiler_params=_cparams("parallel", "parallel", "arbitrary"),
        name="attention_ctx" if lat_chunks == 0 else "attention_lat",
    )(*args)


def _dft_cs(n):
    idx = np.arange(n)
    ang = 2.0 * np.pi * ((idx[:, None] * idx[None, :]) % n) / n
    return np.cos(ang), np.sin(ang)


def _const_split(m):
    m = np.asarray(m, dtype=np.float32)
    hi = jnp.asarray(m).astype(BF16)
    lo = (jnp.asarray(m) - hi.astype(F32)).astype(BF16)
    return jnp.stack([hi, lo])


def _fft1_kernel(x_ref, fa_ref, tw_ref, o_ref, *, a):
    w = o_ref.shape[2]
    for r in range(o_ref.shape[0]):
        z = _dot_split((fa_ref[0], fa_ref[1]), _split_bf16(x_ref[:, r * w:(r + 1) * w]))
        zr, zi = z[:a], z[a:]
        tc = jnp.concatenate([tw_ref[r, 0]] * (w // LANES), axis=1)
        ts = jnp.concatenate([tw_ref[r, 1]] * (w // LANES), axis=1)
        o_ref[r, :a, :] = zr * tc - zi * ts
        o_ref[r, a:, :] = zr * ts + zi * tc


def _fft2_kernel(zr_ref, zi_ref, m2_ref, mc_ref, o_ref, *, scale):
    z = jnp.concatenate([zr_ref[...], zi_ref[...]], axis=0)
    v = _dot_split((m2_ref[0], m2_ref[1]), _split_bf16(z))
    vr, vi = v[:FFT_B], v[FFT_B:]
    outs = []
    for g in range(o_ref.shape[1] // LANES):
        u = jnp.concatenate([vr[:, g * LANES:(g + 1) * LANES], vi[:, g * LANES:(g + 1) * LANES]], axis=1)
        outs.append(_dot_split(_split_bf16(u), (mc_ref[0], mc_ref[1])))
    o_ref[...] = jnp.concatenate(outs, axis=1) * scale


def _fourier_latent(f_all, rows):
    b, seq = rows.batch, rows.seq
    w = FOURIER_WIDTH
    a = seq // FFT_B
    assert a % SUBLANES == 0 and FOURIER_GROUP_DIM == LANES
    ca, sa = _dft_cs(a)
    fa = _const_split(np.concatenate([ca, sa], axis=0))
    p1b = (np.arange(a)[None, :] * np.arange(FFT_B)[:, None]) % seq
    ang = 2.0 * np.pi * p1b / seq
    tw = np.stack([np.cos(ang), np.sin(ang)], axis=1)[..., None]
    tw = jnp.asarray(np.broadcast_to(tw, (FFT_B, 2, a, LANES)), dtype=F32)
    cb, sb = _dft_cs(FFT_B)
    m2 = _const_split(np.block([[cb, -sb], [sb, cb]]))
    cc, sc = _dft_cs(FOURIER_GROUP_DIM)
    mc = _const_split(np.concatenate([cc, -sc], axis=0))

    blk = FFT_B_BLOCK
    x2 = f_all.reshape(rows.n // FFT_B, FFT_B * w)
    z = pl.pallas_call(
        functools.partial(_fft1_kernel, a=a),
        out_shape=jax.ShapeDtypeStruct((b, FFT_B, 2 * a, w), F32),
        grid=(b, FFT_B // blk),
        in_specs=[
            pl.BlockSpec((a, blk * w), lambda bi, j: (bi, j)),
            pl.BlockSpec((2, 2 * a, a), lambda bi, j: (0, 0, 0)),
            pl.BlockSpec((blk, 2, a, LANES), lambda bi, j: (j, 0, 0, 0)),
        ],
        out_specs=pl.BlockSpec((None, blk, 2 * a, w), lambda bi, j: (bi, j, 0, 0)),
        compiler_params=_cparams("parallel", "parallel"),
        name="fourier_stage1",
    )(x2, fa, tw)

    tc = min(FFT_COL_TILE, a * w)
    ncol = (a * w) // tc
    z2 = z.reshape(b, FFT_B, 2 * a * w)
    y = pl.pallas_call(
        functools.partial(_fft2_kernel, scale=1.0 / math.sqrt(seq * FOURIER_GROUP_DIM)),
        out_shape=jax.ShapeDtypeStruct((b, FFT_B, a * w), F32),
        grid=(b, ncol),
        in_specs=[
            pl.BlockSpec((None, FFT_B, tc), lambda bi, j: (bi, 0, j)),
            pl.BlockSpec((None, FFT_B, tc), lambda bi, j: (bi, 0, ncol + j)),
            pl.BlockSpec((2, 2 * FFT_B, 2 * FFT_B), lambda bi, j: (0, 0, 0)),
            pl.BlockSpec((2, 2 * LANES, LANES), lambda bi, j: (0, 0, 0)),
        ],
        out_specs=pl.BlockSpec((None, FFT_B, tc), lambda bi, j: (bi, 0, j)),
        compiler_params=_cparams("parallel", "parallel"),
        name="fourier_stage2",
    )(z2, z2, m2, mc)
    return y.reshape(b * seq, w)


def _dft_ctx_kernel(x_ref, cn_ref, sn_ref, cc_ref, sc_ref, o_ref, *, scale):
    x = x_ref[...]
    outs = []
    for g in range(x.shape[1] // LANES):
        xg = x[:, g * LANES:(g + 1) * LANES]
        outs.append(_dot_hi(cn_ref[...], _dot_hi(xg, cc_ref[...])) - _dot_hi(sn_ref[...], _dot_hi(xg, sc_ref[...])))
    o_ref[...] = jnp.concatenate(outs, axis=1) * scale


def _fourier_ctx(f_all, rows):
    b, lc = rows.batch, rows.ctx_len
    w = FOURIER_WIDTH
    cn, sn = _dft_cs(lc)
    cc, sc = _dft_cs(FOURIER_GROUP_DIM)
    blk0 = rows.n_lat // lc
    mat = lambda m: pl.BlockSpec(m.shape, lambda bi: (0, 0))
    consts = [jnp.asarray(m, dtype=F32) for m in (cn, sn, cc, sc)]
    return pl.pallas_call(
        functools.partial(_dft_ctx_kernel, scale=1.0 / math.sqrt(lc * FOURIER_GROUP_DIM)),
        out_shape=jax.ShapeDtypeStruct((b * lc, w), F32),
        grid=(b,),
        in_specs=[pl.BlockSpec((lc, w), lambda bi: (blk0 + bi, 0))] + [mat(m) for m in consts],
        out_specs=pl.BlockSpec((lc, w), lambda bi: (bi, 0)),
        compiler_params=_cparams("parallel"),
        name="fourier_ctx",
    )(f_all, *consts)


def _ab_out_kernel(x1l_ref, x1c_ref, x2_ref, w_ref, h_ref, mod_ref, o_ref, *, lat_tiles):
    x1 = jnp.where(pl.program_id(0) < lat_tiles, x1l_ref[...], x1c_ref[...])
    x = jnp.concatenate([x1.astype(BF16), x2_ref[...]], axis=1)
    o_ref[...] = h_ref[...] + mod_ref[2:3, :] * _dot(x, w_ref[...])


def _ab_out(x1_lat, x1_ctx, x2, w_out, h, mods, rows):
    d = h.shape[1]
    tm = rows.tile
    lt = rows.lat_tiles
    return pl.pallas_call(
        functools.partial(_ab_out_kernel, lat_tiles=lt),
        out_shape=jax.ShapeDtypeStruct((rows.n, d), F32),
        grid=(rows.tiles,),
        in_specs=[
            pl.BlockSpec((tm, x1_lat.shape[1]), lambda i: (jnp.minimum(i, lt - 1), 0)),
            pl.BlockSpec((tm, x1_ctx.shape[1]), lambda i: (jnp.maximum(i - lt, 0), 0)),
            pl.BlockSpec((tm, x2.shape[1]), lambda i: (i, 0)),
            pl.BlockSpec(w_out.shape, lambda i: (0, 0)),
            pl.BlockSpec((tm, d), lambda i: (i, 0)),
            pl.BlockSpec((None, None, 3, d), lambda i: (rows.group(i), 1, 0, 0)),
        ],
        out_specs=pl.BlockSpec((tm, d), lambda i: (i, 0)),
        compiler_params=_cparams("parallel"),
        name="ab_out",
    )(x1_lat, x1_ctx, x2, w_out, h, mods)


def _hgrn_out_kernel(ofw_ref, obw_ref, g_ref, gain_ref, w_ref, h_ref, mod_ref, o_ref):
    o = ofw_ref[...] + obw_ref[...]
    parts = []
    for hh in range(o.shape[1] // HEAD_DIM):
        oh = o[:, hh * HEAD_DIM:(hh + 1) * HEAD_DIM]
        parts.append(oh * lax.rsqrt(jnp.mean(oh * oh, axis=-1, keepdims=True) + EPS))
    on = jnp.concatenate(parts, axis=1) * gain_ref[...]
    y = (on * jax.nn.sigmoid(g_ref[...])).astype(BF16)
    o_ref[...] = h_ref[...] + mod_ref[2:3, :] * _dot(y, w_ref[...])


def _hgrn_out(o_fw, o_bw, p, gain, w_out, h, mods, rows_r, n_tiles):
    d = h.shape[1]
    tm = rows_r.tile
    g_blk = (p.shape[1] - d) // d
    return pl.pallas_call(
        _hgrn_out_kernel,
        out_shape=jax.ShapeDtypeStruct((n_tiles * tm, d), F32),
        grid=(n_tiles,),
        in_specs=[
            pl.BlockSpec((tm, d), lambda i: (i, 0)),
            pl.BlockSpec((tm, d), lambda i: (i, 0)),
            pl.BlockSpec((tm, d), lambda i: (i, g_blk)),
            pl.BlockSpec((1, d), lambda i: (0, 0)),
            pl.BlockSpec(w_out.shape, lambda i: (0, 0)),
            pl.BlockSpec((tm, d), lambda i: (i, 0)),
            pl.BlockSpec((None, None, 3, d), lambda i: (rows_r.group(i), 1, 0, 0)),
        ],
        out_specs=pl.BlockSpec((tm, d), lambda i: (i, 0)),
        compiler_params=_cparams("parallel"),
        name="hgrn_out",
    )(o_fw, o_bw, p, gain, w_out, h, mods)


def _hgrn_proj_kernel(h_ref, mod_ref, gain_ref, w_ref, o_ref, xn_ref, *, n_silu):
    j = pl.program_id(1)

    @pl.when(j == 0)
    def _():
        xn_ref[...] = _ada(h_ref[...], gain_ref[...], mod_ref[0:1, :], mod_ref[1:2, :]).astype(BF16)

    acc = _dot(xn_ref[...], w_ref[...])
    o_ref[...] = jnp.where(j < n_silu, _silu(acc), acc)


def _hgrn_proj(h, mods, gain, w_in, rows):
    d = h.shape[1]
    tm, tn = rows.tile, PROJ_COL_TILE
    nw = w_in.shape[1]
    return pl.pallas_call(
        functools.partial(_hgrn_proj_kernel, n_silu=d // tn),
        out_shape=jax.ShapeDtypeStruct((rows.n, nw), F32),
        grid=(rows.tiles, nw // tn),
        in_specs=[
            pl.BlockSpec((tm, d), lambda i, j: (i, 0)),
            pl.BlockSpec((None, None, 3, d), lambda i, j: (rows.group(i), 1, 0, 0)),
            pl.BlockSpec((1, d), lambda i, j: (0, 0)),
            pl.BlockSpec((d, tn), lambda i, j: (0, j)),
        ],
        out_specs=pl.BlockSpec((tm, tn), lambda i, j: (i, j)),
        scratch_shapes=[pltpu.VMEM((tm, d), BF16)],
        compiler_params=_cparams("parallel", "arbitrary"),
        name="hgrn_proj",
    )(h, mods, gain, w_in)


def _scan_levels(chunk):
    return [chunk >> (i + 1) for i in range(int(math.log2(chunk)))]


def _scan_consts(chunk, reverse):
    t = np.arange(chunk)[:, None]
    s = np.arange(chunk)[None, :]
    tri = (s >= t) if reverse else (s <= t)
    wide, narrow = [], []
    for h in _scan_levels(chunk):
        same = (t // (2 * h)) == (s // (2 * h))
        t_up = (t // h) % 2 == 1
        s_up = (s // h) % 2 == 1
        mask = same & ((~t_up & s_up) if reverse else (t_up & ~s_up))
        if h >= SUBLANES:
            wide.append(mask[np.nonzero(~t_up[:, 0] if reverse else t_up[:, 0])[0]])
        else:
            narrow.append(mask)
    narrow.append(t == s)
    return (jnp.asarray(tri, dtype=BF16), jnp.asarray(np.stack(wide), dtype=F32),
            jnp.asarray(np.stack(narrow), dtype=F32))


def _seg_bcast(x, h, reverse):
    c, w = x.shape
    off = h if reverse else h - 1
    if 2 * h >= 2 * SUBLANES:
        pieces = [jnp.broadcast_to(x[g * 2 * h + off:g * 2 * h + off + 1, :], (2 * h, w)) for g in range(c // (2 * h))]
        return pieces[0] if len(pieces) == 1 else jnp.concatenate(pieces, axis=0)
    x3 = x.reshape(c // SUBLANES, SUBLANES, w)
    sub = lax.broadcasted_iota(jnp.int32, x3.shape, 1)
    y = None
    for g in range(SUBLANES // (2 * h)):
        piece = jnp.broadcast_to(x3[:, g * 2 * h + off:g * 2 * h + off + 1, :], x3.shape)
        y = piece if y is None else jnp.where(sub >= g * 2 * h, piece, y)
    return y.reshape(c, w)


def _scan_kernel(q_ref, f_ref, v_ref, lb_ref, tri_ref, wmask_ref, nmask_ref, o_ref, st_ref, *, reverse, chunk):
    @pl.when(pl.program_id(2) == 0)
    def _():
        st_ref[...] = jnp.zeros_like(st_ref)

    rows, width = q_ref.shape
    levels = _scan_levels(chunk)
    spans = [slice(ci * chunk, (ci + 1) * chunk) for ci in range(rows // chunk)]
    lanes = [slice(hh * HEAD_DIM, (hh + 1) * HEAD_DIM) for hh in range(width // HEAD_DIM)]
    q = q_ref[...]
    fl = f_ref[...]
    lb = lb_ref[...]
    u = jnp.exp(-jnp.abs(fl))
    key = (1.0 - lb) * (jnp.where(fl > 0.0, u, 1.0) / (1.0 + u))
    log_sig = jnp.minimum(fl, 0.0) - jnp.log(1.0 + u)
    x1 = jnp.log(lb)
    x2 = jnp.log1p(-lb) + log_sig
    delta = x1 - x2
    lf = jnp.where(jnp.isnan(delta), x1 + x2, jnp.maximum(x1, x2) + jnp.log(1.0 + jnp.exp(-jnp.abs(delta))))
    lf = lf * LOG2_E
    l1 = lf.astype(BF16)
    r1 = lf - l1.astype(F32)
    l2 = r1.astype(BF16)
    l3 = (r1 - l2.astype(F32)).astype(BF16)
    parts = jnp.concatenate([l1, l2, l3], axis=1)
    tri = tri_ref[...]
    cs = jnp.concatenate([_dot(tri, parts[sp]) for sp in spans], axis=0)
    b = cs[:, :width] + cs[:, width:2 * width] + cs[:, 2 * width:]
    totals = [b[sp.start:sp.start + 1, :] if reverse else b[sp.stop - 1:sp.stop, :] for sp in spans]
    total_rows = jnp.concatenate([jnp.broadcast_to(t, (chunk, width)) for t in totals], axis=0)

    q_in = (q * jnp.exp2(b)).astype(BF16)
    k_out = (key * jnp.exp2(total_rows - b)).astype(BF16)
    v = v_ref[...].astype(BF16)

    order = list(range(len(spans)))[::-1] if reverse else list(range(len(spans)))
    kv = [[_dot_tn(v[sp, sl], k_out[sp, sl]) for sl in lanes] for sp in spans]
    o_state = [[None] * len(lanes) for _ in spans]
    for hh, sl in enumerate(lanes):
        st = st_ref[hh]
        for ci in order:
            o_state[ci][hh] = _dot_nt(q_in[spans[ci], sl], st.astype(BF16))
            st = st * jnp.exp2(totals[ci][:, sl]) + kv[ci][hh]
        st_ref[hh] = st

    n_blk = chunk // SUBLANES
    wide_levels = [h for h in levels if h >= SUBLANES]
    narrow_levels = [h for h in levels if h < SUBLANES]

    def narrow_scores(li, ql, kl):
        out = []
        for sp in spans:
            row = []
            for sl in lanes:
                s = nmask_ref[li] * _dot_nt(ql[sp, sl], kl[sp, sl])
                row.append([s[k * SUBLANES:(k + 1) * SUBLANES, :] for k in range(n_blk)])
            out.append(row)
        return out

    a = narrow_scores(len(narrow_levels), q.astype(BF16), key.astype(BF16))
    row_id = lax.broadcasted_iota(jnp.int32, (rows, width), 0)
    for li, h in enumerate(narrow_levels):
        if h == 1:
            e = jnp.where((row_id & 1) == (0 if reverse else 1), jnp.exp2(lf), 1.0)
        else:
            e = jnp.exp2(-jnp.abs(b - _seg_bcast(b, h, reverse)))
        new = narrow_scores(li, (q * e).astype(BF16), (key * e).astype(BF16))
        a = [[[x + y for x, y in zip(xb, yb)] for xb, yb in zip(xa, ya)] for xa, ya in zip(a, new)]

    for li, h in enumerate(wide_levels):
        q_parts, k_parts = [], []
        for g in range(rows // (2 * h)):
            lo = slice(g * 2 * h, g * 2 * h + h)
            hi = slice(g * 2 * h + h, (g + 1) * 2 * h)
            r = g * 2 * h + (h if reverse else h - 1)
            b_ref = jnp.broadcast_to(b[r:r + 1, :], (h, width))
            q_half, k_half = (lo, hi) if reverse else (hi, lo)
            q_parts.append(q[q_half] * jnp.exp2(b[q_half] - b_ref))
            k_part = key[k_half] * jnp.exp2(b_ref - b[k_half])
            zeros = jnp.zeros((h, width), F32)
            k_parts += [zeros, k_part] if reverse else [k_part, zeros]
        q_sel = jnp.concatenate(q_parts, axis=0).astype(BF16)
        k_hat = jnp.concatenate(k_parts, axis=0).astype(BF16)
        half = chunk // 2
        for ci, sp in enumerate(spans):
            for hh, sl in enumerate(lanes):
                s = wmask_ref[li] * _dot_nt(q_sel[ci * half:(ci + 1) * half, sl], k_hat[sp, sl])
                for j in range(chunk // (2 * h)):
                    first = (j * 2 * h + (0 if reverse else h)) // SUBLANES
                    for k in range(h // SUBLANES):
                        r0 = j * h + k * SUBLANES
                        a[ci][hh][first + k] = a[ci][hh][first + k] + s[r0:r0 + SUBLANES, :]

    for ci, sp in enumerate(spans):
        for hh, sl in enumerate(lanes):
            pairs = jnp.concatenate(a[ci][hh], axis=0).astype(BF16)
            o_ref[sp, sl] = o_state[ci][hh] + _dot(pairs, v[sp, sl])


def _hgrn_scan(p, lower_bound, rows, reverse):
    b, seq, lc = rows.batch, rows.seq, rows.ctx_len
    d = lower_bound.shape[0]
    c, hb = SCAN_CHUNK, SCAN_HEADS_PER_BLOCK
    r = c * SCAN_CHUNKS_PER_STEP
    wb = hb * HEAD_DIM
    ncb = d // wb
    assert lc % r == 0 and seq % r == 0
    nctx, nlat = lc // r, seq // r
    ctx0 = rows.n_lat // r
    f_blk = (2 if reverse else 1) * ncb
    v_blk = 3 * ncb

    def row(bi, s):
        if reverse:
            return jnp.where(s < nctx, ctx0 + bi * nctx + (nctx - 1 - s), bi * nlat + (nlat - 1 - (s - nctx)))
        return jnp.where(s < nctx, ctx0 + bi * nctx + s, bi * nlat + (s - nctx))

    tri, wmasks, nmasks = _scan_consts(c, reverse)
    return pl.pallas_call(
        functools.partial(_scan_kernel, reverse=reverse, chunk=c),
        out_shape=jax.ShapeDtypeStruct((rows.n, d), F32),
        grid=(b, ncb, nctx + nlat),
        in_specs=[
            pl.BlockSpec((r, wb), lambda bi, hi, s: (row(bi, s), hi)),
            pl.BlockSpec((r, wb), lambda bi, hi, s: (row(bi, s), f_blk + hi)),
            pl.BlockSpec((r, wb), lambda bi, hi, s: (row(bi, s), v_blk + hi)),
            pl.BlockSpec((1, wb), lambda bi, hi, s: (0, hi)),
            pl.BlockSpec(tri.shape, lambda bi, hi, s: (0, 0)),
            pl.BlockSpec(wmasks.shape, lambda bi, hi, s: (0, 0, 0)),
            pl.BlockSpec(nmasks.shape, lambda bi, hi, s: (0, 0, 0)),
        ],
        out_specs=pl.BlockSpec((r, wb), lambda bi, hi, s: (row(bi, s), hi)),
        scratch_shapes=[pltpu.VMEM((hb, HEAD_DIM, HEAD_DIM), F32)],
        compiler_params=_cparams("parallel", "parallel", "arbitrary"),
        name="hgrn_scan_bw" if reverse else "hgrn_scan_fw",
    )(p, p, p, lower_bound.reshape(1, d), tri, wmasks, nmasks)


def kernel(x, c, ctx, c_ctx, w_mod, b_mod, norm_gains, ffn_w_in, ffn_w_out, ab_w_in, qk_norm, ab_w_out,
           hgrn_w_in, hgrn_lb_logits, hgrn_o_norm, hgrn_w_out, final_norm):
    batch, seq, d = x.shape
    lc = ctx.shape[1]
    depth = w_mod.shape[0]
    rows = _Rows(batch, seq, lc, ROW_TILE)
    rows_r = _Rows(batch, seq, lc, READOUT_ROW_TILE)
    assert seq % GRID_W == 0 and batch + 1 <= SUBLANES

    c_rows = jnp.concatenate([c_ctx[None, :], c, jnp.zeros((SUBLANES - 1 - batch, d), F32)], axis=0)
    mods_all = _modulation(c_rows, w_mod, b_mod).reshape(depth, SUBLANES, 3, 3, d)

    lb_cum = jnp.cumsum(jax.nn.softmax(hgrn_lb_logits.astype(F32), axis=0), axis=0)
    lower_bounds = lb_cum - lb_cum[0]

    ffn_w_in_b = ffn_w_in.astype(BF16)
    ffn_w_out_b = ffn_w_out.astype(BF16)
    fin = final_norm.reshape(1, d)
    rope = _rope_tables(seq, ROW_TILE)

    h = (x.reshape(batch * seq, d), ctx.reshape(batch * lc, d))
    for layer in range(depth):
        last = layer == depth - 1
        mods = mods_all[layer]
        gains = norm_gains[layer].reshape(3, 1, d)
        h = _ffn(h, mods, 0, gains[0], ffn_w_in_b, ffn_w_out_b, layer, 0, fin, rows, rows.tiles, False)
        if layer % 2 == 0:
            e = layer // 2
            f_all, q_all, k_all, vt_all = _ab_proj(h, mods, gains[1], ab_w_in[e].astype(BF16), qk_norm[e], rope, rows)
            attn = _attention(q_all, k_all, vt_all, rows)
            attn = _attention(q_all, k_all, vt_all, rows, lat_out=attn)
            h = _ab_out(_fourier_latent(f_all, rows), _fourier_ctx(f_all, rows), attn,
                        ab_w_out[e].astype(BF16), h, mods, rows)
        else:
            o = layer // 2
            p = _hgrn_proj(h, mods, gains[1], hgrn_w_in[o].astype(BF16), rows)
            o_fw = _hgrn_scan(p, lower_bounds[layer], rows, False)
            o_bw = _hgrn_scan(p, lower_bounds[layer], rows, True)
            gain_o = jnp.tile(hgrn_o_norm[o], d // HEAD_DIM).reshape(1, d)
            n_t = rows_r.lat_tiles if last else rows_r.tiles
            h = _hgrn_out(o_fw, o_bw, p, gain_o, hgrn_w_out[o].astype(BF16), h, mods, rows_r, n_t)
        n_t = rows.lat_tiles if last else rows.tiles
        h = _ffn(h, mods, 2, gains[2], ffn_w_in_b, ffn_w_out_b, layer, 1, fin, rows, n_t, last)
    return h[:batch * seq].reshape(batch, seq, d)
```

```python
import functools
import math

import jax
import jax.numpy as jnp
import numpy as np
from jax import lax
from jax.experimental import pallas as pl
from jax.experimental.pallas import tpu as pltpu

F32 = jnp.float32
BF16 = jnp.bfloat16

EPS = 1e-6
N_MOD = 9
HEAD_DIM = 128
N_KV_HEADS = 4
FOURIER_WIDTH = 512
FOURIER_GROUP_DIM = 128
GRID_W = 64
ROPE_THETA = 10000.0
ROPE_AXIS_DIM = HEAD_DIM // 2
ATTN_SCALE = HEAD_DIM ** -0.5
LOG2_E = math.log2(math.e)

LANES = 128
SUBLANES = 8
BF16_SUBLANES = 16
VT_ROWS = HEAD_DIM + BF16_SUBLANES
VMEM_LIMIT_BYTES = 56 * 1024 * 1024

ROW_TILE = 512
READOUT_ROW_TILE = 256
FFN_TILE = 512
PROJ_COL_TILE = 2048
MOD_COL_TILE = 1024
ATTN_Q_TILE = 1024
SCAN_CHUNK = 128
SCAN_HEADS_PER_BLOCK = 8
SCAN_CHUNKS_PER_STEP = 2
FFT_B = 128
FFT_B_BLOCK = 8


def _cparams(*sem):
    return pltpu.CompilerParams(dimension_semantics=sem, vmem_limit_bytes=VMEM_LIMIT_BYTES)


def _dot(a, b):
    return jnp.dot(a, b, preferred_element_type=F32)


def _dot_hi(a, b):
    return jnp.dot(a, b, preferred_element_type=F32, precision=lax.Precision.HIGHEST)


def _split_bf16(x):
    hi = x.astype(BF16)
    return hi, (x - hi.astype(F32)).astype(BF16)


def _dot_split(a, b):
    (ah, al), (bh, bl) = a, b
    return _dot(ah, bh) + _dot(al, bh) + _dot(ah, bl)


def _dot_nt(a, b):
    return lax.dot_general(a, b, (((1,), (1,)), ((), ())), preferred_element_type=F32)


def _dot_tn(a, b):
    return lax.dot_general(a, b, (((0,), (0,)), ((), ())), preferred_element_type=F32)


def _silu(x):
    return x * jax.nn.sigmoid(x)


def _rms(x, gain):
    return x * lax.rsqrt(jnp.mean(x * x, axis=-1, keepdims=True) + EPS) * gain


def _ada(h, gain, shift, scale):
    return _rms(h, gain) * (1.0 + scale) + shift


def _mod_kernel(c_ref, w_ref, b_ref, o_ref):
    a = _silu(c_ref[...]).astype(BF16)
    o_ref[...] = _dot(a, w_ref[...].astype(BF16)) + b_ref[...]


def _modulation(c_rows, w_mod, b_mod):
    depth, d, nd = w_mod.shape
    tn = MOD_COL_TILE
    return pl.pallas_call(
        _mod_kernel,
        out_shape=jax.ShapeDtypeStruct((depth, SUBLANES, nd), F32),
        grid=(depth, nd // tn),
        in_specs=[
            pl.BlockSpec((SUBLANES, d), lambda l, j: (0, 0)),
            pl.BlockSpec((None, d, tn), lambda l, j: (l, 0, j)),
            pl.BlockSpec((None, 1, tn), lambda l, j: (l, 0, j)),
        ],
        out_specs=pl.BlockSpec((None, SUBLANES, tn), lambda l, j: (l, 0, j)),
        compiler_params=_cparams("parallel", "arbitrary"),
        name="modulation",
    )(c_rows, w_mod, b_mod.reshape(depth, 1, nd))


class _Rows:
    def __init__(self, batch, seq, ctx_len, tile):
        assert seq % tile == 0 and (batch * ctx_len) % tile == 0
        self.batch, self.seq, self.ctx_len, self.tile = batch, seq, ctx_len, tile
        self.n_lat = batch * seq
        self.n = self.n_lat + batch * ctx_len
        self.lat_tiles = self.n_lat // tile
        self.tiles = self.n // tile
        self.tiles_per_batch = seq // tile

    def group(self, i):
        return jnp.where(i < self.lat_tiles, 1 + i // self.tiles_per_batch, 0)


def _ffn_kernel(*refs, final, split_tiles):
    if split_tiles is None:
        h_ref, mod_ref, gain_ref, wa_ref, wb_ref, wo_ref, fin_ref, o_ref, xn_ref, acc_ref = refs
        load_h = lambda: h_ref[...]
    else:
        hl_ref, hc_ref, mod_ref, gain_ref, wa_ref, wb_ref, wo_ref, fin_ref, o_ref, xn_ref, acc_ref = refs
        load_h = lambda: jnp.where(pl.program_id(0) < split_tiles, hl_ref[...], hc_ref[...])
    i = pl.program_id(0)
    j = pl.program_id(1)

    @pl.when(j == 0)
    def _():
        xn_ref[...] = _ada(load_h(), gain_ref[...], mod_ref[0:1, :], mod_ref[1:2, :]).astype(BF16)

    @pl.when(jnp.logical_and(i == 0, j == 0))
    def _():
        acc_ref[...] = jnp.zeros_like(acc_ref)

    xn = xn_ref[...]
    a = _dot(xn, wa_ref[...])
    b = _dot(xn, wb_ref[...])
    g = (_silu(a) * b).astype(BF16)
    acc_ref[...] = jnp.where(j == 0, 0.0, acc_ref[...]) + _dot(g, wo_ref[...])

    @pl.when(j == pl.num_programs(1) - 1)
    def _():
        out = load_h() + 0.5 * mod_ref[2:3, :] * acc_ref[...]
        if final:
            out = _rms(out, fin_ref[...])
        o_ref[...] = out


def _ffn(h, mods, sub, gain, w_in, w_out, layer, which, fin, rows, n_tiles, final):
    f, d = w_out.shape[2:]
    tm, tf = rows.tile, FFN_TILE
    nf = f // tf
    if isinstance(h, tuple):
        lt = rows.lat_tiles
        row_specs = [pl.BlockSpec((tm, d), lambda i, j: (jnp.minimum(i, lt - 1), 0)),
                     pl.BlockSpec((tm, d), lambda i, j: (jnp.maximum(i - lt, 0), 0))]
        row_args, split_tiles = h, lt
    else:
        row_specs = [pl.BlockSpec((tm, d), lambda i, j: (i, 0))]
        row_args, split_tiles = (h,), None
    return pl.pallas_call(
        functools.partial(_ffn_kernel, final=final, split_tiles=split_tiles),
        out_shape=jax.ShapeDtypeStruct((n_tiles * tm, d), F32),
        grid=(n_tiles, nf),
        in_specs=row_specs + [
            pl.BlockSpec((None, None, 3, d), lambda i, j: (rows.group(i), sub, 0, 0)),
            pl.BlockSpec((1, d), lambda i, j: (0, 0)),
            pl.BlockSpec((None, None, d, tf), lambda i, j: (layer, which, 0, j)),
            pl.BlockSpec((None, None, d, tf), lambda i, j: (layer, which, 0, nf + j)),
            pl.BlockSpec((None, None, tf, d), lambda i, j: (layer, which, j, 0)),
            pl.BlockSpec((1, d), lambda i, j: (0, 0)),
        ],
        out_specs=pl.BlockSpec((tm, d), lambda i, j: (i, 0)),
        scratch_shapes=[pltpu.VMEM((tm, d), BF16), pltpu.VMEM((tm, d), F32)],
        compiler_params=_cparams("arbitrary", "arbitrary"),
        name="ffn",
    )(*row_args, mods, gain, w_in, w_in, w_out, fin)


def _rope_tables(seq, tile):
    t = np.arange(seq)
    inv_freq = ROPE_THETA ** (-np.arange(0, ROPE_AXIS_DIM, 2, dtype=np.float64) / ROPE_AXIS_DIM)
    ang = np.concatenate([(t // GRID_W)[:, None] * inv_freq, (t % GRID_W)[:, None] * inv_freq], axis=-1)
    nf = ROPE_AXIS_DIM // 2
    cos = np.cos(ang).reshape(seq, 2, 1, nf)
    sin = np.sin(ang).reshape(seq, 2, 1, nf)
    zero = np.zeros_like(sin)
    c_full = np.broadcast_to(cos, (seq, 2, 2, nf)).reshape(seq, HEAD_DIM)
    s_up = np.concatenate([-sin, zero], axis=2).reshape(seq, HEAD_DIM)
    s_dn = np.concatenate([zero, sin], axis=2).reshape(seq, HEAD_DIM)
    lat = np.concatenate([c_full, s_up, s_dn], axis=1)
    ident = np.concatenate([np.ones((tile, HEAD_DIM)), np.zeros((tile, 2 * HEAD_DIM))], axis=1)
    return jnp.asarray(np.concatenate([lat, ident], axis=0), dtype=F32)


def _norm_rope_heads(acc, gain, rope, post_scale=None):
    nf = ROPE_AXIS_DIM // 2
    c, s_up, s_dn = rope[:, :HEAD_DIM], rope[:, HEAD_DIM:2 * HEAD_DIM], rope[:, 2 * HEAD_DIM:]
    heads = []
    for hh in range(acc.shape[1] // HEAD_DIM):
        y = _rms(acc[:, hh * HEAD_DIM:(hh + 1) * HEAD_DIM], gain)
        y = y * c + pltpu.roll(y, HEAD_DIM - nf, 1) * s_up + pltpu.roll(y, nf, 1) * s_dn
        heads.append(y if post_scale is None else y * post_scale)
    return heads


def _ab_proj_kernel(h_ref, mod_ref, gain_ref, w_ref, qkn_ref, rope_ref, f_ref, qt_ref, k_ref, vt_ref):
    xn = _ada(h_ref[...], gain_ref[...], mod_ref[0:1, :], mod_ref[1:2, :]).astype(BF16)
    c0 = f_ref.shape[1]
    c1 = c0 + qt_ref.shape[0] * HEAD_DIM
    c2 = c1 + k_ref.shape[1]
    q_heads = _norm_rope_heads(_dot(xn, w_ref[:, c0:c1]), qkn_ref[0:1, :], rope_ref[...], ATTN_SCALE * LOG2_E)
    for hh, y in enumerate(q_heads):
        qt_ref[hh] = y.T.astype(BF16)
    k_heads = _norm_rope_heads(_dot(xn, w_ref[:, c1:c2]), qkn_ref[1:2, :], rope_ref[...])
    k_ref[...] = jnp.concatenate(k_heads, axis=1).astype(BF16)
    v = _dot(xn, w_ref[:, c2:])
    ones = jnp.ones((vt_ref.shape[1] - HEAD_DIM, vt_ref.shape[2]), BF16)
    for hh in range(N_KV_HEADS):
        vt_ref[hh, :HEAD_DIM, :] = v[:, hh * HEAD_DIM:(hh + 1) * HEAD_DIM].T.astype(BF16)
        vt_ref[hh, HEAD_DIM:, :] = ones
    f_ref[...] = _dot(xn, w_ref[:, :c0])


def _ab_proj(h, mods, gain, w_in, qk_norm, rope, rows):
    d = h.shape[1]
    tm = rows.tile
    kv_width = N_KV_HEADS * HEAD_DIM
    q_width = w_in.shape[1] - FOURIER_WIDTH - 2 * kv_width
    n = rows.n
    rope_blk = lambda i: (jnp.where(i < rows.lat_tiles, i % rows.tiles_per_batch, rows.tiles_per_batch), 0)
    return pl.pallas_call(
        _ab_proj_kernel,
        out_shape=(
            jax.ShapeDtypeStruct((n, FOURIER_WIDTH), F32),
            jax.ShapeDtypeStruct((q_width // HEAD_DIM, rows.tiles, HEAD_DIM, tm), BF16),
            jax.ShapeDtypeStruct((n, kv_width), BF16),
            jax.ShapeDtypeStruct((N_KV_HEADS, rows.tiles, VT_ROWS, tm), BF16),
        ),
        grid=(rows.tiles,),
        in_specs=[
            pl.BlockSpec((tm, d), lambda i: (i, 0)),
            pl.BlockSpec((None, None, 3, d), lambda i: (rows.group(i), 1, 0, 0)),
            pl.BlockSpec((1, d), lambda i: (0, 0)),
            pl.BlockSpec(w_in.shape, lambda i: (0, 0)),
            pl.BlockSpec((2, HEAD_DIM), lambda i: (0, 0)),
            pl.BlockSpec((tm, 3 * HEAD_DIM), rope_blk),
        ],
        out_specs=(
            pl.BlockSpec((tm, FOURIER_WIDTH), lambda i: (i, 0)),
            pl.BlockSpec((q_width // HEAD_DIM, None, HEAD_DIM, tm), lambda i: (0, i, 0, 0)),
            pl.BlockSpec((tm, kv_width), lambda i: (i, 0)),
            pl.BlockSpec((N_KV_HEADS, None, VT_ROWS, tm), lambda i: (0, i, 0, 0)),
        ),
        compiler_params=_cparams("parallel"),
        name="ab_proj",
    )(h, mods, gain, w_in, qk_norm, rope)


def _attn_kernel(*refs, lat_chunks, group):
    if lat_chunks:
        qt_ref, kc_ref, vtc_ref, kl_ref, vtl_ref, o_ref, acc_ref, s_ref = refs
        qt = jnp.concatenate([qt_ref[g, t] for g in range(group) for t in range(qt_ref.shape[1])], axis=1)
    else:
        qt_ref, kc_ref, vtc_ref, _, o_ref, acc_ref = refs
        qt = jnp.concatenate([qt_ref[g] for g in range(group)], axis=1)
    tq = o_ref.shape[0]
    nq = group * tq
    acc_ref[...] = jnp.zeros_like(acc_ref)

    def scores(k):
        return _dot(k, qt)

    def update(s, vt, m):
        m_new = jnp.maximum(m, jnp.max(s, axis=0, keepdims=True))
        p = jnp.exp2(s - m_new).astype(BF16)
        acc_ref[...] = jnp.exp2(m - m_new) * acc_ref[...] + _dot(vt, p)
        return m_new

    m = update(scores(kc_ref[...]), vtc_ref[...], jnp.full((1, nq), -jnp.inf, F32))
    if lat_chunks:
        tk = vtl_ref.shape[2]

        def lat_scores(c):
            return scores(kl_ref[pl.ds(pl.multiple_of(c * tk, tk), tk), :])

        assert lat_chunks % 2 == 0
        s_ref[0] = lat_scores(0)

        def body(i, m):
            c = 2 * i
            s_ref[1] = lat_scores(c + 1)
            m = update(s_ref[0], vtl_ref[c], m)
            s_ref[0] = lat_scores(jnp.minimum(c + 2, lat_chunks - 1))
            return update(s_ref[1], vtl_ref[c + 1], m)

        m = lax.fori_loop(0, lat_chunks // 2, body, m)

    out = (acc_ref[:HEAD_DIM, :] / acc_ref[HEAD_DIM:HEAD_DIM + 1, :]).T
    o_ref[...] = jnp.concatenate([out[g * tq:(g + 1) * tq, :] for g in range(group)], axis=1).astype(BF16)


def _attention(qt_all, k_all, vt_all, rows, lat_out=None):
    b, seq, lc, tile = rows.batch, rows.seq, rows.ctx_len, rows.tile
    group = qt_all.shape[0] // N_KV_HEADS
    gw = group * HEAD_DIM
    assert tile % lc == 0 and seq % tile == 0
    ctx_blk0 = rows.n_lat // lc
    ctx_tile = lambda bi: ((rows.n_lat + bi * lc) // tile, 0, ((rows.n_lat + bi * lc) % tile) // lc)
    kc_spec = pl.BlockSpec((lc, HEAD_DIM), lambda bi, hi, i: (ctx_blk0 + bi, hi))
    vtc_spec = pl.BlockSpec((None, None, VT_ROWS, lc), lambda bi, hi, i: (hi,) + ctx_tile(bi))
    if lat_out is None:
        tq = ATTN_Q_TILE
        assert tq % tile == 0
        nqt = seq // tq
        lat_chunks = seq // tile
        q_spec = pl.BlockSpec((group, tq // tile, HEAD_DIM, tile), lambda bi, hi, i: (hi, bi * nqt + i, 0, 0))
        o_spec = pl.BlockSpec((tq, gw), lambda bi, hi, i: (bi * nqt + i, hi))
        kl_spec = pl.BlockSpec((seq, HEAD_DIM), lambda bi, hi, i: (bi, hi))
        vtl_spec = pl.BlockSpec((None, lat_chunks, VT_ROWS, tile), lambda bi, hi, i: (hi, bi, 0, 0))
        in_specs = [q_spec, kc_spec, vtc_spec, kl_spec, vtl_spec]
        args = (qt_all, k_all, vt_all, k_all, vt_all)
        aliases = {}
    else:
        tq = lc
        nqt = 1
        lat_chunks = 0
        q_spec = pl.BlockSpec((group, None, HEAD_DIM, lc), lambda bi, hi, i: (hi,) + ctx_tile(bi))
        o_spec = pl.BlockSpec((tq, gw), lambda bi, hi, i: (ctx_blk0 + bi, hi))
        in_specs = [q_spec, kc_spec, vtc_spec, pl.BlockSpec(memory_space=pl.ANY)]
        args = (qt_all, k_all, vt_all, lat_out)
        aliases = {3: 0}
    return pl.pallas_call(
        functools.partial(_attn_kernel, lat_chunks=lat_chunks, group=group),
        out_shape=jax.ShapeDtypeStruct((rows.n, qt_all.shape[0] * HEAD_DIM), BF16),
        grid=(b, N_KV_HEADS, nqt),
        in_specs=in_specs,
        out_specs=o_spec,
        scratch_shapes=[pltpu.VMEM((VT_ROWS, group * tq), F32)]
        + ([pltpu.VMEM((2, tile, group * tq), F32)] if lat_chunks else []),
        input_output_aliases=aliases,
        compiler_params=_cparams("parallel", "parallel", "arbitrary"),
        name="attention_ctx" if lat_chunks == 0 else "attention_lat",
    )(*args)


def _dft_cs(n):
    idx = np.arange(n)
    ang = 2.0 * np.pi * ((idx[:, None] * idx[None, :]) % n) / n
    return np.cos(ang), np.sin(ang)


def _const_split(m):
    m = np.asarray(m, dtype=np.float32)
    hi = jnp.asarray(m).astype(BF16)
    lo = (jnp.asarray(m) - hi.astype(F32)).astype(BF16)
    return jnp.stack([hi, lo])


def _fft1_kernel(x_ref, fa_ref, tw_ref, o_ref, *, a):
    w = o_ref.shape[2]
    for r in range(o_ref.shape[0]):
        z = _dot_split((fa_ref[0], fa_ref[1]), _split_bf16(x_ref[:, r * w:(r + 1) * w]))
        zr, zi = z[:a], z[a:]
        tc = jnp.concatenate([tw_ref[r, 0]] * (w // LANES), axis=1)
        ts = jnp.concatenate([tw_ref[r, 1]] * (w // LANES), axis=1)
        o_ref[r, :a, :] = zr * tc - zi * ts
        o_ref[r, a:, :] = zr * ts + zi * tc


def _fft2_kernel(zr_ref, zi_ref, m2_ref, mc_ref, o_ref, *, scale):
    for r in range(o_ref.shape[1]):
        z = jnp.concatenate([zr_ref[:, r, :], zi_ref[:, r, :]], axis=0)
        v = _dot_split((m2_ref[0], m2_ref[1]), _split_bf16(z))
        vr, vi = v[:FFT_B], v[FFT_B:]
        outs = []
        for g in range(o_ref.shape[2] // LANES):
            u = jnp.concatenate([vr[:, g * LANES:(g + 1) * LANES], vi[:, g * LANES:(g + 1) * LANES]], axis=1)
            outs.append(_dot_split(_split_bf16(u), (mc_ref[0], mc_ref[1])))
        o_ref[:, r, :] = jnp.concatenate(outs, axis=1) * scale


def _fourier_latent(f_all, rows):
    b, seq = rows.batch, rows.seq
    w = FOURIER_WIDTH
    a = seq // FFT_B
    assert a % SUBLANES == 0 and FOURIER_GROUP_DIM == LANES
    ca, sa = _dft_cs(a)
    fa = _const_split(np.concatenate([ca, sa], axis=0))
    p1b = (np.arange(a)[None, :] * np.arange(FFT_B)[:, None]) % seq
    ang = 2.0 * np.pi * p1b / seq
    tw = np.stack([np.cos(ang), np.sin(ang)], axis=1)[..., None]
    tw = jnp.asarray(np.broadcast_to(tw, (FFT_B, 2, a, LANES)), dtype=F32)
    cb, sb = _dft_cs(FFT_B)
    m2 = _const_split(np.block([[cb, -sb], [sb, cb]]))
    cc, sc = _dft_cs(FOURIER_GROUP_DIM)
    mc = _const_split(np.concatenate([cc, -sc], axis=0))

    blk = FFT_B_BLOCK
    x2 = f_all.reshape(rows.n // FFT_B, FFT_B * w)
    z = pl.pallas_call(
        functools.partial(_fft1_kernel, a=a),
        out_shape=jax.ShapeDtypeStruct((b, FFT_B, 2 * a, w), F32),
        grid=(b, FFT_B // blk),
        in_specs=[
            pl.BlockSpec((a, blk * w), lambda bi, j: (bi, j)),
            pl.BlockSpec((2, 2 * a, a), lambda bi, j: (0, 0, 0)),
            pl.BlockSpec((blk, 2, a, LANES), lambda bi, j: (j, 0, 0, 0)),
        ],
        out_specs=pl.BlockSpec((None, blk, 2 * a, w), lambda bi, j: (bi, j, 0, 0)),
        compiler_params=_cparams("parallel", "parallel"),
        name="fourier_stage1",
    )(x2, fa, tw)

    nblk = a // blk
    y = pl.pallas_call(
        functools.partial(_fft2_kernel, scale=1.0 / math.sqrt(seq * FOURIER_GROUP_DIM)),
        out_shape=jax.ShapeDtypeStruct((b, FFT_B, a, w), F32),
        grid=(b, nblk),
        in_specs=[
            pl.BlockSpec((None, FFT_B, blk, w), lambda bi, j: (bi, 0, j, 0)),
            pl.BlockSpec((None, FFT_B, blk, w), lambda bi, j: (bi, 0, nblk + j, 0)),
            pl.BlockSpec((2, 2 * FFT_B, 2 * FFT_B), lambda bi, j: (0, 0, 0)),
            pl.BlockSpec((2, 2 * LANES, LANES), lambda bi, j: (0, 0, 0)),
        ],
        out_specs=pl.BlockSpec((None, FFT_B, blk, w), lambda bi, j: (bi, 0, j, 0)),
        compiler_params=_cparams("parallel", "parallel"),
        name="fourier_stage2",
    )(z, z, m2, mc)
    return y.reshape(b * seq, w)


def _dft_ctx_kernel(x_ref, cn_ref, sn_ref, cc_ref, sc_ref, o_ref, *, scale):
    x = x_ref[...]
    outs = []
    for g in range(x.shape[1] // LANES):
        xg = x[:, g * LANES:(g + 1) * LANES]
        outs.append(_dot_hi(cn_ref[...], _dot_hi(xg, cc_ref[...])) - _dot_hi(sn_ref[...], _dot_hi(xg, sc_ref[...])))
    o_ref[...] = jnp.concatenate(outs, axis=1) * scale


def _fourier_ctx(f_all, rows):
    b, lc = rows.batch, rows.ctx_len
    w = FOURIER_WIDTH
    cn, sn = _dft_cs(lc)
    cc, sc = _dft_cs(FOURIER_GROUP_DIM)
    blk0 = rows.n_lat // lc
    mat = lambda m: pl.BlockSpec(m.shape, lambda bi: (0, 0))
    consts = [jnp.asarray(m, dtype=F32) for m in (cn, sn, cc, sc)]
    return pl.pallas_call(
        functools.partial(_dft_ctx_kernel, scale=1.0 / math.sqrt(lc * FOURIER_GROUP_DIM)),
        out_shape=jax.ShapeDtypeStruct((b * lc, w), F32),
        grid=(b,),
        in_specs=[pl.BlockSpec((lc, w), lambda bi: (blk0 + bi, 0))] + [mat(m) for m in consts],
        out_specs=pl.BlockSpec((lc, w), lambda bi: (bi, 0)),
        compiler_params=_cparams("parallel"),
        name="fourier_ctx",
    )(f_all, *consts)


def _ab_out_kernel(x1l_ref, x1c_ref, x2_ref, w_ref, h_ref, mod_ref, o_ref, *, lat_tiles):
    x1 = jnp.where(pl.program_id(0) < lat_tiles, x1l_ref[...], x1c_ref[...])
    x = jnp.concatenate([x1.astype(BF16), x2_ref[...]], axis=1)
    o_ref[...] = h_ref[...] + mod_ref[2:3, :] * _dot(x, w_ref[...])


def _ab_out(x1_lat, x1_ctx, x2, w_out, h, mods, rows):
    d = h.shape[1]
    tm = rows.tile
    lt = rows.lat_tiles
    return pl.pallas_call(
        functools.partial(_ab_out_kernel, lat_tiles=lt),
        out_shape=jax.ShapeDtypeStruct((rows.n, d), F32),
        grid=(rows.tiles,),
        in_specs=[
            pl.BlockSpec((tm, x1_lat.shape[1]), lambda i: (jnp.minimum(i, lt - 1), 0)),
            pl.BlockSpec((tm, x1_ctx.shape[1]), lambda i: (jnp.maximum(i - lt, 0), 0)),
            pl.BlockSpec((tm, x2.shape[1]), lambda i: (i, 0)),
            pl.BlockSpec(w_out.shape, lambda i: (0, 0)),
            pl.BlockSpec((tm, d), lambda i: (i, 0)),
            pl.BlockSpec((None, None, 3, d), lambda i: (rows.group(i), 1, 0, 0)),
        ],
        out_specs=pl.BlockSpec((tm, d), lambda i: (i, 0)),
        compiler_params=_cparams("parallel"),
        name="ab_out",
    )(x1_lat, x1_ctx, x2, w_out, h, mods)


def _hgrn_out_kernel(ofw_ref, obw_ref, g_ref, gain_ref, w_ref, h_ref, mod_ref, o_ref):
    o = ofw_ref[...] + obw_ref[...]
    parts = []
    for hh in range(o.shape[1] // HEAD_DIM):
        oh = o[:, hh * HEAD_DIM:(hh + 1) * HEAD_DIM]
        parts.append(oh * lax.rsqrt(jnp.mean(oh * oh, axis=-1, keepdims=True) + EPS))
    on = jnp.concatenate(parts, axis=1) * gain_ref[...]
    y = (on * jax.nn.sigmoid(g_ref[...])).astype(BF16)
    o_ref[...] = h_ref[...] + mod_ref[2:3, :] * _dot(y, w_ref[...])


def _hgrn_out(o_fw, o_bw, p, gain, w_out, h, mods, rows_r, n_tiles):
    d = h.shape[1]
    tm = rows_r.tile
    g_blk = (p.shape[1] - d) // d
    return pl.pallas_call(
        _hgrn_out_kernel,
        out_shape=jax.ShapeDtypeStruct((n_tiles * tm, d), F32),
        grid=(n_tiles,),
        in_specs=[
            pl.BlockSpec((tm, d), lambda i: (i, 0)),
            pl.BlockSpec((tm, d), lambda i: (i, 0)),
            pl.BlockSpec((tm, d), lambda i: (i, g_blk)),
            pl.BlockSpec((1, d), lambda i: (0, 0)),
            pl.BlockSpec(w_out.shape, lambda i: (0, 0)),
            pl.BlockSpec((tm, d), lambda i: (i, 0)),
            pl.BlockSpec((None, None, 3, d), lambda i: (rows_r.group(i), 1, 0, 0)),
        ],
        out_specs=pl.BlockSpec((tm, d), lambda i: (i, 0)),
        compiler_params=_cparams("parallel"),
        name="hgrn_out",
    )(o_fw, o_bw, p, gain, w_out, h, mods)


def _hgrn_proj_kernel(h_ref, mod_ref, gain_ref, w_ref, o_ref, xn_ref, *, n_silu):
    j = pl.program_id(1)

    @pl.when(j == 0)
    def _():
        xn_ref[...] = _ada(h_ref[...], gain_ref[...], mod_ref[0:1, :], mod_ref[1:2, :]).astype(BF16)

    acc = _dot(xn_ref[...], w_ref[...])
    o_ref[...] = jnp.where(j < n_silu, _silu(acc), acc)


def _hgrn_proj(h, mods, gain, w_in, rows):
    d = h.shape[1]
    tm, tn = rows.tile, PROJ_COL_TILE
    nw = w_in.shape[1]
    return pl.pallas_call(
        functools.partial(_hgrn_proj_kernel, n_silu=d // tn),
        out_shape=jax.ShapeDtypeStruct((rows.n, nw), F32),
        grid=(rows.tiles, nw // tn),
        in_specs=[
            pl.BlockSpec((tm, d), lambda i, j: (i, 0)),
            pl.BlockSpec((None, None, 3, d), lambda i, j: (rows.group(i), 1, 0, 0)),
            pl.BlockSpec((1, d), lambda i, j: (0, 0)),
            pl.BlockSpec((d, tn), lambda i, j: (0, j)),
        ],
        out_specs=pl.BlockSpec((tm, tn), lambda i, j: (i, j)),
        scratch_shapes=[pltpu.VMEM((tm, d), BF16)],
        compiler_params=_cparams("parallel", "arbitrary"),
        name="hgrn_proj",
    )(h, mods, gain, w_in)


def _scan_levels(chunk):
    return [chunk >> (i + 1) for i in range(int(math.log2(chunk)))]


def _scan_consts(chunk, reverse):
    t = np.arange(chunk)[:, None]
    s = np.arange(chunk)[None, :]
    tri = (s >= t) if reverse else (s <= t)
    wide, narrow = [], []
    for h in _scan_levels(chunk):
        same = (t // (2 * h)) == (s // (2 * h))
        t_up = (t // h) % 2 == 1
        s_up = (s // h) % 2 == 1
        mask = same & ((~t_up & s_up) if reverse else (t_up & ~s_up))
        if h >= SUBLANES:
            wide.append(mask[np.nonzero(~t_up[:, 0] if reverse else t_up[:, 0])[0]])
        else:
            narrow.append(mask)
    narrow.append(t == s)
    return (jnp.asarray(tri, dtype=BF16), jnp.asarray(np.stack(wide), dtype=F32),
            jnp.asarray(np.stack(narrow), dtype=F32))


def _seg_bcast(x, h, reverse):
    c, w = x.shape
    off = h if reverse else h - 1
    if 2 * h >= 2 * SUBLANES:
        pieces = [jnp.broadcast_to(x[g * 2 * h + off:g * 2 * h + off + 1, :], (2 * h, w)) for g in range(c // (2 * h))]
        return pieces[0] if len(pieces) == 1 else jnp.concatenate(pieces, axis=0)
    x3 = x.reshape(c // SUBLANES, SUBLANES, w)
    sub = lax.broadcasted_iota(jnp.int32, x3.shape, 1)
    y = None
    for g in range(SUBLANES // (2 * h)):
        piece = jnp.broadcast_to(x3[:, g * 2 * h + off:g * 2 * h + off + 1, :], x3.shape)
        y = piece if y is None else jnp.where(sub >= g * 2 * h, piece, y)
    return y.reshape(c, w)


def _scan_kernel(q_ref, f_ref, v_ref, lb_ref, tri_ref, wmask_ref, nmask_ref, o_ref, st_ref, *, reverse, chunk):
    @pl.when(pl.program_id(2) == 0)
    def _():
        st_ref[...] = jnp.zeros_like(st_ref)

    rows, width = q_ref.shape
    levels = _scan_levels(chunk)
    spans = [slice(ci * chunk, (ci + 1) * chunk) for ci in range(rows // chunk)]
    lanes = [slice(hh * HEAD_DIM, (hh + 1) * HEAD_DIM) for hh in range(width // HEAD_DIM)]
    q = q_ref[...]
    fl = f_ref[...]
    lb = lb_ref[...]
    u = jnp.exp(-jnp.abs(fl))
    key = (1.0 - lb) * (jnp.where(fl > 0.0, u, 1.0) / (1.0 + u))
    log_sig = jnp.minimum(fl, 0.0) - jnp.log(1.0 + u)
    x1 = jnp.log(lb)
    x2 = jnp.log1p(-lb) + log_sig
    delta = x1 - x2
    lf = jnp.where(jnp.isnan(delta), x1 + x2, jnp.maximum(x1, x2) + jnp.log(1.0 + jnp.exp(-jnp.abs(delta))))
    lf = lf * LOG2_E
    l1 = lf.astype(BF16)
    r1 = lf - l1.astype(F32)
    l2 = r1.astype(BF16)
    l3 = (r1 - l2.astype(F32)).astype(BF16)
    parts = jnp.concatenate([l1, l2, l3], axis=1)
    tri = tri_ref[...]
    cs = jnp.concatenate([_dot(tri, parts[sp]) for sp in spans], axis=0)
    b = cs[:, :width] + cs[:, width:2 * width] + cs[:, 2 * width:]
    totals = [b[sp.start:sp.start + 1, :] if reverse else b[sp.stop - 1:sp.stop, :] for sp in spans]
    total_rows = jnp.concatenate([jnp.broadcast_to(t, (chunk, width)) for t in totals], axis=0)

    q_in = (q * jnp.exp2(b)).astype(BF16)
    k_out = (key * jnp.exp2(total_rows - b)).astype(BF16)
    v = v_ref[...].astype(BF16)

    order = list(range(len(spans)))[::-1] if reverse else list(range(len(spans)))
    kv = [[_dot_tn(v[sp, sl], k_out[sp, sl]) for sl in lanes] for sp in spans]
    o_state = [[None] * len(lanes) for _ in spans]
    for hh, sl in enumerate(lanes):
        st = st_ref[hh]
        for ci in order:
            o_state[ci][hh] = _dot_nt(q_in[spans[ci], sl], st.astype(BF16))
            st = st * jnp.exp2(totals[ci][:, sl]) + kv[ci][hh]
        st_ref[hh] = st

    n_blk = chunk // SUBLANES
    wide_levels = [h for h in levels if h >= SUBLANES]
    narrow_levels = [h for h in levels if h < SUBLANES]

    def narrow_scores(li, ql, kl):
        out = []
        for sp in spans:
            row = []
            for sl in lanes:
                s = nmask_ref[li] * _dot_nt(ql[sp, sl], kl[sp, sl])
                row.append([s[k * SUBLANES:(k + 1) * SUBLANES, :] for k in range(n_blk)])
            out.append(row)
        return out

    a = narrow_scores(len(narrow_levels), q.astype(BF16), key.astype(BF16))
    row_id = lax.broadcasted_iota(jnp.int32, (rows, width), 0)
    for li, h in enumerate(narrow_levels):
        if h == 1:
            e = jnp.where((row_id & 1) == (0 if reverse else 1), jnp.exp2(lf), 1.0)
        else:
            e = jnp.exp2(-jnp.abs(b - _seg_bcast(b, h, reverse)))
        new = narrow_scores(li, (q * e).astype(BF16), (key * e).astype(BF16))
        a = [[[x + y for x, y in zip(xb, yb)] for xb, yb in zip(xa, ya)] for xa, ya in zip(a, new)]

    for li, h in enumerate(wide_levels):
        q_parts, k_parts = [], []
        for g in range(rows // (2 * h)):
            lo = slice(g * 2 * h, g * 2 * h + h)
            hi = slice(g * 2 * h + h, (g + 1) * 2 * h)
            r = g * 2 * h + (h if reverse else h - 1)
            b_ref = jnp.broadcast_to(b[r:r + 1, :], (h, width))
            q_half, k_half = (lo, hi) if reverse else (hi, lo)
            q_parts.append(q[q_half] * jnp.exp2(b[q_half] - b_ref))
            k_part = key[k_half] * jnp.exp2(b_ref - b[k_half])
            zeros = jnp.zeros((h, width), F32)
            k_parts += [zeros, k_part] if reverse else [k_part, zeros]
        q_sel = jnp.concatenate(q_parts, axis=0).astype(BF16)
        k_hat = jnp.concatenate(k_parts, axis=0).astype(BF16)
        half = chunk // 2
        for ci, sp in enumerate(spans):
            for hh, sl in enumerate(lanes):
                s = wmask_ref[li] * _dot_nt(q_sel[ci * half:(ci + 1) * half, sl], k_hat[sp, sl])
                for j in range(chunk // (2 * h)):
                    first = (j * 2 * h + (0 if reverse else h)) // SUBLANES
                    for k in range(h // SUBLANES):
                        r0 = j * h + k * SUBLANES
                        a[ci][hh][first + k] = a[ci][hh][first + k] + s[r0:r0 + SUBLANES, :]

    for ci, sp in enumerate(spans):
        for hh, sl in enumerate(lanes):
            pairs = jnp.concatenate(a[ci][hh], axis=0).astype(BF16)
            o_ref[sp, sl] = o_state[ci][hh] + _dot(pairs, v[sp, sl])


def _hgrn_scan(p, lower_bound, rows, reverse):
    b, seq, lc = rows.batch, rows.seq, rows.ctx_len
    d = lower_bound.shape[0]
    c, hb = SCAN_CHUNK, SCAN_HEADS_PER_BLOCK
    r = c * SCAN_CHUNKS_PER_STEP
    wb = hb * HEAD_DIM
    ncb = d // wb
    assert lc % r == 0 and seq % r == 0
    nctx, nlat = lc // r, seq // r
    ctx0 = rows.n_lat // r
    f_blk = (2 if reverse else 1) * ncb
    v_blk = 3 * ncb

    def row(bi, s):
        if reverse:
            return jnp.where(s < nctx, ctx0 + bi * nctx + (nctx - 1 - s), bi * nlat + (nlat - 1 - (s - nctx)))
        return jnp.where(s < nctx, ctx0 + bi * nctx + s, bi * nlat + (s - nctx))

    tri, wmasks, nmasks = _scan_consts(c, reverse)
    return pl.pallas_call(
        functools.partial(_scan_kernel, reverse=reverse, chunk=c),
        out_shape=jax.ShapeDtypeStruct((rows.n, d), F32),
        grid=(b, ncb, nctx + nlat),
        in_specs=[
            pl.BlockSpec((r, wb), lambda bi, hi, s: (row(bi, s), hi)),
            pl.BlockSpec((r, wb), lambda bi, hi, s: (row(bi, s), f_blk + hi)),
            pl.BlockSpec((r, wb), lambda bi, hi, s: (row(bi, s), v_blk + hi)),
            pl.BlockSpec((1, wb), lambda bi, hi, s: (0, hi)),
            pl.BlockSpec(tri.shape, lambda bi, hi, s: (0, 0)),
            pl.BlockSpec(wmasks.shape, lambda bi, hi, s: (0, 0, 0)),
            pl.BlockSpec(nmasks.shape, lambda bi, hi, s: (0, 0, 0)),
        ],
        out_specs=pl.BlockSpec((r, wb), lambda bi, hi, s: (row(bi, s), hi)),
        scratch_shapes=[pltpu.VMEM((hb, HEAD_DIM, HEAD_DIM), F32)],
        compiler_params=_cparams("parallel", "parallel", "arbitrary"),
        name="hgrn_scan_bw" if reverse else "hgrn_scan_fw",
    )(p, p, p, lower_bound.reshape(1, d), tri, wmasks, nmasks)


def kernel(x, c, ctx, c_ctx, w_mod, b_mod, norm_gains, ffn_w_in, ffn_w_out, ab_w_in, qk_norm, ab_w_out,
           hgrn_w_in, hgrn_lb_logits, hgrn_o_norm, hgrn_w_out, final_norm):
    batch, seq, d = x.shape
    lc = ctx.shape[1]
    depth = w_mod.shape[0]
    rows = _Rows(batch, seq, lc, ROW_TILE)
    rows_r = _Rows(batch, seq, lc, READOUT_ROW_TILE)
    assert seq % GRID_W == 0 and batch + 1 <= SUBLANES

    c_rows = jnp.concatenate([c_ctx[None, :], c, jnp.zeros((SUBLANES - 1 - batch, d), F32)], axis=0)
    mods_all = _modulation(c_rows, w_mod, b_mod).reshape(depth, SUBLANES, 3, 3, d)

    lb_cum = jnp.cumsum(jax.nn.softmax(hgrn_lb_logits.astype(F32), axis=0), axis=0)
    lower_bounds = lb_cum - lb_cum[0]

    ffn_w_in_b = ffn_w_in.astype(BF16)
    ffn_w_out_b = ffn_w_out.astype(BF16)
    fin = final_norm.reshape(1, d)
    rope = _rope_tables(seq, ROW_TILE)

    h = (x.reshape(batch * seq, d), ctx.reshape(batch * lc, d))
    for layer in range(depth):
        last = layer == depth - 1
        mods = mods_all[layer]
        gains = norm_gains[layer].reshape(3, 1, d)
        h = _ffn(h, mods, 0, gains[0], ffn_w_in_b, ffn_w_out_b, layer, 0, fin, rows, rows.tiles, False)
        if layer % 2 == 0:
            e = layer // 2
            f_all, q_all, k_all, vt_all = _ab_proj(h, mods, gains[1], ab_w_in[e].astype(BF16), qk_norm[e], rope, rows)
            attn = _attention(q_all, k_all, vt_all, rows)
            attn = _attention(q_all, k_all, vt_all, rows, lat_out=attn)
            h = _ab_out(_fourier_latent(f_all, rows), _fourier_ctx(f_all, rows), attn,
                        ab_w_out[e].astype(BF16), h, mods, rows)
        else:
            o = layer // 2
            p = _hgrn_proj(h, mods, gains[1], hgrn_w_in[o].astype(BF16), rows)
            o_fw = _hgrn_scan(p, lower_bounds[layer], rows, False)
            o_bw = _hgrn_scan(p, lower_bounds[layer], rows, True)
            gain_o = jnp.tile(hgrn_o_norm[o], d // HEAD_DIM).reshape(1, d)
            n_t = rows_r.lat_tiles if last else rows_r.tiles
            h = _hgrn_out(o_fw, o_bw, p, gain_o, hgrn_w_out[o].astype(BF16), h, mods, rows_r, n_t)
        n_t = rows.lat_tiles if last else rows.tiles
        h = _ffn(h, mods, 2, gains[2], ffn_w_in_b, ffn_w_out_b, layer, 1, fin, rows, n_t, last)
    return h[:batch * seq].reshape(batch, seq, d)
```

```python
import functools
import math

import jax
import jax.numpy as jnp
import numpy as np
from jax import lax
from jax.experimental import pallas as pl
from jax.experimental.pallas import tpu as pltpu

F32 = jnp.float32
BF16 = jnp.bfloat16

EPS = 1e-6
N_MOD = 9
HEAD_DIM = 128
N_KV_HEADS = 4
FOURIER_WIDTH = 512
FOURIER_GROUP_DIM = 128
GRID_W = 64
ROPE_THETA = 10000.0
ROPE_AXIS_DIM = HEAD_DIM // 2
ATTN_SCALE = HEAD_DIM ** -0.5
LOG2_E = math.log2(math.e)

LANES = 128
SUBLANES = 8
BF16_SUBLANES = 16
VT_ROWS = HEAD_DIM + BF16_SUBLANES
VMEM_LIMIT_BYTES = 56 * 1024 * 1024

ROW_TILE = 512
READOUT_ROW_TILE = 512
FFN_TILE = 512
PROJ_COL_TILE = 2048
MOD_COL_TILE = 1024
ATTN_Q_TILE = 1024
SCAN_CHUNK = 128
SCAN_HEADS_PER_BLOCK = 8
SCAN_CHUNKS_PER_STEP = 2
FFT_B = 128
FFT_B_BLOCK = 8


def _cparams(*sem):
    return pltpu.CompilerParams(dimension_semantics=sem, vmem_limit_bytes=VMEM_LIMIT_BYTES)


def _dot(a, b):
    return jnp.dot(a, b, preferred_element_type=F32)


def _dot_hi(a, b):
    return jnp.dot(a, b, preferred_element_type=F32, precision=lax.Precision.HIGHEST)


def _split_bf16(x):
    hi = x.astype(BF16)
    return hi, (x - hi.astype(F32)).astype(BF16)


def _dot_split(a, b):
    (ah, al), (bh, bl) = a, b
    return _dot(ah, bh) + _dot(al, bh) + _dot(ah, bl)


def _dot_nt(a, b):
    return lax.dot_general(a, b, (((1,), (1,)), ((), ())), preferred_element_type=F32)


def _dot_tn(a, b):
    return lax.dot_general(a, b, (((0,), (0,)), ((), ())), preferred_element_type=F32)


def _silu(x):
    return x * jax.nn.sigmoid(x)


def _rms(x, gain):
    return x * lax.rsqrt(jnp.mean(x * x, axis=-1, keepdims=True) + EPS) * gain


def _ada(h, gain, shift, scale):
    return _rms(h, gain) * (1.0 + scale) + shift


def _mod_kernel(c_ref, w_ref, b_ref, o_ref):
    a = _silu(c_ref[...]).astype(BF16)
    o_ref[...] = _dot(a, w_ref[...].astype(BF16)) + b_ref[...]


def _modulation(c_rows, w_mod, b_mod):
    depth, d, nd = w_mod.shape
    tn = MOD_COL_TILE
    return pl.pallas_call(
        _mod_kernel,
        out_shape=jax.ShapeDtypeStruct((depth, SUBLANES, nd), F32),
        grid=(depth, nd // tn),
        in_specs=[
            pl.BlockSpec((SUBLANES, d), lambda l, j: (0, 0)),
            pl.BlockSpec((None, d, tn), lambda l, j: (l, 0, j)),
            pl.BlockSpec((None, 1, tn), lambda l, j: (l, 0, j)),
        ],
        out_specs=pl.BlockSpec((None, SUBLANES, tn), lambda l, j: (l, 0, j)),
        compiler_params=_cparams("parallel", "arbitrary"),
        name="modulation",
    )(c_rows, w_mod, b_mod.reshape(depth, 1, nd))


class _Rows:
    def __init__(self, batch, seq, ctx_len, tile):
        assert seq % tile == 0 and (batch * ctx_len) % tile == 0
        self.batch, self.seq, self.ctx_len, self.tile = batch, seq, ctx_len, tile
        self.n_lat = batch * seq
        self.n = self.n_lat + batch * ctx_len
        self.lat_tiles = self.n_lat // tile
        self.tiles = self.n // tile
        self.tiles_per_batch = seq // tile

    def group(self, i):
        return jnp.where(i < self.lat_tiles, 1 + i // self.tiles_per_batch, 0)


def _ffn_kernel(*refs, final, split_tiles):
    if split_tiles is None:
        h_ref, mod_ref, gain_ref, wa_ref, wb_ref, wo_ref, fin_ref, o_ref, xn_ref, acc_ref = refs
        load_h = lambda: h_ref[...]
    else:
        hl_ref, hc_ref, mod_ref, gain_ref, wa_ref, wb_ref, wo_ref, fin_ref, o_ref, xn_ref, acc_ref = refs
        load_h = lambda: jnp.where(pl.program_id(0) < split_tiles, hl_ref[...], hc_ref[...])
    i = pl.program_id(0)
    j = pl.program_id(1)

    @pl.when(j == 0)
    def _():
        xn_ref[...] = _ada(load_h(), gain_ref[...], mod_ref[0:1, :], mod_ref[1:2, :]).astype(BF16)

    @pl.when(jnp.logical_and(i == 0, j == 0))
    def _():
        acc_ref[...] = jnp.zeros_like(acc_ref)

    xn = xn_ref[...]
    a = _dot(xn, wa_ref[...])
    b = _dot(xn, wb_ref[...])
    g = (_silu(a) * b).astype(BF16)
    acc_ref[...] = jnp.where(j == 0, 0.0, acc_ref[...]) + _dot(g, wo_ref[...])

    @pl.when(j == pl.num_programs(1) - 1)
    def _():
        out = load_h() + 0.5 * mod_ref[2:3, :] * acc_ref[...]
        if final:
            out = _rms(out, fin_ref[...])
        o_ref[...] = out


def _ffn(h, mods, sub, gain, w_in, w_out, layer, which, fin, rows, n_tiles, final):
    f, d = w_out.shape[2:]
    tm, tf = rows.tile, FFN_TILE
    nf = f // tf
    if isinstance(h, tuple):
        lt = rows.lat_tiles
        row_specs = [pl.BlockSpec((tm, d), lambda i, j: (jnp.minimum(i, lt - 1), 0)),
                     pl.BlockSpec((tm, d), lambda i, j: (jnp.maximum(i - lt, 0), 0))]
        row_args, split_tiles = h, lt
    else:
        row_specs = [pl.BlockSpec((tm, d), lambda i, j: (i, 0))]
        row_args, split_tiles = (h,), None
    return pl.pallas_call(
        functools.partial(_ffn_kernel, final=final, split_tiles=split_tiles),
        out_shape=jax.ShapeDtypeStruct((n_tiles * tm, d), F32),
        grid=(n_tiles, nf),
        in_specs=row_specs + [
            pl.BlockSpec((None, None, 3, d), lambda i, j: (rows.group(i), sub, 0, 0)),
            pl.BlockSpec((1, d), lambda i, j: (0, 0)),
            pl.BlockSpec((None, None, d, tf), lambda i, j: (layer, which, 0, j)),
            pl.BlockSpec((None, None, d, tf), lambda i, j: (layer, which, 0, nf + j)),
            pl.BlockSpec((None, None, tf, d), lambda i, j: (layer, which, j, 0)),
            pl.BlockSpec((1, d), lambda i, j: (0, 0)),
        ],
        out_specs=pl.BlockSpec((tm, d), lambda i, j: (i, 0)),
        scratch_shapes=[pltpu.VMEM((tm, d), BF16), pltpu.VMEM((tm, d), F32)],
        compiler_params=_cparams("arbitrary", "arbitrary"),
        name="ffn",
    )(*row_args, mods, gain, w_in, w_in, w_out, fin)


def _rope_tables(seq, tile):
    t = np.arange(seq)
    inv_freq = ROPE_THETA ** (-np.arange(0, ROPE_AXIS_DIM, 2, dtype=np.float64) / ROPE_AXIS_DIM)
    ang = np.concatenate([(t // GRID_W)[:, None] * inv_freq, (t % GRID_W)[:, None] * inv_freq], axis=-1)
    nf = ROPE_AXIS_DIM // 2
    cos = np.cos(ang).reshape(seq, 2, 1, nf)
    sin = np.sin(ang).reshape(seq, 2, 1, nf)
    zero = np.zeros_like(sin)
    c_full = np.broadcast_to(cos, (seq, 2, 2, nf)).reshape(seq, HEAD_DIM)
    s_up = np.concatenate([-sin, zero], axis=2).reshape(seq, HEAD_DIM)
    s_dn = np.concatenate([zero, sin], axis=2).reshape(seq, HEAD_DIM)
    lat = np.concatenate([c_full, s_up, s_dn], axis=1)
    ident = np.concatenate([np.ones((tile, HEAD_DIM)), np.zeros((tile, 2 * HEAD_DIM))], axis=1)
    return jnp.asarray(np.concatenate([lat, ident], axis=0), dtype=F32)


def _norm_rope_heads(acc, gain, rope, post_scale=None):
    nf = ROPE_AXIS_DIM // 2
    c, s_up, s_dn = rope[:, :HEAD_DIM], rope[:, HEAD_DIM:2 * HEAD_DIM], rope[:, 2 * HEAD_DIM:]
    heads = []
    for hh in range(acc.shape[1] // HEAD_DIM):
        y = _rms(acc[:, hh * HEAD_DIM:(hh + 1) * HEAD_DIM], gain)
        y = y * c + pltpu.roll(y, HEAD_DIM - nf, 1) * s_up + pltpu.roll(y, nf, 1) * s_dn
        heads.append(y if post_scale is None else y * post_scale)
    return jnp.concatenate(heads, axis=1).astype(BF16)


def _ab_proj_kernel(h_ref, mod_ref, gain_ref, w_ref, qkn_ref, rope_ref, f_ref, q_ref, k_ref, vt_ref):
    xn = _ada(h_ref[...], gain_ref[...], mod_ref[0:1, :], mod_ref[1:2, :]).astype(BF16)
    c0 = f_ref.shape[1]
    c1 = c0 + q_ref.shape[1]
    c2 = c1 + k_ref.shape[1]
    q_ref[...] = _norm_rope_heads(_dot(xn, w_ref[:, c0:c1]), qkn_ref[0:1, :], rope_ref[...], ATTN_SCALE * LOG2_E)
    k_ref[...] = _norm_rope_heads(_dot(xn, w_ref[:, c1:c2]), qkn_ref[1:2, :], rope_ref[...])
    v = _dot(xn, w_ref[:, c2:])
    ones = jnp.ones((vt_ref.shape[1] - HEAD_DIM, vt_ref.shape[2]), BF16)
    for hh in range(N_KV_HEADS):
        vt_ref[hh, :HEAD_DIM, :] = v[:, hh * HEAD_DIM:(hh + 1) * HEAD_DIM].T.astype(BF16)
        vt_ref[hh, HEAD_DIM:, :] = ones
    f_ref[...] = _dot(xn, w_ref[:, :c0])


def _ab_proj(h, mods, gain, w_in, qk_norm, rope, rows):
    d = h.shape[1]
    tm = rows.tile
    kv_width = N_KV_HEADS * HEAD_DIM
    q_width = w_in.shape[1] - FOURIER_WIDTH - 2 * kv_width
    n = rows.n
    rope_blk = lambda i: (jnp.where(i < rows.lat_tiles, i % rows.tiles_per_batch, rows.tiles_per_batch), 0)
    return pl.pallas_call(
        _ab_proj_kernel,
        out_shape=(
            jax.ShapeDtypeStruct((n, FOURIER_WIDTH), F32),
            jax.ShapeDtypeStruct((n, q_width), BF16),
            jax.ShapeDtypeStruct((n, kv_width), BF16),
            jax.ShapeDtypeStruct((N_KV_HEADS, rows.tiles, VT_ROWS, tm), BF16),
        ),
        grid=(rows.tiles,),
        in_specs=[
            pl.BlockSpec((tm, d), lambda i: (i, 0)),
            pl.BlockSpec((None, None, 3, d), lambda i: (rows.group(i), 1, 0, 0)),
            pl.BlockSpec((1, d), lambda i: (0, 0)),
            pl.BlockSpec(w_in.shape, lambda i: (0, 0)),
            pl.BlockSpec((2, HEAD_DIM), lambda i: (0, 0)),
            pl.BlockSpec((tm, 3 * HEAD_DIM), rope_blk),
        ],
        out_specs=(
            pl.BlockSpec((tm, FOURIER_WIDTH), lambda i: (i, 0)),
            pl.BlockSpec((tm, q_width), lambda i: (i, 0)),
            pl.BlockSpec((tm, kv_width), lambda i: (i, 0)),
            pl.BlockSpec((N_KV_HEADS, None, VT_ROWS, tm), lambda i: (0, i, 0, 0)),
        ),
        compiler_params=_cparams("parallel"),
        name="ab_proj",
    )(h, mods, gain, w_in, qk_norm, rope)


def _attn_kernel(*refs, lat_chunks, group):
    if lat_chunks:
        q_ref, kc_ref, vtc_ref, kl_ref, vtl_ref, o_ref, acc_ref, s_ref = refs
    else:
        q_ref, kc_ref, vtc_ref, _, o_ref, acc_ref = refs
    tq = q_ref.shape[0]
    q = q_ref[...]
    qs = jnp.concatenate([q[:, g * HEAD_DIM:(g + 1) * HEAD_DIM] for g in range(group)], axis=0)
    nq = group * tq
    acc_ref[...] = jnp.zeros_like(acc_ref)

    def scores(k):
        return _dot_nt(k, qs)

    def update(s, vt, m):
        m_new = jnp.maximum(m, jnp.max(s, axis=0, keepdims=True))
        p = jnp.exp2(s - m_new).astype(BF16)
        acc_ref[...] = jnp.exp2(m - m_new) * acc_ref[...] + _dot(vt, p)
        return m_new

    m = update(scores(kc_ref[...]), vtc_ref[...], jnp.full((1, nq), -jnp.inf, F32))
    if lat_chunks:
        tk = vtl_ref.shape[2]

        def lat_scores(c):
            return scores(kl_ref[pl.ds(pl.multiple_of(c * tk, tk), tk), :])

        assert lat_chunks % 2 == 0
        s_ref[0] = lat_scores(0)

        def body(i, m):
            c = 2 * i
            s_ref[1] = lat_scores(c + 1)
            m = update(s_ref[0], vtl_ref[c], m)
            s_ref[0] = lat_scores(jnp.minimum(c + 2, lat_chunks - 1))
            return update(s_ref[1], vtl_ref[c + 1], m)

        m = lax.fori_loop(0, lat_chunks // 2, body, m)

    out = (acc_ref[:HEAD_DIM, :] / acc_ref[HEAD_DIM:HEAD_DIM + 1, :]).T
    o_ref[...] = jnp.concatenate([out[g * tq:(g + 1) * tq, :] for g in range(group)], axis=1).astype(BF16)


def _attention(q_all, k_all, vt_all, rows, lat_out=None):
    b, seq, lc, tile = rows.batch, rows.seq, rows.ctx_len, rows.tile
    group = q_all.shape[1] // (N_KV_HEADS * HEAD_DIM)
    gw = group * HEAD_DIM
    assert tile % lc == 0 and seq % tile == 0
    ctx_blk0 = rows.n_lat // lc
    ctx_tile = lambda bi: ((rows.n_lat + bi * lc) // tile, 0, ((rows.n_lat + bi * lc) % tile) // lc)
    kc_spec = pl.BlockSpec((lc, HEAD_DIM), lambda bi, hi, i: (ctx_blk0 + bi, hi))
    vtc_spec = pl.BlockSpec((None, None, VT_ROWS, lc), lambda bi, hi, i: (hi,) + ctx_tile(bi))
    if lat_out is None:
        tq = ATTN_Q_TILE
        assert tq % tile == 0
        nqt = seq // tq
        lat_chunks = seq // tile
        q_spec = pl.BlockSpec((tq, gw), lambda bi, hi, i: (bi * nqt + i, hi))
        kl_spec = pl.BlockSpec((seq, HEAD_DIM), lambda bi, hi, i: (bi, hi))
        vtl_spec = pl.BlockSpec((None, lat_chunks, VT_ROWS, tile), lambda bi, hi, i: (hi, bi, 0, 0))
        in_specs = [q_spec, kc_spec, vtc_spec, kl_spec, vtl_spec]
        args = (q_all, k_all, vt_all, k_all, vt_all)
        aliases = {}
    else:
        tq = lc
        nqt = 1
        lat_chunks = 0
        q_spec = pl.BlockSpec((tq, gw), lambda bi, hi, i: (ctx_blk0 + bi, hi))
        in_specs = [q_spec, kc_spec, vtc_spec, pl.BlockSpec(memory_space=pl.ANY)]
        args = (q_all, k_all, vt_all, lat_out)
        aliases = {3: 0}
    return pl.pallas_call(
        functools.partial(_attn_kernel, lat_chunks=lat_chunks, group=group),
        out_shape=jax.ShapeDtypeStruct((rows.n, q_all.shape[1]), BF16),
        grid=(b, N_KV_HEADS, nqt),
        in_specs=in_specs,
        out_specs=q_spec,
        scratch_shapes=[pltpu.VMEM((VT_ROWS, group * tq), F32)]
        + ([pltpu.VMEM((2, tile, group * tq), F32)] if lat_chunks else []),
        input_output_aliases=aliases,
        compiler_params=_cparams("parallel", "parallel", "arbitrary"),
        name="attention_ctx" if lat_chunks == 0 else "attention_lat",
    )(*args)


def _dft_cs(n):
    idx = np.arange(n)
    ang = 2.0 * np.pi * ((idx[:, None] * idx[None, :]) % n) / n
    return np.cos(ang), np.sin(ang)


def _const_split(m):
    m = np.asarray(m, dtype=np.float32)
    hi = jnp.asarray(m).astype(BF16)
    lo = (jnp.asarray(m) - hi.astype(F32)).astype(BF16)
    return jnp.stack([hi, lo])


def _fft1_kernel(x_ref, fa_ref, tw_ref, o_ref, *, a):
    w = o_ref.shape[2]
    for r in range(o_ref.shape[0]):
        z = _dot_split((fa_ref[0], fa_ref[1]), _split_bf16(x_ref[:, r * w:(r + 1) * w]))
        zr, zi = z[:a], z[a:]
        tc = jnp.concatenate([tw_ref[r, 0]] * (w // LANES), axis=1)
        ts = jnp.concatenate([tw_ref[r, 1]] * (w // LANES), axis=1)
        o_ref[r, :a, :] = zr * tc - zi * ts
        o_ref[r, a:, :] = zr * ts + zi * tc


def _fft2_kernel(zr_ref, zi_ref, m2_ref, mc_ref, o_ref, *, scale):
    for r in range(o_ref.shape[1]):
        z = jnp.concatenate([zr_ref[:, r, :], zi_ref[:, r, :]], axis=0)
        v = _dot_split((m2_ref[0], m2_ref[1]), _split_bf16(z))
        vr, vi = v[:FFT_B], v[FFT_B:]
        outs = []
        for g in range(o_ref.shape[2] // LANES):
            u = jnp.concatenate([vr[:, g * LANES:(g + 1) * LANES], vi[:, g * LANES:(g + 1) * LANES]], axis=1)
            outs.append(_dot_split(_split_bf16(u), (mc_ref[0], mc_ref[1])))
        o_ref[:, r, :] = jnp.concatenate(outs, axis=1) * scale


def _fourier_latent(f_all, rows):
    b, seq = rows.batch, rows.seq
    w = FOURIER_WIDTH
    a = seq // FFT_B
    assert a % SUBLANES == 0 and FOURIER_GROUP_DIM == LANES
    ca, sa = _dft_cs(a)
    fa = _const_split(np.concatenate([ca, sa], axis=0))
    p1b = (np.arange(a)[None, :] * np.arange(FFT_B)[:, None]) % seq
    ang = 2.0 * np.pi * p1b / seq
    tw = np.stack([np.cos(ang), np.sin(ang)], axis=1)[..., None]
    tw = jnp.asarray(np.broadcast_to(tw, (FFT_B, 2, a, LANES)), dtype=F32)
    cb, sb = _dft_cs(FFT_B)
    m2 = _const_split(np.block([[cb, -sb], [sb, cb]]))
    cc, sc = _dft_cs(FOURIER_GROUP_DIM)
    mc = _const_split(np.concatenate([cc, -sc], axis=0))

    blk = FFT_B_BLOCK
    x2 = f_all.reshape(rows.n // FFT_B, FFT_B * w)
    z = pl.pallas_call(
        functools.partial(_fft1_kernel, a=a),
        out_shape=jax.ShapeDtypeStruct((b, FFT_B, 2 * a, w), F32),
        grid=(b, FFT_B // blk),
        in_specs=[
            pl.BlockSpec((a, blk * w), lambda bi, j: (bi, j)),
            pl.BlockSpec((2, 2 * a, a), lambda bi, j: (0, 0, 0)),
            pl.BlockSpec((blk, 2, a, LANES), lambda bi, j: (j, 0, 0, 0)),
        ],
        out_specs=pl.BlockSpec((None, blk, 2 * a, w), lambda bi, j: (bi, j, 0, 0)),
        compiler_params=_cparams("parallel", "parallel"),
        name="fourier_stage1",
    )(x2, fa, tw)

    nblk = a // blk
    y = pl.pallas_call(
        functools.partial(_fft2_kernel, scale=1.0 / math.sqrt(seq * FOURIER_GROUP_DIM)),
        out_shape=jax.ShapeDtypeStruct((b, FFT_B, a, w), F32),
        grid=(b, nblk),
        in_specs=[
            pl.BlockSpec((None, FFT_B, blk, w), lambda bi, j: (bi, 0, j, 0)),
            pl.BlockSpec((None, FFT_B, blk, w), lambda bi, j: (bi, 0, nblk + j, 0)),
            pl.BlockSpec((2, 2 * FFT_B, 2 * FFT_B), lambda bi, j: (0, 0, 0)),
            pl.BlockSpec((2, 2 * LANES, LANES), lambda bi, j: (0, 0, 0)),
        ],
        out_specs=pl.BlockSpec((None, FFT_B, blk, w), lambda bi, j: (bi, 0, j, 0)),
        compiler_params=_cparams("parallel", "parallel"),
        name="fourier_stage2",
    )(z, z, m2, mc)
    return y.reshape(b * seq, w)


def _dft_ctx_kernel(x_ref, cn_ref, sn_ref, cc_ref, sc_ref, o_ref, *, scale):
    x = x_ref[...]
    outs = []
    for g in range(x.shape[1] // LANES):
        xg = x[:, g * LANES:(g + 1) * LANES]
        outs.append(_dot_hi(cn_ref[...], _dot_hi(xg, cc_ref[...])) - _dot_hi(sn_ref[...], _dot_hi(xg, sc_ref[...])))
    o_ref[...] = jnp.concatenate(outs, axis=1) * scale


def _fourier_ctx(f_all, rows):
    b, lc = rows.batch, rows.ctx_len
    w = FOURIER_WIDTH
    cn, sn = _dft_cs(lc)
    cc, sc = _dft_cs(FOURIER_GROUP_DIM)
    blk0 = rows.n_lat // lc
    mat = lambda m: pl.BlockSpec(m.shape, lambda bi: (0, 0))
    consts = [jnp.asarray(m, dtype=F32) for m in (cn, sn, cc, sc)]
    return pl.pallas_call(
        functools.partial(_dft_ctx_kernel, scale=1.0 / math.sqrt(lc * FOURIER_GROUP_DIM)),
        out_shape=jax.ShapeDtypeStruct((b * lc, w), F32),
        grid=(b,),
        in_specs=[pl.BlockSpec((lc, w), lambda bi: (blk0 + bi, 0))] + [mat(m) for m in consts],
        out_specs=pl.BlockSpec((lc, w), lambda bi: (bi, 0)),
        compiler_params=_cparams("parallel"),
        name="fourier_ctx",
    )(f_all, *consts)


def _ab_out_kernel(x1l_ref, x1c_ref, x2_ref, w_ref, h_ref, mod_ref, o_ref, *, lat_tiles):
    x1 = jnp.where(pl.program_id(0) < lat_tiles, x1l_ref[...], x1c_ref[...])
    x = jnp.concatenate([x1.astype(BF16), x2_ref[...]], axis=1)
    o_ref[...] = h_ref[...] + mod_ref[2:3, :] * _dot(x, w_ref[...])


def _ab_out(x1_lat, x1_ctx, x2, w_out, h, mods, rows):
    d = h.shape[1]
    tm = rows.tile
    lt = rows.lat_tiles
    return pl.pallas_call(
        functools.partial(_ab_out_kernel, lat_tiles=lt),
        out_shape=jax.ShapeDtypeStruct((rows.n, d), F32),
        grid=(rows.tiles,),
        in_specs=[
            pl.BlockSpec((tm, x1_lat.shape[1]), lambda i: (jnp.minimum(i, lt - 1), 0)),
            pl.BlockSpec((tm, x1_ctx.shape[1]), lambda i: (jnp.maximum(i - lt, 0), 0)),
            pl.BlockSpec((tm, x2.shape[1]), lambda i: (i, 0)),
            pl.BlockSpec(w_out.shape, lambda i: (0, 0)),
            pl.BlockSpec((tm, d), lambda i: (i, 0)),
            pl.BlockSpec((None, None, 3, d), lambda i: (rows.group(i), 1, 0, 0)),
        ],
        out_specs=pl.BlockSpec((tm, d), lambda i: (i, 0)),
        compiler_params=_cparams("parallel"),
        name="ab_out",
    )(x1_lat, x1_ctx, x2, w_out, h, mods)


def _hgrn_out_kernel(osum_ref, g_ref, gain_ref, w_ref, h_ref, mod_ref, o_ref):
    o = osum_ref[...]
    parts = []
    for hh in range(o.shape[1] // HEAD_DIM):
        oh = o[:, hh * HEAD_DIM:(hh + 1) * HEAD_DIM]
        parts.append(oh * lax.rsqrt(jnp.mean(oh * oh, axis=-1, keepdims=True) + EPS))
    on = jnp.concatenate(parts, axis=1) * gain_ref[...]
    y = (on * jax.nn.sigmoid(g_ref[...])).astype(BF16)
    o_ref[...] = h_ref[...] + mod_ref[2:3, :] * _dot(y, w_ref[...])


def _hgrn_out(o_sum, p, gain, w_out, h, mods, rows_r, n_tiles):
    d = h.shape[1]
    tm = rows_r.tile
    g_blk = (p.shape[1] - d) // d
    return pl.pallas_call(
        _hgrn_out_kernel,
        out_shape=jax.ShapeDtypeStruct((n_tiles * tm, d), F32),
        grid=(n_tiles,),
        in_specs=[
            pl.BlockSpec((tm, d), lambda i: (i, 0)),
            pl.BlockSpec((tm, d), lambda i: (i, g_blk)),
            pl.BlockSpec((1, d), lambda i: (0, 0)),
            pl.BlockSpec(w_out.shape, lambda i: (0, 0)),
            pl.BlockSpec((tm, d), lambda i: (i, 0)),
            pl.BlockSpec((None, None, 3, d), lambda i: (rows_r.group(i), 1, 0, 0)),
        ],
        out_specs=pl.BlockSpec((tm, d), lambda i: (i, 0)),
        compiler_params=_cparams("parallel"),
        name="hgrn_out",
    )(o_sum, p, gain, w_out, h, mods)


def _hgrn_proj_kernel(h_ref, mod_ref, gain_ref, w_ref, o_ref, xn_ref, *, n_silu):
    j = pl.program_id(1)

    @pl.when(j == 0)
    def _():
        xn_ref[...] = _ada(h_ref[...], gain_ref[...], mod_ref[0:1, :], mod_ref[1:2, :]).astype(BF16)

    acc = _dot(xn_ref[...], w_ref[...])
    o_ref[...] = jnp.where(j < n_silu, _silu(acc), acc)


def _hgrn_proj(h, mods, gain, w_in, rows):
    d = h.shape[1]
    tm, tn = rows.tile, PROJ_COL_TILE
    nw = w_in.shape[1]
    return pl.pallas_call(
        functools.partial(_hgrn_proj_kernel, n_silu=d // tn),
        out_shape=jax.ShapeDtypeStruct((rows.n, nw), F32),
        grid=(rows.tiles, nw // tn),
        in_specs=[
            pl.BlockSpec((tm, d), lambda i, j: (i, 0)),
            pl.BlockSpec((None, None, 3, d), lambda i, j: (rows.group(i), 1, 0, 0)),
            pl.BlockSpec((1, d), lambda i, j: (0, 0)),
            pl.BlockSpec((d, tn), lambda i, j: (0, j)),
        ],
        out_specs=pl.BlockSpec((tm, tn), lambda i, j: (i, j)),
        scratch_shapes=[pltpu.VMEM((tm, d), BF16)],
        compiler_params=_cparams("parallel", "arbitrary"),
        name="hgrn_proj",
    )(h, mods, gain, w_in)


def _scan_levels(chunk):
    return [chunk >> (i + 1) for i in range(int(math.log2(chunk)))]


def _scan_consts(chunk, reverse):
    t = np.arange(chunk)[:, None]
    s = np.arange(chunk)[None, :]
    tri = (s >= t) if reverse else (s <= t)
    wide, narrow = [], []
    for h in _scan_levels(chunk):
        same = (t // (2 * h)) == (s // (2 * h))
        t_up = (t // h) % 2 == 1
        s_up = (s // h) % 2 == 1
        mask = same & ((~t_up & s_up) if reverse else (t_up & ~s_up))
        if h >= SUBLANES:
            wide.append(mask[np.nonzero(~t_up[:, 0] if reverse else t_up[:, 0])[0]])
        else:
            narrow.append(mask)
    narrow.append(t == s)
    return (jnp.asarray(tri, dtype=BF16), jnp.asarray(np.stack(wide), dtype=F32),
            jnp.asarray(np.stack(narrow), dtype=F32))


def _seg_bcast(x, h, reverse):
    c, w = x.shape
    off = h if reverse else h - 1
    if 2 * h >= 2 * SUBLANES:
        pieces = [jnp.broadcast_to(x[g * 2 * h + off:g * 2 * h + off + 1, :], (2 * h, w)) for g in range(c // (2 * h))]
        return pieces[0] if len(pieces) == 1 else jnp.concatenate(pieces, axis=0)
    x3 = x.reshape(c // SUBLANES, SUBLANES, w)
    sub = lax.broadcasted_iota(jnp.int32, x3.shape, 1)
    y = None
    for g in range(SUBLANES // (2 * h)):
        piece = jnp.broadcast_to(x3[:, g * 2 * h + off:g * 2 * h + off + 1, :], x3.shape)
        y = piece if y is None else jnp.where(sub >= g * 2 * h, piece, y)
    return y.reshape(c, w)


def _scan_kernel(*refs, reverse, chunk, accumulate):
    if accumulate:
        q_ref, f_ref, v_ref, lb_ref, tri_ref, wmask_ref, nmask_ref, other_ref, o_ref, st_ref = refs
    else:
        q_ref, f_ref, v_ref, lb_ref, tri_ref, wmask_ref, nmask_ref, o_ref, st_ref = refs

    @pl.when(pl.program_id(2) == 0)
    def _():
        st_ref[...] = jnp.zeros_like(st_ref)

    rows, width = q_ref.shape
    levels = _scan_levels(chunk)
    spans = [slice(ci * chunk, (ci + 1) * chunk) for ci in range(rows // chunk)]
    lanes = [slice(hh * HEAD_DIM, (hh + 1) * HEAD_DIM) for hh in range(width // HEAD_DIM)]
    q = q_ref[...]
    fl = f_ref[...]
    lb = lb_ref[...]
    u = jnp.exp(-jnp.abs(fl))
    key = (1.0 - lb) * (jnp.where(fl > 0.0, u, 1.0) / (1.0 + u))
    log_sig = jnp.minimum(fl, 0.0) - jnp.log(1.0 + u)
    x1 = jnp.log(lb)
    x2 = jnp.log1p(-lb) + log_sig
    delta = x1 - x2
    lf = jnp.where(jnp.isnan(delta), x1 + x2, jnp.maximum(x1, x2) + jnp.log(1.0 + jnp.exp(-jnp.abs(delta))))
    lf = lf * LOG2_E
    l1 = lf.astype(BF16)
    r1 = lf - l1.astype(F32)
    l2 = r1.astype(BF16)
    l3 = (r1 - l2.astype(F32)).astype(BF16)
    parts = jnp.concatenate([l1, l2, l3], axis=1)
    tri = tri_ref[...]
    cs = jnp.concatenate([_dot(tri, parts[sp]) for sp in spans], axis=0)
    b = cs[:, :width] + cs[:, width:2 * width] + cs[:, 2 * width:]
    totals = [b[sp.start:sp.start + 1, :] if reverse else b[sp.stop - 1:sp.stop, :] for sp in spans]
    total_rows = jnp.concatenate([jnp.broadcast_to(t, (chunk, width)) for t in totals], axis=0)

    q_in = (q * jnp.exp2(b)).astype(BF16)
    k_out = (key * jnp.exp2(total_rows - b)).astype(BF16)
    v = v_ref[...].astype(BF16)

    order = list(range(len(spans)))[::-1] if reverse else list(range(len(spans)))
    kv = [[_dot_tn(v[sp, sl], k_out[sp, sl]) for sl in lanes] for sp in spans]
    o_state = [[None] * len(lanes) for _ in spans]
    for hh, sl in enumerate(lanes):
        st = st_ref[hh]
        for ci in order:
            o_state[ci][hh] = _dot_nt(q_in[spans[ci], sl], st.astype(BF16))
            st = st * jnp.exp2(totals[ci][:, sl]) + kv[ci][hh]
        st_ref[hh] = st

    n_blk = chunk // SUBLANES
    wide_levels = [h for h in levels if h >= SUBLANES]
    narrow_levels = [h for h in levels if h < SUBLANES]

    def narrow_scores(li, ql, kl):
        out = []
        for sp in spans:
            row = []
            for sl in lanes:
                s = nmask_ref[li] * _dot_nt(ql[sp, sl], kl[sp, sl])
                row.append([s[k * SUBLANES:(k + 1) * SUBLANES, :] for k in range(n_blk)])
            out.append(row)
        return out

    a = narrow_scores(len(narrow_levels), q.astype(BF16), key.astype(BF16))
    row_id = lax.broadcasted_iota(jnp.int32, (rows, width), 0)
    for li, h in enumerate(narrow_levels):
        if h == 1:
            e = jnp.where((row_id & 1) == (0 if reverse else 1), jnp.exp2(lf), 1.0)
        else:
            e = jnp.exp2(-jnp.abs(b - _seg_bcast(b, h, reverse)))
        new = narrow_scores(li, (q * e).astype(BF16), (key * e).astype(BF16))
        a = [[[x + y for x, y in zip(xb, yb)] for xb, yb in zip(xa, ya)] for xa, ya in zip(a, new)]

    for li, h in enumerate(wide_levels):
        q_parts, k_parts = [], []
        for g in range(rows // (2 * h)):
            lo = slice(g * 2 * h, g * 2 * h + h)
            hi = slice(g * 2 * h + h, (g + 1) * 2 * h)
            r = g * 2 * h + (h if reverse else h - 1)
            b_ref = jnp.broadcast_to(b[r:r + 1, :], (h, width))
            q_half, k_half = (lo, hi) if reverse else (hi, lo)
            q_parts.append(q[q_half] * jnp.exp2(b[q_half] - b_ref))
            k_part = key[k_half] * jnp.exp2(b_ref - b[k_half])
            zeros = jnp.zeros((h, width), F32)
            k_parts += [zeros, k_part] if reverse else [k_part, zeros]
        q_sel = jnp.concatenate(q_parts, axis=0).astype(BF16)
        k_hat = jnp.concatenate(k_parts, axis=0).astype(BF16)
        half = chunk // 2
        for ci, sp in enumerate(spans):
            for hh, sl in enumerate(lanes):
                s = wmask_ref[li] * _dot_nt(q_sel[ci * half:(ci + 1) * half, sl], k_hat[sp, sl])
                for j in range(chunk // (2 * h)):
                    first = (j * 2 * h + (0 if reverse else h)) // SUBLANES
                    for k in range(h // SUBLANES):
                        r0 = j * h + k * SUBLANES
                        a[ci][hh][first + k] = a[ci][hh][first + k] + s[r0:r0 + SUBLANES, :]

    for ci, sp in enumerate(spans):
        for hh, sl in enumerate(lanes):
            pairs = jnp.concatenate(a[ci][hh], axis=0).astype(BF16)
            o = o_state[ci][hh] + _dot(pairs, v[sp, sl])
            o_ref[sp, sl] = o + other_ref[sp, sl] if accumulate else o


def _hgrn_scan(p, lower_bound, rows, reverse, other=None):
    b, seq, lc = rows.batch, rows.seq, rows.ctx_len
    d = lower_bound.shape[0]
    c, hb = SCAN_CHUNK, SCAN_HEADS_PER_BLOCK
    r = c * SCAN_CHUNKS_PER_STEP
    wb = hb * HEAD_DIM
    ncb = d // wb
    assert lc % r == 0 and seq % r == 0
    nctx, nlat = lc // r, seq // r
    ctx0 = rows.n_lat // r
    f_blk = (2 if reverse else 1) * ncb
    v_blk = 3 * ncb

    def row(bi, s):
        if reverse:
            return jnp.where(s < nctx, ctx0 + bi * nctx + (nctx - 1 - s), bi * nlat + (nlat - 1 - (s - nctx)))
        return jnp.where(s < nctx, ctx0 + bi * nctx + s, bi * nlat + (s - nctx))

    tri, wmasks, nmasks = _scan_consts(c, reverse)
    o_spec = pl.BlockSpec((r, wb), lambda bi, hi, s: (row(bi, s), hi))
    extra = () if other is None else (other,)
    return pl.pallas_call(
        functools.partial(_scan_kernel, reverse=reverse, chunk=c, accumulate=other is not None),
        out_shape=jax.ShapeDtypeStruct((rows.n, d), F32),
        grid=(b, ncb, nctx + nlat),
        in_specs=[
            pl.BlockSpec((r, wb), lambda bi, hi, s: (row(bi, s), hi)),
            pl.BlockSpec((r, wb), lambda bi, hi, s: (row(bi, s), f_blk + hi)),
            pl.BlockSpec((r, wb), lambda bi, hi, s: (row(bi, s), v_blk + hi)),
            pl.BlockSpec((1, wb), lambda bi, hi, s: (0, hi)),
            pl.BlockSpec(tri.shape, lambda bi, hi, s: (0, 0)),
            pl.BlockSpec(wmasks.shape, lambda bi, hi, s: (0, 0, 0)),
            pl.BlockSpec(nmasks.shape, lambda bi, hi, s: (0, 0, 0)),
        ] + [o_spec] * len(extra),
        out_specs=o_spec,
        scratch_shapes=[pltpu.VMEM((hb, HEAD_DIM, HEAD_DIM), F32)],
        compiler_params=_cparams("parallel", "parallel", "arbitrary"),
        name="hgrn_scan_bw" if reverse else "hgrn_scan_fw",
    )(p, p, p, lower_bound.reshape(1, d), tri, wmasks, nmasks, *extra)


def kernel(x, c, ctx, c_ctx, w_mod, b_mod, norm_gains, ffn_w_in, ffn_w_out, ab_w_in, qk_norm, ab_w_out,
           hgrn_w_in, hgrn_lb_logits, hgrn_o_norm, hgrn_w_out, final_norm):
    batch, seq, d = x.shape
    lc = ctx.shape[1]
    depth = w_mod.shape[0]
    rows = _Rows(batch, seq, lc, ROW_TILE)
    rows_r = _Rows(batch, seq, lc, READOUT_ROW_TILE)
    assert seq % GRID_W == 0 and batch + 1 <= SUBLANES

    c_rows = jnp.concatenate([c_ctx[None, :], c, jnp.zeros((SUBLANES - 1 - batch, d), F32)], axis=0)
    mods_all = _modulation(c_rows, w_mod, b_mod).reshape(depth, SUBLANES, 3, 3, d)

    lb_cum = jnp.cumsum(jax.nn.softmax(hgrn_lb_logits.astype(F32), axis=0), axis=0)
    lower_bounds = lb_cum - lb_cum[0]

    ffn_w_in_b = ffn_w_in.astype(BF16)
    ffn_w_out_b = ffn_w_out.astype(BF16)
    fin = final_norm.reshape(1, d)
    rope = _rope_tables(seq, ROW_TILE)

    h = (x.reshape(batch * seq, d), ctx.reshape(batch * lc, d))
    for layer in range(depth):
        last = layer == depth - 1
        mods = mods_all[layer]
        gains = norm_gains[layer].reshape(3, 1, d)
        h = _ffn(h, mods, 0, gains[0], ffn_w_in_b, ffn_w_out_b, layer, 0, fin, rows, rows.tiles, False)
        if layer % 2 == 0:
            e = layer // 2
            f_all, q_all, k_all, vt_all = _ab_proj(h, mods, gains[1], ab_w_in[e].astype(BF16), qk_norm[e], rope, rows)
            attn = _attention(q_all, k_all, vt_all, rows)
            attn = _attention(q_all, k_all, vt_all, rows, lat_out=attn)
            h = _ab_out(_fourier_latent(f_all, rows), _fourier_ctx(f_all, rows), attn,
                        ab_w_out[e].astype(BF16), h, mods, rows)
        else:
            o = layer // 2
            p = _hgrn_proj(h, mods, gains[1], hgrn_w_in[o].astype(BF16), rows)
            o_fw = _hgrn_scan(p, lower_bounds[layer], rows, False)
            o_sum = _hgrn_scan(p, lower_bounds[layer], rows, True, other=o_fw)
            gain_o = jnp.tile(hgrn_o_norm[o], d // HEAD_DIM).reshape(1, d)
            n_t = rows_r.lat_tiles if last else rows_r.tiles
            h = _hgrn_out(o_sum, p, gain_o, hgrn_w_out[o].astype(BF16), h, mods, rows_r, n_t)
        n_t = rows.lat_tiles if last else rows.tiles
        h = _ffn(h, mods, 2, gains[2], ffn_w_in_b, ffn_w_out_b, layer, 1, fin, rows, n_t, last)
    return h[:batch * seq].reshape(batch, seq, d)
```

```python
import functools
import math

import jax
import jax.numpy as jnp
import numpy as np
from jax import lax
from jax.experimental import pallas as pl
from jax.experimental.pallas import tpu as pltpu

F32 = jnp.float32
BF16 = jnp.bfloat16

EPS = 1e-6
N_MOD = 9
HEAD_DIM = 128
N_KV_HEADS = 4
FOURIER_WIDTH = 512
FOURIER_GROUP_DIM = 128
GRID_W = 64
ROPE_THETA = 10000.0
ROPE_AXIS_DIM = HEAD_DIM // 2
ATTN_SCALE = HEAD_DIM ** -0.5
LOG2_E = math.log2(math.e)

LANES = 128
SUBLANES = 8
BF16_SUBLANES = 16
VT_ROWS = HEAD_DIM + BF16_SUBLANES
VMEM_LIMIT_BYTES = 56 * 1024 * 1024

ROW_TILE = 512
READOUT_ROW_TILE = 512
FFN_TILE = 512
PROJ_COL_TILE = 2048
PROJ_ROW_GROUP = 3
MOD_COL_TILE = 1024
ATTN_Q_TILE = 1024
SCAN_CHUNK = 128
SCAN_HEADS_PER_BLOCK = 8
SCAN_CHUNKS_PER_STEP = 2
FFT_B = 128
FFT_B_BLOCK = 8


def _cparams(*sem):
    return pltpu.CompilerParams(dimension_semantics=sem, vmem_limit_bytes=VMEM_LIMIT_BYTES)


def _dot(a, b):
    return jnp.dot(a, b, preferred_element_type=F32)


def _dot_hi(a, b):
    return jnp.dot(a, b, preferred_element_type=F32, precision=lax.Precision.HIGHEST)


def _split_bf16(x):
    hi = x.astype(BF16)
    return hi, (x - hi.astype(F32)).astype(BF16)


def _dot_split(a, b):
    (ah, al), (bh, bl) = a, b
    return _dot(ah, bh) + _dot(al, bh) + _dot(ah, bl)


def _dot_nt(a, b):
    return lax.dot_general(a, b, (((1,), (1,)), ((), ())), preferred_element_type=F32)


def _dot_tn(a, b):
    return lax.dot_general(a, b, (((0,), (0,)), ((), ())), preferred_element_type=F32)


def _silu(x):
    return x * jax.nn.sigmoid(x)


def _rms(x, gain):
    return x * lax.rsqrt(jnp.mean(x * x, axis=-1, keepdims=True) + EPS) * gain


def _ada(h, gain, shift, scale):
    return _rms(h, gain) * (1.0 + scale) + shift


def _mod_kernel(c_ref, w_ref, b_ref, o_ref):
    a = _silu(c_ref[...]).astype(BF16)
    o_ref[...] = _dot(a, w_ref[...].astype(BF16)) + b_ref[...]


def _modulation(c_rows, w_mod, b_mod):
    depth, d, nd = w_mod.shape
    tn = MOD_COL_TILE
    return pl.pallas_call(
        _mod_kernel,
        out_shape=jax.ShapeDtypeStruct((depth, SUBLANES, nd), F32),
        grid=(depth, nd // tn),
        in_specs=[
            pl.BlockSpec((SUBLANES, d), lambda l, j: (0, 0)),
            pl.BlockSpec((None, d, tn), lambda l, j: (l, 0, j)),
            pl.BlockSpec((None, 1, tn), lambda l, j: (l, 0, j)),
        ],
        out_specs=pl.BlockSpec((None, SUBLANES, tn), lambda l, j: (l, 0, j)),
        compiler_params=_cparams("parallel", "arbitrary"),
        name="modulation",
    )(c_rows, w_mod, b_mod.reshape(depth, 1, nd))


class _Rows:
    def __init__(self, batch, seq, ctx_len, tile):
        assert seq % tile == 0 and (batch * ctx_len) % tile == 0
        self.batch, self.seq, self.ctx_len, self.tile = batch, seq, ctx_len, tile
        self.n_lat = batch * seq
        self.n = self.n_lat + batch * ctx_len
        self.lat_tiles = self.n_lat // tile
        self.tiles = self.n // tile
        self.tiles_per_batch = seq // tile

    def group(self, i):
        return jnp.where(i < self.lat_tiles, 1 + i // self.tiles_per_batch, 0)


def _ffn_kernel(*refs, final, split_tiles):
    if split_tiles is None:
        h_ref, mod_ref, gain_ref, wa_ref, wb_ref, wo_ref, fin_ref, o_ref, xn_ref, acc_ref = refs
        load_h = lambda: h_ref[...]
    else:
        hl_ref, hc_ref, mod_ref, gain_ref, wa_ref, wb_ref, wo_ref, fin_ref, o_ref, xn_ref, acc_ref = refs
        load_h = lambda: jnp.where(pl.program_id(0) < split_tiles, hl_ref[...], hc_ref[...])
    i = pl.program_id(0)
    j = pl.program_id(1)

    @pl.when(j == 0)
    def _():
        xn_ref[...] = _ada(load_h(), gain_ref[...], mod_ref[0:1, :], mod_ref[1:2, :]).astype(BF16)

    @pl.when(jnp.logical_and(i == 0, j == 0))
    def _():
        acc_ref[...] = jnp.zeros_like(acc_ref)

    xn = xn_ref[...]
    a = _dot(xn, wa_ref[...])
    b = _dot(xn, wb_ref[...])
    g = (_silu(a) * b).astype(BF16)
    acc_ref[...] = jnp.where(j == 0, 0.0, acc_ref[...]) + _dot(g, wo_ref[...])

    @pl.when(j == pl.num_programs(1) - 1)
    def _():
        out = load_h() + 0.5 * mod_ref[2:3, :] * acc_ref[...]
        if final:
            out = _rms(out, fin_ref[...])
        o_ref[...] = out


def _ffn(h, mods, sub, gain, w_in, w_out, layer, which, fin, rows, n_tiles, final):
    f, d = w_out.shape[2:]
    tm, tf = rows.tile, FFN_TILE
    nf = f // tf
    if isinstance(h, tuple):
        lt = rows.lat_tiles
        row_specs = [pl.BlockSpec((tm, d), lambda i, j: (jnp.minimum(i, lt - 1), 0)),
                     pl.BlockSpec((tm, d), lambda i, j: (jnp.maximum(i - lt, 0), 0))]
        row_args, split_tiles = h, lt
    else:
        row_specs = [pl.BlockSpec((tm, d), lambda i, j: (i, 0))]
        row_args, split_tiles = (h,), None
    return pl.pallas_call(
        functools.partial(_ffn_kernel, final=final, split_tiles=split_tiles),
        out_shape=jax.ShapeDtypeStruct((n_tiles * tm, d), F32),
        grid=(n_tiles, nf),
        in_specs=row_specs + [
            pl.BlockSpec((None, None, 3, d), lambda i, j: (rows.group(i), sub, 0, 0)),
            pl.BlockSpec((1, d), lambda i, j: (0, 0)),
            pl.BlockSpec((None, None, d, tf), lambda i, j: (layer, which, 0, j)),
            pl.BlockSpec((None, None, d, tf), lambda i, j: (layer, which, 0, nf + j)),
            pl.BlockSpec((None, None, tf, d), lambda i, j: (layer, which, j, 0)),
            pl.BlockSpec((1, d), lambda i, j: (0, 0)),
        ],
        out_specs=pl.BlockSpec((tm, d), lambda i, j: (i, 0)),
        scratch_shapes=[pltpu.VMEM((tm, d), BF16), pltpu.VMEM((tm, d), F32)],
        compiler_params=_cparams("arbitrary", "arbitrary"),
        name="ffn",
    )(*row_args, mods, gain, w_in, w_in, w_out, fin)


def _rope_tables(seq, tile):
    t = np.arange(seq)
    inv_freq = ROPE_THETA ** (-np.arange(0, ROPE_AXIS_DIM, 2, dtype=np.float64) / ROPE_AXIS_DIM)
    ang = np.concatenate([(t // GRID_W)[:, None] * inv_freq, (t % GRID_W)[:, None] * inv_freq], axis=-1)
    nf = ROPE_AXIS_DIM // 2
    cos = np.cos(ang).reshape(seq, 2, 1, nf)
    sin = np.sin(ang).reshape(seq, 2, 1, nf)
    zero = np.zeros_like(sin)
    c_full = np.broadcast_to(cos, (seq, 2, 2, nf)).reshape(seq, HEAD_DIM)
    s_up = np.concatenate([-sin, zero], axis=2).reshape(seq, HEAD_DIM)
    s_dn = np.concatenate([zero, sin], axis=2).reshape(seq, HEAD_DIM)
    lat = np.concatenate([c_full, s_up, s_dn], axis=1)
    ident = np.concatenate([np.ones((tile, HEAD_DIM)), np.zeros((tile, 2 * HEAD_DIM))], axis=1)
    return jnp.asarray(np.concatenate([lat, ident], axis=0), dtype=F32)


def _norm_rope_heads(acc, gain, rope, post_scale=None):
    nf = ROPE_AXIS_DIM // 2
    c, s_up, s_dn = rope[:, :HEAD_DIM], rope[:, HEAD_DIM:2 * HEAD_DIM], rope[:, 2 * HEAD_DIM:]
    heads = []
    for hh in range(acc.shape[1] // HEAD_DIM):
        y = _rms(acc[:, hh * HEAD_DIM:(hh + 1) * HEAD_DIM], gain)
        y = y * c + pltpu.roll(y, HEAD_DIM - nf, 1) * s_up + pltpu.roll(y, nf, 1) * s_dn
        heads.append(y if post_scale is None else y * post_scale)
    return jnp.concatenate(heads, axis=1).astype(BF16)


def _ab_proj_kernel(h_ref, mod_ref, gain_ref, w_ref, qkn_ref, rope_ref, f_ref, q_ref, k_ref, vt_ref):
    xn = _ada(h_ref[...], gain_ref[...], mod_ref[0:1, :], mod_ref[1:2, :]).astype(BF16)
    c0 = f_ref.shape[1]
    c1 = c0 + q_ref.shape[1]
    c2 = c1 + k_ref.shape[1]
    q_ref[...] = _norm_rope_heads(_dot(xn, w_ref[:, c0:c1]), qkn_ref[0:1, :], rope_ref[...], ATTN_SCALE * LOG2_E)
    k_ref[...] = _norm_rope_heads(_dot(xn, w_ref[:, c1:c2]), qkn_ref[1:2, :], rope_ref[...])
    v = _dot(xn, w_ref[:, c2:])
    ones = jnp.ones((vt_ref.shape[1] - HEAD_DIM, vt_ref.shape[2]), BF16)
    for hh in range(N_KV_HEADS):
        vt_ref[hh, :HEAD_DIM, :] = v[:, hh * HEAD_DIM:(hh + 1) * HEAD_DIM].T.astype(BF16)
        vt_ref[hh, HEAD_DIM:, :] = ones
    f_ref[...] = _dot(xn, w_ref[:, :c0])


def _ab_proj(h, mods, gain, w_in, qk_norm, rope, rows):
    d = h.shape[1]
    tm = rows.tile
    kv_width = N_KV_HEADS * HEAD_DIM
    q_width = w_in.shape[1] - FOURIER_WIDTH - 2 * kv_width
    n = rows.n
    rope_blk = lambda i: (jnp.where(i < rows.lat_tiles, i % rows.tiles_per_batch, rows.tiles_per_batch), 0)
    return pl.pallas_call(
        _ab_proj_kernel,
        out_shape=(
            jax.ShapeDtypeStruct((n, FOURIER_WIDTH), F32),
            jax.ShapeDtypeStruct((n, q_width), BF16),
            jax.ShapeDtypeStruct((n, kv_width), BF16),
            jax.ShapeDtypeStruct((N_KV_HEADS, rows.tiles, VT_ROWS, tm), BF16),
        ),
        grid=(rows.tiles,),
        in_specs=[
            pl.BlockSpec((tm, d), lambda i: (i, 0)),
            pl.BlockSpec((None, None, 3, d), lambda i: (rows.group(i), 1, 0, 0)),
            pl.BlockSpec((1, d), lambda i: (0, 0)),
            pl.BlockSpec(w_in.shape, lambda i: (0, 0)),
            pl.BlockSpec((2, HEAD_DIM), lambda i: (0, 0)),
            pl.BlockSpec((tm, 3 * HEAD_DIM), rope_blk),
        ],
        out_specs=(
            pl.BlockSpec((tm, FOURIER_WIDTH), lambda i: (i, 0)),
            pl.BlockSpec((tm, q_width), lambda i: (i, 0)),
            pl.BlockSpec((tm, kv_width), lambda i: (i, 0)),
            pl.BlockSpec((N_KV_HEADS, None, VT_ROWS, tm), lambda i: (0, i, 0, 0)),
        ),
        compiler_params=_cparams("parallel"),
        name="ab_proj",
    )(h, mods, gain, w_in, qk_norm, rope)


def _attn_kernel(*refs, lat_chunks, group):
    if lat_chunks:
        q_ref, kc_ref, vtc_ref, kl_ref, vtl_ref, o_ref, acc_ref, s_ref = refs
    else:
        q_ref, kc_ref, vtc_ref, _, o_ref, acc_ref = refs
    tq = q_ref.shape[0]
    q = q_ref[...]
    qs = jnp.concatenate([q[:, g * HEAD_DIM:(g + 1) * HEAD_DIM] for g in range(group)], axis=0)
    nq = group * tq
    acc_ref[...] = jnp.zeros_like(acc_ref)

    def scores(k):
        return _dot_nt(k, qs)

    def update(s, vt, m):
        m_new = jnp.maximum(m, jnp.max(s, axis=0, keepdims=True))
        p = jnp.exp2(s - m_new).astype(BF16)
        acc_ref[...] = jnp.exp2(m - m_new) * acc_ref[...] + _dot(vt, p)
        return m_new

    m = update(scores(kc_ref[...]), vtc_ref[...], jnp.full((1, nq), -jnp.inf, F32))
    if lat_chunks:
        tk = vtl_ref.shape[2]

        def lat_scores(c):
            return scores(kl_ref[pl.ds(pl.multiple_of(c * tk, tk), tk), :])

        assert lat_chunks % 2 == 0
        s_ref[0] = lat_scores(0)

        def body(i, m):
            c = 2 * i
            s_ref[1] = lat_scores(c + 1)
            m = update(s_ref[0], vtl_ref[c], m)
            s_ref[0] = lat_scores(jnp.minimum(c + 2, lat_chunks - 1))
            return update(s_ref[1], vtl_ref[c + 1], m)

        m = lax.fori_loop(0, lat_chunks // 2, body, m)

    out = (acc_ref[:HEAD_DIM, :] / acc_ref[HEAD_DIM:HEAD_DIM + 1, :]).T
    o_ref[...] = jnp.concatenate([out[g * tq:(g + 1) * tq, :] for g in range(group)], axis=1).astype(BF16)


def _attention(q_all, k_all, vt_all, rows, lat_out=None):
    b, seq, lc, tile = rows.batch, rows.seq, rows.ctx_len, rows.tile
    group = q_all.shape[1] // (N_KV_HEADS * HEAD_DIM)
    gw = group * HEAD_DIM
    assert tile % lc == 0 and seq % tile == 0
    ctx_blk0 = rows.n_lat // lc
    ctx_tile = lambda bi: ((rows.n_lat + bi * lc) // tile, 0, ((rows.n_lat + bi * lc) % tile) // lc)
    kc_spec = pl.BlockSpec((lc, HEAD_DIM), lambda bi, hi, i: (ctx_blk0 + bi, hi))
    vtc_spec = pl.BlockSpec((None, None, VT_ROWS, lc), lambda bi, hi, i: (hi,) + ctx_tile(bi))
    if lat_out is None:
        tq = ATTN_Q_TILE
        assert tq % tile == 0
        nqt = seq // tq
        lat_chunks = seq // tile
        q_spec = pl.BlockSpec((tq, gw), lambda bi, hi, i: (bi * nqt + i, hi))
        kl_spec = pl.BlockSpec((seq, HEAD_DIM), lambda bi, hi, i: (bi, hi))
        vtl_spec = pl.BlockSpec((None, lat_chunks, VT_ROWS, tile), lambda bi, hi, i: (hi, bi, 0, 0))
        in_specs = [q_spec, kc_spec, vtc_spec, kl_spec, vtl_spec]
        args = (q_all, k_all, vt_all, k_all, vt_all)
        aliases = {}
    else:
        tq = lc
        nqt = 1
        lat_chunks = 0
        q_spec = pl.BlockSpec((tq, gw), lambda bi, hi, i: (ctx_blk0 + bi, hi))
        in_specs = [q_spec, kc_spec, vtc_spec, pl.BlockSpec(memory_space=pl.ANY)]
        args = (q_all, k_all, vt_all, lat_out)
        aliases = {3: 0}
    return pl.pallas_call(
        functools.partial(_attn_kernel, lat_chunks=lat_chunks, group=group),
        out_shape=jax.ShapeDtypeStruct((rows.n, q_all.shape[1]), BF16),
        grid=(b, N_KV_HEADS, nqt),
        in_specs=in_specs,
        out_specs=q_spec,
        scratch_shapes=[pltpu.VMEM((VT_ROWS, group * tq), F32)]
        + ([pltpu.VMEM((2, tile, group * tq), F32)] if lat_chunks else []),
        input_output_aliases=aliases,
        compiler_params=_cparams("parallel", "parallel", "arbitrary"),
        name="attention_ctx" if lat_chunks == 0 else "attention_lat",
    )(*args)


def _dft_cs(n):
    idx = np.arange(n)
    ang = 2.0 * np.pi * ((idx[:, None] * idx[None, :]) % n) / n
    return np.cos(ang), np.sin(ang)


def _const_split(m):
    m = np.asarray(m, dtype=np.float32)
    hi = jnp.asarray(m).astype(BF16)
    lo = (jnp.asarray(m) - hi.astype(F32)).astype(BF16)
    return jnp.stack([hi, lo])


def _fft1_kernel(x_ref, fa_ref, tw_ref, o_ref, *, a):
    w = o_ref.shape[2]
    for r in range(o_ref.shape[0]):
        z = _dot_split((fa_ref[0], fa_ref[1]), _split_bf16(x_ref[:, r * w:(r + 1) * w]))
        zr, zi = z[:a], z[a:]
        tc = jnp.concatenate([tw_ref[r, 0]] * (w // LANES), axis=1)
        ts = jnp.concatenate([tw_ref[r, 1]] * (w // LANES), axis=1)
        o_ref[r, :a, :] = zr * tc - zi * ts
        o_ref[r, a:, :] = zr * ts + zi * tc


def _fft2_kernel(zr_ref, zi_ref, m2_ref, mc_ref, o_ref, *, scale):
    for r in range(o_ref.shape[1]):
        z = jnp.concatenate([zr_ref[:, r, :], zi_ref[:, r, :]], axis=0)
        v = _dot_split((m2_ref[0], m2_ref[1]), _split_bf16(z))
        vr, vi = v[:FFT_B], v[FFT_B:]
        outs = []
        for g in range(o_ref.shape[2] // LANES):
            u = jnp.concatenate([vr[:, g * LANES:(g + 1) * LANES], vi[:, g * LANES:(g + 1) * LANES]], axis=1)
            outs.append(_dot_split(_split_bf16(u), (mc_ref[0], mc_ref[1])))
        o_ref[:, r, :] = jnp.concatenate(outs, axis=1) * scale


def _fourier_latent(f_all, rows):
    b, seq = rows.batch, rows.seq
    w = FOURIER_WIDTH
    a = seq // FFT_B
    assert a % SUBLANES == 0 and FOURIER_GROUP_DIM == LANES
    ca, sa = _dft_cs(a)
    fa = _const_split(np.concatenate([ca, sa], axis=0))
    p1b = (np.arange(a)[None, :] * np.arange(FFT_B)[:, None]) % seq
    ang = 2.0 * np.pi * p1b / seq
    tw = np.stack([np.cos(ang), np.sin(ang)], axis=1)[..., None]
    tw = jnp.asarray(np.broadcast_to(tw, (FFT_B, 2, a, LANES)), dtype=F32)
    cb, sb = _dft_cs(FFT_B)
    m2 = _const_split(np.block([[cb, -sb], [sb, cb]]))
    cc, sc = _dft_cs(FOURIER_GROUP_DIM)
    mc = _const_split(np.concatenate([cc, -sc], axis=0))

    blk = FFT_B_BLOCK
    x2 = f_all.reshape(rows.n // FFT_B, FFT_B * w)
    z = pl.pallas_call(
        functools.partial(_fft1_kernel, a=a),
        out_shape=jax.ShapeDtypeStruct((b, FFT_B, 2 * a, w), F32),
        grid=(b, FFT_B // blk),
        in_specs=[
            pl.BlockSpec((a, blk * w), lambda bi, j: (bi, j)),
            pl.BlockSpec((2, 2 * a, a), lambda bi, j: (0, 0, 0)),
            pl.BlockSpec((blk, 2, a, LANES), lambda bi, j: (j, 0, 0, 0)),
        ],
        out_specs=pl.BlockSpec((None, blk, 2 * a, w), lambda bi, j: (bi, j, 0, 0)),
        compiler_params=_cparams("parallel", "parallel"),
        name="fourier_stage1",
    )(x2, fa, tw)

    nblk = a // blk
    y = pl.pallas_call(
        functools.partial(_fft2_kernel, scale=1.0 / math.sqrt(seq * FOURIER_GROUP_DIM)),
        out_shape=jax.ShapeDtypeStruct((b, FFT_B, a, w), F32),
        grid=(b, nblk),
        in_specs=[
            pl.BlockSpec((None, FFT_B, blk, w), lambda bi, j: (bi, 0, j, 0)),
            pl.BlockSpec((None, FFT_B, blk, w), lambda bi, j: (bi, 0, nblk + j, 0)),
            pl.BlockSpec((2, 2 * FFT_B, 2 * FFT_B), lambda bi, j: (0, 0, 0)),
            pl.BlockSpec((2, 2 * LANES, LANES), lambda bi, j: (0, 0, 0)),
        ],
        out_specs=pl.BlockSpec((None, FFT_B, blk, w), lambda bi, j: (bi, 0, j, 0)),
        compiler_params=_cparams("parallel", "parallel"),
        name="fourier_stage2",
    )(z, z, m2, mc)
    return y.reshape(b * seq, w)


def _dft_ctx_kernel(x_ref, cn_ref, sn_ref, cc_ref, sc_ref, o_ref, *, scale):
    x = x_ref[...]
    outs = []
    for g in range(x.shape[1] // LANES):
        xg = x[:, g * LANES:(g + 1) * LANES]
        outs.append(_dot_hi(cn_ref[...], _dot_hi(xg, cc_ref[...])) - _dot_hi(sn_ref[...], _dot_hi(xg, sc_ref[...])))
    o_ref[...] = jnp.concatenate(outs, axis=1) * scale


def _fourier_ctx(f_all, rows):
    b, lc = rows.batch, rows.ctx_len
    w = FOURIER_WIDTH
    cn, sn = _dft_cs(lc)
    cc, sc = _dft_cs(FOURIER_GROUP_DIM)
    blk0 = rows.n_lat // lc
    mat = lambda m: pl.BlockSpec(m.shape, lambda bi: (0, 0))
    consts = [jnp.asarray(m, dtype=F32) for m in (cn, sn, cc, sc)]
    return pl.pallas_call(
        functools.partial(_dft_ctx_kernel, scale=1.0 / math.sqrt(lc * FOURIER_GROUP_DIM)),
        out_shape=jax.ShapeDtypeStruct((b * lc, w), F32),
        grid=(b,),
        in_specs=[pl.BlockSpec((lc, w), lambda bi: (blk0 + bi, 0))] + [mat(m) for m in consts],
        out_specs=pl.BlockSpec((lc, w), lambda bi: (bi, 0)),
        compiler_params=_cparams("parallel"),
        name="fourier_ctx",
    )(f_all, *consts)


def _ab_out_kernel(x1l_ref, x1c_ref, x2_ref, w_ref, h_ref, mod_ref, o_ref, *, lat_tiles):
    x1 = jnp.where(pl.program_id(0) < lat_tiles, x1l_ref[...], x1c_ref[...])
    x = jnp.concatenate([x1.astype(BF16), x2_ref[...]], axis=1)
    o_ref[...] = h_ref[...] + mod_ref[2:3, :] * _dot(x, w_ref[...])


def _ab_out(x1_lat, x1_ctx, x2, w_out, h, mods, rows):
    d = h.shape[1]
    tm = rows.tile
    lt = rows.lat_tiles
    return pl.pallas_call(
        functools.partial(_ab_out_kernel, lat_tiles=lt),
        out_shape=jax.ShapeDtypeStruct((rows.n, d), F32),
        grid=(rows.tiles,),
        in_specs=[
            pl.BlockSpec((tm, x1_lat.shape[1]), lambda i: (jnp.minimum(i, lt - 1), 0)),
            pl.BlockSpec((tm, x1_ctx.shape[1]), lambda i: (jnp.maximum(i - lt, 0), 0)),
            pl.BlockSpec((tm, x2.shape[1]), lambda i: (i, 0)),
            pl.BlockSpec(w_out.shape, lambda i: (0, 0)),
            pl.BlockSpec((tm, d), lambda i: (i, 0)),
            pl.BlockSpec((None, None, 3, d), lambda i: (rows.group(i), 1, 0, 0)),
        ],
        out_specs=pl.BlockSpec((tm, d), lambda i: (i, 0)),
        compiler_params=_cparams("parallel"),
        name="ab_out",
    )(x1_lat, x1_ctx, x2, w_out, h, mods)


def _hgrn_out_kernel(osum_ref, g_ref, gain_ref, w_ref, h_ref, mod_ref, o_ref):
    o = osum_ref[...]
    parts = []
    for hh in range(o.shape[1] // HEAD_DIM):
        oh = o[:, hh * HEAD_DIM:(hh + 1) * HEAD_DIM]
        parts.append(oh * lax.rsqrt(jnp.mean(oh * oh, axis=-1, keepdims=True) + EPS))
    on = jnp.concatenate(parts, axis=1) * gain_ref[...]
    y = (on * jax.nn.sigmoid(g_ref[...])).astype(BF16)
    o_ref[...] = h_ref[...] + mod_ref[2:3, :] * _dot(y, w_ref[...])


def _hgrn_out(o_sum, p, gain, w_out, h, mods, rows_r, n_tiles):
    d = h.shape[1]
    tm = rows_r.tile
    g_blk = (p.shape[1] - d) // d
    return pl.pallas_call(
        _hgrn_out_kernel,
        out_shape=jax.ShapeDtypeStruct((n_tiles * tm, d), F32),
        grid=(n_tiles,),
        in_specs=[
            pl.BlockSpec((tm, d), lambda i: (i, 0)),
            pl.BlockSpec((tm, d), lambda i: (i, g_blk)),
            pl.BlockSpec((1, d), lambda i: (0, 0)),
            pl.BlockSpec(w_out.shape, lambda i: (0, 0)),
            pl.BlockSpec((tm, d), lambda i: (i, 0)),
            pl.BlockSpec((None, None, 3, d), lambda i: (rows_r.group(i), 1, 0, 0)),
        ],
        out_specs=pl.BlockSpec((tm, d), lambda i: (i, 0)),
        compiler_params=_cparams("parallel"),
        name="hgrn_out",
    )(o_sum, p, gain, w_out, h, mods)


def _hgrn_proj_kernel(h_ref, mod_ref, gain_ref, w_ref, o_ref, xn_ref, *, n_silu):
    j = pl.program_id(1)
    r = pl.program_id(2)

    @pl.when(j == 0)
    def _():
        xn_ref[r] = _ada(h_ref[...], gain_ref[...], mod_ref[0:1, :], mod_ref[1:2, :]).astype(BF16)

    acc = _dot(xn_ref[r], w_ref[...])
    o_ref[...] = jnp.where(j < n_silu, _silu(acc), acc)


def _hgrn_proj(h, mods, gain, w_in, rows):
    d = h.shape[1]
    tm, tn = rows.tile, PROJ_COL_TILE
    nw = w_in.shape[1]
    gs = next(k for k in (PROJ_ROW_GROUP, 2, 1) if rows.tiles % k == 0)
    tile = lambda g, r: g * gs + r
    return pl.pallas_call(
        functools.partial(_hgrn_proj_kernel, n_silu=d // tn),
        out_shape=jax.ShapeDtypeStruct((rows.n, nw), F32),
        grid=(rows.tiles // gs, nw // tn, gs),
        in_specs=[
            pl.BlockSpec((tm, d), lambda g, j, r: (jnp.where(j == 0, tile(g, r), tile(g, gs - 1)), 0)),
            pl.BlockSpec((None, None, 3, d), lambda g, j, r: (rows.group(tile(g, r)), 1, 0, 0)),
            pl.BlockSpec((1, d), lambda g, j, r: (0, 0)),
            pl.BlockSpec((d, tn), lambda g, j, r: (0, j)),
        ],
        out_specs=pl.BlockSpec((tm, tn), lambda g, j, r: (tile(g, r), j)),
        scratch_shapes=[pltpu.VMEM((gs, tm, d), BF16)],
        compiler_params=_cparams("arbitrary", "arbitrary", "arbitrary"),
        name="hgrn_proj",
    )(h, mods, gain, w_in)


def _scan_levels(chunk):
    return [chunk >> (i + 1) for i in range(int(math.log2(chunk)))]


def _scan_consts(chunk, reverse):
    t = np.arange(chunk)[:, None]
    s = np.arange(chunk)[None, :]
    tri = (s >= t) if reverse else (s <= t)
    wide, narrow = [], []
    for h in _scan_levels(chunk):
        same = (t // (2 * h)) == (s // (2 * h))
        t_up = (t // h) % 2 == 1
        s_up = (s // h) % 2 == 1
        mask = same & ((~t_up & s_up) if reverse else (t_up & ~s_up))
        if h >= SUBLANES:
            wide.append(mask[np.nonzero(~t_up[:, 0] if reverse else t_up[:, 0])[0]])
        else:
            narrow.append(mask)
    narrow.append(t == s)
    return (jnp.asarray(tri, dtype=BF16), jnp.asarray(np.stack(wide), dtype=F32),
            jnp.asarray(np.stack(narrow), dtype=F32))


def _seg_bcast(x, h, reverse):
    c, w = x.shape
    off = h if reverse else h - 1
    if 2 * h >= 2 * SUBLANES:
        pieces = [jnp.broadcast_to(x[g * 2 * h + off:g * 2 * h + off + 1, :], (2 * h, w)) for g in range(c // (2 * h))]
        return pieces[0] if len(pieces) == 1 else jnp.concatenate(pieces, axis=0)
    x3 = x.reshape(c // SUBLANES, SUBLANES, w)
    sub = lax.broadcasted_iota(jnp.int32, x3.shape, 1)
    y = None
    for g in range(SUBLANES // (2 * h)):
        piece = jnp.broadcast_to(x3[:, g * 2 * h + off:g * 2 * h + off + 1, :], x3.shape)
        y = piece if y is None else jnp.where(sub >= g * 2 * h, piece, y)
    return y.reshape(c, w)


def _scan_kernel(*refs, reverse, chunk, accumulate):
    if accumulate:
        q_ref, f_ref, v_ref, lb_ref, tri_ref, wmask_ref, nmask_ref, other_ref, o_ref, st_ref = refs
    else:
        q_ref, f_ref, v_ref, lb_ref, tri_ref, wmask_ref, nmask_ref, o_ref, st_ref = refs

    @pl.when(pl.program_id(2) == 0)
    def _():
        st_ref[...] = jnp.zeros_like(st_ref)

    rows, width = q_ref.shape
    levels = _scan_levels(chunk)
    spans = [slice(ci * chunk, (ci + 1) * chunk) for ci in range(rows // chunk)]
    lanes = [slice(hh * HEAD_DIM, (hh + 1) * HEAD_DIM) for hh in range(width // HEAD_DIM)]
    q = q_ref[...]
    fl = f_ref[...]
    lb = lb_ref[...]
    u = jnp.exp(-jnp.abs(fl))
    key = (1.0 - lb) * (jnp.where(fl > 0.0, u, 1.0) / (1.0 + u))
    log_sig = jnp.minimum(fl, 0.0) - jnp.log(1.0 + u)
    x1 = jnp.log(lb)
    x2 = jnp.log1p(-lb) + log_sig
    delta = x1 - x2
    lf = jnp.where(jnp.isnan(delta), x1 + x2, jnp.maximum(x1, x2) + jnp.log(1.0 + jnp.exp(-jnp.abs(delta))))
    lf = lf * LOG2_E
    l1 = lf.astype(BF16)
    r1 = lf - l1.astype(F32)
    l2 = r1.astype(BF16)
    l3 = (r1 - l2.astype(F32)).astype(BF16)
    parts = jnp.concatenate([l1, l2, l3], axis=1)
    tri = tri_ref[...]
    cs = jnp.concatenate([_dot(tri, parts[sp]) for sp in spans], axis=0)
    b = cs[:, :width] + cs[:, width:2 * width] + cs[:, 2 * width:]
    totals = [b[sp.start:sp.start + 1, :] if reverse else b[sp.stop - 1:sp.stop, :] for sp in spans]
    total_rows = jnp.concatenate([jnp.broadcast_to(t, (chunk, width)) for t in totals], axis=0)

    q_in = (q * jnp.exp2(b)).astype(BF16)
    k_out = (key * jnp.exp2(total_rows - b)).astype(BF16)
    v = v_ref[...].astype(BF16)

    order = list(range(len(spans)))[::-1] if reverse else list(range(len(spans)))
    kv = [[_dot_tn(v[sp, sl], k_out[sp, sl]) for sl in lanes] for sp in spans]
    o_state = [[None] * len(lanes) for _ in spans]
    for hh, sl in enumerate(lanes):
        st = st_ref[hh]
        for ci in order:
            o_state[ci][hh] = _dot_nt(q_in[spans[ci], sl], st.astype(BF16))
            st = st * jnp.exp2(totals[ci][:, sl]) + kv[ci][hh]
        st_ref[hh] = st

    n_blk = chunk // SUBLANES
    wide_levels = [h for h in levels if h >= SUBLANES]
    narrow_levels = [h for h in levels if h < SUBLANES]

    def narrow_scores(li, ql, kl):
        out = []
        for sp in spans:
            row = []
            for sl in lanes:
                s = nmask_ref[li] * _dot_nt(ql[sp, sl], kl[sp, sl])
                row.append([s[k * SUBLANES:(k + 1) * SUBLANES, :] for k in range(n_blk)])
            out.append(row)
        return out

    a = narrow_scores(len(narrow_levels), q.astype(BF16), key.astype(BF16))
    row_id = lax.broadcasted_iota(jnp.int32, (rows, width), 0)
    for li, h in enumerate(narrow_levels):
        if h == 1:
            e = jnp.where((row_id & 1) == (0 if reverse else 1), jnp.exp2(lf), 1.0)
        else:
            e = jnp.exp2(-jnp.abs(b - _seg_bcast(b, h, reverse)))
        new = narrow_scores(li, (q * e).astype(BF16), (key * e).astype(BF16))
        a = [[[x + y for x, y in zip(xb, yb)] for xb, yb in zip(xa, ya)] for xa, ya in zip(a, new)]

    for li, h in enumerate(wide_levels):
        q_parts, k_parts = [], []
        for g in range(rows // (2 * h)):
            lo = slice(g * 2 * h, g * 2 * h + h)
            hi = slice(g * 2 * h + h, (g + 1) * 2 * h)
            r = g * 2 * h + (h if reverse else h - 1)
            b_ref = jnp.broadcast_to(b[r:r + 1, :], (h, width))
            q_half, k_half = (lo, hi) if reverse else (hi, lo)
            q_parts.append(q[q_half] * jnp.exp2(b[q_half] - b_ref))
            k_part = key[k_half] * jnp.exp2(b_ref - b[k_half])
            zeros = jnp.zeros((h, width), F32)
            k_parts += [zeros, k_part] if reverse else [k_part, zeros]
        q_sel = jnp.concatenate(q_parts, axis=0).astype(BF16)
        k_hat = jnp.concatenate(k_parts, axis=0).astype(BF16)
        half = chunk // 2
        for ci, sp in enumerate(spans):
            for hh, sl in enumerate(lanes):
                s = wmask_ref[li] * _dot_nt(q_sel[ci * half:(ci + 1) * half, sl], k_hat[sp, sl])
                for j in range(chunk // (2 * h)):
                    first = (j * 2 * h + (0 if reverse else h)) // SUBLANES
                    for k in range(h // SUBLANES):
                        r0 = j * h + k * SUBLANES
                        a[ci][hh][first + k] = a[ci][hh][first + k] + s[r0:r0 + SUBLANES, :]

    for ci, sp in enumerate(spans):
        for hh, sl in enumerate(lanes):
            pairs = jnp.concatenate(a[ci][hh], axis=0).astype(BF16)
            o = o_state[ci][hh] + _dot(pairs, v[sp, sl])
            o_ref[sp, sl] = o + other_ref[sp, sl] if accumulate else o


def _hgrn_scan(p, lower_bound, rows, reverse, other=None):
    b, seq, lc = rows.batch, rows.seq, rows.ctx_len
    d = lower_bound.shape[0]
    c, hb = SCAN_CHUNK, SCAN_HEADS_PER_BLOCK
    r = c * SCAN_CHUNKS_PER_STEP
    wb = hb * HEAD_DIM
    ncb = d // wb
    assert lc % r == 0 and seq % r == 0
    nctx, nlat = lc // r, seq // r
    ctx0 = rows.n_lat // r
    f_blk = (2 if reverse else 1) * ncb
    v_blk = 3 * ncb

    def row(bi, s):
        if reverse:
            return jnp.where(s < nctx, ctx0 + bi * nctx + (nctx - 1 - s), bi * nlat + (nlat - 1 - (s - nctx)))
        return jnp.where(s < nctx, ctx0 + bi * nctx + s, bi * nlat + (s - nctx))

    tri, wmasks, nmasks = _scan_consts(c, reverse)
    o_spec = pl.BlockSpec((r, wb), lambda bi, hi, s: (row(bi, s), hi))
    extra = () if other is None else (other,)
    return pl.pallas_call(
        functools.partial(_scan_kernel, reverse=reverse, chunk=c, accumulate=other is not None),
        out_shape=jax.ShapeDtypeStruct((rows.n, d), F32),
        grid=(b, ncb, nctx + nlat),
        in_specs=[
            pl.BlockSpec((r, wb), lambda bi, hi, s: (row(bi, s), hi)),
            pl.BlockSpec((r, wb), lambda bi, hi, s: (row(bi, s), f_blk + hi)),
            pl.BlockSpec((r, wb), lambda bi, hi, s: (row(bi, s), v_blk + hi)),
            pl.BlockSpec((1, wb), lambda bi, hi, s: (0, hi)),
            pl.BlockSpec(tri.shape, lambda bi, hi, s: (0, 0)),
            pl.BlockSpec(wmasks.shape, lambda bi, hi, s: (0, 0, 0)),
            pl.BlockSpec(nmasks.shape, lambda bi, hi, s: (0, 0, 0)),
        ] + [o_spec] * len(extra),
        out_specs=o_spec,
        scratch_shapes=[pltpu.VMEM((hb, HEAD_DIM, HEAD_DIM), F32)],
        compiler_params=_cparams("parallel", "parallel", "arbitrary"),
        name="hgrn_scan_bw" if reverse else "hgrn_scan_fw",
    )(p, p, p, lower_bound.reshape(1, d), tri, wmasks, nmasks, *extra)


def kernel(x, c, ctx, c_ctx, w_mod, b_mod, norm_gains, ffn_w_in, ffn_w_out, ab_w_in, qk_norm, ab_w_out,
           hgrn_w_in, hgrn_lb_logits, hgrn_o_norm, hgrn_w_out, final_norm):
    batch, seq, d = x.shape
    lc = ctx.shape[1]
    depth = w_mod.shape[0]
    rows = _Rows(batch, seq, lc, ROW_TILE)
    rows_r = _Rows(batch, seq, lc, READOUT_ROW_TILE)
    assert seq % GRID_W == 0 and batch + 1 <= SUBLANES

    c_rows = jnp.concatenate([c_ctx[None, :], c, jnp.zeros((SUBLANES - 1 - batch, d), F32)], axis=0)
    mods_all = _modulation(c_rows, w_mod, b_mod).reshape(depth, SUBLANES, 3, 3, d)

    lb_cum = jnp.cumsum(jax.nn.softmax(hgrn_lb_logits.astype(F32), axis=0), axis=0)
    lower_bounds = lb_cum - lb_cum[0]

    ffn_w_in_b = ffn_w_in.astype(BF16)
    ffn_w_out_b = ffn_w_out.astype(BF16)
    fin = final_norm.reshape(1, d)
    rope = _rope_tables(seq, ROW_TILE)

    h = (x.reshape(batch * seq, d), ctx.reshape(batch * lc, d))
    for layer in range(depth):
        last = layer == depth - 1
        mods = mods_all[layer]
        gains = norm_gains[layer].reshape(3, 1, d)
        h = _ffn(h, mods, 0, gains[0], ffn_w_in_b, ffn_w_out_b, layer, 0, fin, rows, rows.tiles, False)
        if layer % 2 == 0:
            e = layer // 2
            f_all, q_all, k_all, vt_all = _ab_proj(h, mods, gains[1], ab_w_in[e].astype(BF16), qk_norm[e], rope, rows)
            attn = _attention(q_all, k_all, vt_all, rows)
            attn = _attention(q_all, k_all, vt_all, rows, lat_out=attn)
            h = _ab_out(_fourier_latent(f_all, rows), _fourier_ctx(f_all, rows), attn,
                        ab_w_out[e].astype(BF16), h, mods, rows)
        else:
            o = layer // 2
            p = _hgrn_proj(h, mods, gains[1], hgrn_w_in[o].astype(BF16), rows)
            o_fw = _hgrn_scan(p, lower_bounds[layer], rows, False)
            o_sum = _hgrn_scan(p, lower_bounds[layer], rows, True, other=o_fw)
            gain_o = jnp.tile(hgrn_o_norm[o], d // HEAD_DIM).reshape(1, d)
            n_t = rows_r.lat_tiles if last else rows_r.tiles
            h = _hgrn_out(o_sum, p, gain_o, hgrn_w_out[o].astype(BF16), h, mods, rows_r, n_t)
        n_t = rows.lat_tiles if last else rows.tiles
        h = _ffn(h, mods, 2, gains[2], ffn_w_in_b, ffn_w_out_b, layer, 1, fin, rows, n_t, last)
    return h[:batch * seq].reshape(batch, seq, d)
```

```python
import functools
import math

import jax
import jax.numpy as jnp
import numpy as np
from jax import lax
from jax.experimental import pallas as pl
from jax.experimental.pallas import tpu as pltpu

F32 = jnp.float32
BF16 = jnp.bfloat16

EPS = 1e-6
N_MOD = 9
HEAD_DIM = 128
N_KV_HEADS = 4
FOURIER_WIDTH = 512
FOURIER_GROUP_DIM = 128
GRID_W = 64
ROPE_THETA = 10000.0
ROPE_AXIS_DIM = HEAD_DIM // 2
ATTN_SCALE = HEAD_DIM ** -0.5
LOG2_E = math.log2(math.e)

LANES = 128
SUBLANES = 8
BF16_SUBLANES = 16
VT_ROWS = HEAD_DIM + BF16_SUBLANES
VMEM_LIMIT_BYTES = 56 * 1024 * 1024

ROW_TILE = 512
READOUT_ROW_TILE = 512
FFN_TILE = 512
FFN_ROW_GROUP = 3
PROJ_COL_TILE = 2048
PROJ_ROW_GROUP = 3
MOD_COL_TILE = 1024
ATTN_Q_TILE = 1024
SCAN_CHUNK = 128
SCAN_HEADS_PER_BLOCK = 8
SCAN_CHUNKS_PER_STEP = 2
FFT_B = 128
FFT_B_BLOCK = 8


def _cparams(*sem):
    return pltpu.CompilerParams(dimension_semantics=sem, vmem_limit_bytes=VMEM_LIMIT_BYTES)


def _dot(a, b):
    return jnp.dot(a, b, preferred_element_type=F32)


def _dot_hi(a, b):
    return jnp.dot(a, b, preferred_element_type=F32, precision=lax.Precision.HIGHEST)


def _split_bf16(x):
    hi = x.astype(BF16)
    return hi, (x - hi.astype(F32)).astype(BF16)


def _dot_split(a, b):
    (ah, al), (bh, bl) = a, b
    return _dot(ah, bh) + _dot(al, bh) + _dot(ah, bl)


def _dot_nt(a, b):
    return lax.dot_general(a, b, (((1,), (1,)), ((), ())), preferred_element_type=F32)


def _dot_tn(a, b):
    return lax.dot_general(a, b, (((0,), (0,)), ((), ())), preferred_element_type=F32)


def _silu(x):
    return x * jax.nn.sigmoid(x)


def _rms(x, gain):
    return x * lax.rsqrt(jnp.mean(x * x, axis=-1, keepdims=True) + EPS) * gain


def _ada(h, gain, shift, scale):
    return _rms(h, gain) * (1.0 + scale) + shift


def _mod_kernel(c_ref, w_ref, b_ref, o_ref):
    a = _silu(c_ref[...]).astype(BF16)
    o_ref[...] = _dot(a, w_ref[...].astype(BF16)) + b_ref[...]


def _modulation(c_rows, w_mod, b_mod):
    depth, d, nd = w_mod.shape
    tn = MOD_COL_TILE
    return pl.pallas_call(
        _mod_kernel,
        out_shape=jax.ShapeDtypeStruct((depth, SUBLANES, nd), F32),
        grid=(depth, nd // tn),
        in_specs=[
            pl.BlockSpec((SUBLANES, d), lambda l, j: (0, 0)),
            pl.BlockSpec((None, d, tn), lambda l, j: (l, 0, j)),
            pl.BlockSpec((None, 1, tn), lambda l, j: (l, 0, j)),
        ],
        out_specs=pl.BlockSpec((None, SUBLANES, tn), lambda l, j: (l, 0, j)),
        compiler_params=_cparams("parallel", "arbitrary"),
        name="modulation",
    )(c_rows, w_mod, b_mod.reshape(depth, 1, nd))


class _Rows:
    def __init__(self, batch, seq, ctx_len, tile):
        assert seq % tile == 0 and (batch * ctx_len) % tile == 0
        self.batch, self.seq, self.ctx_len, self.tile = batch, seq, ctx_len, tile
        self.n_lat = batch * seq
        self.n = self.n_lat + batch * ctx_len
        self.lat_tiles = self.n_lat // tile
        self.tiles = self.n // tile
        self.tiles_per_batch = seq // tile

    def group(self, i):
        return jnp.where(i < self.lat_tiles, 1 + i // self.tiles_per_batch, 0)


def _ffn_kernel(*refs, final, split_tiles, group):
    if split_tiles is None:
        h_ref, mod_ref, gain_ref, wa_ref, wb_ref, wo_ref, fin_ref, o_ref, xn_ref, acc_ref = refs
        load_h = lambda: h_ref[...]
    else:
        hl_ref, hc_ref, mod_ref, gain_ref, wa_ref, wb_ref, wo_ref, fin_ref, o_ref, xn_ref, acc_ref = refs
        tile = pl.program_id(0) * group + pl.program_id(2)
        load_h = lambda: jnp.where(tile < split_tiles, hl_ref[...], hc_ref[...])
    j = pl.program_id(1)
    r = pl.program_id(2)

    @pl.when(j == 0)
    def _():
        xn_ref[r] = _ada(load_h(), gain_ref[...], mod_ref[0:1, :], mod_ref[1:2, :]).astype(BF16)

    @pl.when(jnp.logical_and(pl.program_id(0) == 0, j == 0))
    def _():
        acc_ref[r] = jnp.zeros(acc_ref.shape[1:], F32)

    xn = xn_ref[r]
    a = _dot(xn, wa_ref[...])
    b = _dot(xn, wb_ref[...])
    g = (_silu(a) * b).astype(BF16)
    acc_ref[r] = jnp.where(j == 0, 0.0, acc_ref[r]) + _dot(g, wo_ref[...])

    @pl.when(j == pl.num_programs(1) - 1)
    def _():
        out = load_h() + 0.5 * mod_ref[2:3, :] * acc_ref[r]
        if final:
            out = _rms(out, fin_ref[...])
        o_ref[...] = out


def _ffn(h, mods, sub, gain, w_in, w_out, layer, which, fin, rows, n_tiles, final):
    f, d = w_out.shape[2:]
    tm, tf = rows.tile, FFN_TILE
    nf = f // tf
    split = isinstance(h, tuple)
    gs = 1 if split else next(k for k in (FFN_ROW_GROUP, 2, 1) if n_tiles % k == 0)
    tile = lambda g, r: g * gs + r
    h_tile = lambda g, j, r: jnp.where(jnp.logical_or(j == 0, j == nf - 1), tile(g, r), tile(g, gs - 1))
    o_tile = lambda g, j, r: jnp.where(j == nf - 1, tile(g, r), tile(g, 0))
    if split:
        lt = rows.lat_tiles
        row_specs = [pl.BlockSpec((tm, d), lambda g, j, r: (jnp.minimum(h_tile(g, j, r), lt - 1), 0)),
                     pl.BlockSpec((tm, d), lambda g, j, r: (jnp.maximum(h_tile(g, j, r) - lt, 0), 0))]
        row_args, split_tiles = h, lt
    else:
        row_specs = [pl.BlockSpec((tm, d), lambda g, j, r: (h_tile(g, j, r), 0))]
        row_args, split_tiles = (h,), None
    return pl.pallas_call(
        functools.partial(_ffn_kernel, final=final, split_tiles=split_tiles, group=gs),
        out_shape=jax.ShapeDtypeStruct((n_tiles * tm, d), F32),
        grid=(n_tiles // gs, nf, gs),
        in_specs=row_specs + [
            pl.BlockSpec((None, None, 3, d), lambda g, j, r: (rows.group(tile(g, r)), sub, 0, 0)),
            pl.BlockSpec((1, d), lambda g, j, r: (0, 0)),
            pl.BlockSpec((None, None, d, tf), lambda g, j, r: (layer, which, 0, j)),
            pl.BlockSpec((None, None, d, tf), lambda g, j, r: (layer, which, 0, nf + j)),
            pl.BlockSpec((None, None, tf, d), lambda g, j, r: (layer, which, j, 0)),
            pl.BlockSpec((1, d), lambda g, j, r: (0, 0)),
        ],
        out_specs=pl.BlockSpec((tm, d), lambda g, j, r: (o_tile(g, j, r), 0)),
        scratch_shapes=[pltpu.VMEM((gs, tm, d), BF16), pltpu.VMEM((gs, tm, d), F32)],
        compiler_params=_cparams("arbitrary", "arbitrary", "arbitrary"),
        name="ffn",
    )(*row_args, mods, gain, w_in, w_in, w_out, fin)


def _rope_tables(seq, tile):
    t = np.arange(seq)
    inv_freq = ROPE_THETA ** (-np.arange(0, ROPE_AXIS_DIM, 2, dtype=np.float64) / ROPE_AXIS_DIM)
    ang = np.concatenate([(t // GRID_W)[:, None] * inv_freq, (t % GRID_W)[:, None] * inv_freq], axis=-1)
    nf = ROPE_AXIS_DIM // 2
    cos = np.cos(ang).reshape(seq, 2, 1, nf)
    sin = np.sin(ang).reshape(seq, 2, 1, nf)
    zero = np.zeros_like(sin)
    c_full = np.broadcast_to(cos, (seq, 2, 2, nf)).reshape(seq, HEAD_DIM)
    s_up = np.concatenate([-sin, zero], axis=2).reshape(seq, HEAD_DIM)
    s_dn = np.concatenate([zero, sin], axis=2).reshape(seq, HEAD_DIM)
    lat = np.concatenate([c_full, s_up, s_dn], axis=1)
    ident = np.concatenate([np.ones((tile, HEAD_DIM)), np.zeros((tile, 2 * HEAD_DIM))], axis=1)
    return jnp.asarray(np.concatenate([lat, ident], axis=0), dtype=F32)


def _norm_rope_heads(acc, gain, rope, post_scale=None):
    nf = ROPE_AXIS_DIM // 2
    c, s_up, s_dn = rope[:, :HEAD_DIM], rope[:, HEAD_DIM:2 * HEAD_DIM], rope[:, 2 * HEAD_DIM:]
    heads = []
    for hh in range(acc.shape[1] // HEAD_DIM):
        y = _rms(acc[:, hh * HEAD_DIM:(hh + 1) * HEAD_DIM], gain)
        y = y * c + pltpu.roll(y, HEAD_DIM - nf, 1) * s_up + pltpu.roll(y, nf, 1) * s_dn
        heads.append(y if post_scale is None else y * post_scale)
    return jnp.concatenate(heads, axis=1).astype(BF16)


def _ab_proj_kernel(h_ref, mod_ref, gain_ref, w_ref, qkn_ref, rope_ref, f_ref, q_ref, k_ref, vt_ref):
    xn = _ada(h_ref[...], gain_ref[...], mod_ref[0:1, :], mod_ref[1:2, :]).astype(BF16)
    c0 = f_ref.shape[1]
    c1 = c0 + q_ref.shape[1]
    c2 = c1 + k_ref.shape[1]
    q_ref[...] = _norm_rope_heads(_dot(xn, w_ref[:, c0:c1]), qkn_ref[0:1, :], rope_ref[...], ATTN_SCALE * LOG2_E)
    k_ref[...] = _norm_rope_heads(_dot(xn, w_ref[:, c1:c2]), qkn_ref[1:2, :], rope_ref[...])
    v = _dot(xn, w_ref[:, c2:])
    ones = jnp.ones((vt_ref.shape[1] - HEAD_DIM, vt_ref.shape[2]), BF16)
    for hh in range(N_KV_HEADS):
        vt_ref[hh, :HEAD_DIM, :] = v[:, hh * HEAD_DIM:(hh + 1) * HEAD_DIM].T.astype(BF16)
        vt_ref[hh, HEAD_DIM:, :] = ones
    f_ref[...] = _dot(xn, w_ref[:, :c0])


def _ab_proj(h, mods, gain, w_in, qk_norm, rope, rows):
    d = h.shape[1]
    tm = rows.tile
    kv_width = N_KV_HEADS * HEAD_DIM
    q_width = w_in.shape[1] - FOURIER_WIDTH - 2 * kv_width
    n = rows.n
    rope_blk = lambda i: (jnp.where(i < rows.lat_tiles, i % rows.tiles_per_batch, rows.tiles_per_batch), 0)
    return pl.pallas_call(
        _ab_proj_kernel,
        out_shape=(
            jax.ShapeDtypeStruct((n, FOURIER_WIDTH), F32),
            jax.ShapeDtypeStruct((n, q_width), BF16),
            jax.ShapeDtypeStruct((n, kv_width), BF16),
            jax.ShapeDtypeStruct((N_KV_HEADS, rows.tiles, VT_ROWS, tm), BF16),
        ),
        grid=(rows.tiles,),
        in_specs=[
            pl.BlockSpec((tm, d), lambda i: (i, 0)),
            pl.BlockSpec((None, None, 3, d), lambda i: (rows.group(i), 1, 0, 0)),
            pl.BlockSpec((1, d), lambda i: (0, 0)),
            pl.BlockSpec(w_in.shape, lambda i: (0, 0)),
            pl.BlockSpec((2, HEAD_DIM), lambda i: (0, 0)),
            pl.BlockSpec((tm, 3 * HEAD_DIM), rope_blk),
        ],
        out_specs=(
            pl.BlockSpec((tm, FOURIER_WIDTH), lambda i: (i, 0)),
            pl.BlockSpec((tm, q_width), lambda i: (i, 0)),
            pl.BlockSpec((tm, kv_width), lambda i: (i, 0)),
            pl.BlockSpec((N_KV_HEADS, None, VT_ROWS, tm), lambda i: (0, i, 0, 0)),
        ),
        compiler_params=_cparams("parallel"),
        name="ab_proj",
    )(h, mods, gain, w_in, qk_norm, rope)


def _attn_kernel(*refs, lat_chunks, group):
    if lat_chunks:
        q_ref, kc_ref, vtc_ref, kl_ref, vtl_ref, o_ref, acc_ref, s_ref = refs
    else:
        q_ref, kc_ref, vtc_ref, _, o_ref, acc_ref = refs
    tq = q_ref.shape[0]
    q = q_ref[...]
    qs = jnp.concatenate([q[:, g * HEAD_DIM:(g + 1) * HEAD_DIM] for g in range(group)], axis=0)
    nq = group * tq
    acc_ref[...] = jnp.zeros_like(acc_ref)

    def scores(k):
        return _dot_nt(k, qs)

    def update(s, vt, m):
        m_new = jnp.maximum(m, jnp.max(s, axis=0, keepdims=True))
        p = jnp.exp2(s - m_new).astype(BF16)
        acc_ref[...] = jnp.exp2(m - m_new) * acc_ref[...] + _dot(vt, p)
        return m_new

    m = update(scores(kc_ref[...]), vtc_ref[...], jnp.full((1, nq), -jnp.inf, F32))
    if lat_chunks:
        tk = vtl_ref.shape[2]

        def lat_scores(c):
            return scores(kl_ref[pl.ds(pl.multiple_of(c * tk, tk), tk), :])

        assert lat_chunks % 2 == 0
        s_ref[0] = lat_scores(0)

        def body(i, m):
            c = 2 * i
            s_ref[1] = lat_scores(c + 1)
            m = update(s_ref[0], vtl_ref[c], m)
            s_ref[0] = lat_scores(jnp.minimum(c + 2, lat_chunks - 1))
            return update(s_ref[1], vtl_ref[c + 1], m)

        m = lax.fori_loop(0, lat_chunks // 2, body, m)

    out = (acc_ref[:HEAD_DIM, :] / acc_ref[HEAD_DIM:HEAD_DIM + 1, :]).T
    o_ref[...] = jnp.concatenate([out[g * tq:(g + 1) * tq, :] for g in range(group)], axis=1).astype(BF16)


def _attention(q_all, k_all, vt_all, rows, lat_out=None):
    b, seq, lc, tile = rows.batch, rows.seq, rows.ctx_len, rows.tile
    group = q_all.shape[1] // (N_KV_HEADS * HEAD_DIM)
    gw = group * HEAD_DIM
    assert tile % lc == 0 and seq % tile == 0
    ctx_blk0 = rows.n_lat // lc
    ctx_tile = lambda bi: ((rows.n_lat + bi * lc) // tile, 0, ((rows.n_lat + bi * lc) % tile) // lc)
    kc_spec = pl.BlockSpec((lc, HEAD_DIM), lambda bi, hi, i: (ctx_blk0 + bi, hi))
    vtc_spec = pl.BlockSpec((None, None, VT_ROWS, lc), lambda bi, hi, i: (hi,) + ctx_tile(bi))
    if lat_out is None:
        tq = ATTN_Q_TILE
        assert tq % tile == 0
        nqt = seq // tq
        lat_chunks = seq // tile
        q_spec = pl.BlockSpec((tq, gw), lambda bi, hi, i: (bi * nqt + i, hi))
        kl_spec = pl.BlockSpec((seq, HEAD_DIM), lambda bi, hi, i: (bi, hi))
        vtl_spec = pl.BlockSpec((None, lat_chunks, VT_ROWS, tile), lambda bi, hi, i: (hi, bi, 0, 0))
        in_specs = [q_spec, kc_spec, vtc_spec, kl_spec, vtl_spec]
        args = (q_all, k_all, vt_all, k_all, vt_all)
        aliases = {}
    else:
        tq = lc
        nqt = 1
        lat_chunks = 0
        q_spec = pl.BlockSpec((tq, gw), lambda bi, hi, i: (ctx_blk0 + bi, hi))
        in_specs = [q_spec, kc_spec, vtc_spec, pl.BlockSpec(memory_space=pl.ANY)]
        args = (q_all, k_all, vt_all, lat_out)
        aliases = {3: 0}
    return pl.pallas_call(
        functools.partial(_attn_kernel, lat_chunks=lat_chunks, group=group),
        out_shape=jax.ShapeDtypeStruct((rows.n, q_all.shape[1]), BF16),
        grid=(b, N_KV_HEADS, nqt),
        in_specs=in_specs,
        out_specs=q_spec,
        scratch_shapes=[pltpu.VMEM((VT_ROWS, group * tq), F32)]
        + ([pltpu.VMEM((2, tile, group * tq), F32)] if lat_chunks else []),
        input_output_aliases=aliases,
        compiler_params=_cparams("parallel", "parallel", "arbitrary"),
        name="attention_ctx" if lat_chunks == 0 else "attention_lat",
    )(*args)


def _dft_cs(n):
    idx = np.arange(n)
    ang = 2.0 * np.pi * ((idx[:, None] * idx[None, :]) % n) / n
    return np.cos(ang), np.sin(ang)


def _const_split(m):
    m = np.asarray(m, dtype=np.float32)
    hi = jnp.asarray(m).astype(BF16)
    lo = (jnp.asarray(m) - hi.astype(F32)).astype(BF16)
    return jnp.stack([hi, lo])


def _fft1_kernel(x_ref, fa_ref, tw_ref, o_ref, *, a):
    w = o_ref.shape[2]
    for r in range(o_ref.shape[0]):
        z = _dot_split((fa_ref[0], fa_ref[1]), _split_bf16(x_ref[:, r * w:(r + 1) * w]))
        zr, zi = z[:a], z[a:]
        tc = jnp.concatenate([tw_ref[r, 0]] * (w // LANES), axis=1)
        ts = jnp.concatenate([tw_ref[r, 1]] * (w // LANES), axis=1)
        o_ref[r, :a, :] = zr * tc - zi * ts
        o_ref[r, a:, :] = zr * ts + zi * tc


def _fft2_kernel(zr_ref, zi_ref, m2_ref, mc_ref, o_ref, *, scale):
    for r in range(o_ref.shape[1]):
        z = jnp.concatenate([zr_ref[:, r, :], zi_ref[:, r, :]], axis=0)
        v = _dot_split((m2_ref[0], m2_ref[1]), _split_bf16(z))
        vr, vi = v[:FFT_B], v[FFT_B:]
        outs = []
        for g in range(o_ref.shape[2] // LANES):
            u = jnp.concatenate([vr[:, g * LANES:(g + 1) * LANES], vi[:, g * LANES:(g + 1) * LANES]], axis=1)
            outs.append(_dot_split(_split_bf16(u), (mc_ref[0], mc_ref[1])))
        o_ref[:, r, :] = jnp.concatenate(outs, axis=1) * scale


def _fourier_latent(f_all, rows):
    b, seq = rows.batch, rows.seq
    w = FOURIER_WIDTH
    a = seq // FFT_B
    assert a % SUBLANES == 0 and FOURIER_GROUP_DIM == LANES
    ca, sa = _dft_cs(a)
    fa = _const_split(np.concatenate([ca, sa], axis=0))
    p1b = (np.arange(a)[None, :] * np.arange(FFT_B)[:, None]) % seq
    ang = 2.0 * np.pi * p1b / seq
    tw = np.stack([np.cos(ang), np.sin(ang)], axis=1)[..., None]
    tw = jnp.asarray(np.broadcast_to(tw, (FFT_B, 2, a, LANES)), dtype=F32)
    cb, sb = _dft_cs(FFT_B)
    m2 = _const_split(np.block([[cb, -sb], [sb, cb]]))
    cc, sc = _dft_cs(FOURIER_GROUP_DIM)
    mc = _const_split(np.concatenate([cc, -sc], axis=0))

    blk = FFT_B_BLOCK
    x2 = f_all.reshape(rows.n // FFT_B, FFT_B * w)
    z = pl.pallas_call(
        functools.partial(_fft1_kernel, a=a),
        out_shape=jax.ShapeDtypeStruct((b, FFT_B, 2 * a, w), F32),
        grid=(b, FFT_B // blk),
        in_specs=[
            pl.BlockSpec((a, blk * w), lambda bi, j: (bi, j)),
            pl.BlockSpec((2, 2 * a, a), lambda bi, j: (0, 0, 0)),
            pl.BlockSpec((blk, 2, a, LANES), lambda bi, j: (j, 0, 0, 0)),
        ],
        out_specs=pl.BlockSpec((None, blk, 2 * a, w), lambda bi, j: (bi, j, 0, 0)),
        compiler_params=_cparams("parallel", "parallel"),
        name="fourier_stage1",
    )(x2, fa, tw)

    nblk = a // blk
    y = pl.pallas_call(
        functools.partial(_fft2_kernel, scale=1.0 / math.sqrt(seq * FOURIER_GROUP_DIM)),
        out_shape=jax.ShapeDtypeStruct((b, FFT_B, a, w), F32),
        grid=(b, nblk),
        in_specs=[
            pl.BlockSpec((None, FFT_B, blk, w), lambda bi, j: (bi, 0, j, 0)),
            pl.BlockSpec((None, FFT_B, blk, w), lambda bi, j: (bi, 0, nblk + j, 0)),
            pl.BlockSpec((2, 2 * FFT_B, 2 * FFT_B), lambda bi, j: (0, 0, 0)),
            pl.BlockSpec((2, 2 * LANES, LANES), lambda bi, j: (0, 0, 0)),
        ],
        out_specs=pl.BlockSpec((None, FFT_B, blk, w), lambda bi, j: (bi, 0, j, 0)),
        compiler_params=_cparams("parallel", "parallel"),
        name="fourier_stage2",
    )(z, z, m2, mc)
    return y.reshape(b * seq, w)


def _dft_ctx_kernel(x_ref, cn_ref, sn_ref, cc_ref, sc_ref, o_ref, *, scale):
    x = x_ref[...]
    outs = []
    for g in range(x.shape[1] // LANES):
        xg = x[:, g * LANES:(g + 1) * LANES]
        outs.append(_dot_hi(cn_ref[...], _dot_hi(xg, cc_ref[...])) - _dot_hi(sn_ref[...], _dot_hi(xg, sc_ref[...])))
    o_ref[...] = jnp.concatenate(outs, axis=1) * scale


def _fourier_ctx(f_all, rows):
    b, lc = rows.batch, rows.ctx_len
    w = FOURIER_WIDTH
    cn, sn = _dft_cs(lc)
    cc, sc = _dft_cs(FOURIER_GROUP_DIM)
    blk0 = rows.n_lat // lc
    mat = lambda m: pl.BlockSpec(m.shape, lambda bi: (0, 0))
    consts = [jnp.asarray(m, dtype=F32) for m in (cn, sn, cc, sc)]
    return pl.pallas_call(
        functools.partial(_dft_ctx_kernel, scale=1.0 / math.sqrt(lc * FOURIER_GROUP_DIM)),
        out_shape=jax.ShapeDtypeStruct((b * lc, w), F32),
        grid=(b,),
        in_specs=[pl.BlockSpec((lc, w), lambda bi: (blk0 + bi, 0))] + [mat(m) for m in consts],
        out_specs=pl.BlockSpec((lc, w), lambda bi: (bi, 0)),
        compiler_params=_cparams("parallel"),
        name="fourier_ctx",
    )(f_all, *consts)


def _ab_out_kernel(x1l_ref, x1c_ref, x2_ref, w_ref, h_ref, mod_ref, o_ref, *, lat_tiles):
    x1 = jnp.where(pl.program_id(0) < lat_tiles, x1l_ref[...], x1c_ref[...])
    x = jnp.concatenate([x1.astype(BF16), x2_ref[...]], axis=1)
    o_ref[...] = h_ref[...] + mod_ref[2:3, :] * _dot(x, w_ref[...])


def _ab_out(x1_lat, x1_ctx, x2, w_out, h, mods, rows):
    d = h.shape[1]
    tm = rows.tile
    lt = rows.lat_tiles
    return pl.pallas_call(
        functools.partial(_ab_out_kernel, lat_tiles=lt),
        out_shape=jax.ShapeDtypeStruct((rows.n, d), F32),
        grid=(rows.tiles,),
        in_specs=[
            pl.BlockSpec((tm, x1_lat.shape[1]), lambda i: (jnp.minimum(i, lt - 1), 0)),
            pl.BlockSpec((tm, x1_ctx.shape[1]), lambda i: (jnp.maximum(i - lt, 0), 0)),
            pl.BlockSpec((tm, x2.shape[1]), lambda i: (i, 0)),
            pl.BlockSpec(w_out.shape, lambda i: (0, 0)),
            pl.BlockSpec((tm, d), lambda i: (i, 0)),
            pl.BlockSpec((None, None, 3, d), lambda i: (rows.group(i), 1, 0, 0)),
        ],
        out_specs=pl.BlockSpec((tm, d), lambda i: (i, 0)),
        compiler_params=_cparams("parallel"),
        name="ab_out",
    )(x1_lat, x1_ctx, x2, w_out, h, mods)


def _hgrn_out_kernel(osum_ref, g_ref, gain_ref, w_ref, h_ref, mod_ref, o_ref):
    o = osum_ref[...]
    parts = []
    for hh in range(o.shape[1] // HEAD_DIM):
        oh = o[:, hh * HEAD_DIM:(hh + 1) * HEAD_DIM]
        parts.append(oh * lax.rsqrt(jnp.mean(oh * oh, axis=-1, keepdims=True) + EPS))
    on = jnp.concatenate(parts, axis=1) * gain_ref[...]
    y = (on * jax.nn.sigmoid(g_ref[...])).astype(BF16)
    o_ref[...] = h_ref[...] + mod_ref[2:3, :] * _dot(y, w_ref[...])


def _hgrn_out(o_sum, p, gain, w_out, h, mods, rows_r, n_tiles):
    d = h.shape[1]
    tm = rows_r.tile
    g_blk = (p.shape[1] - d) // d
    return pl.pallas_call(
        _hgrn_out_kernel,
        out_shape=jax.ShapeDtypeStruct((n_tiles * tm, d), F32),
        grid=(n_tiles,),
        in_specs=[
            pl.BlockSpec((tm, d), lambda i: (i, 0)),
            pl.BlockSpec((tm, d), lambda i: (i, g_blk)),
            pl.BlockSpec((1, d), lambda i: (0, 0)),
            pl.BlockSpec(w_out.shape, lambda i: (0, 0)),
            pl.BlockSpec((tm, d), lambda i: (i, 0)),
            pl.BlockSpec((None, None, 3, d), lambda i: (rows_r.group(i), 1, 0, 0)),
        ],
        out_specs=pl.BlockSpec((tm, d), lambda i: (i, 0)),
        compiler_params=_cparams("parallel"),
        name="hgrn_out",
    )(o_sum, p, gain, w_out, h, mods)


def _hgrn_proj_kernel(h_ref, mod_ref, gain_ref, w_ref, o_ref, xn_ref, *, n_silu):
    j = pl.program_id(1)
    r = pl.program_id(2)

    @pl.when(j == 0)
    def _():
        xn_ref[r] = _ada(h_ref[...], gain_ref[...], mod_ref[0:1, :], mod_ref[1:2, :]).astype(BF16)

    acc = _dot(xn_ref[r], w_ref[...])
    o_ref[...] = jnp.where(j < n_silu, _silu(acc), acc)


def _hgrn_proj(h, mods, gain, w_in, rows):
    d = h.shape[1]
    tm, tn = rows.tile, PROJ_COL_TILE
    nw = w_in.shape[1]
    gs = next(k for k in (PROJ_ROW_GROUP, 2, 1) if rows.tiles % k == 0)
    tile = lambda g, r: g * gs + r
    return pl.pallas_call(
        functools.partial(_hgrn_proj_kernel, n_silu=d // tn),
        out_shape=jax.ShapeDtypeStruct((rows.n, nw), F32),
        grid=(rows.tiles // gs, nw // tn, gs),
        in_specs=[
            pl.BlockSpec((tm, d), lambda g, j, r: (jnp.where(j == 0, tile(g, r), tile(g, gs - 1)), 0)),
            pl.BlockSpec((None, None, 3, d), lambda g, j, r: (rows.group(tile(g, r)), 1, 0, 0)),
            pl.BlockSpec((1, d), lambda g, j, r: (0, 0)),
            pl.BlockSpec((d, tn), lambda g, j, r: (0, j)),
        ],
        out_specs=pl.BlockSpec((tm, tn), lambda g, j, r: (tile(g, r), j)),
        scratch_shapes=[pltpu.VMEM((gs, tm, d), BF16)],
        compiler_params=_cparams("arbitrary", "arbitrary", "arbitrary"),
        name="hgrn_proj",
    )(h, mods, gain, w_in)


def _scan_levels(chunk):
    return [chunk >> (i + 1) for i in range(int(math.log2(chunk)))]


def _scan_consts(chunk, reverse):
    t = np.arange(chunk)[:, None]
    s = np.arange(chunk)[None, :]
    tri = (s >= t) if reverse else (s <= t)
    wide, narrow = [], []
    for h in _scan_levels(chunk):
        same = (t // (2 * h)) == (s // (2 * h))
        t_up = (t // h) % 2 == 1
        s_up = (s // h) % 2 == 1
        mask = same & ((~t_up & s_up) if reverse else (t_up & ~s_up))
        if h >= SUBLANES:
            wide.append(mask[np.nonzero(~t_up[:, 0] if reverse else t_up[:, 0])[0]])
        else:
            narrow.append(mask)
    narrow.append(t == s)
    return (jnp.asarray(tri, dtype=BF16), jnp.asarray(np.stack(wide), dtype=F32),
            jnp.asarray(np.stack(narrow), dtype=F32))


def _seg_bcast(x, h, reverse):
    c, w = x.shape
    off = h if reverse else h - 1
    if 2 * h >= 2 * SUBLANES:
        pieces = [jnp.broadcast_to(x[g * 2 * h + off:g * 2 * h + off + 1, :], (2 * h, w)) for g in range(c // (2 * h))]
        return pieces[0] if len(pieces) == 1 else jnp.concatenate(pieces, axis=0)
    x3 = x.reshape(c // SUBLANES, SUBLANES, w)
    sub = lax.broadcasted_iota(jnp.int32, x3.shape, 1)
    y = None
    for g in range(SUBLANES // (2 * h)):
        piece = jnp.broadcast_to(x3[:, g * 2 * h + off:g * 2 * h + off + 1, :], x3.shape)
        y = piece if y is None else jnp.where(sub >= g * 2 * h, piece, y)
    return y.reshape(c, w)


def _scan_kernel(*refs, reverse, chunk, accumulate):
    if accumulate:
        q_ref, f_ref, v_ref, lb_ref, tri_ref, wmask_ref, nmask_ref, other_ref, o_ref, st_ref = refs
    else:
        q_ref, f_ref, v_ref, lb_ref, tri_ref, wmask_ref, nmask_ref, o_ref, st_ref = refs

    @pl.when(pl.program_id(2) == 0)
    def _():
        st_ref[...] = jnp.zeros_like(st_ref)

    rows, width = q_ref.shape
    levels = _scan_levels(chunk)
    spans = [slice(ci * chunk, (ci + 1) * chunk) for ci in range(rows // chunk)]
    lanes = [slice(hh * HEAD_DIM, (hh + 1) * HEAD_DIM) for hh in range(width // HEAD_DIM)]
    q = q_ref[...]
    fl = f_ref[...]
    lb = lb_ref[...]
    u = jnp.exp(-jnp.abs(fl))
    key = (1.0 - lb) * (jnp.where(fl > 0.0, u, 1.0) / (1.0 + u))
    log_sig = jnp.minimum(fl, 0.0) - jnp.log(1.0 + u)
    x1 = jnp.log(lb)
    x2 = jnp.log1p(-lb) + log_sig
    delta = x1 - x2
    lf = jnp.where(jnp.isnan(delta), x1 + x2, jnp.maximum(x1, x2) + jnp.log(1.0 + jnp.exp(-jnp.abs(delta))))
    lf = lf * LOG2_E
    l1 = lf.astype(BF16)
    r1 = lf - l1.astype(F32)
    l2 = r1.astype(BF16)
    l3 = (r1 - l2.astype(F32)).astype(BF16)
    parts = jnp.concatenate([l1, l2, l3], axis=1)
    tri = tri_ref[...]
    cs = jnp.concatenate([_dot(tri, parts[sp]) for sp in spans], axis=0)
    b = cs[:, :width] + cs[:, width:2 * width] + cs[:, 2 * width:]
    totals = [b[sp.start:sp.start + 1, :] if reverse else b[sp.stop - 1:sp.stop, :] for sp in spans]
    total_rows = jnp.concatenate([jnp.broadcast_to(t, (chunk, width)) for t in totals], axis=0)

    q_in = (q * jnp.exp2(b)).astype(BF16)
    k_out = (key * jnp.exp2(total_rows - b)).astype(BF16)
    v = v_ref[...].astype(BF16)

    order = list(range(len(spans)))[::-1] if reverse else list(range(len(spans)))
    kv = [[_dot_tn(v[sp, sl], k_out[sp, sl]) for sl in lanes] for sp in spans]
    o_state = [[None] * len(lanes) for _ in spans]
    for hh, sl in enumerate(lanes):
        st = st_ref[hh]
        for ci in order:
            o_state[ci][hh] = _dot_nt(q_in[spans[ci], sl], st.astype(BF16))
            st = st * jnp.exp2(totals[ci][:, sl]) + kv[ci][hh]
        st_ref[hh] = st

    n_blk = chunk // SUBLANES
    wide_levels = [h for h in levels if h >= SUBLANES]
    narrow_levels = [h for h in levels if h < SUBLANES]

    def narrow_scores(li, ql, kl):
        out = []
        for sp in spans:
            row = []
            for sl in lanes:
                s = nmask_ref[li] * _dot_nt(ql[sp, sl], kl[sp, sl])
                row.append([s[k * SUBLANES:(k + 1) * SUBLANES, :] for k in range(n_blk)])
            out.append(row)
        return out

    a = narrow_scores(len(narrow_levels), q.astype(BF16), key.astype(BF16))
    row_id = lax.broadcasted_iota(jnp.int32, (rows, width), 0)
    for li, h in enumerate(narrow_levels):
        if h == 1:
            e = jnp.where((row_id & 1) == (0 if reverse else 1), jnp.exp2(lf), 1.0)
        else:
            e = jnp.exp2(-jnp.abs(b - _seg_bcast(b, h, reverse)))
        new = narrow_scores(li, (q * e).astype(BF16), (key * e).astype(BF16))
        a = [[[x + y for x, y in zip(xb, yb)] for xb, yb in zip(xa, ya)] for xa, ya in zip(a, new)]

    for li, h in enumerate(wide_levels):
        q_parts, k_parts = [], []
        for g in range(rows // (2 * h)):
            lo = slice(g * 2 * h, g * 2 * h + h)
            hi = slice(g * 2 * h + h, (g + 1) * 2 * h)
            r = g * 2 * h + (h if reverse else h - 1)
            b_ref = jnp.broadcast_to(b[r:r + 1, :], (h, width))
            q_half, k_half = (lo, hi) if reverse else (hi, lo)
            q_parts.append(q[q_half] * jnp.exp2(b[q_half] - b_ref))
            k_part = key[k_half] * jnp.exp2(b_ref - b[k_half])
            zeros = jnp.zeros((h, width), F32)
            k_parts += [zeros, k_part] if reverse else [k_part, zeros]
        q_sel = jnp.concatenate(q_parts, axis=0).astype(BF16)
        k_hat = jnp.concatenate(k_parts, axis=0).astype(BF16)
        half = chunk // 2
        for ci, sp in enumerate(spans):
            for hh, sl in enumerate(lanes):
                s = wmask_ref[li] * _dot_nt(q_sel[ci * half:(ci + 1) * half, sl], k_hat[sp, sl])
                for j in range(chunk // (2 * h)):
                    first = (j * 2 * h + (0 if reverse else h)) // SUBLANES
                    for k in range(h // SUBLANES):
                        r0 = j * h + k * SUBLANES
                        a[ci][hh][first + k] = a[ci][hh][first + k] + s[r0:r0 + SUBLANES, :]

    for ci, sp in enumerate(spans):
        for hh, sl in enumerate(lanes):
            pairs = jnp.concatenate(a[ci][hh], axis=0).astype(BF16)
            o = o_state[ci][hh] + _dot(pairs, v[sp, sl])
            o_ref[sp, sl] = o + other_ref[sp, sl] if accumulate else o


def _hgrn_scan(p, lower_bound, rows, reverse, other=None):
    b, seq, lc = rows.batch, rows.seq, rows.ctx_len
    d = lower_bound.shape[0]
    c, hb = SCAN_CHUNK, SCAN_HEADS_PER_BLOCK
    r = c * SCAN_CHUNKS_PER_STEP
    wb = hb * HEAD_DIM
    ncb = d // wb
    assert lc % r == 0 and seq % r == 0
    nctx, nlat = lc // r, seq // r
    ctx0 = rows.n_lat // r
    f_blk = (2 if reverse else 1) * ncb
    v_blk = 3 * ncb

    def row(bi, s):
        if reverse:
            return jnp.where(s < nctx, ctx0 + bi * nctx + (nctx - 1 - s), bi * nlat + (nlat - 1 - (s - nctx)))
        return jnp.where(s < nctx, ctx0 + bi * nctx + s, bi * nlat + (s - nctx))

    tri, wmasks, nmasks = _scan_consts(c, reverse)
    o_spec = pl.BlockSpec((r, wb), lambda bi, hi, s: (row(bi, s), hi))
    extra = () if other is None else (other,)
    return pl.pallas_call(
        functools.partial(_scan_kernel, reverse=reverse, chunk=c, accumulate=other is not None),
        out_shape=jax.ShapeDtypeStruct((rows.n, d), F32),
        grid=(b, ncb, nctx + nlat),
        in_specs=[
            pl.BlockSpec((r, wb), lambda bi, hi, s: (row(bi, s), hi)),
            pl.BlockSpec((r, wb), lambda bi, hi, s: (row(bi, s), f_blk + hi)),
            pl.BlockSpec((r, wb), lambda bi, hi, s: (row(bi, s), v_blk + hi)),
            pl.BlockSpec((1, wb), lambda bi, hi, s: (0, hi)),
            pl.BlockSpec(tri.shape, lambda bi, hi, s: (0, 0)),
            pl.BlockSpec(wmasks.shape, lambda bi, hi, s: (0, 0, 0)),
            pl.BlockSpec(nmasks.shape, lambda bi, hi, s: (0, 0, 0)),
        ] + [o_spec] * len(extra),
        out_specs=o_spec,
        scratch_shapes=[pltpu.VMEM((hb, HEAD_DIM, HEAD_DIM), F32)],
        compiler_params=_cparams("parallel", "parallel", "arbitrary"),
        name="hgrn_scan_bw" if reverse else "hgrn_scan_fw",
    )(p, p, p, lower_bound.reshape(1, d), tri, wmasks, nmasks, *extra)


def kernel(x, c, ctx, c_ctx, w_mod, b_mod, norm_gains, ffn_w_in, ffn_w_out, ab_w_in, qk_norm, ab_w_out,
           hgrn_w_in, hgrn_lb_logits, hgrn_o_norm, hgrn_w_out, final_norm):
    batch, seq, d = x.shape
    lc = ctx.shape[1]
    depth = w_mod.shape[0]
    rows = _Rows(batch, seq, lc, ROW_TILE)
    rows_r = _Rows(batch, seq, lc, READOUT_ROW_TILE)
    assert seq % GRID_W == 0 and batch + 1 <= SUBLANES

    c_rows = jnp.concatenate([c_ctx[None, :], c, jnp.zeros((SUBLANES - 1 - batch, d), F32)], axis=0)
    mods_all = _modulation(c_rows, w_mod, b_mod).reshape(depth, SUBLANES, 3, 3, d)

    lb_cum = jnp.cumsum(jax.nn.softmax(hgrn_lb_logits.astype(F32), axis=0), axis=0)
    lower_bounds = lb_cum - lb_cum[0]

    ffn_w_in_b = ffn_w_in.astype(BF16)
    ffn_w_out_b = ffn_w_out.astype(BF16)
    fin = final_norm.reshape(1, d)
    rope = _rope_tables(seq, ROW_TILE)

    h = (x.reshape(batch * seq, d), ctx.reshape(batch * lc, d))
    for layer in range(depth):
        last = layer == depth - 1
        mods = mods_all[layer]
        gains = norm_gains[layer].reshape(3, 1, d)
        h = _ffn(h, mods, 0, gains[0], ffn_w_in_b, ffn_w_out_b, layer, 0, fin, rows, rows.tiles, False)
        if layer % 2 == 0:
            e = layer // 2
            f_all, q_all, k_all, vt_all = _ab_proj(h, mods, gains[1], ab_w_in[e].astype(BF16), qk_norm[e], rope, rows)
            attn = _attention(q_all, k_all, vt_all, rows)
            attn = _attention(q_all, k_all, vt_all, rows, lat_out=attn)
            h = _ab_out(_fourier_latent(f_all, rows), _fourier_ctx(f_all, rows), attn,
                        ab_w_out[e].astype(BF16), h, mods, rows)
        else:
            o = layer // 2
            p = _hgrn_proj(h, mods, gains[1], hgrn_w_in[o].astype(BF16), rows)
            o_fw = _hgrn_scan(p, lower_bounds[layer], rows, False)
            o_sum = _hgrn_scan(p, lower_bounds[layer], rows, True, other=o_fw)
            gain_o = jnp.tile(hgrn_o_norm[o], d // HEAD_DIM).reshape(1, d)
            n_t = rows_r.lat_tiles if last else rows_r.tiles
            h = _hgrn_out(o_sum, p, gain_o, hgrn_w_out[o].astype(BF16), h, mods, rows_r, n_t)
        n_t = rows.lat_tiles if last else rows.tiles
        h = _ffn(h, mods, 2, gains[2], ffn_w_in_b, ffn_w_out_b, layer, 1, fin, rows, n_t, last)
    return h[:batch * seq].reshape(batch, seq, d)
```

```python
import functools
import math

import jax
import jax.numpy as jnp
import numpy as np
from jax import lax
from jax.experimental import pallas as pl
from jax.experimental.pallas import tpu as pltpu

F32 = jnp.float32
BF16 = jnp.bfloat16

EPS = 1e-6
N_MOD = 9
HEAD_DIM = 128
N_KV_HEADS = 4
FOURIER_WIDTH = 512
FOURIER_GROUP_DIM = 128
GRID_W = 64
ROPE_THETA = 10000.0
ROPE_AXIS_DIM = HEAD_DIM // 2
ATTN_SCALE = HEAD_DIM ** -0.5
LOG2_E = math.log2(math.e)

LANES = 128
SUBLANES = 8
BF16_SUBLANES = 16
VT_ROWS = HEAD_DIM + BF16_SUBLANES
VMEM_LIMIT_BYTES = 56 * 1024 * 1024

ROW_TILE = 512
READOUT_ROW_TILE = 512
FFN_TILE = 512
PROJ_COL_TILE = 2048
PROJ_ROW_GROUP = 3
MOD_COL_TILE = 1024
ATTN_Q_TILE = 1024
SCAN_CHUNK = 128
SCAN_HEADS_PER_BLOCK = 8
SCAN_CHUNKS_PER_STEP = 2
FFT_B = 128
FFT_B_BLOCK = 8


def _cparams(*sem):
    return pltpu.CompilerParams(dimension_semantics=sem, vmem_limit_bytes=VMEM_LIMIT_BYTES)


def _dot(a, b):
    return jnp.dot(a, b, preferred_element_type=F32)


def _dot_hi(a, b):
    return jnp.dot(a, b, preferred_element_type=F32, precision=lax.Precision.HIGHEST)


def _split_bf16(x):
    hi = x.astype(BF16)
    return hi, (x - hi.astype(F32)).astype(BF16)


def _dot_split(a, b):
    (ah, al), (bh, bl) = a, b
    return _dot(ah, bh) + _dot(al, bh) + _dot(ah, bl)


def _dot_nt(a, b):
    return lax.dot_general(a, b, (((1,), (1,)), ((), ())), preferred_element_type=F32)


def _dot_tn(a, b):
    return lax.dot_general(a, b, (((0,), (0,)), ((), ())), preferred_element_type=F32)


def _silu(x):
    return x * jax.nn.sigmoid(x)


def _rms(x, gain):
    return x * lax.rsqrt(jnp.mean(x * x, axis=-1, keepdims=True) + EPS) * gain


def _ada(h, gain, shift, scale):
    return _rms(h, gain) * (1.0 + scale) + shift


def _mod_kernel(c_ref, w_ref, b_ref, o_ref):
    a = _silu(c_ref[...]).astype(BF16)
    o_ref[...] = _dot(a, w_ref[...].astype(BF16)) + b_ref[...]


def _modulation(c_rows, w_mod, b_mod):
    depth, d, nd = w_mod.shape
    tn = MOD_COL_TILE
    return pl.pallas_call(
        _mod_kernel,
        out_shape=jax.ShapeDtypeStruct((depth, SUBLANES, nd), F32),
        grid=(depth, nd // tn),
        in_specs=[
            pl.BlockSpec((SUBLANES, d), lambda l, j: (0, 0)),
            pl.BlockSpec((None, d, tn), lambda l, j: (l, 0, j)),
            pl.BlockSpec((None, 1, tn), lambda l, j: (l, 0, j)),
        ],
        out_specs=pl.BlockSpec((None, SUBLANES, tn), lambda l, j: (l, 0, j)),
        compiler_params=_cparams("parallel", "arbitrary"),
        name="modulation",
    )(c_rows, w_mod, b_mod.reshape(depth, 1, nd))


class _Rows:
    def __init__(self, batch, seq, ctx_len, tile):
        assert seq % tile == 0 and (batch * ctx_len) % tile == 0
        self.batch, self.seq, self.ctx_len, self.tile = batch, seq, ctx_len, tile
        self.n_lat = batch * seq
        self.n = self.n_lat + batch * ctx_len
        self.lat_tiles = self.n_lat // tile
        self.tiles = self.n // tile
        self.tiles_per_batch = seq // tile

    def group(self, i):
        return jnp.where(i < self.lat_tiles, 1 + i // self.tiles_per_batch, 0)


def _ffn_kernel(*refs, final, split_tiles):
    if split_tiles is None:
        h_ref, mod_ref, gain_ref, wa_ref, wb_ref, wo_ref, fin_ref, o_ref, xn_ref, acc_ref = refs
        load_h = lambda: h_ref[...]
    else:
        hl_ref, hc_ref, mod_ref, gain_ref, wa_ref, wb_ref, wo_ref, fin_ref, o_ref, xn_ref, acc_ref = refs
        load_h = lambda: jnp.where(pl.program_id(0) < split_tiles, hl_ref[...], hc_ref[...])
    i = pl.program_id(0)
    j = pl.program_id(1)

    @pl.when(j == 0)
    def _():
        xn_ref[...] = _ada(load_h(), gain_ref[...], mod_ref[0:1, :], mod_ref[1:2, :]).astype(BF16)

    @pl.when(jnp.logical_and(i == 0, j == 0))
    def _():
        acc_ref[...] = jnp.zeros_like(acc_ref)

    xn = xn_ref[...]
    a = _dot(xn, wa_ref[...])
    b = _dot(xn, wb_ref[...])
    g = (_silu(a) * b).astype(BF16)
    acc_ref[...] = jnp.where(j == 0, 0.0, acc_ref[...]) + _dot(g, wo_ref[...].astype(BF16))

    @pl.when(j == pl.num_programs(1) - 1)
    def _():
        out = load_h() + 0.5 * mod_ref[2:3, :] * acc_ref[...]
        if final:
            out = _rms(out, fin_ref[...])
        o_ref[...] = out


def _ffn(h, mods, sub, gain, w_in, w_out, layer, which, fin, rows, n_tiles, final):
    f, d = w_out.shape[2:]
    tm, tf = rows.tile, FFN_TILE
    nf = f // tf
    if isinstance(h, tuple):
        lt = rows.lat_tiles
        row_specs = [pl.BlockSpec((tm, d), lambda i, j: (jnp.minimum(i, lt - 1), 0)),
                     pl.BlockSpec((tm, d), lambda i, j: (jnp.maximum(i - lt, 0), 0))]
        row_args, split_tiles = h, lt
    else:
        row_specs = [pl.BlockSpec((tm, d), lambda i, j: (i, 0))]
        row_args, split_tiles = (h,), None
    return pl.pallas_call(
        functools.partial(_ffn_kernel, final=final, split_tiles=split_tiles),
        out_shape=jax.ShapeDtypeStruct((n_tiles * tm, d), F32),
        grid=(n_tiles, nf),
        in_specs=row_specs + [
            pl.BlockSpec((None, None, 3, d), lambda i, j: (rows.group(i), sub, 0, 0)),
            pl.BlockSpec((1, d), lambda i, j: (0, 0)),
            pl.BlockSpec((None, None, d, tf), lambda i, j: (layer, which, 0, j)),
            pl.BlockSpec((None, None, d, tf), lambda i, j: (layer, which, 0, nf + j)),
            pl.BlockSpec((None, None, tf, d), lambda i, j: (layer, which, j, 0)),
            pl.BlockSpec((1, d), lambda i, j: (0, 0)),
        ],
        out_specs=pl.BlockSpec((tm, d), lambda i, j: (i, 0)),
        scratch_shapes=[pltpu.VMEM((tm, d), BF16), pltpu.VMEM((tm, d), F32)],
        compiler_params=_cparams("arbitrary", "arbitrary"),
        name="ffn",
    )(*row_args, mods, gain, w_in, w_in, w_out, fin)


def _rope_tables(seq, tile):
    t = np.arange(seq)
    inv_freq = ROPE_THETA ** (-np.arange(0, ROPE_AXIS_DIM, 2, dtype=np.float64) / ROPE_AXIS_DIM)
    ang = np.concatenate([(t // GRID_W)[:, None] * inv_freq, (t % GRID_W)[:, None] * inv_freq], axis=-1)
    nf = ROPE_AXIS_DIM // 2
    cos = np.cos(ang).reshape(seq, 2, 1, nf)
    sin = np.sin(ang).reshape(seq, 2, 1, nf)
    zero = np.zeros_like(sin)
    c_full = np.broadcast_to(cos, (seq, 2, 2, nf)).reshape(seq, HEAD_DIM)
    s_up = np.concatenate([-sin, zero], axis=2).reshape(seq, HEAD_DIM)
    s_dn = np.concatenate([zero, sin], axis=2).reshape(seq, HEAD_DIM)
    lat = np.concatenate([c_full, s_up, s_dn], axis=1)
    ident = np.concatenate([np.ones((tile, HEAD_DIM)), np.zeros((tile, 2 * HEAD_DIM))], axis=1)
    return jnp.asarray(np.concatenate([lat, ident], axis=0), dtype=F32)


def _norm_rope_heads(acc, gain, rope, post_scale=None):
    nf = ROPE_AXIS_DIM // 2
    c, s_up, s_dn = rope[:, :HEAD_DIM], rope[:, HEAD_DIM:2 * HEAD_DIM], rope[:, 2 * HEAD_DIM:]
    heads = []
    for hh in range(acc.shape[1] // HEAD_DIM):
        y = _rms(acc[:, hh * HEAD_DIM:(hh + 1) * HEAD_DIM], gain)
        y = y * c + pltpu.roll(y, HEAD_DIM - nf, 1) * s_up + pltpu.roll(y, nf, 1) * s_dn
        heads.append(y if post_scale is None else y * post_scale)
    return jnp.concatenate(heads, axis=1).astype(BF16)


def _ab_proj_kernel(h_ref, mod_ref, gain_ref, w_ref, qkn_ref, rope_ref, f_ref, q_ref, k_ref, vt_ref):
    xn = _ada(h_ref[...], gain_ref[...], mod_ref[0:1, :], mod_ref[1:2, :]).astype(BF16)
    c0 = f_ref.shape[1]
    c1 = c0 + q_ref.shape[1]
    c2 = c1 + k_ref.shape[1]
    q_ref[...] = _norm_rope_heads(_dot(xn, w_ref[:, c0:c1]), qkn_ref[0:1, :], rope_ref[...], ATTN_SCALE * LOG2_E)
    k_ref[...] = _norm_rope_heads(_dot(xn, w_ref[:, c1:c2]), qkn_ref[1:2, :], rope_ref[...])
    v = _dot(xn, w_ref[:, c2:])
    ones = jnp.ones((vt_ref.shape[1] - HEAD_DIM, vt_ref.shape[2]), BF16)
    for hh in range(N_KV_HEADS):
        vt_ref[hh, :HEAD_DIM, :] = v[:, hh * HEAD_DIM:(hh + 1) * HEAD_DIM].T.astype(BF16)
        vt_ref[hh, HEAD_DIM:, :] = ones
    f_ref[...] = _dot(xn, w_ref[:, :c0])


def _ab_proj(h, mods, gain, w_in, qk_norm, rope, rows):
    d = h.shape[1]
    tm = rows.tile
    kv_width = N_KV_HEADS * HEAD_DIM
    q_width = w_in.shape[1] - FOURIER_WIDTH - 2 * kv_width
    n = rows.n
    rope_blk = lambda i: (jnp.where(i < rows.lat_tiles, i % rows.tiles_per_batch, rows.tiles_per_batch), 0)
    return pl.pallas_call(
        _ab_proj_kernel,
        out_shape=(
            jax.ShapeDtypeStruct((n, FOURIER_WIDTH), F32),
            jax.ShapeDtypeStruct((n, q_width), BF16),
            jax.ShapeDtypeStruct((n, kv_width), BF16),
            jax.ShapeDtypeStruct((N_KV_HEADS, rows.tiles, VT_ROWS, tm), BF16),
        ),
        grid=(rows.tiles,),
        in_specs=[
            pl.BlockSpec((tm, d), lambda i: (i, 0)),
            pl.BlockSpec((None, None, 3, d), lambda i: (rows.group(i), 1, 0, 0)),
            pl.BlockSpec((1, d), lambda i: (0, 0)),
            pl.BlockSpec(w_in.shape, lambda i: (0, 0)),
            pl.BlockSpec((2, HEAD_DIM), lambda i: (0, 0)),
            pl.BlockSpec((tm, 3 * HEAD_DIM), rope_blk),
        ],
        out_specs=(
            pl.BlockSpec((tm, FOURIER_WIDTH), lambda i: (i, 0)),
            pl.BlockSpec((tm, q_width), lambda i: (i, 0)),
            pl.BlockSpec((tm, kv_width), lambda i: (i, 0)),
            pl.BlockSpec((N_KV_HEADS, None, VT_ROWS, tm), lambda i: (0, i, 0, 0)),
        ),
        compiler_params=_cparams("parallel"),
        name="ab_proj",
    )(h, mods, gain, w_in, qk_norm, rope)


def _attn_kernel(*refs, lat_chunks, group):
    if lat_chunks:
        q_ref, kc_ref, vtc_ref, kl_ref, vtl_ref, o_ref, acc_ref, s_ref = refs
    else:
        q_ref, kc_ref, vtc_ref, _, o_ref, acc_ref = refs
    tq = q_ref.shape[0]
    q = q_ref[...]
    qs = jnp.concatenate([q[:, g * HEAD_DIM:(g + 1) * HEAD_DIM] for g in range(group)], axis=0)
    nq = group * tq
    acc_ref[...] = jnp.zeros_like(acc_ref)

    def scores(k):
        return _dot_nt(k, qs)

    def update(s, vt, m):
        m_new = jnp.maximum(m, jnp.max(s, axis=0, keepdims=True))
        p = jnp.exp2(s - m_new).astype(BF16)
        acc_ref[...] = jnp.exp2(m - m_new) * acc_ref[...] + _dot(vt, p)
        return m_new

    s_ctx = scores(kc_ref[...])
    m0 = jnp.full((1, nq), -jnp.inf, F32)
    if not lat_chunks:
        m = update(s_ctx, vtc_ref[...], m0)
    else:
        tk = vtl_ref.shape[2]

        def lat_scores(c):
            return scores(kl_ref[pl.ds(pl.multiple_of(c * tk, tk), tk), :])

        assert lat_chunks % 2 == 0
        s_ref[0] = lat_scores(0)
        m = update(s_ctx, vtc_ref[...], m0)

        def body(i, m):
            c = 2 * i
            s_ref[1] = lat_scores(c + 1)
            m = update(s_ref[0], vtl_ref[c], m)
            s_ref[0] = lat_scores(jnp.minimum(c + 2, lat_chunks - 1))
            return update(s_ref[1], vtl_ref[c + 1], m)

        m = lax.fori_loop(0, lat_chunks // 2, body, m)

    out = (acc_ref[:HEAD_DIM, :] / acc_ref[HEAD_DIM:HEAD_DIM + 1, :]).T
    o_ref[...] = jnp.concatenate([out[g * tq:(g + 1) * tq, :] for g in range(group)], axis=1).astype(BF16)


def _attention(q_all, k_all, vt_all, rows, lat_out=None):
    b, seq, lc, tile = rows.batch, rows.seq, rows.ctx_len, rows.tile
    group = q_all.shape[1] // (N_KV_HEADS * HEAD_DIM)
    gw = group * HEAD_DIM
    assert tile % lc == 0 and seq % tile == 0
    ctx_blk0 = rows.n_lat // lc
    ctx_tile = lambda bi: ((rows.n_lat + bi * lc) // tile, 0, ((rows.n_lat + bi * lc) % tile) // lc)
    kc_spec = pl.BlockSpec((lc, HEAD_DIM), lambda bi, hi, i: (ctx_blk0 + bi, hi))
    vtc_spec = pl.BlockSpec((None, None, VT_ROWS, lc), lambda bi, hi, i: (hi,) + ctx_tile(bi))
    if lat_out is None:
        tq = ATTN_Q_TILE
        assert tq % tile == 0
        nqt = seq // tq
        lat_chunks = seq // tile
        q_spec = pl.BlockSpec((tq, gw), lambda bi, hi, i: (bi * nqt + i, hi))
        kl_spec = pl.BlockSpec((seq, HEAD_DIM), lambda bi, hi, i: (bi, hi))
        vtl_spec = pl.BlockSpec((None, lat_chunks, VT_ROWS, tile), lambda bi, hi, i: (hi, bi, 0, 0))
        in_specs = [q_spec, kc_spec, vtc_spec, kl_spec, vtl_spec]
        args = (q_all, k_all, vt_all, k_all, vt_all)
        aliases = {}
    else:
        tq = lc
        nqt = 1
        lat_chunks = 0
        q_spec = pl.BlockSpec((tq, gw), lambda bi, hi, i: (ctx_blk0 + bi, hi))
        in_specs = [q_spec, kc_spec, vtc_spec, pl.BlockSpec(memory_space=pl.ANY)]
        args = (q_all, k_all, vt_all, lat_out)
        aliases = {3: 0}
    return pl.pallas_call(
        functools.partial(_attn_kernel, lat_chunks=lat_chunks, group=group),
        out_shape=jax.ShapeDtypeStruct((rows.n, q_all.shape[1]), BF16),
        grid=(b, N_KV_HEADS, nqt),
        in_specs=in_specs,
        out_specs=q_spec,
        scratch_shapes=[pltpu.VMEM((VT_ROWS, group * tq), F32)]
        + ([pltpu.VMEM((2, tile, group * tq), F32)] if lat_chunks else []),
        input_output_aliases=aliases,
        compiler_params=_cparams("parallel", "parallel", "arbitrary"),
        name="attention_ctx" if lat_chunks == 0 else "attention_lat",
    )(*args)


def _dft_cs(n):
    idx = np.arange(n)
    ang = 2.0 * np.pi * ((idx[:, None] * idx[None, :]) % n) / n
    return np.cos(ang), np.sin(ang)


def _const_split(m):
    m = np.asarray(m, dtype=np.float32)
    hi = jnp.asarray(m).astype(BF16)
    lo = (jnp.asarray(m) - hi.astype(F32)).astype(BF16)
    return jnp.stack([hi, lo])


def _fft1_kernel(x_ref, fa_ref, tw_ref, o_ref, *, a):
    w = o_ref.shape[2]
    for r in range(o_ref.shape[0]):
        z = _dot_split((fa_ref[0], fa_ref[1]), _split_bf16(x_ref[:, r * w:(r + 1) * w]))
        zr, zi = z[:a], z[a:]
        tc = jnp.concatenate([tw_ref[r, 0]] * (w // LANES), axis=1)
        ts = jnp.concatenate([tw_ref[r, 1]] * (w // LANES), axis=1)
        o_ref[r, :a, :] = zr * tc - zi * ts
        o_ref[r, a:, :] = zr * ts + zi * tc


def _fft2_kernel(zr_ref, zi_ref, m2_ref, mc_ref, o_ref, *, scale):
    for r in range(o_ref.shape[1]):
        z = jnp.concatenate([zr_ref[:, r, :], zi_ref[:, r, :]], axis=0)
        v = _dot_split((m2_ref[0], m2_ref[1]), _split_bf16(z))
        vr, vi = v[:FFT_B], v[FFT_B:]
        outs = []
        for g in range(o_ref.shape[2] // LANES):
            u = jnp.concatenate([vr[:, g * LANES:(g + 1) * LANES], vi[:, g * LANES:(g + 1) * LANES]], axis=1)
            outs.append(_dot_split(_split_bf16(u), (mc_ref[0], mc_ref[1])))
        o_ref[:, r, :] = jnp.concatenate(outs, axis=1) * scale


def _fourier_latent(f_all, rows):
    b, seq = rows.batch, rows.seq
    w = FOURIER_WIDTH
    a = seq // FFT_B
    assert a % SUBLANES == 0 and FOURIER_GROUP_DIM == LANES
    ca, sa = _dft_cs(a)
    fa = _const_split(np.concatenate([ca, sa], axis=0))
    p1b = (np.arange(a)[None, :] * np.arange(FFT_B)[:, None]) % seq
    ang = 2.0 * np.pi * p1b / seq
    tw = np.stack([np.cos(ang), np.sin(ang)], axis=1)[..., None]
    tw = jnp.asarray(np.broadcast_to(tw, (FFT_B, 2, a, LANES)), dtype=F32)
    cb, sb = _dft_cs(FFT_B)
    m2 = _const_split(np.block([[cb, -sb], [sb, cb]]))
    cc, sc = _dft_cs(FOURIER_GROUP_DIM)
    mc = _const_split(np.concatenate([cc, -sc], axis=0))

    blk = FFT_B_BLOCK
    x2 = f_all.reshape(rows.n // FFT_B, FFT_B * w)
    z = pl.pallas_call(
        functools.partial(_fft1_kernel, a=a),
        out_shape=jax.ShapeDtypeStruct((b, FFT_B, 2 * a, w), F32),
        grid=(b, FFT_B // blk),
        in_specs=[
            pl.BlockSpec((a, blk * w), lambda bi, j: (bi, j)),
            pl.BlockSpec((2, 2 * a, a), lambda bi, j: (0, 0, 0)),
            pl.BlockSpec((blk, 2, a, LANES), lambda bi, j: (j, 0, 0, 0)),
        ],
        out_specs=pl.BlockSpec((None, blk, 2 * a, w), lambda bi, j: (bi, j, 0, 0)),
        compiler_params=_cparams("parallel", "parallel"),
        name="fourier_stage1",
    )(x2, fa, tw)

    nblk = a // blk
    y = pl.pallas_call(
        functools.partial(_fft2_kernel, scale=1.0 / math.sqrt(seq * FOURIER_GROUP_DIM)),
        out_shape=jax.ShapeDtypeStruct((b, FFT_B, a, w), F32),
        grid=(b, nblk),
        in_specs=[
            pl.BlockSpec((None, FFT_B, blk, w), lambda bi, j: (bi, 0, j, 0)),
            pl.BlockSpec((None, FFT_B, blk, w), lambda bi, j: (bi, 0, nblk + j, 0)),
            pl.BlockSpec((2, 2 * FFT_B, 2 * FFT_B), lambda bi, j: (0, 0, 0)),
            pl.BlockSpec((2, 2 * LANES, LANES), lambda bi, j: (0, 0, 0)),
        ],
        out_specs=pl.BlockSpec((None, FFT_B, blk, w), lambda bi, j: (bi, 0, j, 0)),
        compiler_params=_cparams("parallel", "parallel"),
        name="fourier_stage2",
    )(z, z, m2, mc)
    return y.reshape(b * seq, w)


def _dft_ctx_kernel(x_ref, cn_ref, sn_ref, cc_ref, sc_ref, o_ref, *, scale):
    x = x_ref[...]
    outs = []
    for g in range(x.shape[1] // LANES):
        xg = x[:, g * LANES:(g + 1) * LANES]
        outs.append(_dot_hi(cn_ref[...], _dot_hi(xg, cc_ref[...])) - _dot_hi(sn_ref[...], _dot_hi(xg, sc_ref[...])))
    o_ref[...] = jnp.concatenate(outs, axis=1) * scale


def _fourier_ctx(f_all, rows):
    b, lc = rows.batch, rows.ctx_len
    w = FOURIER_WIDTH
    cn, sn = _dft_cs(lc)
    cc, sc = _dft_cs(FOURIER_GROUP_DIM)
    blk0 = rows.n_lat // lc
    mat = lambda m: pl.BlockSpec(m.shape, lambda bi: (0, 0))
    consts = [jnp.asarray(m, dtype=F32) for m in (cn, sn, cc, sc)]
    return pl.pallas_call(
        functools.partial(_dft_ctx_kernel, scale=1.0 / math.sqrt(lc * FOURIER_GROUP_DIM)),
        out_shape=jax.ShapeDtypeStruct((b * lc, w), F32),
        grid=(b,),
        in_specs=[pl.BlockSpec((lc, w), lambda bi: (blk0 + bi, 0))] + [mat(m) for m in consts],
        out_specs=pl.BlockSpec((lc, w), lambda bi: (bi, 0)),
        compiler_params=_cparams("parallel"),
        name="fourier_ctx",
    )(f_all, *consts)


def _ab_out_kernel(x1l_ref, x1c_ref, x2_ref, w_ref, h_ref, mod_ref, o_ref, *, lat_tiles):
    x1 = jnp.where(pl.program_id(0) < lat_tiles, x1l_ref[...], x1c_ref[...])
    x = jnp.concatenate([x1.astype(BF16), x2_ref[...]], axis=1)
    o_ref[...] = h_ref[...] + mod_ref[2:3, :] * _dot(x, w_ref[...])


def _ab_out(x1_lat, x1_ctx, x2, w_out, h, mods, rows):
    d = h.shape[1]
    tm = rows.tile
    lt = rows.lat_tiles
    return pl.pallas_call(
        functools.partial(_ab_out_kernel, lat_tiles=lt),
        out_shape=jax.ShapeDtypeStruct((rows.n, d), F32),
        grid=(rows.tiles,),
        in_specs=[
            pl.BlockSpec((tm, x1_lat.shape[1]), lambda i: (jnp.minimum(i, lt - 1), 0)),
            pl.BlockSpec((tm, x1_ctx.shape[1]), lambda i: (jnp.maximum(i - lt, 0), 0)),
            pl.BlockSpec((tm, x2.shape[1]), lambda i: (i, 0)),
            pl.BlockSpec(w_out.shape, lambda i: (0, 0)),
            pl.BlockSpec((tm, d), lambda i: (i, 0)),
            pl.BlockSpec((None, None, 3, d), lambda i: (rows.group(i), 1, 0, 0)),
        ],
        out_specs=pl.BlockSpec((tm, d), lambda i: (i, 0)),
        compiler_params=_cparams("parallel"),
        name="ab_out",
    )(x1_lat, x1_ctx, x2, w_out, h, mods)


def _hgrn_out_kernel(osum_ref, g_ref, gain_ref, w_ref, h_ref, mod_ref, o_ref):
    o = osum_ref[...]
    parts = []
    for hh in range(o.shape[1] // HEAD_DIM):
        oh = o[:, hh * HEAD_DIM:(hh + 1) * HEAD_DIM]
        parts.append(oh * lax.rsqrt(jnp.mean(oh * oh, axis=-1, keepdims=True) + EPS))
    on = jnp.concatenate(parts, axis=1) * gain_ref[...]
    y = (on * jax.nn.sigmoid(g_ref[...])).astype(BF16)
    o_ref[...] = h_ref[...] + mod_ref[2:3, :] * _dot(y, w_ref[...])


def _hgrn_out(o_sum, p, gain, w_out, h, mods, rows_r, n_tiles):
    d = h.shape[1]
    tm = rows_r.tile
    g_blk = (p.shape[1] - d) // d
    return pl.pallas_call(
        _hgrn_out_kernel,
        out_shape=jax.ShapeDtypeStruct((n_tiles * tm, d), F32),
        grid=(n_tiles,),
        in_specs=[
            pl.BlockSpec((tm, d), lambda i: (i, 0)),
            pl.BlockSpec((tm, d), lambda i: (i, g_blk)),
            pl.BlockSpec((1, d), lambda i: (0, 0)),
            pl.BlockSpec(w_out.shape, lambda i: (0, 0)),
            pl.BlockSpec((tm, d), lambda i: (i, 0)),
            pl.BlockSpec((None, None, 3, d), lambda i: (rows_r.group(i), 1, 0, 0)),
        ],
        out_specs=pl.BlockSpec((tm, d), lambda i: (i, 0)),
        compiler_params=_cparams("parallel"),
        name="hgrn_out",
    )(o_sum, p, gain, w_out, h, mods)


def _hgrn_proj_kernel(h_ref, mod_ref, gain_ref, w_ref, o_ref, xn_ref, *, n_silu):
    j = pl.program_id(1)
    r = pl.program_id(2)

    @pl.when(j == 0)
    def _():
        xn_ref[r] = _ada(h_ref[...], gain_ref[...], mod_ref[0:1, :], mod_ref[1:2, :]).astype(BF16)

    acc = _dot(xn_ref[r], w_ref[...])
    o_ref[...] = jnp.where(j < n_silu, _silu(acc), acc)


def _hgrn_proj(h, mods, gain, w_in, rows):
    d = h.shape[1]
    tm, tn = rows.tile, PROJ_COL_TILE
    nw = w_in.shape[1]
    gs = next(k for k in (PROJ_ROW_GROUP, 2, 1) if rows.tiles % k == 0)
    tile = lambda g, r: g * gs + r
    return pl.pallas_call(
        functools.partial(_hgrn_proj_kernel, n_silu=d // tn),
        out_shape=jax.ShapeDtypeStruct((rows.n, nw), F32),
        grid=(rows.tiles // gs, nw // tn, gs),
        in_specs=[
            pl.BlockSpec((tm, d), lambda g, j, r: (jnp.where(j == 0, tile(g, r), tile(g, gs - 1)), 0)),
            pl.BlockSpec((None, None, 3, d), lambda g, j, r: (rows.group(tile(g, r)), 1, 0, 0)),
            pl.BlockSpec((1, d), lambda g, j, r: (0, 0)),
            pl.BlockSpec((d, tn), lambda g, j, r: (0, j)),
        ],
        out_specs=pl.BlockSpec((tm, tn), lambda g, j, r: (tile(g, r), j)),
        scratch_shapes=[pltpu.VMEM((gs, tm, d), BF16)],
        compiler_params=_cparams("arbitrary", "arbitrary", "arbitrary"),
        name="hgrn_proj",
    )(h, mods, gain, w_in)


def _scan_levels(chunk):
    return [chunk >> (i + 1) for i in range(int(math.log2(chunk)))]


def _scan_consts(chunk, reverse):
    t = np.arange(chunk)[:, None]
    s = np.arange(chunk)[None, :]
    tri = (s >= t) if reverse else (s <= t)
    wide, narrow = [], []
    for h in _scan_levels(chunk):
        same = (t // (2 * h)) == (s // (2 * h))
        t_up = (t // h) % 2 == 1
        s_up = (s // h) % 2 == 1
        mask = same & ((~t_up & s_up) if reverse else (t_up & ~s_up))
        if h >= SUBLANES:
            wide.append(mask[np.nonzero(~t_up[:, 0] if reverse else t_up[:, 0])[0]])
        else:
            narrow.append(mask)
    narrow.append(t == s)
    return (jnp.asarray(tri, dtype=BF16), jnp.asarray(np.stack(wide), dtype=F32),
            jnp.asarray(np.stack(narrow), dtype=F32))


def _seg_bcast(x, h, reverse):
    c, w = x.shape
    off = h if reverse else h - 1
    if 2 * h >= 2 * SUBLANES:
        pieces = [jnp.broadcast_to(x[g * 2 * h + off:g * 2 * h + off + 1, :], (2 * h, w)) for g in range(c // (2 * h))]
        return pieces[0] if len(pieces) == 1 else jnp.concatenate(pieces, axis=0)
    x3 = x.reshape(c // SUBLANES, SUBLANES, w)
    sub = lax.broadcasted_iota(jnp.int32, x3.shape, 1)
    y = None
    for g in range(SUBLANES // (2 * h)):
        piece = jnp.broadcast_to(x3[:, g * 2 * h + off:g * 2 * h + off + 1, :], x3.shape)
        y = piece if y is None else jnp.where(sub >= g * 2 * h, piece, y)
    return y.reshape(c, w)


def _scan_kernel(*refs, reverse, chunk, accumulate):
    if accumulate:
        q_ref, f_ref, v_ref, lb_ref, tri_ref, wmask_ref, nmask_ref, other_ref, o_ref, st_ref = refs
    else:
        q_ref, f_ref, v_ref, lb_ref, tri_ref, wmask_ref, nmask_ref, o_ref, st_ref = refs

    @pl.when(pl.program_id(2) == 0)
    def _():
        st_ref[...] = jnp.zeros_like(st_ref)

    rows, width = q_ref.shape
    levels = _scan_levels(chunk)
    spans = [slice(ci * chunk, (ci + 1) * chunk) for ci in range(rows // chunk)]
    lanes = [slice(hh * HEAD_DIM, (hh + 1) * HEAD_DIM) for hh in range(width // HEAD_DIM)]
    q = q_ref[...]
    fl = f_ref[...]
    lb = lb_ref[...]
    u = jnp.exp(-jnp.abs(fl))
    key = (1.0 - lb) * (jnp.where(fl > 0.0, u, 1.0) / (1.0 + u))
    log_sig = jnp.minimum(fl, 0.0) - jnp.log(1.0 + u)
    x1 = jnp.log(lb)
    x2 = jnp.log1p(-lb) + log_sig
    delta = x1 - x2
    lf = jnp.where(jnp.isnan(delta), x1 + x2, jnp.maximum(x1, x2) + jnp.log(1.0 + jnp.exp(-jnp.abs(delta))))
    lf = lf * LOG2_E
    l1 = lf.astype(BF16)
    r1 = lf - l1.astype(F32)
    l2 = r1.astype(BF16)
    l3 = (r1 - l2.astype(F32)).astype(BF16)
    parts = jnp.concatenate([l1, l2, l3], axis=1)
    tri = tri_ref[...]
    cs = jnp.concatenate([_dot(tri, parts[sp]) for sp in spans], axis=0)
    b = cs[:, :width] + cs[:, width:2 * width] + cs[:, 2 * width:]
    totals = [b[sp.start:sp.start + 1, :] if reverse else b[sp.stop - 1:sp.stop, :] for sp in spans]
    total_rows = jnp.concatenate([jnp.broadcast_to(t, (chunk, width)) for t in totals], axis=0)

    q_in = (q * jnp.exp2(b)).astype(BF16)
    k_out = (key * jnp.exp2(total_rows - b)).astype(BF16)
    v = v_ref[...].astype(BF16)

    order = list(range(len(spans)))[::-1] if reverse else list(range(len(spans)))
    kv = [[_dot_tn(v[sp, sl], k_out[sp, sl]) for sl in lanes] for sp in spans]
    o_state = [[None] * len(lanes) for _ in spans]
    for hh, sl in enumerate(lanes):
        st = st_ref[hh]
        for ci in order:
            o_state[ci][hh] = _dot_nt(q_in[spans[ci], sl], st.astype(BF16))
            st = st * jnp.exp2(totals[ci][:, sl]) + kv[ci][hh]
        st_ref[hh] = st

    n_blk = chunk // SUBLANES
    wide_levels = [h for h in levels if h >= SUBLANES]
    narrow_levels = [h for h in levels if h < SUBLANES]

    def narrow_scores(li, ql, kl):
        out = []
        for sp in spans:
            row = []
            for sl in lanes:
                s = nmask_ref[li] * _dot_nt(ql[sp, sl], kl[sp, sl])
                row.append([s[k * SUBLANES:(k + 1) * SUBLANES, :] for k in range(n_blk)])
            out.append(row)
        return out

    a = narrow_scores(len(narrow_levels), q.astype(BF16), key.astype(BF16))
    row_id = lax.broadcasted_iota(jnp.int32, (rows, width), 0)
    for li, h in enumerate(narrow_levels):
        if h == 1:
            e = jnp.where((row_id & 1) == (0 if reverse else 1), jnp.exp2(lf), 1.0)
        else:
            e = jnp.exp2(-jnp.abs(b - _seg_bcast(b, h, reverse)))
        new = narrow_scores(li, (q * e).astype(BF16), (key * e).astype(BF16))
        a = [[[x + y for x, y in zip(xb, yb)] for xb, yb in zip(xa, ya)] for xa, ya in zip(a, new)]

    for li, h in enumerate(wide_levels):
        q_parts, k_parts = [], []
        for g in range(rows // (2 * h)):
            lo = slice(g * 2 * h, g * 2 * h + h)
            hi = slice(g * 2 * h + h, (g + 1) * 2 * h)
            r = g * 2 * h + (h if reverse else h - 1)
            b_ref = jnp.broadcast_to(b[r:r + 1, :], (h, width))
            q_half, k_half = (lo, hi) if reverse else (hi, lo)
            q_parts.append(q[q_half] * jnp.exp2(b[q_half] - b_ref))
            k_part = key[k_half] * jnp.exp2(b_ref - b[k_half])
            zeros = jnp.zeros((h, width), F32)
            k_parts += [zeros, k_part] if reverse else [k_part, zeros]
        q_sel = jnp.concatenate(q_parts, axis=0).astype(BF16)
        k_hat = jnp.concatenate(k_parts, axis=0).astype(BF16)
        half = chunk // 2
        for ci, sp in enumerate(spans):
            for hh, sl in enumerate(lanes):
                s = wmask_ref[li] * _dot_nt(q_sel[ci * half:(ci + 1) * half, sl], k_hat[sp, sl])
                for j in range(chunk // (2 * h)):
                    first = (j * 2 * h + (0 if reverse else h)) // SUBLANES
                    for k in range(h // SUBLANES):
                        r0 = j * h + k * SUBLANES
                        a[ci][hh][first + k] = a[ci][hh][first + k] + s[r0:r0 + SUBLANES, :]

    for ci, sp in enumerate(spans):
        for hh, sl in enumerate(lanes):
            pairs = jnp.concatenate(a[ci][hh], axis=0).astype(BF16)
            o = o_state[ci][hh] + _dot(pairs, v[sp, sl])
            o_ref[sp, sl] = o + other_ref[sp, sl] if accumulate else o


def _hgrn_scan(p, lower_bound, rows, reverse, other=None):
    b, seq, lc = rows.batch, rows.seq, rows.ctx_len
    d = lower_bound.shape[0]
    c, hb = SCAN_CHUNK, SCAN_HEADS_PER_BLOCK
    r = c * SCAN_CHUNKS_PER_STEP
    wb = hb * HEAD_DIM
    ncb = d // wb
    assert lc % r == 0 and seq % r == 0
    nctx, nlat = lc // r, seq // r
    ctx0 = rows.n_lat // r
    f_blk = (2 if reverse else 1) * ncb
    v_blk = 3 * ncb

    def row(bi, s):
        if reverse:
            return jnp.where(s < nctx, ctx0 + bi * nctx + (nctx - 1 - s), bi * nlat + (nlat - 1 - (s - nctx)))
        return jnp.where(s < nctx, ctx0 + bi * nctx + s, bi * nlat + (s - nctx))

    tri, wmasks, nmasks = _scan_consts(c, reverse)
    o_spec = pl.BlockSpec((r, wb), lambda bi, hi, s: (row(bi, s), hi))
    extra = () if other is None else (other,)
    return pl.pallas_call(
        functools.partial(_scan_kernel, reverse=reverse, chunk=c, accumulate=other is not None),
        out_shape=jax.ShapeDtypeStruct((rows.n, d), F32),
        grid=(b, ncb, nctx + nlat),
        in_specs=[
            pl.BlockSpec((r, wb), lambda bi, hi, s: (row(bi, s), hi)),
            pl.BlockSpec((r, wb), lambda bi, hi, s: (row(bi, s), f_blk + hi)),
            pl.BlockSpec((r, wb), lambda bi, hi, s: (row(bi, s), v_blk + hi)),
            pl.BlockSpec((1, wb), lambda bi, hi, s: (0, hi)),
            pl.BlockSpec(tri.shape, lambda bi, hi, s: (0, 0)),
            pl.BlockSpec(wmasks.shape, lambda bi, hi, s: (0, 0, 0)),
            pl.BlockSpec(nmasks.shape, lambda bi, hi, s: (0, 0, 0)),
        ] + [o_spec] * len(extra),
        out_specs=o_spec,
        scratch_shapes=[pltpu.VMEM((hb, HEAD_DIM, HEAD_DIM), F32)],
        compiler_params=_cparams("parallel", "parallel", "arbitrary"),
        name="hgrn_scan_bw" if reverse else "hgrn_scan_fw",
    )(p, p, p, lower_bound.reshape(1, d), tri, wmasks, nmasks, *extra)


def kernel(x, c, ctx, c_ctx, w_mod, b_mod, norm_gains, ffn_w_in, ffn_w_out, ab_w_in, qk_norm, ab_w_out,
           hgrn_w_in, hgrn_lb_logits, hgrn_o_norm, hgrn_w_out, final_norm):
    batch, seq, d = x.shape
    lc = ctx.shape[1]
    depth = w_mod.shape[0]
    rows = _Rows(batch, seq, lc, ROW_TILE)
    rows_r = _Rows(batch, seq, lc, READOUT_ROW_TILE)
    assert seq % GRID_W == 0 and batch + 1 <= SUBLANES

    c_rows = jnp.concatenate([c_ctx[None, :], c, jnp.zeros((SUBLANES - 1 - batch, d), F32)], axis=0)
    mods_all = _modulation(c_rows, w_mod, b_mod).reshape(depth, SUBLANES, 3, 3, d)

    lb_cum = jnp.cumsum(jax.nn.softmax(hgrn_lb_logits.astype(F32), axis=0), axis=0)
    lower_bounds = lb_cum - lb_cum[0]

    ffn_w_in_b = ffn_w_in.astype(BF16)
    ffn_w_out_b = ffn_w_out
    fin = final_norm.reshape(1, d)
    rope = _rope_tables(seq, ROW_TILE)

    h = (x.reshape(batch * seq, d), ctx.reshape(batch * lc, d))
    for layer in range(depth):
        last = layer == depth - 1
        mods = mods_all[layer]
        gains = norm_gains[layer].reshape(3, 1, d)
        h = _ffn(h, mods, 0, gains[0], ffn_w_in_b, ffn_w_out_b, layer, 0, fin, rows, rows.tiles, False)
        if layer % 2 == 0:
            e = layer // 2
            f_all, q_all, k_all, vt_all = _ab_proj(h, mods, gains[1], ab_w_in[e].astype(BF16), qk_norm[e], rope, rows)
            attn = _attention(q_all, k_all, vt_all, rows)
            attn = _attention(q_all, k_all, vt_all, rows, lat_out=attn)
            h = _ab_out(_fourier_latent(f_all, rows), _fourier_ctx(f_all, rows), attn,
                        ab_w_out[e].astype(BF16), h, mods, rows)
        else:
            o = layer // 2
            p = _hgrn_proj(h, mods, gains[1], hgrn_w_in[o].astype(BF16), rows)
            o_fw = _hgrn_scan(p, lower_bounds[layer], rows, False)
            o_sum = _hgrn_scan(p, lower_bounds[layer], rows, True, other=o_fw)
            gain_o = jnp.tile(hgrn_o_norm[o], d // HEAD_DIM).reshape(1, d)
            n_t = rows_r.lat_tiles if last else rows_r.tiles
            h = _hgrn_out(o_sum, p, gain_o, hgrn_w_out[o].astype(BF16), h, mods, rows_r, n_t)
        n_t = rows.lat_tiles if last else rows.tiles
        h = _ffn(h, mods, 2, gains[2], ffn_w_in_b, ffn_w_out_b, layer, 1, fin, rows, n_t, last)
    return h[:batch * seq].reshape(batch, seq, d)
```

```python
import functools
import math

import jax
import jax.numpy as jnp
import numpy as np
from jax import lax
from jax.experimental import pallas as pl
from jax.experimental.pallas import tpu as pltpu

F32 = jnp.float32
BF16 = jnp.bfloat16

EPS = 1e-6
N_MOD = 9
HEAD_DIM = 128
N_KV_HEADS = 4
FOURIER_WIDTH = 512
FOURIER_GROUP_DIM = 128
GRID_W = 64
ROPE_THETA = 10000.0
ROPE_AXIS_DIM = HEAD_DIM // 2
ATTN_SCALE = HEAD_DIM ** -0.5
LOG2_E = math.log2(math.e)

LANES = 128
SUBLANES = 8
BF16_SUBLANES = 16
VT_ROWS = HEAD_DIM + BF16_SUBLANES
VMEM_LIMIT_BYTES = 56 * 1024 * 1024

ROW_TILE = 512
READOUT_ROW_TILE = 512
FFN_TILE = 512
PROJ_COL_TILE = 2048
PROJ_ROW_GROUP = 3
MOD_COL_TILE = 1024
ATTN_Q_TILE = 1024
SCAN_CHUNK = 128
SCAN_HEADS_PER_BLOCK = 16
SCAN_CHUNKS_PER_STEP = 2
FFT_B = 128
FFT_B_BLOCK = 8


def _cparams(*sem):
    return pltpu.CompilerParams(dimension_semantics=sem, vmem_limit_bytes=VMEM_LIMIT_BYTES)


def _dot(a, b):
    return jnp.dot(a, b, preferred_element_type=F32)


def _dot_hi(a, b):
    return jnp.dot(a, b, preferred_element_type=F32, precision=lax.Precision.HIGHEST)


def _split_bf16(x):
    hi = x.astype(BF16)
    return hi, (x - hi.astype(F32)).astype(BF16)


def _dot_split(a, b):
    (ah, al), (bh, bl) = a, b
    return _dot(ah, bh) + _dot(al, bh) + _dot(ah, bl)


def _dot_nt(a, b):
    return lax.dot_general(a, b, (((1,), (1,)), ((), ())), preferred_element_type=F32)


def _dot_tn(a, b):
    return lax.dot_general(a, b, (((0,), (0,)), ((), ())), preferred_element_type=F32)


def _silu(x):
    return x * jax.nn.sigmoid(x)


def _rms(x, gain):
    return x * lax.rsqrt(jnp.mean(x * x, axis=-1, keepdims=True) + EPS) * gain


def _ada(h, gain, shift, scale):
    return _rms(h, gain) * (1.0 + scale) + shift


def _mod_kernel(c_ref, w_ref, b_ref, o_ref):
    a = _silu(c_ref[...]).astype(BF16)
    o_ref[...] = _dot(a, w_ref[...].astype(BF16)) + b_ref[...]


def _modulation(c_rows, w_mod, b_mod):
    depth, d, nd = w_mod.shape
    tn = MOD_COL_TILE
    return pl.pallas_call(
        _mod_kernel,
        out_shape=jax.ShapeDtypeStruct((depth, SUBLANES, nd), F32),
        grid=(depth, nd // tn),
        in_specs=[
            pl.BlockSpec((SUBLANES, d), lambda l, j: (0, 0)),
            pl.BlockSpec((None, d, tn), lambda l, j: (l, 0, j)),
            pl.BlockSpec((None, 1, tn), lambda l, j: (l, 0, j)),
        ],
        out_specs=pl.BlockSpec((None, SUBLANES, tn), lambda l, j: (l, 0, j)),
        compiler_params=_cparams("parallel", "arbitrary"),
        name="modulation",
    )(c_rows, w_mod, b_mod.reshape(depth, 1, nd))


class _Rows:
    def __init__(self, batch, seq, ctx_len, tile):
        assert seq % tile == 0 and (batch * ctx_len) % tile == 0
        self.batch, self.seq, self.ctx_len, self.tile = batch, seq, ctx_len, tile
        self.n_lat = batch * seq
        self.n = self.n_lat + batch * ctx_len
        self.lat_tiles = self.n_lat // tile
        self.tiles = self.n // tile
        self.tiles_per_batch = seq // tile

    def group(self, i):
        return jnp.where(i < self.lat_tiles, 1 + i // self.tiles_per_batch, 0)


def _ffn_kernel(*refs, final, split_tiles):
    i = pl.program_id(0)
    j = pl.program_id(1)
    if split_tiles is None:
        h_ref, mod_ref, gain_ref, wa_ref, wb_ref, wo_ref, fin_ref, o_ref, xn_ref, acc_ref = refs
        sources = [(h_ref, None)]
    else:
        hl_ref, hc_ref, mod_ref, gain_ref, wa_ref, wb_ref, wo_ref, fin_ref, o_ref, xn_ref, acc_ref = refs
        sources = [(hl_ref, i < split_tiles), (hc_ref, i >= split_tiles)]

    def when_rows(cond, fn):
        for src, mine in sources:
            pl.when(cond if mine is None else jnp.logical_and(cond, mine))(functools.partial(fn, src))

    def prologue(src):
        xn_ref[...] = _ada(src[...], gain_ref[...], mod_ref[0:1, :], mod_ref[1:2, :]).astype(BF16)

    when_rows(j == 0, prologue)

    @pl.when(jnp.logical_and(i == 0, j == 0))
    def _():
        acc_ref[...] = jnp.zeros_like(acc_ref)

    xn = xn_ref[...]
    a = _dot(xn, wa_ref[...])
    b = _dot(xn, wb_ref[...])
    g = (_silu(a) * b).astype(BF16)
    acc_ref[...] = jnp.where(j == 0, 0.0, acc_ref[...]) + _dot(g, wo_ref[...])

    def epilogue(src):
        out = src[...] + 0.5 * mod_ref[2:3, :] * acc_ref[...]
        if final:
            out = _rms(out, fin_ref[...])
        o_ref[...] = out

    when_rows(j == pl.num_programs(1) - 1, epilogue)


def _ffn(h, mods, sub, gain, w_in, w_out, layer, which, fin, rows, n_tiles, final):
    f, d = w_out.shape[2:]
    tm, tf = rows.tile, FFN_TILE
    nf = f // tf
    if isinstance(h, tuple):
        lt = rows.lat_tiles
        row_specs = [pl.BlockSpec((tm, d), lambda i, j: (jnp.minimum(i, lt - 1), 0)),
                     pl.BlockSpec((tm, d), lambda i, j: (jnp.maximum(i - lt, 0), 0))]
        row_args, split_tiles = h, lt
    else:
        row_specs = [pl.BlockSpec((tm, d), lambda i, j: (i, 0))]
        row_args, split_tiles = (h,), None
    return pl.pallas_call(
        functools.partial(_ffn_kernel, final=final, split_tiles=split_tiles),
        out_shape=jax.ShapeDtypeStruct((n_tiles * tm, d), F32),
        grid=(n_tiles, nf),
        in_specs=row_specs + [
            pl.BlockSpec((None, None, 3, d), lambda i, j: (rows.group(i), sub, 0, 0)),
            pl.BlockSpec((1, d), lambda i, j: (0, 0)),
            pl.BlockSpec((None, None, d, tf), lambda i, j: (layer, which, 0, j)),
            pl.BlockSpec((None, None, d, tf), lambda i, j: (layer, which, 0, nf + j)),
            pl.BlockSpec((None, None, tf, d), lambda i, j: (layer, which, j, 0)),
            pl.BlockSpec((1, d), lambda i, j: (0, 0)),
        ],
        out_specs=pl.BlockSpec((tm, d), lambda i, j: (i, 0)),
        scratch_shapes=[pltpu.VMEM((tm, d), BF16), pltpu.VMEM((tm, d), F32)],
        compiler_params=_cparams("arbitrary", "arbitrary"),
        name="ffn",
    )(*row_args, mods, gain, w_in, w_in, w_out, fin)


def _rope_tables(seq, tile):
    t = np.arange(seq)
    inv_freq = ROPE_THETA ** (-np.arange(0, ROPE_AXIS_DIM, 2, dtype=np.float64) / ROPE_AXIS_DIM)
    ang = np.concatenate([(t // GRID_W)[:, None] * inv_freq, (t % GRID_W)[:, None] * inv_freq], axis=-1)
    nf = ROPE_AXIS_DIM // 2
    cos = np.cos(ang).reshape(seq, 2, 1, nf)
    sin = np.sin(ang).reshape(seq, 2, 1, nf)
    zero = np.zeros_like(sin)
    c_full = np.broadcast_to(cos, (seq, 2, 2, nf)).reshape(seq, HEAD_DIM)
    s_up = np.concatenate([-sin, zero], axis=2).reshape(seq, HEAD_DIM)
    s_dn = np.concatenate([zero, sin], axis=2).reshape(seq, HEAD_DIM)
    lat = np.concatenate([c_full, s_up, s_dn], axis=1)
    ident = np.concatenate([np.ones((tile, HEAD_DIM)), np.zeros((tile, 2 * HEAD_DIM))], axis=1)
    return jnp.asarray(np.concatenate([lat, ident], axis=0), dtype=F32)


def _norm_rope_heads(acc, gain, rope, post_scale=None):
    nf = ROPE_AXIS_DIM // 2
    c, s_up, s_dn = rope[:, :HEAD_DIM], rope[:, HEAD_DIM:2 * HEAD_DIM], rope[:, 2 * HEAD_DIM:]
    heads = []
    for hh in range(acc.shape[1] // HEAD_DIM):
        y = _rms(acc[:, hh * HEAD_DIM:(hh + 1) * HEAD_DIM], gain)
        y = y * c + pltpu.roll(y, HEAD_DIM - nf, 1) * s_up + pltpu.roll(y, nf, 1) * s_dn
        heads.append(y if post_scale is None else y * post_scale)
    return jnp.concatenate(heads, axis=1).astype(BF16)


def _ab_proj_kernel(h_ref, mod_ref, gain_ref, w_ref, qkn_ref, rope_ref, f_ref, q_ref, k_ref, vt_ref):
    xn = _ada(h_ref[...], gain_ref[...], mod_ref[0:1, :], mod_ref[1:2, :]).astype(BF16)
    c0 = f_ref.shape[1]
    c1 = c0 + q_ref.shape[1]
    c2 = c1 + k_ref.shape[1]
    q_ref[...] = _norm_rope_heads(_dot(xn, w_ref[:, c0:c1]), qkn_ref[0:1, :], rope_ref[...], ATTN_SCALE * LOG2_E)
    k_ref[...] = _norm_rope_heads(_dot(xn, w_ref[:, c1:c2]), qkn_ref[1:2, :], rope_ref[...])
    v = _dot(xn, w_ref[:, c2:])
    ones = jnp.ones((vt_ref.shape[1] - HEAD_DIM, vt_ref.shape[2]), BF16)
    for hh in range(N_KV_HEADS):
        vt_ref[hh, :HEAD_DIM, :] = v[:, hh * HEAD_DIM:(hh + 1) * HEAD_DIM].T.astype(BF16)
        vt_ref[hh, HEAD_DIM:, :] = ones
    f_ref[...] = _dot(xn, w_ref[:, :c0])


def _ab_proj(h, mods, gain, w_in, qk_norm, rope, rows):
    d = h.shape[1]
    tm = rows.tile
    kv_width = N_KV_HEADS * HEAD_DIM
    q_width = w_in.shape[1] - FOURIER_WIDTH - 2 * kv_width
    n = rows.n
    rope_blk = lambda i: (jnp.where(i < rows.lat_tiles, i % rows.tiles_per_batch, rows.tiles_per_batch), 0)
    return pl.pallas_call(
        _ab_proj_kernel,
        out_shape=(
            jax.ShapeDtypeStruct((n, FOURIER_WIDTH), F32),
            jax.ShapeDtypeStruct((n, q_width), BF16),
            jax.ShapeDtypeStruct((n, kv_width), BF16),
            jax.ShapeDtypeStruct((N_KV_HEADS, rows.tiles, VT_ROWS, tm), BF16),
        ),
        grid=(rows.tiles,),
        in_specs=[
            pl.BlockSpec((tm, d), lambda i: (i, 0)),
            pl.BlockSpec((None, None, 3, d), lambda i: (rows.group(i), 1, 0, 0)),
            pl.BlockSpec((1, d), lambda i: (0, 0)),
            pl.BlockSpec(w_in.shape, lambda i: (0, 0)),
            pl.BlockSpec((2, HEAD_DIM), lambda i: (0, 0)),
            pl.BlockSpec((tm, 3 * HEAD_DIM), rope_blk),
        ],
        out_specs=(
            pl.BlockSpec((tm, FOURIER_WIDTH), lambda i: (i, 0)),
            pl.BlockSpec((tm, q_width), lambda i: (i, 0)),
            pl.BlockSpec((tm, kv_width), lambda i: (i, 0)),
            pl.BlockSpec((N_KV_HEADS, None, VT_ROWS, tm), lambda i: (0, i, 0, 0)),
        ),
        compiler_params=_cparams("parallel"),
        name="ab_proj",
    )(h, mods, gain, w_in, qk_norm, rope)


def _attn_kernel(*refs, lat_chunks, group):
    if lat_chunks:
        q_ref, kc_ref, vtc_ref, kl_ref, vtl_ref, o_ref, acc_ref, s_ref = refs
    else:
        q_ref, kc_ref, vtc_ref, _, o_ref, acc_ref = refs
    tq = q_ref.shape[0]
    q = q_ref[...]
    qs = jnp.concatenate([q[:, g * HEAD_DIM:(g + 1) * HEAD_DIM] for g in range(group)], axis=0)
    nq = group * tq
    acc_ref[...] = jnp.zeros_like(acc_ref)

    def scores(k):
        return _dot_nt(k, qs)

    def update(s, vt, m):
        m_new = jnp.maximum(m, jnp.max(s, axis=0, keepdims=True))
        p = jnp.exp2(s - m_new).astype(BF16)
        acc_ref[...] = jnp.exp2(m - m_new) * acc_ref[...] + _dot(vt, p)
        return m_new

    m = update(scores(kc_ref[...]), vtc_ref[...], jnp.full((1, nq), -jnp.inf, F32))
    if lat_chunks:
        tk = vtl_ref.shape[2]

        def lat_scores(c):
            return scores(kl_ref[pl.ds(pl.multiple_of(c * tk, tk), tk), :])

        assert lat_chunks % 2 == 0
        s_ref[0] = lat_scores(0)

        def body(i, m):
            c = 2 * i
            s_ref[1] = lat_scores(c + 1)
            m = update(s_ref[0], vtl_ref[c], m)
            s_ref[0] = lat_scores(jnp.minimum(c + 2, lat_chunks - 1))
            return update(s_ref[1], vtl_ref[c + 1], m)

        m = lax.fori_loop(0, lat_chunks // 2, body, m)

    out = (acc_ref[:HEAD_DIM, :] / acc_ref[HEAD_DIM:HEAD_DIM + 1, :]).T
    o_ref[...] = jnp.concatenate([out[g * tq:(g + 1) * tq, :] for g in range(group)], axis=1).astype(BF16)


def _attention(q_all, k_all, vt_all, rows, lat_out=None):
    b, seq, lc, tile = rows.batch, rows.seq, rows.ctx_len, rows.tile
    group = q_all.shape[1] // (N_KV_HEADS * HEAD_DIM)
    gw = group * HEAD_DIM
    assert tile % lc == 0 and seq % tile == 0
    ctx_blk0 = rows.n_lat // lc
    ctx_tile = lambda bi: ((rows.n_lat + bi * lc) // tile, 0, ((rows.n_lat + bi * lc) % tile) // lc)
    kc_spec = pl.BlockSpec((lc, HEAD_DIM), lambda bi, hi, i: (ctx_blk0 + bi, hi))
    vtc_spec = pl.BlockSpec((None, None, VT_ROWS, lc), lambda bi, hi, i: (hi,) + ctx_tile(bi))
    if lat_out is None:
        tq = ATTN_Q_TILE
        assert tq % tile == 0
        nqt = seq // tq
        lat_chunks = seq // tile
        q_spec = pl.BlockSpec((tq, gw), lambda bi, hi, i: (bi * nqt + i, hi))
        kl_spec = pl.BlockSpec((seq, HEAD_DIM), lambda bi, hi, i: (bi, hi))
        vtl_spec = pl.BlockSpec((None, lat_chunks, VT_ROWS, tile), lambda bi, hi, i: (hi, bi, 0, 0))
        in_specs = [q_spec, kc_spec, vtc_spec, kl_spec, vtl_spec]
        args = (q_all, k_all, vt_all, k_all, vt_all)
        aliases = {}
    else:
        tq = lc
        nqt = 1
        lat_chunks = 0
        q_spec = pl.BlockSpec((tq, gw), lambda bi, hi, i: (ctx_blk0 + bi, hi))
        in_specs = [q_spec, kc_spec, vtc_spec, pl.BlockSpec(memory_space=pl.ANY)]
        args = (q_all, k_all, vt_all, lat_out)
        aliases = {3: 0}
    return pl.pallas_call(
        functools.partial(_attn_kernel, lat_chunks=lat_chunks, group=group),
        out_shape=jax.ShapeDtypeStruct((rows.n, q_all.shape[1]), BF16),
        grid=(b, N_KV_HEADS, nqt),
        in_specs=in_specs,
        out_specs=q_spec,
        scratch_shapes=[pltpu.VMEM((VT_ROWS, group * tq), F32)]
        + ([pltpu.VMEM((2, tile, group * tq), F32)] if lat_chunks else []),
        input_output_aliases=aliases,
        compiler_params=_cparams("parallel", "parallel", "arbitrary"),
        name="attention_ctx" if lat_chunks == 0 else "attention_lat",
    )(*args)


def _dft_cs(n):
    idx = np.arange(n)
    ang = 2.0 * np.pi * ((idx[:, None] * idx[None, :]) % n) / n
    return np.cos(ang), np.sin(ang)


def _const_split(m):
    m = np.asarray(m, dtype=np.float32)
    hi = jnp.asarray(m).astype(BF16)
    lo = (jnp.asarray(m) - hi.astype(F32)).astype(BF16)
    return jnp.stack([hi, lo])


def _fft1_kernel(x_ref, fa_ref, tw_ref, o_ref, *, a):
    w = o_ref.shape[2]
    for r in range(o_ref.shape[0]):
        z = _dot_split((fa_ref[0], fa_ref[1]), _split_bf16(x_ref[:, r * w:(r + 1) * w]))
        zr, zi = z[:a], z[a:]
        tc = jnp.concatenate([tw_ref[r, 0]] * (w // LANES), axis=1)
        ts = jnp.concatenate([tw_ref[r, 1]] * (w // LANES), axis=1)
        o_ref[r, :a, :] = zr * tc - zi * ts
        o_ref[r, a:, :] = zr * ts + zi * tc


def _fft2_kernel(zr_ref, zi_ref, m2_ref, mc_ref, o_ref, *, scale):
    for r in range(o_ref.shape[1]):
        z = jnp.concatenate([zr_ref[:, r, :], zi_ref[:, r, :]], axis=0)
        v = _dot_split((m2_ref[0], m2_ref[1]), _split_bf16(z))
        vr, vi = v[:FFT_B], v[FFT_B:]
        outs = []
        for g in range(o_ref.shape[2] // LANES):
            u = jnp.concatenate([vr[:, g * LANES:(g + 1) * LANES], vi[:, g * LANES:(g + 1) * LANES]], axis=1)
            outs.append(_dot_split(_split_bf16(u), (mc_ref[0], mc_ref[1])))
        o_ref[:, r, :] = jnp.concatenate(outs, axis=1) * scale


def _fourier_latent(f_all, rows):
    b, seq = rows.batch, rows.seq
    w = FOURIER_WIDTH
    a = seq // FFT_B
    assert a % SUBLANES == 0 and FOURIER_GROUP_DIM == LANES
    ca, sa = _dft_cs(a)
    fa = _const_split(np.concatenate([ca, sa], axis=0))
    p1b = (np.arange(a)[None, :] * np.arange(FFT_B)[:, None]) % seq
    ang = 2.0 * np.pi * p1b / seq
    tw = np.stack([np.cos(ang), np.sin(ang)], axis=1)[..., None]
    tw = jnp.asarray(np.broadcast_to(tw, (FFT_B, 2, a, LANES)), dtype=F32)
    cb, sb = _dft_cs(FFT_B)
    m2 = _const_split(np.block([[cb, -sb], [sb, cb]]))
    cc, sc = _dft_cs(FOURIER_GROUP_DIM)
    mc = _const_split(np.concatenate([cc, -sc], axis=0))

    blk = FFT_B_BLOCK
    x2 = f_all.reshape(rows.n // FFT_B, FFT_B * w)
    z = pl.pallas_call(
        functools.partial(_fft1_kernel, a=a),
        out_shape=jax.ShapeDtypeStruct((b, FFT_B, 2 * a, w), F32),
        grid=(b, FFT_B // blk),
        in_specs=[
            pl.BlockSpec((a, blk * w), lambda bi, j: (bi, j)),
            pl.BlockSpec((2, 2 * a, a), lambda bi, j: (0, 0, 0)),
            pl.BlockSpec((blk, 2, a, LANES), lambda bi, j: (j, 0, 0, 0)),
        ],
        out_specs=pl.BlockSpec((None, blk, 2 * a, w), lambda bi, j: (bi, j, 0, 0)),
        compiler_params=_cparams("parallel", "parallel"),
        name="fourier_stage1",
    )(x2, fa, tw)

    nblk = a // blk
    y = pl.pallas_call(
        functools.partial(_fft2_kernel, scale=1.0 / math.sqrt(seq * FOURIER_GROUP_DIM)),
        out_shape=jax.ShapeDtypeStruct((b, FFT_B, a, w), F32),
        grid=(b, nblk),
        in_specs=[
            pl.BlockSpec((None, FFT_B, blk, w), lambda bi, j: (bi, 0, j, 0)),
            pl.BlockSpec((None, FFT_B, blk, w), lambda bi, j: (bi, 0, nblk + j, 0)),
            pl.BlockSpec((2, 2 * FFT_B, 2 * FFT_B), lambda bi, j: (0, 0, 0)),
            pl.BlockSpec((2, 2 * LANES, LANES), lambda bi, j: (0, 0, 0)),
        ],
        out_specs=pl.BlockSpec((None, FFT_B, blk, w), lambda bi, j: (bi, 0, j, 0)),
        compiler_params=_cparams("parallel", "parallel"),
        name="fourier_stage2",
    )(z, z, m2, mc)
    return y.reshape(b * seq, w)


def _dft_ctx_kernel(x_ref, cn_ref, sn_ref, cc_ref, sc_ref, o_ref, *, scale):
    x = x_ref[...]
    outs = []
    for g in range(x.shape[1] // LANES):
        xg = x[:, g * LANES:(g + 1) * LANES]
        outs.append(_dot_hi(cn_ref[...], _dot_hi(xg, cc_ref[...])) - _dot_hi(sn_ref[...], _dot_hi(xg, sc_ref[...])))
    o_ref[...] = jnp.concatenate(outs, axis=1) * scale


def _fourier_ctx(f_all, rows):
    b, lc = rows.batch, rows.ctx_len
    w = FOURIER_WIDTH
    cn, sn = _dft_cs(lc)
    cc, sc = _dft_cs(FOURIER_GROUP_DIM)
    blk0 = rows.n_lat // lc
    mat = lambda m: pl.BlockSpec(m.shape, lambda bi: (0, 0))
    consts = [jnp.asarray(m, dtype=F32) for m in (cn, sn, cc, sc)]
    return pl.pallas_call(
        functools.partial(_dft_ctx_kernel, scale=1.0 / math.sqrt(lc * FOURIER_GROUP_DIM)),
        out_shape=jax.ShapeDtypeStruct((b * lc, w), F32),
        grid=(b,),
        in_specs=[pl.BlockSpec((lc, w), lambda bi: (blk0 + bi, 0))] + [mat(m) for m in consts],
        out_specs=pl.BlockSpec((lc, w), lambda bi: (bi, 0)),
        compiler_params=_cparams("parallel"),
        name="fourier_ctx",
    )(f_all, *consts)


def _ab_out_kernel(x1l_ref, x1c_ref, x2_ref, w_ref, h_ref, mod_ref, o_ref, *, lat_tiles):
    x1 = jnp.where(pl.program_id(0) < lat_tiles, x1l_ref[...], x1c_ref[...])
    x = jnp.concatenate([x1.astype(BF16), x2_ref[...]], axis=1)
    o_ref[...] = h_ref[...] + mod_ref[2:3, :] * _dot(x, w_ref[...])


def _ab_out(x1_lat, x1_ctx, x2, w_out, h, mods, rows):
    d = h.shape[1]
    tm = rows.tile
    lt = rows.lat_tiles
    return pl.pallas_call(
        functools.partial(_ab_out_kernel, lat_tiles=lt),
        out_shape=jax.ShapeDtypeStruct((rows.n, d), F32),
        grid=(rows.tiles,),
        in_specs=[
            pl.BlockSpec((tm, x1_lat.shape[1]), lambda i: (jnp.minimum(i, lt - 1), 0)),
            pl.BlockSpec((tm, x1_ctx.shape[1]), lambda i: (jnp.maximum(i - lt, 0), 0)),
            pl.BlockSpec((tm, x2.shape[1]), lambda i: (i, 0)),
            pl.BlockSpec(w_out.shape, lambda i: (0, 0)),
            pl.BlockSpec((tm, d), lambda i: (i, 0)),
            pl.BlockSpec((None, None, 3, d), lambda i: (rows.group(i), 1, 0, 0)),
        ],
        out_specs=pl.BlockSpec((tm, d), lambda i: (i, 0)),
        compiler_params=_cparams("parallel"),
        name="ab_out",
    )(x1_lat, x1_ctx, x2, w_out, h, mods)


def _hgrn_out_kernel(osum_ref, g_ref, gain_ref, w_ref, h_ref, mod_ref, o_ref):
    o = osum_ref[...]
    parts = []
    for hh in range(o.shape[1] // HEAD_DIM):
        oh = o[:, hh * HEAD_DIM:(hh + 1) * HEAD_DIM]
        parts.append(oh * lax.rsqrt(jnp.mean(oh * oh, axis=-1, keepdims=True) + EPS))
    on = jnp.concatenate(parts, axis=1) * gain_ref[...]
    y = (on * jax.nn.sigmoid(g_ref[...])).astype(BF16)
    o_ref[...] = h_ref[...] + mod_ref[2:3, :] * _dot(y, w_ref[...])


def _hgrn_out(o_sum, p, gain, w_out, h, mods, rows_r, n_tiles):
    d = h.shape[1]
    tm = rows_r.tile
    g_blk = (p.shape[1] - d) // d
    return pl.pallas_call(
        _hgrn_out_kernel,
        out_shape=jax.ShapeDtypeStruct((n_tiles * tm, d), F32),
        grid=(n_tiles,),
        in_specs=[
            pl.BlockSpec((tm, d), lambda i: (i, 0)),
            pl.BlockSpec((tm, d), lambda i: (i, g_blk)),
            pl.BlockSpec((1, d), lambda i: (0, 0)),
            pl.BlockSpec(w_out.shape, lambda i: (0, 0)),
            pl.BlockSpec((tm, d), lambda i: (i, 0)),
            pl.BlockSpec((None, None, 3, d), lambda i: (rows_r.group(i), 1, 0, 0)),
        ],
        out_specs=pl.BlockSpec((tm, d), lambda i: (i, 0)),
        compiler_params=_cparams("parallel"),
        name="hgrn_out",
    )(o_sum, p, gain, w_out, h, mods)


def _hgrn_proj_kernel(h_ref, mod_ref, gain_ref, w_ref, o_ref, xn_ref, *, n_silu):
    j = pl.program_id(1)
    r = pl.program_id(2)

    @pl.when(j == 0)
    def _():
        xn_ref[r] = _ada(h_ref[...], gain_ref[...], mod_ref[0:1, :], mod_ref[1:2, :]).astype(BF16)

    acc = _dot(xn_ref[r], w_ref[...])
    o_ref[...] = jnp.where(j < n_silu, _silu(acc), acc)


def _hgrn_proj(h, mods, gain, w_in, rows):
    d = h.shape[1]
    tm, tn = rows.tile, PROJ_COL_TILE
    nw = w_in.shape[1]
    gs = next(k for k in (PROJ_ROW_GROUP, 2, 1) if rows.tiles % k == 0)
    tile = lambda g, r: g * gs + r
    return pl.pallas_call(
        functools.partial(_hgrn_proj_kernel, n_silu=d // tn),
        out_shape=jax.ShapeDtypeStruct((rows.n, nw), F32),
        grid=(rows.tiles // gs, nw // tn, gs),
        in_specs=[
            pl.BlockSpec((tm, d), lambda g, j, r: (jnp.where(j == 0, tile(g, r), tile(g, gs - 1)), 0)),
            pl.BlockSpec((None, None, 3, d), lambda g, j, r: (rows.group(tile(g, r)), 1, 0, 0)),
            pl.BlockSpec((1, d), lambda g, j, r: (0, 0)),
            pl.BlockSpec((d, tn), lambda g, j, r: (0, j)),
        ],
        out_specs=pl.BlockSpec((tm, tn), lambda g, j, r: (tile(g, r), j)),
        scratch_shapes=[pltpu.VMEM((gs, tm, d), BF16)],
        compiler_params=_cparams("arbitrary", "arbitrary", "arbitrary"),
        name="hgrn_proj",
    )(h, mods, gain, w_in)


def _scan_levels(chunk):
    return [chunk >> (i + 1) for i in range(int(math.log2(chunk)))]


def _scan_consts(chunk, reverse):
    t = np.arange(chunk)[:, None]
    s = np.arange(chunk)[None, :]
    tri = (s >= t) if reverse else (s <= t)
    wide, narrow = [], []
    for h in _scan_levels(chunk):
        same = (t // (2 * h)) == (s // (2 * h))
        t_up = (t // h) % 2 == 1
        s_up = (s // h) % 2 == 1
        mask = same & ((~t_up & s_up) if reverse else (t_up & ~s_up))
        if h >= SUBLANES:
            wide.append(mask[np.nonzero(~t_up[:, 0] if reverse else t_up[:, 0])[0]])
        else:
            narrow.append(mask)
    narrow.append(t == s)
    return (jnp.asarray(tri, dtype=BF16), jnp.asarray(np.stack(wide), dtype=F32),
            jnp.asarray(np.stack(narrow), dtype=F32))


def _seg_bcast(x, h, reverse):
    c, w = x.shape
    off = h if reverse else h - 1
    if 2 * h >= 2 * SUBLANES:
        pieces = [jnp.broadcast_to(x[g * 2 * h + off:g * 2 * h + off + 1, :], (2 * h, w)) for g in range(c // (2 * h))]
        return pieces[0] if len(pieces) == 1 else jnp.concatenate(pieces, axis=0)
    x3 = x.reshape(c // SUBLANES, SUBLANES, w)
    sub = lax.broadcasted_iota(jnp.int32, x3.shape, 1)
    y = None
    for g in range(SUBLANES // (2 * h)):
        piece = jnp.broadcast_to(x3[:, g * 2 * h + off:g * 2 * h + off + 1, :], x3.shape)
        y = piece if y is None else jnp.where(sub >= g * 2 * h, piece, y)
    return y.reshape(c, w)


def _scan_kernel(*refs, reverse, chunk, accumulate):
    if accumulate:
        q_ref, f_ref, v_ref, lb_ref, tri_ref, wmask_ref, nmask_ref, other_ref, o_ref, st_ref = refs
    else:
        q_ref, f_ref, v_ref, lb_ref, tri_ref, wmask_ref, nmask_ref, o_ref, st_ref = refs

    @pl.when(pl.program_id(2) == 0)
    def _():
        st_ref[...] = jnp.zeros_like(st_ref)

    rows, width = q_ref.shape
    levels = _scan_levels(chunk)
    spans = [slice(ci * chunk, (ci + 1) * chunk) for ci in range(rows // chunk)]
    lanes = [slice(hh * HEAD_DIM, (hh + 1) * HEAD_DIM) for hh in range(width // HEAD_DIM)]
    q = q_ref[...]
    fl = f_ref[...]
    lb = lb_ref[...]
    u = jnp.exp(-jnp.abs(fl))
    key = (1.0 - lb) * (jnp.where(fl > 0.0, u, 1.0) / (1.0 + u))
    log_sig = jnp.minimum(fl, 0.0) - jnp.log(1.0 + u)
    x1 = jnp.log(lb)
    x2 = jnp.log1p(-lb) + log_sig
    delta = x1 - x2
    lf = jnp.where(jnp.isnan(delta), x1 + x2, jnp.maximum(x1, x2) + jnp.log(1.0 + jnp.exp(-jnp.abs(delta))))
    lf = lf * LOG2_E
    l1 = lf.astype(BF16)
    r1 = lf - l1.astype(F32)
    l2 = r1.astype(BF16)
    l3 = (r1 - l2.astype(F32)).astype(BF16)
    parts = jnp.concatenate([l1, l2, l3], axis=1)
    tri = tri_ref[...]
    cs = jnp.concatenate([_dot(tri, parts[sp]) for sp in spans], axis=0)
    b = cs[:, :width] + cs[:, width:2 * width] + cs[:, 2 * width:]
    totals = [b[sp.start:sp.start + 1, :] if reverse else b[sp.stop - 1:sp.stop, :] for sp in spans]
    total_rows = jnp.concatenate([jnp.broadcast_to(t, (chunk, width)) for t in totals], axis=0)

    q_in = (q * jnp.exp2(b)).astype(BF16)
    k_out = (key * jnp.exp2(total_rows - b)).astype(BF16)
    v = v_ref[...].astype(BF16)

    order = list(range(len(spans)))[::-1] if reverse else list(range(len(spans)))
    kv = [[_dot_tn(v[sp, sl], k_out[sp, sl]) for sl in lanes] for sp in spans]
    o_state = [[None] * len(lanes) for _ in spans]
    for hh, sl in enumerate(lanes):
        st = st_ref[hh]
        for ci in order:
            o_state[ci][hh] = _dot_nt(q_in[spans[ci], sl], st.astype(BF16))
            st = st * jnp.exp2(totals[ci][:, sl]) + kv[ci][hh]
        st_ref[hh] = st

    n_blk = chunk // SUBLANES
    wide_levels = [h for h in levels if h >= SUBLANES]
    narrow_levels = [h for h in levels if h < SUBLANES]

    def narrow_scores(li, ql, kl):
        out = []
        for sp in spans:
            row = []
            for sl in lanes:
                s = nmask_ref[li] * _dot_nt(ql[sp, sl], kl[sp, sl])
                row.append([s[k * SUBLANES:(k + 1) * SUBLANES, :] for k in range(n_blk)])
            out.append(row)
        return out

    a = narrow_scores(len(narrow_levels), q.astype(BF16), key.astype(BF16))
    row_id = lax.broadcasted_iota(jnp.int32, (rows, width), 0)
    for li, h in enumerate(narrow_levels):
        if h == 1:
            e = jnp.where((row_id & 1) == (0 if reverse else 1), jnp.exp2(lf), 1.0)
        else:
            e = jnp.exp2(-jnp.abs(b - _seg_bcast(b, h, reverse)))
        new = narrow_scores(li, (q * e).astype(BF16), (key * e).astype(BF16))
        a = [[[x + y for x, y in zip(xb, yb)] for xb, yb in zip(xa, ya)] for xa, ya in zip(a, new)]

    for li, h in enumerate(wide_levels):
        q_parts, k_parts = [], []
        for g in range(rows // (2 * h)):
            lo = slice(g * 2 * h, g * 2 * h + h)
            hi = slice(g * 2 * h + h, (g + 1) * 2 * h)
            r = g * 2 * h + (h if reverse else h - 1)
            b_ref = jnp.broadcast_to(b[r:r + 1, :], (h, width))
            q_half, k_half = (lo, hi) if reverse else (hi, lo)
            q_parts.append(q[q_half] * jnp.exp2(b[q_half] - b_ref))
            k_part = key[k_half] * jnp.exp2(b_ref - b[k_half])
            zeros = jnp.zeros((h, width), F32)
            k_parts += [zeros, k_part] if reverse else [k_part, zeros]
        q_sel = jnp.concatenate(q_parts, axis=0).astype(BF16)
        k_hat = jnp.concatenate(k_parts, axis=0).astype(BF16)
        half = chunk // 2
        for ci, sp in enumerate(spans):
            for hh, sl in enumerate(lanes):
                s = wmask_ref[li] * _dot_nt(q_sel[ci * half:(ci + 1) * half, sl], k_hat[sp, sl])
                for j in range(chunk // (2 * h)):
                    first = (j * 2 * h + (0 if reverse else h)) // SUBLANES
                    for k in range(h // SUBLANES):
                        r0 = j * h + k * SUBLANES
                        a[ci][hh][first + k] = a[ci][hh][first + k] + s[r0:r0 + SUBLANES, :]

    for ci, sp in enumerate(spans):
        for hh, sl in enumerate(lanes):
            pairs = jnp.concatenate(a[ci][hh], axis=0).astype(BF16)
            o = o_state[ci][hh] + _dot(pairs, v[sp, sl])
            o_ref[sp, sl] = o + other_ref[sp, sl] if accumulate else o


def _hgrn_scan(p, lower_bound, rows, reverse, other=None):
    b, seq, lc = rows.batch, rows.seq, rows.ctx_len
    d = lower_bound.shape[0]
    c, hb = SCAN_CHUNK, SCAN_HEADS_PER_BLOCK
    r = c * SCAN_CHUNKS_PER_STEP
    wb = hb * HEAD_DIM
    ncb = d // wb
    assert lc % r == 0 and seq % r == 0
    nctx, nlat = lc // r, seq // r
    ctx0 = rows.n_lat // r
    f_blk = (2 if reverse else 1) * ncb
    v_blk = 3 * ncb

    def row(bi, s):
        if reverse:
            return jnp.where(s < nctx, ctx0 + bi * nctx + (nctx - 1 - s), bi * nlat + (nlat - 1 - (s - nctx)))
        return jnp.where(s < nctx, ctx0 + bi * nctx + s, bi * nlat + (s - nctx))

    tri, wmasks, nmasks = _scan_consts(c, reverse)
    o_spec = pl.BlockSpec((r, wb), lambda bi, hi, s: (row(bi, s), hi))
    extra = () if other is None else (other,)
    return pl.pallas_call(
        functools.partial(_scan_kernel, reverse=reverse, chunk=c, accumulate=other is not None),
        out_shape=jax.ShapeDtypeStruct((rows.n, d), F32),
        grid=(b, ncb, nctx + nlat),
        in_specs=[
            pl.BlockSpec((r, wb), lambda bi, hi, s: (row(bi, s), hi)),
            pl.BlockSpec((r, wb), lambda bi, hi, s: (row(bi, s), f_blk + hi)),
            pl.BlockSpec((r, wb), lambda bi, hi, s: (row(bi, s), v_blk + hi)),
            pl.BlockSpec((1, wb), lambda bi, hi, s: (0, hi)),
            pl.BlockSpec(tri.shape, lambda bi, hi, s: (0, 0)),
            pl.BlockSpec(wmasks.shape, lambda bi, hi, s: (0, 0, 0)),
            pl.BlockSpec(nmasks.shape, lambda bi, hi, s: (0, 0, 0)),
        ] + [o_spec] * len(extra),
        out_specs=o_spec,
        scratch_shapes=[pltpu.VMEM((hb, HEAD_DIM, HEAD_DIM), F32)],
        compiler_params=_cparams("parallel", "parallel", "arbitrary"),
        name="hgrn_scan_bw" if reverse else "hgrn_scan_fw",
    )(p, p, p, lower_bound.reshape(1, d), tri, wmasks, nmasks, *extra)


def kernel(x, c, ctx, c_ctx, w_mod, b_mod, norm_gains, ffn_w_in, ffn_w_out, ab_w_in, qk_norm, ab_w_out,
           hgrn_w_in, hgrn_lb_logits, hgrn_o_norm, hgrn_w_out, final_norm):
    batch, seq, d = x.shape
    lc = ctx.shape[1]
    depth = w_mod.shape[0]
    rows = _Rows(batch, seq, lc, ROW_TILE)
    rows_r = _Rows(batch, seq, lc, READOUT_ROW_TILE)
    assert seq % GRID_W == 0 and batch + 1 <= SUBLANES

    c_rows = jnp.concatenate([c_ctx[None, :], c, jnp.zeros((SUBLANES - 1 - batch, d), F32)], axis=0)
    mods_all = _modulation(c_rows, w_mod, b_mod).reshape(depth, SUBLANES, 3, 3, d)

    lb_cum = jnp.cumsum(jax.nn.softmax(hgrn_lb_logits.astype(F32), axis=0), axis=0)
    lower_bounds = lb_cum - lb_cum[0]

    ffn_w_in_b = ffn_w_in.astype(BF16)
    ffn_w_out_b = ffn_w_out.astype(BF16)
    fin = final_norm.reshape(1, d)
    rope = _rope_tables(seq, ROW_TILE)

    h = (x.reshape(batch * seq, d), ctx.reshape(batch * lc, d))
    for layer in range(depth):
        last = layer == depth - 1
        mods = mods_all[layer]
        gains = norm_gains[layer].reshape(3, 1, d)
        h = _ffn(h, mods, 0, gains[0], ffn_w_in_b, ffn_w_out_b, layer, 0, fin, rows, rows.tiles, False)
        if layer % 2 == 0:
            e = layer // 2
            f_all, q_all, k_all, vt_all = _ab_proj(h, mods, gains[1], ab_w_in[e].astype(BF16), qk_norm[e], rope, rows)
            attn = _attention(q_all, k_all, vt_all, rows)
            attn = _attention(q_all, k_all, vt_all, rows, lat_out=attn)
            h = _ab_out(_fourier_latent(f_all, rows), _fourier_ctx(f_all, rows), attn,
                        ab_w_out[e].astype(BF16), h, mods, rows)
        else:
            o = layer // 2
            p = _hgrn_proj(h, mods, gains[1], hgrn_w_in[o].astype(BF16), rows)
            o_fw = _hgrn_scan(p, lower_bounds[layer], rows, False)
            o_sum = _hgrn_scan(p, lower_bounds[layer], rows, True, other=o_fw)
            gain_o = jnp.tile(hgrn_o_norm[o], d // HEAD_DIM).reshape(1, d)
            n_t = rows_r.lat_tiles if last else rows_r.tiles
            h = _hgrn_out(o_sum, p, gain_o, hgrn_w_out[o].astype(BF16), h, mods, rows_r, n_t)
        n_t = rows.lat_tiles if last else rows.tiles
        h = _ffn(h, mods, 2, gains[2], ffn_w_in_b, ffn_w_out_b, layer, 1, fin, rows, n_t, last)
    return h[:batch * seq].reshape(batch, seq, d)
```

```python
import functools
import math

import jax
import jax.numpy as jnp
import numpy as np
from jax import lax
from jax.experimental import pallas as pl
from jax.experimental.pallas import tpu as pltpu

F32 = jnp.float32
BF16 = jnp.bfloat16

EPS = 1e-6
N_MOD = 9
HEAD_DIM = 128
N_KV_HEADS = 4
FOURIER_WIDTH = 512
FOURIER_GROUP_DIM = 128
GRID_W = 64
ROPE_THETA = 10000.0
ROPE_AXIS_DIM = HEAD_DIM // 2
ATTN_SCALE = HEAD_DIM ** -0.5
LOG2_E = math.log2(math.e)

LANES = 128
SUBLANES = 8
BF16_SUBLANES = 16
VT_ROWS = HEAD_DIM + BF16_SUBLANES
VMEM_LIMIT_BYTES = 56 * 1024 * 1024

ROW_TILE = 512
READOUT_ROW_TILE = 512
FFN_TILE = 512
PROJ_COL_TILE = 2048
PROJ_ROW_GROUP = 3
MOD_COL_TILE = 1024
ATTN_Q_TILE = 1024
SCAN_CHUNK = 128
SCAN_HEADS_PER_BLOCK = 8
SCAN_CHUNKS_PER_STEP = 2
FFT_B = 128
FFT_B_BLOCK = 8


def _cparams(*sem):
    return pltpu.CompilerParams(dimension_semantics=sem, vmem_limit_bytes=VMEM_LIMIT_BYTES)


def _dot(a, b):
    return jnp.dot(a, b, preferred_element_type=F32)


def _dot_hi(a, b):
    return jnp.dot(a, b, preferred_element_type=F32, precision=lax.Precision.HIGHEST)


def _split_bf16(x):
    hi = x.astype(BF16)
    return hi, (x - hi.astype(F32)).astype(BF16)


def _dot_split(a, b):
    (ah, al), (bh, bl) = a, b
    return _dot(ah, bh) + _dot(al, bh) + _dot(ah, bl)


def _dot_nt(a, b):
    return lax.dot_general(a, b, (((1,), (1,)), ((), ())), preferred_element_type=F32)


def _dot_tn(a, b):
    return lax.dot_general(a, b, (((0,), (0,)), ((), ())), preferred_element_type=F32)


def _silu(x):
    return x * jax.nn.sigmoid(x)


def _rms(x, gain):
    return x * lax.rsqrt(jnp.mean(x * x, axis=-1, keepdims=True) + EPS) * gain


def _ada(h, gain, shift, scale):
    return _rms(h, gain) * (1.0 + scale) + shift


def _mod_kernel(c_ref, w_ref, b_ref, o_ref):
    a = _silu(c_ref[...]).astype(BF16)
    o_ref[...] = _dot(a, w_ref[...].astype(BF16)) + b_ref[...]


def _modulation(c_rows, w_mod, b_mod):
    depth, d, nd = w_mod.shape
    tn = MOD_COL_TILE
    return pl.pallas_call(
        _mod_kernel,
        out_shape=jax.ShapeDtypeStruct((depth, SUBLANES, nd), F32),
        grid=(depth, nd // tn),
        in_specs=[
            pl.BlockSpec((SUBLANES, d), lambda l, j: (0, 0)),
            pl.BlockSpec((None, d, tn), lambda l, j: (l, 0, j)),
            pl.BlockSpec((None, 1, tn), lambda l, j: (l, 0, j)),
        ],
        out_specs=pl.BlockSpec((None, SUBLANES, tn), lambda l, j: (l, 0, j)),
        compiler_params=_cparams("parallel", "arbitrary"),
        name="modulation",
    )(c_rows, w_mod, b_mod.reshape(depth, 1, nd))


class _Rows:
    def __init__(self, batch, seq, ctx_len, tile):
        assert seq % tile == 0 and (batch * ctx_len) % tile == 0
        self.batch, self.seq, self.ctx_len, self.tile = batch, seq, ctx_len, tile
        self.n_lat = batch * seq
        self.n = self.n_lat + batch * ctx_len
        self.lat_tiles = self.n_lat // tile
        self.tiles = self.n // tile
        self.tiles_per_batch = seq // tile

    def group(self, i):
        return jnp.where(i < self.lat_tiles, 1 + i // self.tiles_per_batch, 0)


def _ffn_kernel(*refs, final, split_tiles):
    if split_tiles is None:
        h_ref, mod_ref, gain_ref, wa_ref, wb_ref, wo_ref, fin_ref, o_ref, xn_ref, acc_ref = refs
        load_h = lambda: h_ref[...]
    else:
        hl_ref, hc_ref, mod_ref, gain_ref, wa_ref, wb_ref, wo_ref, fin_ref, o_ref, xn_ref, acc_ref = refs
        load_h = lambda: jnp.where(pl.program_id(0) < split_tiles, hl_ref[...], hc_ref[...])
    i = pl.program_id(0)
    j = pl.program_id(1)

    @pl.when(j == 0)
    def _():
        xn_ref[...] = _ada(load_h(), gain_ref[...], mod_ref[0:1, :], mod_ref[1:2, :]).astype(BF16)

    @pl.when(jnp.logical_and(i == 0, j == 0))
    def _():
        acc_ref[...] = jnp.zeros_like(acc_ref)

    xn = xn_ref[...]
    a = _dot(xn, wa_ref[...])
    b = _dot(xn, wb_ref[...])
    g = (_silu(a) * b).astype(BF16)
    acc_ref[...] = jnp.where(j == 0, 0.0, acc_ref[...]) + _dot(g, wo_ref[...])

    @pl.when(j == pl.num_programs(1) - 1)
    def _():
        out = load_h() + 0.5 * mod_ref[2:3, :] * acc_ref[...]
        if final:
            out = _rms(out, fin_ref[...])
        o_ref[...] = out


def _ffn(h, mods, sub, gain, w_in, w_out, layer, which, fin, rows, n_tiles, final):
    f, d = w_out.shape[2:]
    tm, tf = rows.tile, FFN_TILE
    nf = f // tf
    if isinstance(h, tuple):
        lt = rows.lat_tiles
        row_specs = [pl.BlockSpec((tm, d), lambda i, j: (jnp.minimum(i, lt - 1), 0)),
                     pl.BlockSpec((tm, d), lambda i, j: (jnp.maximum(i - lt, 0), 0))]
        row_args, split_tiles = h, lt
    else:
        row_specs = [pl.BlockSpec((tm, d), lambda i, j: (i, 0))]
        row_args, split_tiles = (h,), None
    return pl.pallas_call(
        functools.partial(_ffn_kernel, final=final, split_tiles=split_tiles),
        out_shape=jax.ShapeDtypeStruct((n_tiles * tm, d), F32),
        grid=(n_tiles, nf),
        in_specs=row_specs + [
            pl.BlockSpec((None, None, 3, d), lambda i, j: (rows.group(i), sub, 0, 0)),
            pl.BlockSpec((1, d), lambda i, j: (0, 0)),
            pl.BlockSpec((None, None, d, tf), lambda i, j: (layer, which, 0, j)),
            pl.BlockSpec((None, None, d, tf), lambda i, j: (layer, which, 0, nf + j)),
            pl.BlockSpec((None, None, tf, d), lambda i, j: (layer, which, j, 0)),
            pl.BlockSpec((1, d), lambda i, j: (0, 0)),
        ],
        out_specs=pl.BlockSpec((tm, d), lambda i, j: (i, 0)),
        scratch_shapes=[pltpu.VMEM((tm, d), BF16), pltpu.VMEM((tm, d), F32)],
        compiler_params=_cparams("arbitrary", "arbitrary"),
        name="ffn",
    )(*row_args, mods, gain, w_in, w_in, w_out, fin)


def _rope_tables(seq, tile):
    t = np.arange(seq)
    inv_freq = ROPE_THETA ** (-np.arange(0, ROPE_AXIS_DIM, 2, dtype=np.float64) / ROPE_AXIS_DIM)
    ang = np.concatenate([(t // GRID_W)[:, None] * inv_freq, (t % GRID_W)[:, None] * inv_freq], axis=-1)
    nf = ROPE_AXIS_DIM // 2
    cos = np.cos(ang).reshape(seq, 2, 1, nf)
    sin = np.sin(ang).reshape(seq, 2, 1, nf)
    zero = np.zeros_like(sin)
    c_full = np.broadcast_to(cos, (seq, 2, 2, nf)).reshape(seq, HEAD_DIM)
    s_up = np.concatenate([-sin, zero], axis=2).reshape(seq, HEAD_DIM)
    s_dn = np.concatenate([zero, sin], axis=2).reshape(seq, HEAD_DIM)
    lat = np.concatenate([c_full, s_up, s_dn], axis=1)
    ident = np.concatenate([np.ones((tile, HEAD_DIM)), np.zeros((tile, 2 * HEAD_DIM))], axis=1)
    return jnp.asarray(np.concatenate([lat, ident], axis=0), dtype=F32)


def _norm_rope_heads(acc, gain, rope, post_scale=None):
    nf = ROPE_AXIS_DIM // 2
    c, s_up, s_dn = rope[:, :HEAD_DIM], rope[:, HEAD_DIM:2 * HEAD_DIM], rope[:, 2 * HEAD_DIM:]
    heads = []
    for hh in range(acc.shape[1] // HEAD_DIM):
        y = _rms(acc[:, hh * HEAD_DIM:(hh + 1) * HEAD_DIM], gain)
        y = y * c + pltpu.roll(y, HEAD_DIM - nf, 1) * s_up + pltpu.roll(y, nf, 1) * s_dn
        heads.append(y if post_scale is None else y * post_scale)
    return jnp.concatenate(heads, axis=1).astype(BF16)


def _ab_proj_kernel(h_ref, mod_ref, gain_ref, w_ref, qkn_ref, rope_ref, f_ref, q_ref, k_ref, vt_ref):
    xn = _ada(h_ref[...], gain_ref[...], mod_ref[0:1, :], mod_ref[1:2, :]).astype(BF16)
    c0 = f_ref.shape[1]
    c1 = c0 + q_ref.shape[1]
    c2 = c1 + k_ref.shape[1]
    q_ref[...] = _norm_rope_heads(_dot(xn, w_ref[:, c0:c1]), qkn_ref[0:1, :], rope_ref[...], ATTN_SCALE * LOG2_E)
    k_ref[...] = _norm_rope_heads(_dot(xn, w_ref[:, c1:c2]), qkn_ref[1:2, :], rope_ref[...])
    v = _dot(xn, w_ref[:, c2:])
    ones = jnp.ones((vt_ref.shape[1] - HEAD_DIM, vt_ref.shape[2]), BF16)
    for hh in range(N_KV_HEADS):
        vt_ref[hh, :HEAD_DIM, :] = v[:, hh * HEAD_DIM:(hh + 1) * HEAD_DIM].T.astype(BF16)
        vt_ref[hh, HEAD_DIM:, :] = ones
    f_ref[...] = _dot(xn, w_ref[:, :c0])


def _ab_proj(h, mods, gain, w_in, qk_norm, rope, rows):
    d = h.shape[1]
    tm = rows.tile
    kv_width = N_KV_HEADS * HEAD_DIM
    q_width = w_in.shape[1] - FOURIER_WIDTH - 2 * kv_width
    n = rows.n
    rope_blk = lambda i: (jnp.where(i < rows.lat_tiles, i % rows.tiles_per_batch, rows.tiles_per_batch), 0)
    return pl.pallas_call(
        _ab_proj_kernel,
        out_shape=(
            jax.ShapeDtypeStruct((n, FOURIER_WIDTH), F32),
            jax.ShapeDtypeStruct((n, q_width), BF16),
            jax.ShapeDtypeStruct((n, kv_width), BF16),
            jax.ShapeDtypeStruct((N_KV_HEADS, rows.tiles, VT_ROWS, tm), BF16),
        ),
        grid=(rows.tiles,),
        in_specs=[
            pl.BlockSpec((tm, d), lambda i: (i, 0)),
            pl.BlockSpec((None, None, 3, d), lambda i: (rows.group(i), 1, 0, 0)),
            pl.BlockSpec((1, d), lambda i: (0, 0)),
            pl.BlockSpec(w_in.shape, lambda i: (0, 0)),
            pl.BlockSpec((2, HEAD_DIM), lambda i: (0, 0)),
            pl.BlockSpec((tm, 3 * HEAD_DIM), rope_blk),
        ],
        out_specs=(
            pl.BlockSpec((tm, FOURIER_WIDTH), lambda i: (i, 0)),
            pl.BlockSpec((tm, q_width), lambda i: (i, 0)),
            pl.BlockSpec((tm, kv_width), lambda i: (i, 0)),
            pl.BlockSpec((N_KV_HEADS, None, VT_ROWS, tm), lambda i: (0, i, 0, 0)),
        ),
        compiler_params=_cparams("parallel"),
        name="ab_proj",
    )(h, mods, gain, w_in, qk_norm, rope)


def _attn_kernel(*refs, lat_chunks, group):
    if lat_chunks:
        q_ref, kc_ref, vtc_ref, kl_ref, vtl_ref, o_ref, acc_ref, s_ref = refs
    else:
        q_ref, kc_ref, vtc_ref, o_ref, acc_ref = refs
    tq = q_ref.shape[0]
    q = q_ref[...]
    qs = jnp.concatenate([q[:, g * HEAD_DIM:(g + 1) * HEAD_DIM] for g in range(group)], axis=0)
    nq = group * tq
    acc_ref[...] = jnp.zeros_like(acc_ref)

    def scores(k):
        return _dot_nt(k, qs)

    def update(s, vt, m):
        m_new = jnp.maximum(m, jnp.max(s, axis=0, keepdims=True))
        p = jnp.exp2(s - m_new).astype(BF16)
        acc_ref[...] = jnp.exp2(m - m_new) * acc_ref[...] + _dot(vt, p)
        return m_new

    m = update(scores(kc_ref[...]), vtc_ref[...], jnp.full((1, nq), -jnp.inf, F32))
    if lat_chunks:
        tk = vtl_ref.shape[2]

        def lat_scores(c):
            return scores(kl_ref[pl.ds(pl.multiple_of(c * tk, tk), tk), :])

        assert lat_chunks % 2 == 0
        s_ref[0] = lat_scores(0)

        def body(i, m):
            c = 2 * i
            s_ref[1] = lat_scores(c + 1)
            m = update(s_ref[0], vtl_ref[c], m)
            s_ref[0] = lat_scores(jnp.minimum(c + 2, lat_chunks - 1))
            return update(s_ref[1], vtl_ref[c + 1], m)

        m = lax.fori_loop(0, lat_chunks // 2, body, m)

    out = (acc_ref[:HEAD_DIM, :] / acc_ref[HEAD_DIM:HEAD_DIM + 1, :]).T
    o_ref[...] = jnp.concatenate([out[g * tq:(g + 1) * tq, :] for g in range(group)], axis=1).astype(BF16)


def _attention(q_all, k_all, vt_all, rows, latent):
    b, seq, lc, tile = rows.batch, rows.seq, rows.ctx_len, rows.tile
    group = q_all.shape[1] // (N_KV_HEADS * HEAD_DIM)
    gw = group * HEAD_DIM
    assert tile % lc == 0 and seq % tile == 0
    ctx_blk0 = rows.n_lat // lc
    ctx_tile = lambda bi: ((rows.n_lat + bi * lc) // tile, 0, ((rows.n_lat + bi * lc) % tile) // lc)
    kc_spec = pl.BlockSpec((lc, HEAD_DIM), lambda bi, hi, i: (ctx_blk0 + bi, hi))
    vtc_spec = pl.BlockSpec((None, None, VT_ROWS, lc), lambda bi, hi, i: (hi,) + ctx_tile(bi))
    if latent:
        tq = ATTN_Q_TILE
        assert tq % tile == 0
        nqt = seq // tq
        lat_chunks = seq // tile
        q_spec = pl.BlockSpec((tq, gw), lambda bi, hi, i: (bi * nqt + i, hi))
        kl_spec = pl.BlockSpec((seq, HEAD_DIM), lambda bi, hi, i: (bi, hi))
        vtl_spec = pl.BlockSpec((None, lat_chunks, VT_ROWS, tile), lambda bi, hi, i: (hi, bi, 0, 0))
        in_specs = [q_spec, kc_spec, vtc_spec, kl_spec, vtl_spec]
        args = (q_all, k_all, vt_all, k_all, vt_all)
    else:
        tq = lc
        nqt = 1
        lat_chunks = 0
        q_spec = pl.BlockSpec((tq, gw), lambda bi, hi, i: (ctx_blk0 + bi, hi))
        in_specs = [q_spec, kc_spec, vtc_spec]
        args = (q_all, k_all, vt_all)
    return pl.pallas_call(
        functools.partial(_attn_kernel, lat_chunks=lat_chunks, group=group),
        out_shape=jax.ShapeDtypeStruct((b * nqt * tq, q_all.shape[1]), BF16),
        grid=(b, N_KV_HEADS, nqt),
        in_specs=in_specs,
        out_specs=pl.BlockSpec((tq, gw), lambda bi, hi, i: (bi * nqt + i, hi)),
        scratch_shapes=[pltpu.VMEM((VT_ROWS, group * tq), F32)]
        + ([pltpu.VMEM((2, tile, group * tq), F32)] if lat_chunks else []),
        compiler_params=_cparams("parallel", "parallel", "arbitrary"),
        name="attention_ctx" if lat_chunks == 0 else "attention_lat",
    )(*args)


def _dft_cs(n):
    idx = np.arange(n)
    ang = 2.0 * np.pi * ((idx[:, None] * idx[None, :]) % n) / n
    return np.cos(ang), np.sin(ang)


def _const_split(m):
    m = np.asarray(m, dtype=np.float32)
    hi = jnp.asarray(m).astype(BF16)
    lo = (jnp.asarray(m) - hi.astype(F32)).astype(BF16)
    return jnp.stack([hi, lo])


def _fft1_kernel(x_ref, fa_ref, tw_ref, o_ref, *, a):
    w = o_ref.shape[2]
    for r in range(o_ref.shape[0]):
        z = _dot_split((fa_ref[0], fa_ref[1]), _split_bf16(x_ref[:, r * w:(r + 1) * w]))
        zr, zi = z[:a], z[a:]
        tc = jnp.concatenate([tw_ref[r, 0]] * (w // LANES), axis=1)
        ts = jnp.concatenate([tw_ref[r, 1]] * (w // LANES), axis=1)
        o_ref[r, :a, :] = zr * tc - zi * ts
        o_ref[r, a:, :] = zr * ts + zi * tc


def _fft2_kernel(zr_ref, zi_ref, m2_ref, mc_ref, o_ref, *, scale):
    for r in range(o_ref.shape[1]):
        z = jnp.concatenate([zr_ref[:, r, :], zi_ref[:, r, :]], axis=0)
        v = _dot_split((m2_ref[0], m2_ref[1]), _split_bf16(z))
        vr, vi = v[:FFT_B], v[FFT_B:]
        outs = []
        for g in range(o_ref.shape[2] // LANES):
            u = jnp.concatenate([vr[:, g * LANES:(g + 1) * LANES], vi[:, g * LANES:(g + 1) * LANES]], axis=1)
            outs.append(_dot_split(_split_bf16(u), (mc_ref[0], mc_ref[1])))
        o_ref[:, r, :] = jnp.concatenate(outs, axis=1) * scale


def _fourier_latent(f_all, rows):
    b, seq = rows.batch, rows.seq
    w = FOURIER_WIDTH
    a = seq // FFT_B
    assert a % SUBLANES == 0 and FOURIER_GROUP_DIM == LANES
    ca, sa = _dft_cs(a)
    fa = _const_split(np.concatenate([ca, sa], axis=0))
    p1b = (np.arange(a)[None, :] * np.arange(FFT_B)[:, None]) % seq
    ang = 2.0 * np.pi * p1b / seq
    tw = np.stack([np.cos(ang), np.sin(ang)], axis=1)[..., None]
    tw = jnp.asarray(np.broadcast_to(tw, (FFT_B, 2, a, LANES)), dtype=F32)
    cb, sb = _dft_cs(FFT_B)
    m2 = _const_split(np.block([[cb, -sb], [sb, cb]]))
    cc, sc = _dft_cs(FOURIER_GROUP_DIM)
    mc = _const_split(np.concatenate([cc, -sc], axis=0))

    blk = FFT_B_BLOCK
    x2 = f_all.reshape(rows.n // FFT_B, FFT_B * w)
    z = pl.pallas_call(
        functools.partial(_fft1_kernel, a=a),
        out_shape=jax.ShapeDtypeStruct((b, FFT_B, 2 * a, w), F32),
        grid=(b, FFT_B // blk),
        in_specs=[
            pl.BlockSpec((a, blk * w), lambda bi, j: (bi, j)),
            pl.BlockSpec((2, 2 * a, a), lambda bi, j: (0, 0, 0)),
            pl.BlockSpec((blk, 2, a, LANES), lambda bi, j: (j, 0, 0, 0)),
        ],
        out_specs=pl.BlockSpec((None, blk, 2 * a, w), lambda bi, j: (bi, j, 0, 0)),
        compiler_params=_cparams("parallel", "parallel"),
        name="fourier_stage1",
    )(x2, fa, tw)

    nblk = a // blk
    y = pl.pallas_call(
        functools.partial(_fft2_kernel, scale=1.0 / math.sqrt(seq * FOURIER_GROUP_DIM)),
        out_shape=jax.ShapeDtypeStruct((b, FFT_B, a, w), F32),
        grid=(b, nblk),
        in_specs=[
            pl.BlockSpec((None, FFT_B, blk, w), lambda bi, j: (bi, 0, j, 0)),
            pl.BlockSpec((None, FFT_B, blk, w), lambda bi, j: (bi, 0, nblk + j, 0)),
            pl.BlockSpec((2, 2 * FFT_B, 2 * FFT_B), lambda bi, j: (0, 0, 0)),
            pl.BlockSpec((2, 2 * LANES, LANES), lambda bi, j: (0, 0, 0)),
        ],
        out_specs=pl.BlockSpec((None, FFT_B, blk, w), lambda bi, j: (bi, 0, j, 0)),
        compiler_params=_cparams("parallel", "parallel"),
        name="fourier_stage2",
    )(z, z, m2, mc)
    return y.reshape(b * seq, w)


def _dft_ctx_kernel(x_ref, cn_ref, sn_ref, cc_ref, sc_ref, o_ref, *, scale):
    x = x_ref[...]
    outs = []
    for g in range(x.shape[1] // LANES):
        xg = x[:, g * LANES:(g + 1) * LANES]
        outs.append(_dot_hi(cn_ref[...], _dot_hi(xg, cc_ref[...])) - _dot_hi(sn_ref[...], _dot_hi(xg, sc_ref[...])))
    o_ref[...] = jnp.concatenate(outs, axis=1) * scale


def _fourier_ctx(f_all, rows):
    b, lc = rows.batch, rows.ctx_len
    w = FOURIER_WIDTH
    cn, sn = _dft_cs(lc)
    cc, sc = _dft_cs(FOURIER_GROUP_DIM)
    blk0 = rows.n_lat // lc
    mat = lambda m: pl.BlockSpec(m.shape, lambda bi: (0, 0))
    consts = [jnp.asarray(m, dtype=F32) for m in (cn, sn, cc, sc)]
    return pl.pallas_call(
        functools.partial(_dft_ctx_kernel, scale=1.0 / math.sqrt(lc * FOURIER_GROUP_DIM)),
        out_shape=jax.ShapeDtypeStruct((b * lc, w), F32),
        grid=(b,),
        in_specs=[pl.BlockSpec((lc, w), lambda bi: (blk0 + bi, 0))] + [mat(m) for m in consts],
        out_specs=pl.BlockSpec((lc, w), lambda bi: (bi, 0)),
        compiler_params=_cparams("parallel"),
        name="fourier_ctx",
    )(f_all, *consts)


def _ab_out_kernel(x1l_ref, x1c_ref, x2l_ref, x2c_ref, w_ref, h_ref, mod_ref, o_ref, *, lat_tiles):
    is_lat = pl.program_id(0) < lat_tiles
    x1 = jnp.where(is_lat, x1l_ref[...], x1c_ref[...])
    x2 = jnp.where(is_lat, x2l_ref[...], x2c_ref[...])
    x = jnp.concatenate([x1.astype(BF16), x2], axis=1)
    o_ref[...] = h_ref[...] + mod_ref[2:3, :] * _dot(x, w_ref[...])


def _ab_out(x1_lat, x1_ctx, x2_lat, x2_ctx, w_out, h, mods, rows):
    d = h.shape[1]
    tm = rows.tile
    lt = rows.lat_tiles
    lat_blk = lambda i: (jnp.minimum(i, lt - 1), 0)
    ctx_blk = lambda i: (jnp.maximum(i - lt, 0), 0)
    return pl.pallas_call(
        functools.partial(_ab_out_kernel, lat_tiles=lt),
        out_shape=jax.ShapeDtypeStruct((rows.n, d), F32),
        grid=(rows.tiles,),
        in_specs=[
            pl.BlockSpec((tm, x1_lat.shape[1]), lat_blk),
            pl.BlockSpec((tm, x1_ctx.shape[1]), ctx_blk),
            pl.BlockSpec((tm, x2_lat.shape[1]), lat_blk),
            pl.BlockSpec((tm, x2_ctx.shape[1]), ctx_blk),
            pl.BlockSpec(w_out.shape, lambda i: (0, 0)),
            pl.BlockSpec((tm, d), lambda i: (i, 0)),
            pl.BlockSpec((None, None, 3, d), lambda i: (rows.group(i), 1, 0, 0)),
        ],
        out_specs=pl.BlockSpec((tm, d), lambda i: (i, 0)),
        compiler_params=_cparams("parallel"),
        name="ab_out",
    )(x1_lat, x1_ctx, x2_lat, x2_ctx, w_out, h, mods)


def _hgrn_out_kernel(osum_ref, g_ref, gain_ref, w_ref, h_ref, mod_ref, o_ref):
    o = osum_ref[...]
    parts = []
    for hh in range(o.shape[1] // HEAD_DIM):
        oh = o[:, hh * HEAD_DIM:(hh + 1) * HEAD_DIM]
        parts.append(oh * lax.rsqrt(jnp.mean(oh * oh, axis=-1, keepdims=True) + EPS))
    on = jnp.concatenate(parts, axis=1) * gain_ref[...]
    y = (on * jax.nn.sigmoid(g_ref[...])).astype(BF16)
    o_ref[...] = h_ref[...] + mod_ref[2:3, :] * _dot(y, w_ref[...])


def _hgrn_out(o_sum, p, gain, w_out, h, mods, rows_r, n_tiles):
    d = h.shape[1]
    tm = rows_r.tile
    g_blk = (p.shape[1] - d) // d
    return pl.pallas_call(
        _hgrn_out_kernel,
        out_shape=jax.ShapeDtypeStruct((n_tiles * tm, d), F32),
        grid=(n_tiles,),
        in_specs=[
            pl.BlockSpec((tm, d), lambda i: (i, 0)),
            pl.BlockSpec((tm, d), lambda i: (i, g_blk)),
            pl.BlockSpec((1, d), lambda i: (0, 0)),
            pl.BlockSpec(w_out.shape, lambda i: (0, 0)),
            pl.BlockSpec((tm, d), lambda i: (i, 0)),
            pl.BlockSpec((None, None, 3, d), lambda i: (rows_r.group(i), 1, 0, 0)),
        ],
        out_specs=pl.BlockSpec((tm, d), lambda i: (i, 0)),
        compiler_params=_cparams("parallel"),
        name="hgrn_out",
    )(o_sum, p, gain, w_out, h, mods)


def _hgrn_proj_kernel(h_ref, mod_ref, gain_ref, w_ref, o_ref, xn_ref, *, n_silu):
    j = pl.program_id(1)
    r = pl.program_id(2)

    @pl.when(j == 0)
    def _():
        xn_ref[r] = _ada(h_ref[...], gain_ref[...], mod_ref[0:1, :], mod_ref[1:2, :]).astype(BF16)

    acc = _dot(xn_ref[r], w_ref[...])
    o_ref[...] = jnp.where(j < n_silu, _silu(acc), acc)


def _hgrn_proj(h, mods, gain, w_in, rows):
    d = h.shape[1]
    tm, tn = rows.tile, PROJ_COL_TILE
    nw = w_in.shape[1]
    gs = next(k for k in (PROJ_ROW_GROUP, 2, 1) if rows.tiles % k == 0)
    tile = lambda g, r: g * gs + r
    return pl.pallas_call(
        functools.partial(_hgrn_proj_kernel, n_silu=d // tn),
        out_shape=jax.ShapeDtypeStruct((rows.n, nw), F32),
        grid=(rows.tiles // gs, nw // tn, gs),
        in_specs=[
            pl.BlockSpec((tm, d), lambda g, j, r: (jnp.where(j == 0, tile(g, r), tile(g, gs - 1)), 0)),
            pl.BlockSpec((None, None, 3, d), lambda g, j, r: (rows.group(tile(g, r)), 1, 0, 0)),
            pl.BlockSpec((1, d), lambda g, j, r: (0, 0)),
            pl.BlockSpec((d, tn), lambda g, j, r: (0, j)),
        ],
        out_specs=pl.BlockSpec((tm, tn), lambda g, j, r: (tile(g, r), j)),
        scratch_shapes=[pltpu.VMEM((gs, tm, d), BF16)],
        compiler_params=_cparams("arbitrary", "arbitrary", "arbitrary"),
        name="hgrn_proj",
    )(h, mods, gain, w_in)


def _scan_levels(chunk):
    return [chunk >> (i + 1) for i in range(int(math.log2(chunk)))]


def _scan_consts(chunk, reverse):
    t = np.arange(chunk)[:, None]
    s = np.arange(chunk)[None, :]
    tri = (s >= t) if reverse else (s <= t)
    wide, narrow = [], []
    for h in _scan_levels(chunk):
        same = (t // (2 * h)) == (s // (2 * h))
        t_up = (t // h) % 2 == 1
        s_up = (s // h) % 2 == 1
        mask = same & ((~t_up & s_up) if reverse else (t_up & ~s_up))
        if h >= SUBLANES:
            wide.append(mask[np.nonzero(~t_up[:, 0] if reverse else t_up[:, 0])[0]])
        else:
            narrow.append(mask)
    narrow.append(t == s)
    return (jnp.asarray(tri, dtype=BF16), jnp.asarray(np.stack(wide), dtype=F32),
            jnp.asarray(np.stack(narrow), dtype=F32))


def _seg_bcast(x, h, reverse):
    c, w = x.shape
    off = h if reverse else h - 1
    if 2 * h >= 2 * SUBLANES:
        pieces = [jnp.broadcast_to(x[g * 2 * h + off:g * 2 * h + off + 1, :], (2 * h, w)) for g in range(c // (2 * h))]
        return pieces[0] if len(pieces) == 1 else jnp.concatenate(pieces, axis=0)
    x3 = x.reshape(c // SUBLANES, SUBLANES, w)
    sub = lax.broadcasted_iota(jnp.int32, x3.shape, 1)
    y = None
    for g in range(SUBLANES // (2 * h)):
        piece = jnp.broadcast_to(x3[:, g * 2 * h + off:g * 2 * h + off + 1, :], x3.shape)
        y = piece if y is None else jnp.where(sub >= g * 2 * h, piece, y)
    return y.reshape(c, w)


def _scan_kernel(*refs, reverse, chunk, accumulate):
    if accumulate:
        q_ref, f_ref, v_ref, lb_ref, tri_ref, wmask_ref, nmask_ref, other_ref, o_ref, st_ref = refs
    else:
        q_ref, f_ref, v_ref, lb_ref, tri_ref, wmask_ref, nmask_ref, o_ref, st_ref = refs

    @pl.when(pl.program_id(2) == 0)
    def _():
        st_ref[...] = jnp.zeros_like(st_ref)

    rows, width = q_ref.shape
    levels = _scan_levels(chunk)
    spans = [slice(ci * chunk, (ci + 1) * chunk) for ci in range(rows // chunk)]
    lanes = [slice(hh * HEAD_DIM, (hh + 1) * HEAD_DIM) for hh in range(width // HEAD_DIM)]
    q = q_ref[...]
    fl = f_ref[...]
    lb = lb_ref[...]
    u = jnp.exp(-jnp.abs(fl))
    key = (1.0 - lb) * (jnp.where(fl > 0.0, u, 1.0) / (1.0 + u))
    log_sig = jnp.minimum(fl, 0.0) - jnp.log(1.0 + u)
    x1 = jnp.log(lb)
    x2 = jnp.log1p(-lb) + log_sig
    delta = x1 - x2
    lf = jnp.where(jnp.isnan(delta), x1 + x2, jnp.maximum(x1, x2) + jnp.log(1.0 + jnp.exp(-jnp.abs(delta))))
    lf = lf * LOG2_E
    l1 = lf.astype(BF16)
    r1 = lf - l1.astype(F32)
    l2 = r1.astype(BF16)
    l3 = (r1 - l2.astype(F32)).astype(BF16)
    parts = jnp.concatenate([l1, l2, l3], axis=1)
    tri = tri_ref[...]
    cs = jnp.concatenate([_dot(tri, parts[sp]) for sp in spans], axis=0)
    b = cs[:, :width] + cs[:, width:2 * width] + cs[:, 2 * width:]
    totals = [b[sp.start:sp.start + 1, :] if reverse else b[sp.stop - 1:sp.stop, :] for sp in spans]
    total_rows = jnp.concatenate([jnp.broadcast_to(t, (chunk, width)) for t in totals], axis=0)

    q_in = (q * jnp.exp2(b)).astype(BF16)
    k_out = (key * jnp.exp2(total_rows - b)).astype(BF16)
    v = v_ref[...].astype(BF16)

    order = list(range(len(spans)))[::-1] if reverse else list(range(len(spans)))
    kv = [[_dot_tn(v[sp, sl], k_out[sp, sl]) for sl in lanes] for sp in spans]
    o_state = [[None] * len(lanes) for _ in spans]
    for hh, sl in enumerate(lanes):
        st = st_ref[hh]
        for ci in order:
            o_state[ci][hh] = _dot_nt(q_in[spans[ci], sl], st.astype(BF16))
            st = st * jnp.exp2(totals[ci][:, sl]) + kv[ci][hh]
        st_ref[hh] = st

    n_blk = chunk // SUBLANES
    wide_levels = [h for h in levels if h >= SUBLANES]
    narrow_levels = [h for h in levels if h < SUBLANES]

    def narrow_scores(li, ql, kl):
        out = []
        for sp in spans:
            row = []
            for sl in lanes:
                s = nmask_ref[li] * _dot_nt(ql[sp, sl], kl[sp, sl])
                row.append([s[k * SUBLANES:(k + 1) * SUBLANES, :] for k in range(n_blk)])
            out.append(row)
        return out

    a = narrow_scores(len(narrow_levels), q.astype(BF16), key.astype(BF16))
    row_id = lax.broadcasted_iota(jnp.int32, (rows, width), 0)
    for li, h in enumerate(narrow_levels):
        if h == 1:
            e = jnp.where((row_id & 1) == (0 if reverse else 1), jnp.exp2(lf), 1.0)
        else:
            e = jnp.exp2(-jnp.abs(b - _seg_bcast(b, h, reverse)))
        new = narrow_scores(li, (q * e).astype(BF16), (key * e).astype(BF16))
        a = [[[x + y for x, y in zip(xb, yb)] for xb, yb in zip(xa, ya)] for xa, ya in zip(a, new)]

    for li, h in enumerate(wide_levels):
        q_parts, k_parts = [], []
        for g in range(rows // (2 * h)):
            lo = slice(g * 2 * h, g * 2 * h + h)
            hi = slice(g * 2 * h + h, (g + 1) * 2 * h)
            r = g * 2 * h + (h if reverse else h - 1)
            b_ref = jnp.broadcast_to(b[r:r + 1, :], (h, width))
            q_half, k_half = (lo, hi) if reverse else (hi, lo)
            q_parts.append(q[q_half] * jnp.exp2(b[q_half] - b_ref))
            k_part = key[k_half] * jnp.exp2(b_ref - b[k_half])
            zeros = jnp.zeros((h, width), F32)
            k_parts += [zeros, k_part] if reverse else [k_part, zeros]
        q_sel = jnp.concatenate(q_parts, axis=0).astype(BF16)
        k_hat = jnp.concatenate(k_parts, axis=0).astype(BF16)
        half = chunk // 2
        for ci, sp in enumerate(spans):
            for hh, sl in enumerate(lanes):
                s = wmask_ref[li] * _dot_nt(q_sel[ci * half:(ci + 1) * half, sl], k_hat[sp, sl])
                for j in range(chunk // (2 * h)):
                    first = (j * 2 * h + (0 if reverse else h)) // SUBLANES
                    for k in range(h // SUBLANES):
                        r0 = j * h + k * SUBLANES
                        a[ci][hh][first + k] = a[ci][hh][first + k] + s[r0:r0 + SUBLANES, :]

    for ci, sp in enumerate(spans):
        for hh, sl in enumerate(lanes):
            pairs = jnp.concatenate(a[ci][hh], axis=0).astype(BF16)
            o = o_state[ci][hh] + _dot(pairs, v[sp, sl])
            o_ref[sp, sl] = o + other_ref[sp, sl] if accumulate else o


def _hgrn_scan(p, lower_bound, rows, reverse, other=None):
    b, seq, lc = rows.batch, rows.seq, rows.ctx_len
    d = lower_bound.shape[0]
    c, hb = SCAN_CHUNK, SCAN_HEADS_PER_BLOCK
    r = c * SCAN_CHUNKS_PER_STEP
    wb = hb * HEAD_DIM
    ncb = d // wb
    assert lc % r == 0 and seq % r == 0
    nctx, nlat = lc // r, seq // r
    ctx0 = rows.n_lat // r
    f_blk = (2 if reverse else 1) * ncb
    v_blk = 3 * ncb

    def row(bi, s):
        if reverse:
            return jnp.where(s < nctx, ctx0 + bi * nctx + (nctx - 1 - s), bi * nlat + (nlat - 1 - (s - nctx)))
        return jnp.where(s < nctx, ctx0 + bi * nctx + s, bi * nlat + (s - nctx))

    tri, wmasks, nmasks = _scan_consts(c, reverse)
    o_spec = pl.BlockSpec((r, wb), lambda bi, hi, s: (row(bi, s), hi))
    extra = () if other is None else (other,)
    return pl.pallas_call(
        functools.partial(_scan_kernel, reverse=reverse, chunk=c, accumulate=other is not None),
        out_shape=jax.ShapeDtypeStruct((rows.n, d), F32),
        grid=(b, ncb, nctx + nlat),
        in_specs=[
            pl.BlockSpec((r, wb), lambda bi, hi, s: (row(bi, s), hi)),
            pl.BlockSpec((r, wb), lambda bi, hi, s: (row(bi, s), f_blk + hi)),
            pl.BlockSpec((r, wb), lambda bi, hi, s: (row(bi, s), v_blk + hi)),
            pl.BlockSpec((1, wb), lambda bi, hi, s: (0, hi)),
            pl.BlockSpec(tri.shape, lambda bi, hi, s: (0, 0)),
            pl.BlockSpec(wmasks.shape, lambda bi, hi, s: (0, 0, 0)),
            pl.BlockSpec(nmasks.shape, lambda bi, hi, s: (0, 0, 0)),
        ] + [o_spec] * len(extra),
        out_specs=o_spec,
        scratch_shapes=[pltpu.VMEM((hb, HEAD_DIM, HEAD_DIM), F32)],
        compiler_params=_cparams("parallel", "parallel", "arbitrary"),
        name="hgrn_scan_bw" if reverse else "hgrn_scan_fw",
    )(p, p, p, lower_bound.reshape(1, d), tri, wmasks, nmasks, *extra)


def kernel(x, c, ctx, c_ctx, w_mod, b_mod, norm_gains, ffn_w_in, ffn_w_out, ab_w_in, qk_norm, ab_w_out,
           hgrn_w_in, hgrn_lb_logits, hgrn_o_norm, hgrn_w_out, final_norm):
    batch, seq, d = x.shape
    lc = ctx.shape[1]
    depth = w_mod.shape[0]
    rows = _Rows(batch, seq, lc, ROW_TILE)
    rows_r = _Rows(batch, seq, lc, READOUT_ROW_TILE)
    assert seq % GRID_W == 0 and batch + 1 <= SUBLANES

    c_rows = jnp.concatenate([c_ctx[None, :], c, jnp.zeros((SUBLANES - 1 - batch, d), F32)], axis=0)
    mods_all = _modulation(c_rows, w_mod, b_mod).reshape(depth, SUBLANES, 3, 3, d)

    lb_cum = jnp.cumsum(jax.nn.softmax(hgrn_lb_logits.astype(F32), axis=0), axis=0)
    lower_bounds = lb_cum - lb_cum[0]

    ffn_w_in_b = ffn_w_in.astype(BF16)
    ffn_w_out_b = ffn_w_out.astype(BF16)
    fin = final_norm.reshape(1, d)
    rope = _rope_tables(seq, ROW_TILE)

    h = (x.reshape(batch * seq, d), ctx.reshape(batch * lc, d))
    for layer in range(depth):
        last = layer == depth - 1
        mods = mods_all[layer]
        gains = norm_gains[layer].reshape(3, 1, d)
        h = _ffn(h, mods, 0, gains[0], ffn_w_in_b, ffn_w_out_b, layer, 0, fin, rows, rows.tiles, False)
        if layer % 2 == 0:
            e = layer // 2
            f_all, q_all, k_all, vt_all = _ab_proj(h, mods, gains[1], ab_w_in[e].astype(BF16), qk_norm[e], rope, rows)
            h = _ab_out(_fourier_latent(f_all, rows), _fourier_ctx(f_all, rows),
                        _attention(q_all, k_all, vt_all, rows, True), _attention(q_all, k_all, vt_all, rows, False),
                        ab_w_out[e].astype(BF16), h, mods, rows)
        else:
            o = layer // 2
            p = _hgrn_proj(h, mods, gains[1], hgrn_w_in[o].astype(BF16), rows)
            o_fw = _hgrn_scan(p, lower_bounds[layer], rows, False)
            o_sum = _hgrn_scan(p, lower_bounds[layer], rows, True, other=o_fw)
            gain_o = jnp.tile(hgrn_o_norm[o], d // HEAD_DIM).reshape(1, d)
            n_t = rows_r.lat_tiles if last else rows_r.tiles
            h = _hgrn_out(o_sum, p, gain_o, hgrn_w_out[o].astype(BF16), h, mods, rows_r, n_t)
        n_t = rows.lat_tiles if last else rows.tiles
        h = _ffn(h, mods, 2, gains[2], ffn_w_in_b, ffn_w_out_b, layer, 1, fin, rows, n_t, last)
    return h[:batch * seq].reshape(batch, seq, d)
```

```python
import functools
import math

import jax
import jax.numpy as jnp
import numpy as np
from jax import lax
from jax.experimental import pallas as pl
from jax.experimental.pallas import tpu as pltpu

F32 = jnp.float32
BF16 = jnp.bfloat16

EPS = 1e-6
N_MOD = 9
HEAD_DIM = 128
N_KV_HEADS = 4
FOURIER_WIDTH = 512
FOURIER_GROUP_DIM = 128
GRID_W = 64
ROPE_THETA = 10000.0
ROPE_AXIS_DIM = HEAD_DIM // 2
ATTN_SCALE = HEAD_DIM ** -0.5
LOG2_E = math.log2(math.e)

LANES = 128
SUBLANES = 8
BF16_SUBLANES = 16
VT_ROWS = HEAD_DIM + BF16_SUBLANES
VMEM_LIMIT_BYTES = 56 * 1024 * 1024

ROW_TILE = 512
READOUT_ROW_TILE = 512
FFN_TILE = 512
PROJ_COL_TILE = 2048
PROJ_ROW_GROUP = 3
MOD_COL_TILE = 1024
ATTN_Q_TILE = 1024
SCAN_CHUNK = 128
SCAN_HEADS_PER_BLOCK = 8
SCAN_CHUNKS_PER_STEP = 2
FFT_B = 128
FFT_B_BLOCK = 8


def _cparams(*sem):
    return pltpu.CompilerParams(dimension_semantics=sem, vmem_limit_bytes=VMEM_LIMIT_BYTES)


def _dot(a, b):
    return jnp.dot(a, b, preferred_element_type=F32)


def _dot_hi(a, b):
    return jnp.dot(a, b, preferred_element_type=F32, precision=lax.Precision.HIGHEST)


def _split_bf16(x):
    hi = x.astype(BF16)
    return hi, (x - hi.astype(F32)).astype(BF16)


def _dot_split(a, b):
    (ah, al), (bh, bl) = a, b
    return _dot(ah, bh) + _dot(al, bh) + _dot(ah, bl)


def _dot_nt(a, b):
    return lax.dot_general(a, b, (((1,), (1,)), ((), ())), preferred_element_type=F32)


def _dot_tn(a, b):
    return lax.dot_general(a, b, (((0,), (0,)), ((), ())), preferred_element_type=F32)


def _silu(x):
    return x * jax.nn.sigmoid(x)


def _rms(x, gain):
    return x * lax.rsqrt(jnp.mean(x * x, axis=-1, keepdims=True) + EPS) * gain


def _ada(h, gain, shift, scale):
    return _rms(h, gain) * (1.0 + scale) + shift


def _mod_kernel(c_ref, w_ref, b_ref, o_ref):
    a = _silu(c_ref[...]).astype(BF16)
    o_ref[...] = _dot(a, w_ref[...].astype(BF16)) + b_ref[...]


def _modulation(c_rows, w_mod, b_mod):
    depth, d, nd = w_mod.shape
    tn = MOD_COL_TILE
    return pl.pallas_call(
        _mod_kernel,
        out_shape=jax.ShapeDtypeStruct((depth, SUBLANES, nd), F32),
        grid=(depth, nd // tn),
        in_specs=[
            pl.BlockSpec((SUBLANES, d), lambda l, j: (0, 0)),
            pl.BlockSpec((None, d, tn), lambda l, j: (l, 0, j)),
            pl.BlockSpec((None, 1, tn), lambda l, j: (l, 0, j)),
        ],
        out_specs=pl.BlockSpec((None, SUBLANES, tn), lambda l, j: (l, 0, j)),
        compiler_params=_cparams("parallel", "arbitrary"),
        name="modulation",
    )(c_rows, w_mod, b_mod.reshape(depth, 1, nd))


class _Rows:
    def __init__(self, batch, seq, ctx_len, tile):
        assert seq % tile == 0 and (batch * ctx_len) % tile == 0
        self.batch, self.seq, self.ctx_len, self.tile = batch, seq, ctx_len, tile
        self.n_lat = batch * seq
        self.n = self.n_lat + batch * ctx_len
        self.lat_tiles = self.n_lat // tile
        self.tiles = self.n // tile
        self.tiles_per_batch = seq // tile

    def group(self, i):
        return jnp.where(i < self.lat_tiles, 1 + i // self.tiles_per_batch, 0)


def _ffn_kernel(*refs, final, split_tiles, layer, which, nf):
    if split_tiles is None:
        h_ref, mod_ref, gain_ref, win_ref, wout_ref, fin_ref, o_ref, xn_ref, acc_ref, wa_buf, wb_buf, wo_buf, sem = refs
        load_h = lambda: h_ref[...]
    else:
        (hl_ref, hc_ref, mod_ref, gain_ref, win_ref, wout_ref, fin_ref, o_ref, xn_ref, acc_ref,
         wa_buf, wb_buf, wo_buf, sem) = refs
        load_h = lambda: jnp.where(pl.program_id(0) < split_tiles, hl_ref[...], hc_ref[...])
    tf = wa_buf.shape[2]

    def copies(j, slot):
        ca = pl.multiple_of(j * tf, tf)
        cb = pl.multiple_of((nf + j) * tf, tf)
        return (pltpu.make_async_copy(win_ref.at[layer, which, :, pl.ds(ca, tf)], wa_buf.at[slot], sem.at[0, slot]),
                pltpu.make_async_copy(win_ref.at[layer, which, :, pl.ds(cb, tf)], wb_buf.at[slot], sem.at[1, slot]),
                pltpu.make_async_copy(wout_ref.at[layer, which, pl.ds(ca, tf), :], wo_buf.at[slot], sem.at[2, slot]))

    def start(j, slot):
        for c in copies(j, slot):
            c.start()

    def hidden_tile(j, slot, first=False):
        for c in copies(j, slot):
            c.wait()
        xn = xn_ref[...]
        g = (_silu(_dot(xn, wa_buf[slot])) * _dot(xn, wb_buf[slot])).astype(BF16)
        part = _dot(g, wo_buf[slot])
        acc_ref[...] = part if first else acc_ref[...] + part

    start(0, 0)
    xn_ref[...] = _ada(load_h(), gain_ref[...], mod_ref[0:1, :], mod_ref[1:2, :]).astype(BF16)
    if nf > 1:
        start(1, 1)
    hidden_tile(0, 0, first=True)

    def pair(p, carry):
        j = 1 + 2 * p
        start(j + 1, 0)
        hidden_tile(j, 1)

        @pl.when(j + 2 < nf)
        def _():
            start(j + 2, 1)

        hidden_tile(j + 1, 0)
        return carry

    n_pairs = (nf - 1) // 2
    lax.fori_loop(0, n_pairs, pair, 0)
    if (nf - 1) % 2:
        hidden_tile(nf - 1, 1)

    out = load_h() + 0.5 * mod_ref[2:3, :] * acc_ref[...]
    if final:
        out = _rms(out, fin_ref[...])
    o_ref[...] = out


def _ffn(h, mods, sub, gain, w_in, w_out, layer, which, fin, rows, n_tiles, final):
    f, d = w_out.shape[2:]
    tm, tf = rows.tile, FFN_TILE
    nf = f // tf
    if isinstance(h, tuple):
        lt = rows.lat_tiles
        row_specs = [pl.BlockSpec((tm, d), lambda i: (jnp.minimum(i, lt - 1), 0)),
                     pl.BlockSpec((tm, d), lambda i: (jnp.maximum(i - lt, 0), 0))]
        row_args, split_tiles = h, lt
    else:
        row_specs = [pl.BlockSpec((tm, d), lambda i: (i, 0))]
        row_args, split_tiles = (h,), None
    return pl.pallas_call(
        functools.partial(_ffn_kernel, final=final, split_tiles=split_tiles, layer=layer, which=which, nf=nf),
        out_shape=jax.ShapeDtypeStruct((n_tiles * tm, d), F32),
        grid=(n_tiles,),
        in_specs=row_specs + [
            pl.BlockSpec((None, None, 3, d), lambda i: (rows.group(i), sub, 0, 0)),
            pl.BlockSpec((1, d), lambda i: (0, 0)),
            pl.BlockSpec(memory_space=pl.ANY),
            pl.BlockSpec(memory_space=pl.ANY),
            pl.BlockSpec((1, d), lambda i: (0, 0)),
        ],
        out_specs=pl.BlockSpec((tm, d), lambda i: (i, 0)),
        scratch_shapes=[pltpu.VMEM((tm, d), BF16), pltpu.VMEM((tm, d), F32),
                        pltpu.VMEM((2, d, tf), BF16), pltpu.VMEM((2, d, tf), BF16), pltpu.VMEM((2, tf, d), BF16),
                        pltpu.SemaphoreType.DMA((3, 2))],
        compiler_params=_cparams("arbitrary"),
        name="ffn",
    )(*row_args, mods, gain, w_in, w_out, fin)


def _rope_tables(seq, tile):
    t = np.arange(seq)
    inv_freq = ROPE_THETA ** (-np.arange(0, ROPE_AXIS_DIM, 2, dtype=np.float64) / ROPE_AXIS_DIM)
    ang = np.concatenate([(t // GRID_W)[:, None] * inv_freq, (t % GRID_W)[:, None] * inv_freq], axis=-1)
    nf = ROPE_AXIS_DIM // 2
    cos = np.cos(ang).reshape(seq, 2, 1, nf)
    sin = np.sin(ang).reshape(seq, 2, 1, nf)
    zero = np.zeros_like(sin)
    c_full = np.broadcast_to(cos, (seq, 2, 2, nf)).reshape(seq, HEAD_DIM)
    s_up = np.concatenate([-sin, zero], axis=2).reshape(seq, HEAD_DIM)
    s_dn = np.concatenate([zero, sin], axis=2).reshape(seq, HEAD_DIM)
    lat = np.concatenate([c_full, s_up, s_dn], axis=1)
    ident = np.concatenate([np.ones((tile, HEAD_DIM)), np.zeros((tile, 2 * HEAD_DIM))], axis=1)
    return jnp.asarray(np.concatenate([lat, ident], axis=0), dtype=F32)


def _norm_rope_heads(acc, gain, rope, post_scale=None):
    nf = ROPE_AXIS_DIM // 2
    c, s_up, s_dn = rope[:, :HEAD_DIM], rope[:, HEAD_DIM:2 * HEAD_DIM], rope[:, 2 * HEAD_DIM:]
    heads = []
    for hh in range(acc.shape[1] // HEAD_DIM):
        y = _rms(acc[:, hh * HEAD_DIM:(hh + 1) * HEAD_DIM], gain)
        y = y * c + pltpu.roll(y, HEAD_DIM - nf, 1) * s_up + pltpu.roll(y, nf, 1) * s_dn
        heads.append(y if post_scale is None else y * post_scale)
    return jnp.concatenate(heads, axis=1).astype(BF16)


def _ab_proj_kernel(h_ref, mod_ref, gain_ref, w_ref, qkn_ref, rope_ref, f_ref, q_ref, k_ref, vt_ref):
    xn = _ada(h_ref[...], gain_ref[...], mod_ref[0:1, :], mod_ref[1:2, :]).astype(BF16)
    c0 = f_ref.shape[1]
    c1 = c0 + q_ref.shape[1]
    c2 = c1 + k_ref.shape[1]
    q_ref[...] = _norm_rope_heads(_dot(xn, w_ref[:, c0:c1]), qkn_ref[0:1, :], rope_ref[...], ATTN_SCALE * LOG2_E)
    k_ref[...] = _norm_rope_heads(_dot(xn, w_ref[:, c1:c2]), qkn_ref[1:2, :], rope_ref[...])
    v = _dot(xn, w_ref[:, c2:])
    ones = jnp.ones((vt_ref.shape[1] - HEAD_DIM, vt_ref.shape[2]), BF16)
    for hh in range(N_KV_HEADS):
        vt_ref[hh, :HEAD_DIM, :] = v[:, hh * HEAD_DIM:(hh + 1) * HEAD_DIM].T.astype(BF16)
        vt_ref[hh, HEAD_DIM:, :] = ones
    f_ref[...] = _dot(xn, w_ref[:, :c0])


def _ab_proj(h, mods, gain, w_in, qk_norm, rope, rows):
    d = h.shape[1]
    tm = rows.tile
    kv_width = N_KV_HEADS * HEAD_DIM
    q_width = w_in.shape[1] - FOURIER_WIDTH - 2 * kv_width
    n = rows.n
    rope_blk = lambda i: (jnp.where(i < rows.lat_tiles, i % rows.tiles_per_batch, rows.tiles_per_batch), 0)
    return pl.pallas_call(
        _ab_proj_kernel,
        out_shape=(
            jax.ShapeDtypeStruct((n, FOURIER_WIDTH), F32),
            jax.ShapeDtypeStruct((n, q_width), BF16),
            jax.ShapeDtypeStruct((n, kv_width), BF16),
            jax.ShapeDtypeStruct((N_KV_HEADS, rows.tiles, VT_ROWS, tm), BF16),
        ),
        grid=(rows.tiles,),
        in_specs=[
            pl.BlockSpec((tm, d), lambda i: (i, 0)),
            pl.BlockSpec((None, None, 3, d), lambda i: (rows.group(i), 1, 0, 0)),
            pl.BlockSpec((1, d), lambda i: (0, 0)),
            pl.BlockSpec(w_in.shape, lambda i: (0, 0)),
            pl.BlockSpec((2, HEAD_DIM), lambda i: (0, 0)),
            pl.BlockSpec((tm, 3 * HEAD_DIM), rope_blk),
        ],
        out_specs=(
            pl.BlockSpec((tm, FOURIER_WIDTH), lambda i: (i, 0)),
            pl.BlockSpec((tm, q_width), lambda i: (i, 0)),
            pl.BlockSpec((tm, kv_width), lambda i: (i, 0)),
            pl.BlockSpec((N_KV_HEADS, None, VT_ROWS, tm), lambda i: (0, i, 0, 0)),
        ),
        compiler_params=_cparams("parallel"),
        name="ab_proj",
    )(h, mods, gain, w_in, qk_norm, rope)


def _attn_kernel(*refs, lat_chunks, group):
    if lat_chunks:
        q_ref, kc_ref, vtc_ref, kl_ref, vtl_ref, o_ref, acc_ref, s_ref = refs
    else:
        q_ref, kc_ref, vtc_ref, o_ref, acc_ref = refs
    tq = q_ref.shape[0]
    q = q_ref[...]
    qs = jnp.concatenate([q[:, g * HEAD_DIM:(g + 1) * HEAD_DIM] for g in range(group)], axis=0)
    nq = group * tq
    acc_ref[...] = jnp.zeros_like(acc_ref)

    def scores(k):
        return _dot_nt(k, qs)

    def update(s, vt, m):
        m_new = jnp.maximum(m, jnp.max(s, axis=0, keepdims=True))
        p = jnp.exp2(s - m_new).astype(BF16)
        acc_ref[...] = jnp.exp2(m - m_new) * acc_ref[...] + _dot(vt, p)
        return m_new

    m = update(scores(kc_ref[...]), vtc_ref[...], jnp.full((1, nq), -jnp.inf, F32))
    if lat_chunks:
        tk = vtl_ref.shape[2]

        def lat_scores(c):
            return scores(kl_ref[pl.ds(pl.multiple_of(c * tk, tk), tk), :])

        assert lat_chunks % 2 == 0
        s_ref[0] = lat_scores(0)

        def body(i, m):
            c = 2 * i
            s_ref[1] = lat_scores(c + 1)
            m = update(s_ref[0], vtl_ref[c], m)
            s_ref[0] = lat_scores(jnp.minimum(c + 2, lat_chunks - 1))
            return update(s_ref[1], vtl_ref[c + 1], m)

        m = lax.fori_loop(0, lat_chunks // 2, body, m)

    out = (acc_ref[:HEAD_DIM, :] / acc_ref[HEAD_DIM:HEAD_DIM + 1, :]).T
    o_ref[...] = jnp.concatenate([out[g * tq:(g + 1) * tq, :] for g in range(group)], axis=1).astype(BF16)


def _attention(q_all, k_all, vt_all, rows, latent):
    b, seq, lc, tile = rows.batch, rows.seq, rows.ctx_len, rows.tile
    group = q_all.shape[1] // (N_KV_HEADS * HEAD_DIM)
    gw = group * HEAD_DIM
    assert tile % lc == 0 and seq % tile == 0
    ctx_blk0 = rows.n_lat // lc
    ctx_tile = lambda bi: ((rows.n_lat + bi * lc) // tile, 0, ((rows.n_lat + bi * lc) % tile) // lc)
    kc_spec = pl.BlockSpec((lc, HEAD_DIM), lambda bi, hi, i: (ctx_blk0 + bi, hi))
    vtc_spec = pl.BlockSpec((None, None, VT_ROWS, lc), lambda bi, hi, i: (hi,) + ctx_tile(bi))
    if latent:
        tq = ATTN_Q_TILE
        assert tq % tile == 0
        nqt = seq // tq
        lat_chunks = seq // tile
        q_spec = pl.BlockSpec((tq, gw), lambda bi, hi, i: (bi * nqt + i, hi))
        kl_spec = pl.BlockSpec((seq, HEAD_DIM), lambda bi, hi, i: (bi, hi))
        vtl_spec = pl.BlockSpec((None, lat_chunks, VT_ROWS, tile), lambda bi, hi, i: (hi, bi, 0, 0))
        in_specs = [q_spec, kc_spec, vtc_spec, kl_spec, vtl_spec]
        args = (q_all, k_all, vt_all, k_all, vt_all)
    else:
        tq = lc
        nqt = 1
        lat_chunks = 0
        q_spec = pl.BlockSpec((tq, gw), lambda bi, hi, i: (ctx_blk0 + bi, hi))
        in_specs = [q_spec, kc_spec, vtc_spec]
        args = (q_all, k_all, vt_all)
    return pl.pallas_call(
        functools.partial(_attn_kernel, lat_chunks=lat_chunks, group=group),
        out_shape=jax.ShapeDtypeStruct((b * nqt * tq, q_all.shape[1]), BF16),
        grid=(b, N_KV_HEADS, nqt),
        in_specs=in_specs,
        out_specs=pl.BlockSpec((tq, gw), lambda bi, hi, i: (bi * nqt + i, hi)),
        scratch_shapes=[pltpu.VMEM((VT_ROWS, group * tq), F32)]
        + ([pltpu.VMEM((2, tile, group * tq), F32)] if lat_chunks else []),
        compiler_params=_cparams("parallel", "parallel", "arbitrary"),
        name="attention_ctx" if lat_chunks == 0 else "attention_lat",
    )(*args)


def _dft_cs(n):
    idx = np.arange(n)
    ang = 2.0 * np.pi * ((idx[:, None] * idx[None, :]) % n) / n
    return np.cos(ang), np.sin(ang)


def _const_split(m):
    m = np.asarray(m, dtype=np.float32)
    hi = jnp.asarray(m).astype(BF16)
    lo = (jnp.asarray(m) - hi.astype(F32)).astype(BF16)
    return jnp.stack([hi, lo])


def _fft1_kernel(x_ref, fa_ref, tw_ref, o_ref, *, a):
    w = o_ref.shape[2]
    for r in range(o_ref.shape[0]):
        z = _dot_split((fa_ref[0], fa_ref[1]), _split_bf16(x_ref[:, r * w:(r + 1) * w]))
        zr, zi = z[:a], z[a:]
        tc = jnp.concatenate([tw_ref[r, 0]] * (w // LANES), axis=1)
        ts = jnp.concatenate([tw_ref[r, 1]] * (w // LANES), axis=1)
        o_ref[r, :a, :] = zr * tc - zi * ts
        o_ref[r, a:, :] = zr * ts + zi * tc


def _fft2_kernel(zr_ref, zi_ref, m2_ref, mc_ref, o_ref, *, scale):
    for r in range(o_ref.shape[1]):
        z = jnp.concatenate([zr_ref[:, r, :], zi_ref[:, r, :]], axis=0)
        v = _dot_split((m2_ref[0], m2_ref[1]), _split_bf16(z))
        vr, vi = v[:FFT_B], v[FFT_B:]
        outs = []
        for g in range(o_ref.shape[2] // LANES):
            u = jnp.concatenate([vr[:, g * LANES:(g + 1) * LANES], vi[:, g * LANES:(g + 1) * LANES]], axis=1)
            outs.append(_dot_split(_split_bf16(u), (mc_ref[0], mc_ref[1])))
        o_ref[:, r, :] = jnp.concatenate(outs, axis=1) * scale


def _fourier_latent(f_all, rows):
    b, seq = rows.batch, rows.seq
    w = FOURIER_WIDTH
    a = seq // FFT_B
    assert a % SUBLANES == 0 and FOURIER_GROUP_DIM == LANES
    ca, sa = _dft_cs(a)
    fa = _const_split(np.concatenate([ca, sa], axis=0))
    p1b = (np.arange(a)[None, :] * np.arange(FFT_B)[:, None]) % seq
    ang = 2.0 * np.pi * p1b / seq
    tw = np.stack([np.cos(ang), np.sin(ang)], axis=1)[..., None]
    tw = jnp.asarray(np.broadcast_to(tw, (FFT_B, 2, a, LANES)), dtype=F32)
    cb, sb = _dft_cs(FFT_B)
    m2 = _const_split(np.block([[cb, -sb], [sb, cb]]))
    cc, sc = _dft_cs(FOURIER_GROUP_DIM)
    mc = _const_split(np.concatenate([cc, -sc], axis=0))

    blk = FFT_B_BLOCK
    x2 = f_all.reshape(rows.n // FFT_B, FFT_B * w)
    z = pl.pallas_call(
        functools.partial(_fft1_kernel, a=a),
        out_shape=jax.ShapeDtypeStruct((b, FFT_B, 2 * a, w), F32),
        grid=(b, FFT_B // blk),
        in_specs=[
            pl.BlockSpec((a, blk * w), lambda bi, j: (bi, j)),
            pl.BlockSpec((2, 2 * a, a), lambda bi, j: (0, 0, 0)),
            pl.BlockSpec((blk, 2, a, LANES), lambda bi, j: (j, 0, 0, 0)),
        ],
        out_specs=pl.BlockSpec((None, blk, 2 * a, w), lambda bi, j: (bi, j, 0, 0)),
        compiler_params=_cparams("parallel", "parallel"),
        name="fourier_stage1",
    )(x2, fa, tw)

    nblk = a // blk
    y = pl.pallas_call(
        functools.partial(_fft2_kernel, scale=1.0 / math.sqrt(seq * FOURIER_GROUP_DIM)),
        out_shape=jax.ShapeDtypeStruct((b, FFT_B, a, w), F32),
        grid=(b, nblk),
        in_specs=[
            pl.BlockSpec((None, FFT_B, blk, w), lambda bi, j: (bi, 0, j, 0)),
            pl.BlockSpec((None, FFT_B, blk, w), lambda bi, j: (bi, 0, nblk + j, 0)),
            pl.BlockSpec((2, 2 * FFT_B, 2 * FFT_B), lambda bi, j: (0, 0, 0)),
            pl.BlockSpec((2, 2 * LANES, LANES), lambda bi, j: (0, 0, 0)),
        ],
        out_specs=pl.BlockSpec((None, FFT_B, blk, w), lambda bi, j: (bi, 0, j, 0)),
        compiler_params=_cparams("parallel", "parallel"),
        name="fourier_stage2",
    )(z, z, m2, mc)
    return y.reshape(b * seq, w)


def _dft_ctx_kernel(x_ref, cn_ref, sn_ref, cc_ref, sc_ref, o_ref, *, scale):
    x = x_ref[...]
    outs = []
    for g in range(x.shape[1] // LANES):
        xg = x[:, g * LANES:(g + 1) * LANES]
        outs.append(_dot_hi(cn_ref[...], _dot_hi(xg, cc_ref[...])) - _dot_hi(sn_ref[...], _dot_hi(xg, sc_ref[...])))
    o_ref[...] = jnp.concatenate(outs, axis=1) * scale


def _fourier_ctx(f_all, rows):
    b, lc = rows.batch, rows.ctx_len
    w = FOURIER_WIDTH
    cn, sn = _dft_cs(lc)
    cc, sc = _dft_cs(FOURIER_GROUP_DIM)
    blk0 = rows.n_lat // lc
    mat = lambda m: pl.BlockSpec(m.shape, lambda bi: (0, 0))
    consts = [jnp.asarray(m, dtype=F32) for m in (cn, sn, cc, sc)]
    return pl.pallas_call(
        functools.partial(_dft_ctx_kernel, scale=1.0 / math.sqrt(lc * FOURIER_GROUP_DIM)),
        out_shape=jax.ShapeDtypeStruct((b * lc, w), F32),
        grid=(b,),
        in_specs=[pl.BlockSpec((lc, w), lambda bi: (blk0 + bi, 0))] + [mat(m) for m in consts],
        out_specs=pl.BlockSpec((lc, w), lambda bi: (bi, 0)),
        compiler_params=_cparams("parallel"),
        name="fourier_ctx",
    )(f_all, *consts)


def _ab_out_kernel(x1l_ref, x1c_ref, x2l_ref, x2c_ref, w_ref, h_ref, mod_ref, o_ref, *, lat_tiles):
    is_lat = pl.program_id(0) < lat_tiles
    x1 = jnp.where(is_lat, x1l_ref[...], x1c_ref[...])
    x2 = jnp.where(is_lat, x2l_ref[...], x2c_ref[...])
    x = jnp.concatenate([x1.astype(BF16), x2], axis=1)
    o_ref[...] = h_ref[...] + mod_ref[2:3, :] * _dot(x, w_ref[...])


def _ab_out(x1_lat, x1_ctx, x2_lat, x2_ctx, w_out, h, mods, rows):
    d = h.shape[1]
    tm = rows.tile
    lt = rows.lat_tiles
    lat_blk = lambda i: (jnp.minimum(i, lt - 1), 0)
    ctx_blk = lambda i: (jnp.maximum(i - lt, 0), 0)
    return pl.pallas_call(
        functools.partial(_ab_out_kernel, lat_tiles=lt),
        out_shape=jax.ShapeDtypeStruct((rows.n, d), F32),
        grid=(rows.tiles,),
        in_specs=[
            pl.BlockSpec((tm, x1_lat.shape[1]), lat_blk),
            pl.BlockSpec((tm, x1_ctx.shape[1]), ctx_blk),
            pl.BlockSpec((tm, x2_lat.shape[1]), lat_blk),
            pl.BlockSpec((tm, x2_ctx.shape[1]), ctx_blk),
            pl.BlockSpec(w_out.shape, lambda i: (0, 0)),
            pl.BlockSpec((tm, d), lambda i: (i, 0)),
            pl.BlockSpec((None, None, 3, d), lambda i: (rows.group(i), 1, 0, 0)),
        ],
        out_specs=pl.BlockSpec((tm, d), lambda i: (i, 0)),
        compiler_params=_cparams("parallel"),
        name="ab_out",
    )(x1_lat, x1_ctx, x2_lat, x2_ctx, w_out, h, mods)


def _hgrn_out_kernel(osum_ref, g_ref, gain_ref, w_ref, h_ref, mod_ref, o_ref):
    o = osum_ref[...]
    parts = []
    for hh in range(o.shape[1] // HEAD_DIM):
        oh = o[:, hh * HEAD_DIM:(hh + 1) * HEAD_DIM]
        parts.append(oh * lax.rsqrt(jnp.mean(oh * oh, axis=-1, keepdims=True) + EPS))
    on = jnp.concatenate(parts, axis=1) * gain_ref[...]
    y = (on * jax.nn.sigmoid(g_ref[...])).astype(BF16)
    o_ref[...] = h_ref[...] + mod_ref[2:3, :] * _dot(y, w_ref[...])


def _hgrn_out(o_sum, p, gain, w_out, h, mods, rows_r, n_tiles):
    d = h.shape[1]
    tm = rows_r.tile
    g_blk = (p.shape[1] - d) // d
    return pl.pallas_call(
        _hgrn_out_kernel,
        out_shape=jax.ShapeDtypeStruct((n_tiles * tm, d), F32),
        grid=(n_tiles,),
        in_specs=[
            pl.BlockSpec((tm, d), lambda i: (i, 0)),
            pl.BlockSpec((tm, d), lambda i: (i, g_blk)),
            pl.BlockSpec((1, d), lambda i: (0, 0)),
            pl.BlockSpec(w_out.shape, lambda i: (0, 0)),
            pl.BlockSpec((tm, d), lambda i: (i, 0)),
            pl.BlockSpec((None, None, 3, d), lambda i: (rows_r.group(i), 1, 0, 0)),
        ],
        out_specs=pl.BlockSpec((tm, d), lambda i: (i, 0)),
        compiler_params=_cparams("parallel"),
        name="hgrn_out",
    )(o_sum, p, gain, w_out, h, mods)


def _hgrn_proj_kernel(h_ref, mod_ref, gain_ref, w_ref, o_ref, xn_ref, *, n_silu):
    j = pl.program_id(1)
    r = pl.program_id(2)

    @pl.when(j == 0)
    def _():
        xn_ref[r] = _ada(h_ref[...], gain_ref[...], mod_ref[0:1, :], mod_ref[1:2, :]).astype(BF16)

    acc = _dot(xn_ref[r], w_ref[...])
    o_ref[...] = jnp.where(j < n_silu, _silu(acc), acc)


def _hgrn_proj(h, mods, gain, w_in, rows):
    d = h.shape[1]
    tm, tn = rows.tile, PROJ_COL_TILE
    nw = w_in.shape[1]
    gs = next(k for k in (PROJ_ROW_GROUP, 2, 1) if rows.tiles % k == 0)
    tile = lambda g, r: g * gs + r
    return pl.pallas_call(
        functools.partial(_hgrn_proj_kernel, n_silu=d // tn),
        out_shape=jax.ShapeDtypeStruct((rows.n, nw), F32),
        grid=(rows.tiles // gs, nw // tn, gs),
        in_specs=[
            pl.BlockSpec((tm, d), lambda g, j, r: (jnp.where(j == 0, tile(g, r), tile(g, gs - 1)), 0)),
            pl.BlockSpec((None, None, 3, d), lambda g, j, r: (rows.group(tile(g, r)), 1, 0, 0)),
            pl.BlockSpec((1, d), lambda g, j, r: (0, 0)),
            pl.BlockSpec((d, tn), lambda g, j, r: (0, j)),
        ],
        out_specs=pl.BlockSpec((tm, tn), lambda g, j, r: (tile(g, r), j)),
        scratch_shapes=[pltpu.VMEM((gs, tm, d), BF16)],
        compiler_params=_cparams("arbitrary", "arbitrary", "arbitrary"),
        name="hgrn_proj",
    )(h, mods, gain, w_in)


def _scan_levels(chunk):
    return [chunk >> (i + 1) for i in range(int(math.log2(chunk)))]


def _scan_consts(chunk, reverse):
    t = np.arange(chunk)[:, None]
    s = np.arange(chunk)[None, :]
    tri = (s >= t) if reverse else (s <= t)
    wide, narrow = [], []
    for h in _scan_levels(chunk):
        same = (t // (2 * h)) == (s // (2 * h))
        t_up = (t // h) % 2 == 1
        s_up = (s // h) % 2 == 1
        mask = same & ((~t_up & s_up) if reverse else (t_up & ~s_up))
        if h >= SUBLANES:
            wide.append(mask[np.nonzero(~t_up[:, 0] if reverse else t_up[:, 0])[0]])
        else:
            narrow.append(mask)
    narrow.append(t == s)
    return (jnp.asarray(tri, dtype=BF16), jnp.asarray(np.stack(wide), dtype=F32),
            jnp.asarray(np.stack(narrow), dtype=F32))


def _seg_bcast(x, h, reverse):
    c, w = x.shape
    off = h if reverse else h - 1
    if 2 * h >= 2 * SUBLANES:
        pieces = [jnp.broadcast_to(x[g * 2 * h + off:g * 2 * h + off + 1, :], (2 * h, w)) for g in range(c // (2 * h))]
        return pieces[0] if len(pieces) == 1 else jnp.concatenate(pieces, axis=0)
    x3 = x.reshape(c // SUBLANES, SUBLANES, w)
    sub = lax.broadcasted_iota(jnp.int32, x3.shape, 1)
    y = None
    for g in range(SUBLANES // (2 * h)):
        piece = jnp.broadcast_to(x3[:, g * 2 * h + off:g * 2 * h + off + 1, :], x3.shape)
        y = piece if y is None else jnp.where(sub >= g * 2 * h, piece, y)
    return y.reshape(c, w)


def _scan_kernel(*refs, reverse, chunk, accumulate):
    if accumulate:
        q_ref, f_ref, v_ref, lb_ref, tri_ref, wmask_ref, nmask_ref, other_ref, o_ref, st_ref = refs
    else:
        q_ref, f_ref, v_ref, lb_ref, tri_ref, wmask_ref, nmask_ref, o_ref, st_ref = refs

    @pl.when(pl.program_id(2) == 0)
    def _():
        st_ref[...] = jnp.zeros_like(st_ref)

    rows, width = q_ref.shape
    levels = _scan_levels(chunk)
    spans = [slice(ci * chunk, (ci + 1) * chunk) for ci in range(rows // chunk)]
    lanes = [slice(hh * HEAD_DIM, (hh + 1) * HEAD_DIM) for hh in range(width // HEAD_DIM)]
    q = q_ref[...]
    fl = f_ref[...]
    lb = lb_ref[...]
    u = jnp.exp(-jnp.abs(fl))
    key = (1.0 - lb) * (jnp.where(fl > 0.0, u, 1.0) / (1.0 + u))
    log_sig = jnp.minimum(fl, 0.0) - jnp.log(1.0 + u)
    x1 = jnp.log(lb)
    x2 = jnp.log1p(-lb) + log_sig
    delta = x1 - x2
    lf = jnp.where(jnp.isnan(delta), x1 + x2, jnp.maximum(x1, x2) + jnp.log(1.0 + jnp.exp(-jnp.abs(delta))))
    lf = lf * LOG2_E
    l1 = lf.astype(BF16)
    r1 = lf - l1.astype(F32)
    l2 = r1.astype(BF16)
    l3 = (r1 - l2.astype(F32)).astype(BF16)
    parts = jnp.concatenate([l1, l2, l3], axis=1)
    tri = tri_ref[...]
    cs = jnp.concatenate([_dot(tri, parts[sp]) for sp in spans], axis=0)
    b = cs[:, :width] + cs[:, width:2 * width] + cs[:, 2 * width:]
    totals = [b[sp.start:sp.start + 1, :] if reverse else b[sp.stop - 1:sp.stop, :] for sp in spans]
    total_rows = jnp.concatenate([jnp.broadcast_to(t, (chunk, width)) for t in totals], axis=0)

    q_in = (q * jnp.exp2(b)).astype(BF16)
    k_out = (key * jnp.exp2(total_rows - b)).astype(BF16)
    v = v_ref[...].astype(BF16)

    order = list(range(len(spans)))[::-1] if reverse else list(range(len(spans)))
    kv = [[_dot_tn(v[sp, sl], k_out[sp, sl]) for sl in lanes] for sp in spans]
    o_state = [[None] * len(lanes) for _ in spans]
    for hh, sl in enumerate(lanes):
        st = st_ref[hh]
        for ci in order:
            o_state[ci][hh] = _dot_nt(q_in[spans[ci], sl], st.astype(BF16))
            st = st * jnp.exp2(totals[ci][:, sl]) + kv[ci][hh]
        st_ref[hh] = st

    n_blk = chunk // SUBLANES
    wide_levels = [h for h in levels if h >= SUBLANES]
    narrow_levels = [h for h in levels if h < SUBLANES]

    def narrow_scores(li, ql, kl):
        out = []
        for sp in spans:
            row = []
            for sl in lanes:
                s = nmask_ref[li] * _dot_nt(ql[sp, sl], kl[sp, sl])
                row.append([s[k * SUBLANES:(k + 1) * SUBLANES, :] for k in range(n_blk)])
            out.append(row)
        return out

    a = narrow_scores(len(narrow_levels), q.astype(BF16), key.astype(BF16))
    row_id = lax.broadcasted_iota(jnp.int32, (rows, width), 0)
    for li, h in enumerate(narrow_levels):
        if h == 1:
            e = jnp.where((row_id & 1) == (0 if reverse else 1), jnp.exp2(lf), 1.0)
        else:
            e = jnp.exp2(-jnp.abs(b - _seg_bcast(b, h, reverse)))
        new = narrow_scores(li, (q * e).astype(BF16), (key * e).astype(BF16))
        a = [[[x + y for x, y in zip(xb, yb)] for xb, yb in zip(xa, ya)] for xa, ya in zip(a, new)]

    for li, h in enumerate(wide_levels):
        q_parts, k_parts = [], []
        for g in range(rows // (2 * h)):
            lo = slice(g * 2 * h, g * 2 * h + h)
            hi = slice(g * 2 * h + h, (g + 1) * 2 * h)
            r = g * 2 * h + (h if reverse else h - 1)
            b_ref = jnp.broadcast_to(b[r:r + 1, :], (h, width))
            q_half, k_half = (lo, hi) if reverse else (hi, lo)
            q_parts.append(q[q_half] * jnp.exp2(b[q_half] - b_ref))
            k_part = key[k_half] * jnp.exp2(b_ref - b[k_half])
            zeros = jnp.zeros((h, width), F32)
            k_parts += [zeros, k_part] if reverse else [k_part, zeros]
        q_sel = jnp.concatenate(q_parts, axis=0).astype(BF16)
        k_hat = jnp.concatenate(k_parts, axis=0).astype(BF16)
        half = chunk // 2
        for ci, sp in enumerate(spans):
            for hh, sl in enumerate(lanes):
                s = wmask_ref[li] * _dot_nt(q_sel[ci * half:(ci + 1) * half, sl], k_hat[sp, sl])
                for j in range(chunk // (2 * h)):
                    first = (j * 2 * h + (0 if reverse else h)) // SUBLANES
                    for k in range(h // SUBLANES):
                        r0 = j * h + k * SUBLANES
                        a[ci][hh][first + k] = a[ci][hh][first + k] + s[r0:r0 + SUBLANES, :]

    for ci, sp in enumerate(spans):
        for hh, sl in enumerate(lanes):
            pairs = jnp.concatenate(a[ci][hh], axis=0).astype(BF16)
            o = o_state[ci][hh] + _dot(pairs, v[sp, sl])
            o_ref[sp, sl] = o + other_ref[sp, sl] if accumulate else o


def _hgrn_scan(p, lower_bound, rows, reverse, other=None):
    b, seq, lc = rows.batch, rows.seq, rows.ctx_len
    d = lower_bound.shape[0]
    c, hb = SCAN_CHUNK, SCAN_HEADS_PER_BLOCK
    r = c * SCAN_CHUNKS_PER_STEP
    wb = hb * HEAD_DIM
    ncb = d // wb
    assert lc % r == 0 and seq % r == 0
    nctx, nlat = lc // r, seq // r
    ctx0 = rows.n_lat // r
    f_blk = (2 if reverse else 1) * ncb
    v_blk = 3 * ncb

    def row(bi, s):
        if reverse:
            return jnp.where(s < nctx, ctx0 + bi * nctx + (nctx - 1 - s), bi * nlat + (nlat - 1 - (s - nctx)))
        return jnp.where(s < nctx, ctx0 + bi * nctx + s, bi * nlat + (s - nctx))

    tri, wmasks, nmasks = _scan_consts(c, reverse)
    o_spec = pl.BlockSpec((r, wb), lambda bi, hi, s: (row(bi, s), hi))
    extra = () if other is None else (other,)
    return pl.pallas_call(
        functools.partial(_scan_kernel, reverse=reverse, chunk=c, accumulate=other is not None),
        out_shape=jax.ShapeDtypeStruct((rows.n, d), F32),
        grid=(b, ncb, nctx + nlat),
        in_specs=[
            pl.BlockSpec((r, wb), lambda bi, hi, s: (row(bi, s), hi)),
            pl.BlockSpec((r, wb), lambda bi, hi, s: (row(bi, s), f_blk + hi)),
            pl.BlockSpec((r, wb), lambda bi, hi, s: (row(bi, s), v_blk + hi)),
            pl.BlockSpec((1, wb), lambda bi, hi, s: (0, hi)),
            pl.BlockSpec(tri.shape, lambda bi, hi, s: (0, 0)),
            pl.BlockSpec(wmasks.shape, lambda bi, hi, s: (0, 0, 0)),
            pl.BlockSpec(nmasks.shape, lambda bi, hi, s: (0, 0, 0)),
        ] + [o_spec] * len(extra),
        out_specs=o_spec,
        scratch_shapes=[pltpu.VMEM((hb, HEAD_DIM, HEAD_DIM), F32)],
        compiler_params=_cparams("parallel", "parallel", "arbitrary"),
        name="hgrn_scan_bw" if reverse else "hgrn_scan_fw",
    )(p, p, p, lower_bound.reshape(1, d), tri, wmasks, nmasks, *extra)


def kernel(x, c, ctx, c_ctx, w_mod, b_mod, norm_gains, ffn_w_in, ffn_w_out, ab_w_in, qk_norm, ab_w_out,
           hgrn_w_in, hgrn_lb_logits, hgrn_o_norm, hgrn_w_out, final_norm):
    batch, seq, d = x.shape
    lc = ctx.shape[1]
    depth = w_mod.shape[0]
    rows = _Rows(batch, seq, lc, ROW_TILE)
    rows_r = _Rows(batch, seq, lc, READOUT_ROW_TILE)
    assert seq % GRID_W == 0 and batch + 1 <= SUBLANES

    c_rows = jnp.concatenate([c_ctx[None, :], c, jnp.zeros((SUBLANES - 1 - batch, d), F32)], axis=0)
    mods_all = _modulation(c_rows, w_mod, b_mod).reshape(depth, SUBLANES, 3, 3, d)

    lb_cum = jnp.cumsum(jax.nn.softmax(hgrn_lb_logits.astype(F32), axis=0), axis=0)
    lower_bounds = lb_cum - lb_cum[0]

    ffn_w_in_b = ffn_w_in.astype(BF16)
    ffn_w_out_b = ffn_w_out.astype(BF16)
    fin = final_norm.reshape(1, d)
    rope = _rope_tables(seq, ROW_TILE)

    h = (x.reshape(batch * seq, d), ctx.reshape(batch * lc, d))
    for layer in range(depth):
        last = layer == depth - 1
        mods = mods_all[layer]
        gains = norm_gains[layer].reshape(3, 1, d)
        h = _ffn(h, mods, 0, gains[0], ffn_w_in_b, ffn_w_out_b, layer, 0, fin, rows, rows.tiles, False)
        if layer % 2 == 0:
            e = layer // 2
            f_all, q_all, k_all, vt_all = _ab_proj(h, mods, gains[1], ab_w_in[e].astype(BF16), qk_norm[e], rope, rows)
            h = _ab_out(_fourier_latent(f_all, rows), _fourier_ctx(f_all, rows),
                        _attention(q_all, k_all, vt_all, rows, True), _attention(q_all, k_all, vt_all, rows, False),
                        ab_w_out[e].astype(BF16), h, mods, rows)
        else:
            o = layer // 2
            p = _hgrn_proj(h, mods, gains[1], hgrn_w_in[o].astype(BF16), rows)
            o_fw = _hgrn_scan(p, lower_bounds[layer], rows, False)
            o_sum = _hgrn_scan(p, lower_bounds[layer], rows, True, other=o_fw)
            gain_o = jnp.tile(hgrn_o_norm[o], d // HEAD_DIM).reshape(1, d)
            n_t = rows_r.lat_tiles if last else rows_r.tiles
            h = _hgrn_out(o_sum, p, gain_o, hgrn_w_out[o].astype(BF16), h, mods, rows_r, n_t)
        n_t = rows.lat_tiles if last else rows.tiles
        h = _ffn(h, mods, 2, gains[2], ffn_w_in_b, ffn_w_out_b, layer, 1, fin, rows, n_t, last)
    return h[:batch * seq].reshape(batch, seq, d)
```

```python
import functools
import math

import jax
import jax.numpy as jnp
import numpy as np
from jax import lax
from jax.experimental import pallas as pl
from jax.experimental.pallas import tpu as pltpu

F32 = jnp.float32
BF16 = jnp.bfloat16

EPS = 1e-6
N_MOD = 9
HEAD_DIM = 128
N_KV_HEADS = 4
FOURIER_WIDTH = 512
FOURIER_GROUP_DIM = 128
GRID_W = 64
ROPE_THETA = 10000.0
ROPE_AXIS_DIM = HEAD_DIM // 2
ATTN_SCALE = HEAD_DIM ** -0.5
LOG2_E = math.log2(math.e)

LANES = 128
SUBLANES = 8
BF16_SUBLANES = 16
VT_ROWS = HEAD_DIM + BF16_SUBLANES
VMEM_LIMIT_BYTES = 56 * 1024 * 1024

ROW_TILE = 512
READOUT_ROW_TILE = 512
FFN_TILE = 512
WEIGHT_LOOKAHEAD = 2
PROJ_COL_TILE = 2048
PROJ_ROW_GROUP = 3
MOD_COL_TILE = 1024
ATTN_Q_TILE = 1024
SCAN_CHUNK = 128
SCAN_HEADS_PER_BLOCK = 8
SCAN_CHUNKS_PER_STEP = 2
FFT_B = 128
FFT_B_BLOCK = 8


def _cparams(*sem):
    return pltpu.CompilerParams(dimension_semantics=sem, vmem_limit_bytes=VMEM_LIMIT_BYTES)


def _dot(a, b):
    return jnp.dot(a, b, preferred_element_type=F32)


def _dot_hi(a, b):
    return jnp.dot(a, b, preferred_element_type=F32, precision=lax.Precision.HIGHEST)


def _split_bf16(x):
    hi = x.astype(BF16)
    return hi, (x - hi.astype(F32)).astype(BF16)


def _dot_split(a, b):
    (ah, al), (bh, bl) = a, b
    return _dot(ah, bh) + _dot(al, bh) + _dot(ah, bl)


def _dot_nt(a, b):
    return lax.dot_general(a, b, (((1,), (1,)), ((), ())), preferred_element_type=F32)


def _dot_tn(a, b):
    return lax.dot_general(a, b, (((0,), (0,)), ((), ())), preferred_element_type=F32)


def _silu(x):
    return x * jax.nn.sigmoid(x)


def _rms(x, gain):
    return x * lax.rsqrt(jnp.mean(x * x, axis=-1, keepdims=True) + EPS) * gain


def _ada(h, gain, shift, scale):
    return _rms(h, gain) * (1.0 + scale) + shift


def _mod_kernel(c_ref, w_ref, b_ref, o_ref):
    a = _silu(c_ref[...]).astype(BF16)
    o_ref[...] = _dot(a, w_ref[...].astype(BF16)) + b_ref[...]


def _modulation(c_rows, w_mod, b_mod):
    depth, d, nd = w_mod.shape
    tn = MOD_COL_TILE
    return pl.pallas_call(
        _mod_kernel,
        out_shape=jax.ShapeDtypeStruct((depth, SUBLANES, nd), F32),
        grid=(depth, nd // tn),
        in_specs=[
            pl.BlockSpec((SUBLANES, d), lambda l, j: (0, 0)),
            pl.BlockSpec((None, d, tn), lambda l, j: (l, 0, j)),
            pl.BlockSpec((None, 1, tn), lambda l, j: (l, 0, j)),
        ],
        out_specs=pl.BlockSpec((None, SUBLANES, tn), lambda l, j: (l, 0, j)),
        compiler_params=_cparams("parallel", "arbitrary"),
        name="modulation",
    )(c_rows, w_mod, b_mod.reshape(depth, 1, nd))


class _Rows:
    def __init__(self, batch, seq, ctx_len, tile):
        assert seq % tile == 0 and (batch * ctx_len) % tile == 0
        self.batch, self.seq, self.ctx_len, self.tile = batch, seq, ctx_len, tile
        self.n_lat = batch * seq
        self.n = self.n_lat + batch * ctx_len
        self.lat_tiles = self.n_lat // tile
        self.tiles = self.n // tile
        self.tiles_per_batch = seq // tile

    def group(self, i):
        return jnp.where(i < self.lat_tiles, 1 + i // self.tiles_per_batch, 0)


def _ffn_kernel(*refs, final, split_tiles, layer, which, nf):
    if split_tiles is None:
        h_ref, mod_ref, gain_ref, win_ref, wout_ref, fin_ref, o_ref, xn_ref, acc_ref, wa_buf, wb_buf, wo_buf, sem = refs
        load_h = lambda: h_ref[...]
    else:
        (hl_ref, hc_ref, mod_ref, gain_ref, win_ref, wout_ref, fin_ref, o_ref, xn_ref, acc_ref,
         wa_buf, wb_buf, wo_buf, sem) = refs
        load_h = lambda: jnp.where(pl.program_id(0) < split_tiles, hl_ref[...], hc_ref[...])
    tf = wa_buf.shape[2]

    def copies(j, slot):
        aligned = (lambda v: v) if isinstance(j, int) else (lambda v: pl.multiple_of(v, tf))
        ca = aligned(j * tf)
        cb = aligned((nf + j) * tf)
        return (pltpu.make_async_copy(win_ref.at[layer, which, :, pl.ds(ca, tf)], wa_buf.at[slot], sem.at[0, slot]),
                pltpu.make_async_copy(win_ref.at[layer, which, :, pl.ds(cb, tf)], wb_buf.at[slot], sem.at[1, slot]),
                pltpu.make_async_copy(wout_ref.at[layer, which, pl.ds(ca, tf), :], wo_buf.at[slot], sem.at[2, slot]))

    slots = wa_buf.shape[0]
    total = pl.num_programs(0) * nf
    t0 = pl.program_id(0) * nf

    def start(t):
        for c in copies(t % nf, t % slots):
            c.start()

    @pl.when(pl.program_id(0) == 0)
    def _():
        acc_ref[...] = jnp.zeros_like(acc_ref)
        for t in range(WEIGHT_LOOKAHEAD):
            start(t)

    xn_ref[...] = _ada(load_h(), gain_ref[...], mod_ref[0:1, :], mod_ref[1:2, :]).astype(BF16)

    def hidden_tile(j, carry):
        t = t0 + j
        slot = t % slots
        for c in copies(j, slot):
            c.wait()

        @pl.when(t + WEIGHT_LOOKAHEAD < total)
        def _():
            start(t + WEIGHT_LOOKAHEAD)

        xn = xn_ref[...]
        g = (_silu(_dot(xn, wa_buf[slot])) * _dot(xn, wb_buf[slot])).astype(BF16)
        acc_ref[...] = jnp.where(j == 0, 0.0, acc_ref[...]) + _dot(g, wo_buf[slot])
        return carry

    lax.fori_loop(0, nf, hidden_tile, 0)

    out = load_h() + 0.5 * mod_ref[2:3, :] * acc_ref[...]
    if final:
        out = _rms(out, fin_ref[...])
    o_ref[...] = out


def _ffn(h, mods, sub, gain, w_in, w_out, layer, which, fin, rows, n_tiles, final):
    f, d = w_out.shape[2:]
    tm, tf = rows.tile, FFN_TILE
    nf = f // tf
    ns = WEIGHT_LOOKAHEAD + 1
    assert n_tiles * nf >= WEIGHT_LOOKAHEAD
    if isinstance(h, tuple):
        lt = rows.lat_tiles
        row_specs = [pl.BlockSpec((tm, d), lambda i: (jnp.minimum(i, lt - 1), 0)),
                     pl.BlockSpec((tm, d), lambda i: (jnp.maximum(i - lt, 0), 0))]
        row_args, split_tiles = h, lt
    else:
        row_specs = [pl.BlockSpec((tm, d), lambda i: (i, 0))]
        row_args, split_tiles = (h,), None
    return pl.pallas_call(
        functools.partial(_ffn_kernel, final=final, split_tiles=split_tiles, layer=layer, which=which, nf=nf),
        out_shape=jax.ShapeDtypeStruct((n_tiles * tm, d), F32),
        grid=(n_tiles,),
        in_specs=row_specs + [
            pl.BlockSpec((None, None, 3, d), lambda i: (rows.group(i), sub, 0, 0)),
            pl.BlockSpec((1, d), lambda i: (0, 0)),
            pl.BlockSpec(memory_space=pl.ANY),
            pl.BlockSpec(memory_space=pl.ANY),
            pl.BlockSpec((1, d), lambda i: (0, 0)),
        ],
        out_specs=pl.BlockSpec((tm, d), lambda i: (i, 0)),
        scratch_shapes=[pltpu.VMEM((tm, d), BF16), pltpu.VMEM((tm, d), F32),
                        pltpu.VMEM((ns, d, tf), BF16), pltpu.VMEM((ns, d, tf), BF16), pltpu.VMEM((ns, tf, d), BF16),
                        pltpu.SemaphoreType.DMA((3, ns))],
        compiler_params=_cparams("arbitrary"),
        name="ffn",
    )(*row_args, mods, gain, w_in, w_out, fin)


def _rope_tables(seq, tile):
    t = np.arange(seq)
    inv_freq = ROPE_THETA ** (-np.arange(0, ROPE_AXIS_DIM, 2, dtype=np.float64) / ROPE_AXIS_DIM)
    ang = np.concatenate([(t // GRID_W)[:, None] * inv_freq, (t % GRID_W)[:, None] * inv_freq], axis=-1)
    nf = ROPE_AXIS_DIM // 2
    cos = np.cos(ang).reshape(seq, 2, 1, nf)
    sin = np.sin(ang).reshape(seq, 2, 1, nf)
    zero = np.zeros_like(sin)
    c_full = np.broadcast_to(cos, (seq, 2, 2, nf)).reshape(seq, HEAD_DIM)
    s_up = np.concatenate([-sin, zero], axis=2).reshape(seq, HEAD_DIM)
    s_dn = np.concatenate([zero, sin], axis=2).reshape(seq, HEAD_DIM)
    lat = np.concatenate([c_full, s_up, s_dn], axis=1)
    ident = np.concatenate([np.ones((tile, HEAD_DIM)), np.zeros((tile, 2 * HEAD_DIM))], axis=1)
    return jnp.asarray(np.concatenate([lat, ident], axis=0), dtype=F32)


def _norm_rope_heads(acc, gain, rope, post_scale=None):
    nf = ROPE_AXIS_DIM // 2
    c, s_up, s_dn = rope[:, :HEAD_DIM], rope[:, HEAD_DIM:2 * HEAD_DIM], rope[:, 2 * HEAD_DIM:]
    heads = []
    for hh in range(acc.shape[1] // HEAD_DIM):
        y = _rms(acc[:, hh * HEAD_DIM:(hh + 1) * HEAD_DIM], gain)
        y = y * c + pltpu.roll(y, HEAD_DIM - nf, 1) * s_up + pltpu.roll(y, nf, 1) * s_dn
        heads.append(y if post_scale is None else y * post_scale)
    return jnp.concatenate(heads, axis=1).astype(BF16)


def _ab_proj_kernel(h_ref, mod_ref, gain_ref, w_ref, qkn_ref, rope_ref, f_ref, q_ref, k_ref, vt_ref):
    xn = _ada(h_ref[...], gain_ref[...], mod_ref[0:1, :], mod_ref[1:2, :]).astype(BF16)
    c0 = f_ref.shape[1]
    c1 = c0 + q_ref.shape[1]
    c2 = c1 + k_ref.shape[1]
    q_ref[...] = _norm_rope_heads(_dot(xn, w_ref[:, c0:c1]), qkn_ref[0:1, :], rope_ref[...], ATTN_SCALE * LOG2_E)
    k_ref[...] = _norm_rope_heads(_dot(xn, w_ref[:, c1:c2]), qkn_ref[1:2, :], rope_ref[...])
    v = _dot(xn, w_ref[:, c2:])
    ones = jnp.ones((vt_ref.shape[1] - HEAD_DIM, vt_ref.shape[2]), BF16)
    for hh in range(N_KV_HEADS):
        vt_ref[hh, :HEAD_DIM, :] = v[:, hh * HEAD_DIM:(hh + 1) * HEAD_DIM].T.astype(BF16)
        vt_ref[hh, HEAD_DIM:, :] = ones
    f_ref[...] = _dot(xn, w_ref[:, :c0])


def _ab_proj(h, mods, gain, w_in, qk_norm, rope, rows):
    d = h.shape[1]
    tm = rows.tile
    kv_width = N_KV_HEADS * HEAD_DIM
    q_width = w_in.shape[1] - FOURIER_WIDTH - 2 * kv_width
    n = rows.n
    rope_blk = lambda i: (jnp.where(i < rows.lat_tiles, i % rows.tiles_per_batch, rows.tiles_per_batch), 0)
    return pl.pallas_call(
        _ab_proj_kernel,
        out_shape=(
            jax.ShapeDtypeStruct((n, FOURIER_WIDTH), F32),
            jax.ShapeDtypeStruct((n, q_width), BF16),
            jax.ShapeDtypeStruct((n, kv_width), BF16),
            jax.ShapeDtypeStruct((N_KV_HEADS, rows.tiles, VT_ROWS, tm), BF16),
        ),
        grid=(rows.tiles,),
        in_specs=[
            pl.BlockSpec((tm, d), lambda i: (i, 0)),
            pl.BlockSpec((None, None, 3, d), lambda i: (rows.group(i), 1, 0, 0)),
            pl.BlockSpec((1, d), lambda i: (0, 0)),
            pl.BlockSpec(w_in.shape, lambda i: (0, 0)),
            pl.BlockSpec((2, HEAD_DIM), lambda i: (0, 0)),
            pl.BlockSpec((tm, 3 * HEAD_DIM), rope_blk),
        ],
        out_specs=(
            pl.BlockSpec((tm, FOURIER_WIDTH), lambda i: (i, 0)),
            pl.BlockSpec((tm, q_width), lambda i: (i, 0)),
            pl.BlockSpec((tm, kv_width), lambda i: (i, 0)),
            pl.BlockSpec((N_KV_HEADS, None, VT_ROWS, tm), lambda i: (0, i, 0, 0)),
        ),
        compiler_params=_cparams("parallel"),
        name="ab_proj",
    )(h, mods, gain, w_in, qk_norm, rope)


def _attn_kernel(*refs, lat_chunks, group):
    if lat_chunks:
        q_ref, kc_ref, vtc_ref, kl_ref, vtl_ref, o_ref, acc_ref, s_ref = refs
    else:
        q_ref, kc_ref, vtc_ref, o_ref, acc_ref = refs
    tq = q_ref.shape[0]
    q = q_ref[...]
    qs = jnp.concatenate([q[:, g * HEAD_DIM:(g + 1) * HEAD_DIM] for g in range(group)], axis=0)
    nq = group * tq
    acc_ref[...] = jnp.zeros_like(acc_ref)

    def scores(k):
        return _dot_nt(k, qs)

    def update(s, vt, m):
        m_new = jnp.maximum(m, jnp.max(s, axis=0, keepdims=True))
        p = jnp.exp2(s - m_new).astype(BF16)
        acc_ref[...] = jnp.exp2(m - m_new) * acc_ref[...] + _dot(vt, p)
        return m_new

    m = update(scores(kc_ref[...]), vtc_ref[...], jnp.full((1, nq), -jnp.inf, F32))
    if lat_chunks:
        tk = vtl_ref.shape[2]

        def lat_scores(c):
            return scores(kl_ref[pl.ds(pl.multiple_of(c * tk, tk), tk), :])

        assert lat_chunks % 2 == 0
        s_ref[0] = lat_scores(0)

        def body(i, m):
            c = 2 * i
            s_ref[1] = lat_scores(c + 1)
            m = update(s_ref[0], vtl_ref[c], m)
            s_ref[0] = lat_scores(jnp.minimum(c + 2, lat_chunks - 1))
            return update(s_ref[1], vtl_ref[c + 1], m)

        m = lax.fori_loop(0, lat_chunks // 2, body, m)

    out = (acc_ref[:HEAD_DIM, :] / acc_ref[HEAD_DIM:HEAD_DIM + 1, :]).T
    o_ref[...] = jnp.concatenate([out[g * tq:(g + 1) * tq, :] for g in range(group)], axis=1).astype(BF16)


def _attention(q_all, k_all, vt_all, rows, latent):
    b, seq, lc, tile = rows.batch, rows.seq, rows.ctx_len, rows.tile
    group = q_all.shape[1] // (N_KV_HEADS * HEAD_DIM)
    gw = group * HEAD_DIM
    assert tile % lc == 0 and seq % tile == 0
    ctx_blk0 = rows.n_lat // lc
    ctx_tile = lambda bi: ((rows.n_lat + bi * lc) // tile, 0, ((rows.n_lat + bi * lc) % tile) // lc)
    kc_spec = pl.BlockSpec((lc, HEAD_DIM), lambda bi, hi, i: (ctx_blk0 + bi, hi))
    vtc_spec = pl.BlockSpec((None, None, VT_ROWS, lc), lambda bi, hi, i: (hi,) + ctx_tile(bi))
    if latent:
        tq = ATTN_Q_TILE
        assert tq % tile == 0
        nqt = seq // tq
        lat_chunks = seq // tile
        q_spec = pl.BlockSpec((tq, gw), lambda bi, hi, i: (bi * nqt + i, hi))
        kl_spec = pl.BlockSpec((seq, HEAD_DIM), lambda bi, hi, i: (bi, hi))
        vtl_spec = pl.BlockSpec((None, lat_chunks, VT_ROWS, tile), lambda bi, hi, i: (hi, bi, 0, 0))
        in_specs = [q_spec, kc_spec, vtc_spec, kl_spec, vtl_spec]
        args = (q_all, k_all, vt_all, k_all, vt_all)
    else:
        tq = lc
        nqt = 1
        lat_chunks = 0
        q_spec = pl.BlockSpec((tq, gw), lambda bi, hi, i: (ctx_blk0 + bi, hi))
        in_specs = [q_spec, kc_spec, vtc_spec]
        args = (q_all, k_all, vt_all)
    return pl.pallas_call(
        functools.partial(_attn_kernel, lat_chunks=lat_chunks, group=group),
        out_shape=jax.ShapeDtypeStruct((b * nqt * tq, q_all.shape[1]), BF16),
        grid=(b, N_KV_HEADS, nqt),
        in_specs=in_specs,
        out_specs=pl.BlockSpec((tq, gw), lambda bi, hi, i: (bi * nqt + i, hi)),
        scratch_shapes=[pltpu.VMEM((VT_ROWS, group * tq), F32)]
        + ([pltpu.VMEM((2, tile, group * tq), F32)] if lat_chunks else []),
        compiler_params=_cparams("parallel", "parallel", "arbitrary"),
        name="attention_ctx" if lat_chunks == 0 else "attention_lat",
    )(*args)


def _dft_cs(n):
    idx = np.arange(n)
    ang = 2.0 * np.pi * ((idx[:, None] * idx[None, :]) % n) / n
    return np.cos(ang), np.sin(ang)


def _const_split(m):
    m = np.asarray(m, dtype=np.float32)
    hi = jnp.asarray(m).astype(BF16)
    lo = (jnp.asarray(m) - hi.astype(F32)).astype(BF16)
    return jnp.stack([hi, lo])


def _fft1_kernel(x_ref, fa_ref, tw_ref, o_ref, *, a):
    w = o_ref.shape[2]
    for r in range(o_ref.shape[0]):
        z = _dot_split((fa_ref[0], fa_ref[1]), _split_bf16(x_ref[:, r * w:(r + 1) * w]))
        zr, zi = z[:a], z[a:]
        tc = jnp.concatenate([tw_ref[r, 0]] * (w // LANES), axis=1)
        ts = jnp.concatenate([tw_ref[r, 1]] * (w // LANES), axis=1)
        o_ref[r, :a, :] = zr * tc - zi * ts
        o_ref[r, a:, :] = zr * ts + zi * tc


def _fft2_kernel(zr_ref, zi_ref, m2_ref, mc_ref, o_ref, *, scale):
    for r in range(o_ref.shape[1]):
        z = jnp.concatenate([zr_ref[:, r, :], zi_ref[:, r, :]], axis=0)
        v = _dot_split((m2_ref[0], m2_ref[1]), _split_bf16(z))
        vr, vi = v[:FFT_B], v[FFT_B:]
        outs = []
        for g in range(o_ref.shape[2] // LANES):
            u = jnp.concatenate([vr[:, g * LANES:(g + 1) * LANES], vi[:, g * LANES:(g + 1) * LANES]], axis=1)
            outs.append(_dot_split(_split_bf16(u), (mc_ref[0], mc_ref[1])))
        o_ref[:, r, :] = jnp.concatenate(outs, axis=1) * scale


def _fourier_latent(f_all, rows):
    b, seq = rows.batch, rows.seq
    w = FOURIER_WIDTH
    a = seq // FFT_B
    assert a % SUBLANES == 0 and FOURIER_GROUP_DIM == LANES
    ca, sa = _dft_cs(a)
    fa = _const_split(np.concatenate([ca, sa], axis=0))
    p1b = (np.arange(a)[None, :] * np.arange(FFT_B)[:, None]) % seq
    ang = 2.0 * np.pi * p1b / seq
    tw = np.stack([np.cos(ang), np.sin(ang)], axis=1)[..., None]
    tw = jnp.asarray(np.broadcast_to(tw, (FFT_B, 2, a, LANES)), dtype=F32)
    cb, sb = _dft_cs(FFT_B)
    m2 = _const_split(np.block([[cb, -sb], [sb, cb]]))
    cc, sc = _dft_cs(FOURIER_GROUP_DIM)
    mc = _const_split(np.concatenate([cc, -sc], axis=0))

    blk = FFT_B_BLOCK
    x2 = f_all.reshape(rows.n // FFT_B, FFT_B * w)
    z = pl.pallas_call(
        functools.partial(_fft1_kernel, a=a),
        out_shape=jax.ShapeDtypeStruct((b, FFT_B, 2 * a, w), F32),
        grid=(b, FFT_B // blk),
        in_specs=[
            pl.BlockSpec((a, blk * w), lambda bi, j: (bi, j)),
            pl.BlockSpec((2, 2 * a, a), lambda bi, j: (0, 0, 0)),
            pl.BlockSpec((blk, 2, a, LANES), lambda bi, j: (j, 0, 0, 0)),
        ],
        out_specs=pl.BlockSpec((None, blk, 2 * a, w), lambda bi, j: (bi, j, 0, 0)),
        compiler_params=_cparams("parallel", "parallel"),
        name="fourier_stage1",
    )(x2, fa, tw)

    nblk = a // blk
    y = pl.pallas_call(
        functools.partial(_fft2_kernel, scale=1.0 / math.sqrt(seq * FOURIER_GROUP_DIM)),
        out_shape=jax.ShapeDtypeStruct((b, FFT_B, a, w), F32),
        grid=(b, nblk),
        in_specs=[
            pl.BlockSpec((None, FFT_B, blk, w), lambda bi, j: (bi, 0, j, 0)),
            pl.BlockSpec((None, FFT_B, blk, w), lambda bi, j: (bi, 0, nblk + j, 0)),
            pl.BlockSpec((2, 2 * FFT_B, 2 * FFT_B), lambda bi, j: (0, 0, 0)),
            pl.BlockSpec((2, 2 * LANES, LANES), lambda bi, j: (0, 0, 0)),
        ],
        out_specs=pl.BlockSpec((None, FFT_B, blk, w), lambda bi, j: (bi, 0, j, 0)),
        compiler_params=_cparams("parallel", "parallel"),
        name="fourier_stage2",
    )(z, z, m2, mc)
    return y.reshape(b * seq, w)


def _dft_ctx_kernel(x_ref, cn_ref, sn_ref, cc_ref, sc_ref, o_ref, *, scale):
    x = x_ref[...]
    outs = []
    for g in range(x.shape[1] // LANES):
        xg = x[:, g * LANES:(g + 1) * LANES]
        outs.append(_dot_hi(cn_ref[...], _dot_hi(xg, cc_ref[...])) - _dot_hi(sn_ref[...], _dot_hi(xg, sc_ref[...])))
    o_ref[...] = jnp.concatenate(outs, axis=1) * scale


def _fourier_ctx(f_all, rows):
    b, lc = rows.batch, rows.ctx_len
    w = FOURIER_WIDTH
    cn, sn = _dft_cs(lc)
    cc, sc = _dft_cs(FOURIER_GROUP_DIM)
    blk0 = rows.n_lat // lc
    mat = lambda m: pl.BlockSpec(m.shape, lambda bi: (0, 0))
    consts = [jnp.asarray(m, dtype=F32) for m in (cn, sn, cc, sc)]
    return pl.pallas_call(
        functools.partial(_dft_ctx_kernel, scale=1.0 / math.sqrt(lc * FOURIER_GROUP_DIM)),
        out_shape=jax.ShapeDtypeStruct((b * lc, w), F32),
        grid=(b,),
        in_specs=[pl.BlockSpec((lc, w), lambda bi: (blk0 + bi, 0))] + [mat(m) for m in consts],
        out_specs=pl.BlockSpec((lc, w), lambda bi: (bi, 0)),
        compiler_params=_cparams("parallel"),
        name="fourier_ctx",
    )(f_all, *consts)


def _ab_out_kernel(x1l_ref, x1c_ref, x2l_ref, x2c_ref, w_ref, h_ref, mod_ref, o_ref, *, lat_tiles):
    is_lat = pl.program_id(0) < lat_tiles
    x1 = jnp.where(is_lat, x1l_ref[...], x1c_ref[...])
    x2 = jnp.where(is_lat, x2l_ref[...], x2c_ref[...])
    x = jnp.concatenate([x1.astype(BF16), x2], axis=1)
    o_ref[...] = h_ref[...] + mod_ref[2:3, :] * _dot(x, w_ref[...])


def _ab_out(x1_lat, x1_ctx, x2_lat, x2_ctx, w_out, h, mods, rows):
    d = h.shape[1]
    tm = rows.tile
    lt = rows.lat_tiles
    lat_blk = lambda i: (jnp.minimum(i, lt - 1), 0)
    ctx_blk = lambda i: (jnp.maximum(i - lt, 0), 0)
    return pl.pallas_call(
        functools.partial(_ab_out_kernel, lat_tiles=lt),
        out_shape=jax.ShapeDtypeStruct((rows.n, d), F32),
        grid=(rows.tiles,),
        in_specs=[
            pl.BlockSpec((tm, x1_lat.shape[1]), lat_blk),
            pl.BlockSpec((tm, x1_ctx.shape[1]), ctx_blk),
            pl.BlockSpec((tm, x2_lat.shape[1]), lat_blk),
            pl.BlockSpec((tm, x2_ctx.shape[1]), ctx_blk),
            pl.BlockSpec(w_out.shape, lambda i: (0, 0)),
            pl.BlockSpec((tm, d), lambda i: (i, 0)),
            pl.BlockSpec((None, None, 3, d), lambda i: (rows.group(i), 1, 0, 0)),
        ],
        out_specs=pl.BlockSpec((tm, d), lambda i: (i, 0)),
        compiler_params=_cparams("parallel"),
        name="ab_out",
    )(x1_lat, x1_ctx, x2_lat, x2_ctx, w_out, h, mods)


def _hgrn_out_kernel(osum_ref, g_ref, gain_ref, w_ref, h_ref, mod_ref, o_ref):
    o = osum_ref[...]
    parts = []
    for hh in range(o.shape[1] // HEAD_DIM):
        oh = o[:, hh * HEAD_DIM:(hh + 1) * HEAD_DIM]
        parts.append(oh * lax.rsqrt(jnp.mean(oh * oh, axis=-1, keepdims=True) + EPS))
    on = jnp.concatenate(parts, axis=1) * gain_ref[...]
    y = (on * jax.nn.sigmoid(g_ref[...])).astype(BF16)
    o_ref[...] = h_ref[...] + mod_ref[2:3, :] * _dot(y, w_ref[...])


def _hgrn_out(o_sum, p, gain, w_out, h, mods, rows_r, n_tiles):
    d = h.shape[1]
    tm = rows_r.tile
    g_blk = (p.shape[1] - d) // d
    return pl.pallas_call(
        _hgrn_out_kernel,
        out_shape=jax.ShapeDtypeStruct((n_tiles * tm, d), F32),
        grid=(n_tiles,),
        in_specs=[
            pl.BlockSpec((tm, d), lambda i: (i, 0)),
            pl.BlockSpec((tm, d), lambda i: (i, g_blk)),
            pl.BlockSpec((1, d), lambda i: (0, 0)),
            pl.BlockSpec(w_out.shape, lambda i: (0, 0)),
            pl.BlockSpec((tm, d), lambda i: (i, 0)),
            pl.BlockSpec((None, None, 3, d), lambda i: (rows_r.group(i), 1, 0, 0)),
        ],
        out_specs=pl.BlockSpec((tm, d), lambda i: (i, 0)),
        compiler_params=_cparams("parallel"),
        name="hgrn_out",
    )(o_sum, p, gain, w_out, h, mods)


def _hgrn_proj_kernel(h_ref, mod_ref, gain_ref, w_ref, o_ref, xn_ref, *, n_silu):
    j = pl.program_id(1)
    r = pl.program_id(2)

    @pl.when(j == 0)
    def _():
        xn_ref[r] = _ada(h_ref[...], gain_ref[...], mod_ref[0:1, :], mod_ref[1:2, :]).astype(BF16)

    acc = _dot(xn_ref[r], w_ref[...])
    o_ref[...] = jnp.where(j < n_silu, _silu(acc), acc)


def _hgrn_proj(h, mods, gain, w_in, rows):
    d = h.shape[1]
    tm, tn = rows.tile, PROJ_COL_TILE
    nw = w_in.shape[1]
    gs = next(k for k in (PROJ_ROW_GROUP, 2, 1) if rows.tiles % k == 0)
    tile = lambda g, r: g * gs + r
    return pl.pallas_call(
        functools.partial(_hgrn_proj_kernel, n_silu=d // tn),
        out_shape=jax.ShapeDtypeStruct((rows.n, nw), F32),
        grid=(rows.tiles // gs, nw // tn, gs),
        in_specs=[
            pl.BlockSpec((tm, d), lambda g, j, r: (jnp.where(j == 0, tile(g, r), tile(g, gs - 1)), 0)),
            pl.BlockSpec((None, None, 3, d), lambda g, j, r: (rows.group(tile(g, r)), 1, 0, 0)),
            pl.BlockSpec((1, d), lambda g, j, r: (0, 0)),
            pl.BlockSpec((d, tn), lambda g, j, r: (0, j)),
        ],
        out_specs=pl.BlockSpec((tm, tn), lambda g, j, r: (tile(g, r), j)),
        scratch_shapes=[pltpu.VMEM((gs, tm, d), BF16)],
        compiler_params=_cparams("arbitrary", "arbitrary", "arbitrary"),
        name="hgrn_proj",
    )(h, mods, gain, w_in)


def _scan_levels(chunk):
    return [chunk >> (i + 1) for i in range(int(math.log2(chunk)))]


def _scan_consts(chunk, reverse):
    t = np.arange(chunk)[:, None]
    s = np.arange(chunk)[None, :]
    tri = (s >= t) if reverse else (s <= t)
    wide, narrow = [], []
    for h in _scan_levels(chunk):
        same = (t // (2 * h)) == (s // (2 * h))
        t_up = (t // h) % 2 == 1
        s_up = (s // h) % 2 == 1
        mask = same & ((~t_up & s_up) if reverse else (t_up & ~s_up))
        if h >= SUBLANES:
            wide.append(mask[np.nonzero(~t_up[:, 0] if reverse else t_up[:, 0])[0]])
        else:
            narrow.append(mask)
    narrow.append(t == s)
    return (jnp.asarray(tri, dtype=BF16), jnp.asarray(np.stack(wide), dtype=F32),
            jnp.asarray(np.stack(narrow), dtype=F32))


def _seg_bcast(x, h, reverse):
    c, w = x.shape
    off = h if reverse else h - 1
    if 2 * h >= 2 * SUBLANES:
        pieces = [jnp.broadcast_to(x[g * 2 * h + off:g * 2 * h + off + 1, :], (2 * h, w)) for g in range(c // (2 * h))]
        return pieces[0] if len(pieces) == 1 else jnp.concatenate(pieces, axis=0)
    x3 = x.reshape(c // SUBLANES, SUBLANES, w)
    sub = lax.broadcasted_iota(jnp.int32, x3.shape, 1)
    y = None
    for g in range(SUBLANES // (2 * h)):
        piece = jnp.broadcast_to(x3[:, g * 2 * h + off:g * 2 * h + off + 1, :], x3.shape)
        y = piece if y is None else jnp.where(sub >= g * 2 * h, piece, y)
    return y.reshape(c, w)


def _scan_kernel(*refs, reverse, chunk, accumulate):
    if accumulate:
        q_ref, f_ref, v_ref, lb_ref, tri_ref, wmask_ref, nmask_ref, other_ref, o_ref, st_ref = refs
    else:
        q_ref, f_ref, v_ref, lb_ref, tri_ref, wmask_ref, nmask_ref, o_ref, st_ref = refs

    @pl.when(pl.program_id(2) == 0)
    def _():
        st_ref[...] = jnp.zeros_like(st_ref)

    rows, width = q_ref.shape
    levels = _scan_levels(chunk)
    spans = [slice(ci * chunk, (ci + 1) * chunk) for ci in range(rows // chunk)]
    lanes = [slice(hh * HEAD_DIM, (hh + 1) * HEAD_DIM) for hh in range(width // HEAD_DIM)]
    q = q_ref[...]
    fl = f_ref[...]
    lb = lb_ref[...]
    u = jnp.exp(-jnp.abs(fl))
    key = (1.0 - lb) * (jnp.where(fl > 0.0, u, 1.0) / (1.0 + u))
    log_sig = jnp.minimum(fl, 0.0) - jnp.log(1.0 + u)
    x1 = jnp.log(lb)
    x2 = jnp.log1p(-lb) + log_sig
    delta = x1 - x2
    lf = jnp.where(jnp.isnan(delta), x1 + x2, jnp.maximum(x1, x2) + jnp.log(1.0 + jnp.exp(-jnp.abs(delta))))
    lf = lf * LOG2_E
    l1 = lf.astype(BF16)
    r1 = lf - l1.astype(F32)
    l2 = r1.astype(BF16)
    l3 = (r1 - l2.astype(F32)).astype(BF16)
    parts = jnp.concatenate([l1, l2, l3], axis=1)
    tri = tri_ref[...]
    cs = jnp.concatenate([_dot(tri, parts[sp]) for sp in spans], axis=0)
    b = cs[:, :width] + cs[:, width:2 * width] + cs[:, 2 * width:]
    totals = [b[sp.start:sp.start + 1, :] if reverse else b[sp.stop - 1:sp.stop, :] for sp in spans]
    total_rows = jnp.concatenate([jnp.broadcast_to(t, (chunk, width)) for t in totals], axis=0)

    q_in = (q * jnp.exp2(b)).astype(BF16)
    k_out = (key * jnp.exp2(total_rows - b)).astype(BF16)
    v = v_ref[...].astype(BF16)

    order = list(range(len(spans)))[::-1] if reverse else list(range(len(spans)))
    kv = [[_dot_tn(v[sp, sl], k_out[sp, sl]) for sl in lanes] for sp in spans]
    o_state = [[None] * len(lanes) for _ in spans]
    for hh, sl in enumerate(lanes):
        st = st_ref[hh]
        for ci in order:
            o_state[ci][hh] = _dot_nt(q_in[spans[ci], sl], st.astype(BF16))
            st = st * jnp.exp2(totals[ci][:, sl]) + kv[ci][hh]
        st_ref[hh] = st

    n_blk = chunk // SUBLANES
    wide_levels = [h for h in levels if h >= SUBLANES]
    narrow_levels = [h for h in levels if h < SUBLANES]

    def narrow_scores(li, ql, kl):
        out = []
        for sp in spans:
            row = []
            for sl in lanes:
                s = nmask_ref[li] * _dot_nt(ql[sp, sl], kl[sp, sl])
                row.append([s[k * SUBLANES:(k + 1) * SUBLANES, :] for k in range(n_blk)])
            out.append(row)
        return out

    a = narrow_scores(len(narrow_levels), q.astype(BF16), key.astype(BF16))
    row_id = lax.broadcasted_iota(jnp.int32, (rows, width), 0)
    for li, h in enumerate(narrow_levels):
        if h == 1:
            e = jnp.where((row_id & 1) == (0 if reverse else 1), jnp.exp2(lf), 1.0)
        else:
            e = jnp.exp2(-jnp.abs(b - _seg_bcast(b, h, reverse)))
        new = narrow_scores(li, (q * e).astype(BF16), (key * e).astype(BF16))
        a = [[[x + y for x, y in zip(xb, yb)] for xb, yb in zip(xa, ya)] for xa, ya in zip(a, new)]

    for li, h in enumerate(wide_levels):
        q_parts, k_parts = [], []
        for g in range(rows // (2 * h)):
            lo = slice(g * 2 * h, g * 2 * h + h)
            hi = slice(g * 2 * h + h, (g + 1) * 2 * h)
            r = g * 2 * h + (h if reverse else h - 1)
            b_ref = jnp.broadcast_to(b[r:r + 1, :], (h, width))
            q_half, k_half = (lo, hi) if reverse else (hi, lo)
            q_parts.append(q[q_half] * jnp.exp2(b[q_half] - b_ref))
            k_part = key[k_half] * jnp.exp2(b_ref - b[k_half])
            zeros = jnp.zeros((h, width), F32)
            k_parts += [zeros, k_part] if reverse else [k_part, zeros]
        q_sel = jnp.concatenate(q_parts, axis=0).astype(BF16)
        k_hat = jnp.concatenate(k_parts, axis=0).astype(BF16)
        half = chunk // 2
        for ci, sp in enumerate(spans):
            for hh, sl in enumerate(lanes):
                s = wmask_ref[li] * _dot_nt(q_sel[ci * half:(ci + 1) * half, sl], k_hat[sp, sl])
                for j in range(chunk // (2 * h)):
                    first = (j * 2 * h + (0 if reverse else h)) // SUBLANES
                    for k in range(h // SUBLANES):
                        r0 = j * h + k * SUBLANES
                        a[ci][hh][first + k] = a[ci][hh][first + k] + s[r0:r0 + SUBLANES, :]

    for ci, sp in enumerate(spans):
        for hh, sl in enumerate(lanes):
            pairs = jnp.concatenate(a[ci][hh], axis=0).astype(BF16)
            o = o_state[ci][hh] + _dot(pairs, v[sp, sl])
            o_ref[sp, sl] = o + other_ref[sp, sl] if accumulate else o


def _hgrn_scan(p, lower_bound, rows, reverse, other=None):
    b, seq, lc = rows.batch, rows.seq, rows.ctx_len
    d = lower_bound.shape[0]
    c, hb = SCAN_CHUNK, SCAN_HEADS_PER_BLOCK
    r = c * SCAN_CHUNKS_PER_STEP
    wb = hb * HEAD_DIM
    ncb = d // wb
    assert lc % r == 0 and seq % r == 0
    nctx, nlat = lc // r, seq // r
    ctx0 = rows.n_lat // r
    f_blk = (2 if reverse else 1) * ncb
    v_blk = 3 * ncb

    def row(bi, s):
        if reverse:
            return jnp.where(s < nctx, ctx0 + bi * nctx + (nctx - 1 - s), bi * nlat + (nlat - 1 - (s - nctx)))
        return jnp.where(s < nctx, ctx0 + bi * nctx + s, bi * nlat + (s - nctx))

    tri, wmasks, nmasks = _scan_consts(c, reverse)
    o_spec = pl.BlockSpec((r, wb), lambda bi, hi, s: (row(bi, s), hi))
    extra = () if other is None else (other,)
    return pl.pallas_call(
        functools.partial(_scan_kernel, reverse=reverse, chunk=c, accumulate=other is not None),
        out_shape=jax.ShapeDtypeStruct((rows.n, d), F32),
        grid=(b, ncb, nctx + nlat),
        in_specs=[
            pl.BlockSpec((r, wb), lambda bi, hi, s: (row(bi, s), hi)),
            pl.BlockSpec((r, wb), lambda bi, hi, s: (row(bi, s), f_blk + hi)),
            pl.BlockSpec((r, wb), lambda bi, hi, s: (row(bi, s), v_blk + hi)),
            pl.BlockSpec((1, wb), lambda bi, hi, s: (0, hi)),
            pl.BlockSpec(tri.shape, lambda bi, hi, s: (0, 0)),
            pl.BlockSpec(wmasks.shape, lambda bi, hi, s: (0, 0, 0)),
            pl.BlockSpec(nmasks.shape, lambda bi, hi, s: (0, 0, 0)),
        ] + [o_spec] * len(extra),
        out_specs=o_spec,
        scratch_shapes=[pltpu.VMEM((hb, HEAD_DIM, HEAD_DIM), F32)],
        compiler_params=_cparams("parallel", "parallel", "arbitrary"),
        name="hgrn_scan_bw" if reverse else "hgrn_scan_fw",
    )(p, p, p, lower_bound.reshape(1, d), tri, wmasks, nmasks, *extra)


def kernel(x, c, ctx, c_ctx, w_mod, b_mod, norm_gains, ffn_w_in, ffn_w_out, ab_w_in, qk_norm, ab_w_out,
           hgrn_w_in, hgrn_lb_logits, hgrn_o_norm, hgrn_w_out, final_norm):
    batch, seq, d = x.shape
    lc = ctx.shape[1]
    depth = w_mod.shape[0]
    rows = _Rows(batch, seq, lc, ROW_TILE)
    rows_r = _Rows(batch, seq, lc, READOUT_ROW_TILE)
    assert seq % GRID_W == 0 and batch + 1 <= SUBLANES

    c_rows = jnp.concatenate([c_ctx[None, :], c, jnp.zeros((SUBLANES - 1 - batch, d), F32)], axis=0)
    mods_all = _modulation(c_rows, w_mod, b_mod).reshape(depth, SUBLANES, 3, 3, d)

    lb_cum = jnp.cumsum(jax.nn.softmax(hgrn_lb_logits.astype(F32), axis=0), axis=0)
    lower_bounds = lb_cum - lb_cum[0]

    ffn_w_in_b = ffn_w_in.astype(BF16)
    ffn_w_out_b = ffn_w_out.astype(BF16)
    fin = final_norm.reshape(1, d)
    rope = _rope_tables(seq, ROW_TILE)

    h = (x.reshape(batch * seq, d), ctx.reshape(batch * lc, d))
    for layer in range(depth):
        last = layer == depth - 1
        mods = mods_all[layer]
        gains = norm_gains[layer].reshape(3, 1, d)
        h = _ffn(h, mods, 0, gains[0], ffn_w_in_b, ffn_w_out_b, layer, 0, fin, rows, rows.tiles, False)
        if layer % 2 == 0:
            e = layer // 2
            f_all, q_all, k_all, vt_all = _ab_proj(h, mods, gains[1], ab_w_in[e].astype(BF16), qk_norm[e], rope, rows)
            h = _ab_out(_fourier_latent(f_all, rows), _fourier_ctx(f_all, rows),
                        _attention(q_all, k_all, vt_all, rows, True), _attention(q_all, k_all, vt_all, rows, False),
                        ab_w_out[e].astype(BF16), h, mods, rows)
        else:
            o = layer // 2
            p = _hgrn_proj(h, mods, gains[1], hgrn_w_in[o].astype(BF16), rows)
            o_fw = _hgrn_scan(p, lower_bounds[layer], rows, False)
            o_sum = _hgrn_scan(p, lower_bounds[layer], rows, True, other=o_fw)
            gain_o = jnp.tile(hgrn_o_norm[o], d // HEAD_DIM).reshape(1, d)
            n_t = rows_r.lat_tiles if last else rows_r.tiles
            h = _hgrn_out(o_sum, p, gain_o, hgrn_w_out[o].astype(BF16), h, mods, rows_r, n_t)
        n_t = rows.lat_tiles if last else rows.tiles
        h = _ffn(h, mods, 2, gains[2], ffn_w_in_b, ffn_w_out_b, layer, 1, fin, rows, n_t, last)
    return h[:batch * seq].reshape(batch, seq, d)
```

```python
import functools
import math

import jax
import jax.numpy as jnp
import numpy as np
from jax import lax
from jax.experimental import pallas as pl
from jax.experimental.pallas import tpu as pltpu

F32 = jnp.float32
BF16 = jnp.bfloat16

EPS = 1e-6
N_MOD = 9
HEAD_DIM = 128
N_KV_HEADS = 4
FOURIER_WIDTH = 512
FOURIER_GROUP_DIM = 128
GRID_W = 64
ROPE_THETA = 10000.0
ROPE_AXIS_DIM = HEAD_DIM // 2
ATTN_SCALE = HEAD_DIM ** -0.5
LOG2_E = math.log2(math.e)

LANES = 128
SUBLANES = 8
BF16_SUBLANES = 16
VT_ROWS = HEAD_DIM + BF16_SUBLANES
VMEM_LIMIT_BYTES = 56 * 1024 * 1024

ROW_TILE = 512
READOUT_ROW_TILE = 512
FFN_TILE = 512
WEIGHT_LOOKAHEAD = 2
PROJ_COL_TILE = 2048
PROJ_ROW_GROUP = 3
MOD_COL_TILE = 1024
ATTN_Q_TILE = 1024
SCAN_CHUNK = 128
SCAN_HEADS_PER_BLOCK = 8
SCAN_CHUNKS_PER_STEP = 2
FFT_B = 128
FFT_B_BLOCK = 8


def _cparams(*sem):
    return pltpu.CompilerParams(dimension_semantics=sem, vmem_limit_bytes=VMEM_LIMIT_BYTES)


def _dot(a, b):
    return jnp.dot(a, b, preferred_element_type=F32)


def _dot_hi(a, b):
    return jnp.dot(a, b, preferred_element_type=F32, precision=lax.Precision.HIGHEST)


def _split_bf16(x):
    hi = x.astype(BF16)
    return hi, (x - hi.astype(F32)).astype(BF16)


def _dot_split(a, b):
    (ah, al), (bh, bl) = a, b
    return _dot(ah, bh) + _dot(al, bh) + _dot(ah, bl)


def _dot_nt(a, b):
    return lax.dot_general(a, b, (((1,), (1,)), ((), ())), preferred_element_type=F32)


def _dot_tn(a, b):
    return lax.dot_general(a, b, (((0,), (0,)), ((), ())), preferred_element_type=F32)


def _silu(x):
    return x * jax.nn.sigmoid(x)


def _rms(x, gain):
    return x * lax.rsqrt(jnp.mean(x * x, axis=-1, keepdims=True) + EPS) * gain


def _ada(h, gain, shift, scale):
    return _rms(h, gain) * (1.0 + scale) + shift


def _mod_kernel(c_ref, w_ref, b_ref, o_ref):
    a = _silu(c_ref[...]).astype(BF16)
    o_ref[...] = _dot(a, w_ref[...].astype(BF16)) + b_ref[...]


def _modulation(c_rows, w_mod, b_mod):
    depth, d, nd = w_mod.shape
    tn = MOD_COL_TILE
    return pl.pallas_call(
        _mod_kernel,
        out_shape=jax.ShapeDtypeStruct((depth, SUBLANES, nd), F32),
        grid=(depth, nd // tn),
        in_specs=[
            pl.BlockSpec((SUBLANES, d), lambda l, j: (0, 0)),
            pl.BlockSpec((None, d, tn), lambda l, j: (l, 0, j)),
            pl.BlockSpec((None, 1, tn), lambda l, j: (l, 0, j)),
        ],
        out_specs=pl.BlockSpec((None, SUBLANES, tn), lambda l, j: (l, 0, j)),
        compiler_params=_cparams("parallel", "arbitrary"),
        name="modulation",
    )(c_rows, w_mod, b_mod.reshape(depth, 1, nd))


class _Rows:
    def __init__(self, batch, seq, ctx_len, tile):
        assert seq % tile == 0 and (batch * ctx_len) % tile == 0
        self.batch, self.seq, self.ctx_len, self.tile = batch, seq, ctx_len, tile
        self.n_lat = batch * seq
        self.n = self.n_lat + batch * ctx_len
        self.lat_tiles = self.n_lat // tile
        self.tiles = self.n // tile
        self.tiles_per_batch = seq // tile

    def group(self, i):
        return jnp.where(i < self.lat_tiles, 1 + i // self.tiles_per_batch, 0)


def _ffn_kernel(*refs, final, split_tiles, layer, which, nf):
    if split_tiles is None:
        h_ref, mod_ref, gain_ref, win_ref, wout_ref, fin_ref, o_ref, xn_ref, acc_ref, wa_buf, wb_buf, wo_buf, sem = refs
        load_h = lambda: h_ref[...]
    else:
        (hl_ref, hc_ref, mod_ref, gain_ref, win_ref, wout_ref, fin_ref, o_ref, xn_ref, acc_ref,
         wa_buf, wb_buf, wo_buf, sem) = refs
        load_h = lambda: jnp.where(pl.program_id(0) < split_tiles, hl_ref[...], hc_ref[...])
    tf = wa_buf.shape[2]

    def copies(j, slot):
        aligned = (lambda v: v) if isinstance(j, int) else (lambda v: pl.multiple_of(v, tf))
        ca = aligned(j * tf)
        cb = aligned((nf + j) * tf)
        return (pltpu.make_async_copy(win_ref.at[layer, which, :, pl.ds(ca, tf)], wa_buf.at[slot], sem.at[0, slot]),
                pltpu.make_async_copy(win_ref.at[layer, which, :, pl.ds(cb, tf)], wb_buf.at[slot], sem.at[1, slot]),
                pltpu.make_async_copy(wout_ref.at[layer, which, pl.ds(ca, tf), :], wo_buf.at[slot], sem.at[2, slot]))

    slots = wa_buf.shape[0]
    total = pl.num_programs(0) * nf
    t0 = pl.program_id(0) * nf

    def start(t):
        for k, c in enumerate(copies(t % nf, t % slots)):
            c.start(priority=k % 2)

    @pl.when(pl.program_id(0) == 0)
    def _():
        acc_ref[...] = jnp.zeros_like(acc_ref)
        for t in range(WEIGHT_LOOKAHEAD):
            start(t)

    xn_ref[...] = _ada(load_h(), gain_ref[...], mod_ref[0:1, :], mod_ref[1:2, :]).astype(BF16)

    def hidden_tile(j, carry):
        t = t0 + j
        slot = t % slots
        for c in copies(j, slot):
            c.wait()

        @pl.when(t + WEIGHT_LOOKAHEAD < total)
        def _():
            start(t + WEIGHT_LOOKAHEAD)

        xn = xn_ref[...]
        g = (_silu(_dot(xn, wa_buf[slot])) * _dot(xn, wb_buf[slot])).astype(BF16)
        acc_ref[...] = jnp.where(j == 0, 0.0, acc_ref[...]) + _dot(g, wo_buf[slot])
        return carry

    lax.fori_loop(0, nf, hidden_tile, 0)

    out = load_h() + 0.5 * mod_ref[2:3, :] * acc_ref[...]
    if final:
        out = _rms(out, fin_ref[...])
    o_ref[...] = out


def _ffn(h, mods, sub, gain, w_in, w_out, layer, which, fin, rows, n_tiles, final):
    f, d = w_out.shape[2:]
    tm, tf = rows.tile, FFN_TILE
    nf = f // tf
    ns = WEIGHT_LOOKAHEAD + 1
    assert n_tiles * nf >= WEIGHT_LOOKAHEAD
    if isinstance(h, tuple):
        lt = rows.lat_tiles
        row_specs = [pl.BlockSpec((tm, d), lambda i: (jnp.minimum(i, lt - 1), 0)),
                     pl.BlockSpec((tm, d), lambda i: (jnp.maximum(i - lt, 0), 0))]
        row_args, split_tiles = h, lt
    else:
        row_specs = [pl.BlockSpec((tm, d), lambda i: (i, 0))]
        row_args, split_tiles = (h,), None
    return pl.pallas_call(
        functools.partial(_ffn_kernel, final=final, split_tiles=split_tiles, layer=layer, which=which, nf=nf),
        out_shape=jax.ShapeDtypeStruct((n_tiles * tm, d), F32),
        grid=(n_tiles,),
        in_specs=row_specs + [
            pl.BlockSpec((None, None, 3, d), lambda i: (rows.group(i), sub, 0, 0)),
            pl.BlockSpec((1, d), lambda i: (0, 0)),
            pl.BlockSpec(memory_space=pl.ANY),
            pl.BlockSpec(memory_space=pl.ANY),
            pl.BlockSpec((1, d), lambda i: (0, 0)),
        ],
        out_specs=pl.BlockSpec((tm, d), lambda i: (i, 0)),
        scratch_shapes=[pltpu.VMEM((tm, d), BF16), pltpu.VMEM((tm, d), F32),
                        pltpu.VMEM((ns, d, tf), BF16), pltpu.VMEM((ns, d, tf), BF16), pltpu.VMEM((ns, tf, d), BF16),
                        pltpu.SemaphoreType.DMA((3, ns))],
        compiler_params=_cparams("arbitrary"),
        name="ffn",
    )(*row_args, mods, gain, w_in, w_out, fin)


def _rope_tables(seq, tile):
    t = np.arange(seq)
    inv_freq = ROPE_THETA ** (-np.arange(0, ROPE_AXIS_DIM, 2, dtype=np.float64) / ROPE_AXIS_DIM)
    ang = np.concatenate([(t // GRID_W)[:, None] * inv_freq, (t % GRID_W)[:, None] * inv_freq], axis=-1)
    nf = ROPE_AXIS_DIM // 2
    cos = np.cos(ang).reshape(seq, 2, 1, nf)
    sin = np.sin(ang).reshape(seq, 2, 1, nf)
    zero = np.zeros_like(sin)
    c_full = np.broadcast_to(cos, (seq, 2, 2, nf)).reshape(seq, HEAD_DIM)
    s_up = np.concatenate([-sin, zero], axis=2).reshape(seq, HEAD_DIM)
    s_dn = np.concatenate([zero, sin], axis=2).reshape(seq, HEAD_DIM)
    lat = np.concatenate([c_full, s_up, s_dn], axis=1)
    ident = np.concatenate([np.ones((tile, HEAD_DIM)), np.zeros((tile, 2 * HEAD_DIM))], axis=1)
    return jnp.asarray(np.concatenate([lat, ident], axis=0), dtype=F32)


def _norm_rope_heads(acc, gain, rope, post_scale=None):
    nf = ROPE_AXIS_DIM // 2
    c, s_up, s_dn = rope[:, :HEAD_DIM], rope[:, HEAD_DIM:2 * HEAD_DIM], rope[:, 2 * HEAD_DIM:]
    heads = []
    for hh in range(acc.shape[1] // HEAD_DIM):
        y = _rms(acc[:, hh * HEAD_DIM:(hh + 1) * HEAD_DIM], gain)
        y = y * c + pltpu.roll(y, HEAD_DIM - nf, 1) * s_up + pltpu.roll(y, nf, 1) * s_dn
        heads.append(y if post_scale is None else y * post_scale)
    return jnp.concatenate(heads, axis=1).astype(BF16)


def _ab_proj_kernel(h_ref, mod_ref, gain_ref, w_ref, qkn_ref, rope_ref, f_ref, q_ref, k_ref, vt_ref):
    xn = _ada(h_ref[...], gain_ref[...], mod_ref[0:1, :], mod_ref[1:2, :]).astype(BF16)
    c0 = f_ref.shape[1]
    c1 = c0 + q_ref.shape[1]
    c2 = c1 + k_ref.shape[1]
    q_ref[...] = _norm_rope_heads(_dot(xn, w_ref[:, c0:c1]), qkn_ref[0:1, :], rope_ref[...], ATTN_SCALE * LOG2_E)
    k_ref[...] = _norm_rope_heads(_dot(xn, w_ref[:, c1:c2]), qkn_ref[1:2, :], rope_ref[...])
    v = _dot(xn, w_ref[:, c2:])
    ones = jnp.ones((vt_ref.shape[1] - HEAD_DIM, vt_ref.shape[2]), BF16)
    for hh in range(N_KV_HEADS):
        vt_ref[hh, :HEAD_DIM, :] = v[:, hh * HEAD_DIM:(hh + 1) * HEAD_DIM].T.astype(BF16)
        vt_ref[hh, HEAD_DIM:, :] = ones
    f_ref[...] = _dot(xn, w_ref[:, :c0])


def _ab_proj(h, mods, gain, w_in, qk_norm, rope, rows):
    d = h.shape[1]
    tm = rows.tile
    kv_width = N_KV_HEADS * HEAD_DIM
    q_width = w_in.shape[1] - FOURIER_WIDTH - 2 * kv_width
    n = rows.n
    rope_blk = lambda i: (jnp.where(i < rows.lat_tiles, i % rows.tiles_per_batch, rows.tiles_per_batch), 0)
    return pl.pallas_call(
        _ab_proj_kernel,
        out_shape=(
            jax.ShapeDtypeStruct((n, FOURIER_WIDTH), F32),
            jax.ShapeDtypeStruct((n, q_width), BF16),
            jax.ShapeDtypeStruct((n, kv_width), BF16),
            jax.ShapeDtypeStruct((N_KV_HEADS, rows.tiles, VT_ROWS, tm), BF16),
        ),
        grid=(rows.tiles,),
        in_specs=[
            pl.BlockSpec((tm, d), lambda i: (i, 0)),
            pl.BlockSpec((None, None, 3, d), lambda i: (rows.group(i), 1, 0, 0)),
            pl.BlockSpec((1, d), lambda i: (0, 0)),
            pl.BlockSpec(w_in.shape, lambda i: (0, 0)),
            pl.BlockSpec((2, HEAD_DIM), lambda i: (0, 0)),
            pl.BlockSpec((tm, 3 * HEAD_DIM), rope_blk),
        ],
        out_specs=(
            pl.BlockSpec((tm, FOURIER_WIDTH), lambda i: (i, 0)),
            pl.BlockSpec((tm, q_width), lambda i: (i, 0)),
            pl.BlockSpec((tm, kv_width), lambda i: (i, 0)),
            pl.BlockSpec((N_KV_HEADS, None, VT_ROWS, tm), lambda i: (0, i, 0, 0)),
        ),
        compiler_params=_cparams("parallel"),
        name="ab_proj",
    )(h, mods, gain, w_in, qk_norm, rope)


def _attn_kernel(*refs, lat_chunks, group):
    if lat_chunks:
        q_ref, kc_ref, vtc_ref, kl_ref, vtl_ref, o_ref, acc_ref, s_ref = refs
    else:
        q_ref, kc_ref, vtc_ref, o_ref, acc_ref = refs
    tq = q_ref.shape[0]
    q = q_ref[...]
    qs = jnp.concatenate([q[:, g * HEAD_DIM:(g + 1) * HEAD_DIM] for g in range(group)], axis=0)
    nq = group * tq
    acc_ref[...] = jnp.zeros_like(acc_ref)

    def scores(k):
        return _dot_nt(k, qs)

    def update(s, vt, m):
        m_new = jnp.maximum(m, jnp.max(s, axis=0, keepdims=True))
        p = jnp.exp2(s - m_new).astype(BF16)
        acc_ref[...] = jnp.exp2(m - m_new) * acc_ref[...] + _dot(vt, p)
        return m_new

    m = update(scores(kc_ref[...]), vtc_ref[...], jnp.full((1, nq), -jnp.inf, F32))
    if lat_chunks:
        tk = vtl_ref.shape[2]

        def lat_scores(c):
            return scores(kl_ref[pl.ds(pl.multiple_of(c * tk, tk), tk), :])

        assert lat_chunks % 2 == 0
        s_ref[0] = lat_scores(0)

        def body(i, m):
            c = 2 * i
            s_ref[1] = lat_scores(c + 1)
            m = update(s_ref[0], vtl_ref[c], m)
            s_ref[0] = lat_scores(jnp.minimum(c + 2, lat_chunks - 1))
            return update(s_ref[1], vtl_ref[c + 1], m)

        m = lax.fori_loop(0, lat_chunks // 2, body, m)

    out = (acc_ref[:HEAD_DIM, :] / acc_ref[HEAD_DIM:HEAD_DIM + 1, :]).T
    o_ref[...] = jnp.concatenate([out[g * tq:(g + 1) * tq, :] for g in range(group)], axis=1).astype(BF16)


def _attention(q_all, k_all, vt_all, rows, latent):
    b, seq, lc, tile = rows.batch, rows.seq, rows.ctx_len, rows.tile
    group = q_all.shape[1] // (N_KV_HEADS * HEAD_DIM)
    gw = group * HEAD_DIM
    assert tile % lc == 0 and seq % tile == 0
    ctx_blk0 = rows.n_lat // lc
    ctx_tile = lambda bi: ((rows.n_lat + bi * lc) // tile, 0, ((rows.n_lat + bi * lc) % tile) // lc)
    kc_spec = pl.BlockSpec((lc, HEAD_DIM), lambda bi, hi, i: (ctx_blk0 + bi, hi))
    vtc_spec = pl.BlockSpec((None, None, VT_ROWS, lc), lambda bi, hi, i: (hi,) + ctx_tile(bi))
    if latent:
        tq = ATTN_Q_TILE
        assert tq % tile == 0
        nqt = seq // tq
        lat_chunks = seq // tile
        q_spec = pl.BlockSpec((tq, gw), lambda bi, hi, i: (bi * nqt + i, hi))
        kl_spec = pl.BlockSpec((seq, HEAD_DIM), lambda bi, hi, i: (bi, hi))
        vtl_spec = pl.BlockSpec((None, lat_chunks, VT_ROWS, tile), lambda bi, hi, i: (hi, bi, 0, 0))
        in_specs = [q_spec, kc_spec, vtc_spec, kl_spec, vtl_spec]
        args = (q_all, k_all, vt_all, k_all, vt_all)
    else:
        tq = lc
        nqt = 1
        lat_chunks = 0
        q_spec = pl.BlockSpec((tq, gw), lambda bi, hi, i: (ctx_blk0 + bi, hi))
        in_specs = [q_spec, kc_spec, vtc_spec]
        args = (q_all, k_all, vt_all)
    return pl.pallas_call(
        functools.partial(_attn_kernel, lat_chunks=lat_chunks, group=group),
        out_shape=jax.ShapeDtypeStruct((b * nqt * tq, q_all.shape[1]), BF16),
        grid=(b, N_KV_HEADS, nqt),
        in_specs=in_specs,
        out_specs=pl.BlockSpec((tq, gw), lambda bi, hi, i: (bi * nqt + i, hi)),
        scratch_shapes=[pltpu.VMEM((VT_ROWS, group * tq), F32)]
        + ([pltpu.VMEM((2, tile, group * tq), F32)] if lat_chunks else []),
        compiler_params=_cparams("parallel", "parallel", "arbitrary"),
        name="attention_ctx" if lat_chunks == 0 else "attention_lat",
    )(*args)


def _dft_cs(n):
    idx = np.arange(n)
    ang = 2.0 * np.pi * ((idx[:, None] * idx[None, :]) % n) / n
    return np.cos(ang), np.sin(ang)


def _const_split(m):
    m = np.asarray(m, dtype=np.float32)
    hi = jnp.asarray(m).astype(BF16)
    lo = (jnp.asarray(m) - hi.astype(F32)).astype(BF16)
    return jnp.stack([hi, lo])


def _fft1_kernel(x_ref, fa_ref, tw_ref, o_ref, *, a):
    w = o_ref.shape[2]
    for r in range(o_ref.shape[0]):
        z = _dot_split((fa_ref[0], fa_ref[1]), _split_bf16(x_ref[:, r * w:(r + 1) * w]))
        zr, zi = z[:a], z[a:]
        tc = jnp.concatenate([tw_ref[r, 0]] * (w // LANES), axis=1)
        ts = jnp.concatenate([tw_ref[r, 1]] * (w // LANES), axis=1)
        o_ref[r, :a, :] = zr * tc - zi * ts
        o_ref[r, a:, :] = zr * ts + zi * tc


def _fft2_kernel(zr_ref, zi_ref, m2_ref, mc_ref, o_ref, *, scale):
    for r in range(o_ref.shape[1]):
        z = jnp.concatenate([zr_ref[:, r, :], zi_ref[:, r, :]], axis=0)
        v = _dot_split((m2_ref[0], m2_ref[1]), _split_bf16(z))
        vr, vi = v[:FFT_B], v[FFT_B:]
        outs = []
        for g in range(o_ref.shape[2] // LANES):
            u = jnp.concatenate([vr[:, g * LANES:(g + 1) * LANES], vi[:, g * LANES:(g + 1) * LANES]], axis=1)
            outs.append(_dot_split(_split_bf16(u), (mc_ref[0], mc_ref[1])))
        o_ref[:, r, :] = jnp.concatenate(outs, axis=1) * scale


def _fourier_latent(f_all, rows):
    b, seq = rows.batch, rows.seq
    w = FOURIER_WIDTH
    a = seq // FFT_B
    assert a % SUBLANES == 0 and FOURIER_GROUP_DIM == LANES
    ca, sa = _dft_cs(a)
    fa = _const_split(np.concatenate([ca, sa], axis=0))
    p1b = (np.arange(a)[None, :] * np.arange(FFT_B)[:, None]) % seq
    ang = 2.0 * np.pi * p1b / seq
    tw = np.stack([np.cos(ang), np.sin(ang)], axis=1)[..., None]
    tw = jnp.asarray(np.broadcast_to(tw, (FFT_B, 2, a, LANES)), dtype=F32)
    cb, sb = _dft_cs(FFT_B)
    m2 = _const_split(np.block([[cb, -sb], [sb, cb]]))
    cc, sc = _dft_cs(FOURIER_GROUP_DIM)
    mc = _const_split(np.concatenate([cc, -sc], axis=0))

    blk = FFT_B_BLOCK
    x2 = f_all.reshape(rows.n // FFT_B, FFT_B * w)
    z = pl.pallas_call(
        functools.partial(_fft1_kernel, a=a),
        out_shape=jax.ShapeDtypeStruct((b, FFT_B, 2 * a, w), F32),
        grid=(b, FFT_B // blk),
        in_specs=[
            pl.BlockSpec((a, blk * w), lambda bi, j: (bi, j)),
            pl.BlockSpec((2, 2 * a, a), lambda bi, j: (0, 0, 0)),
            pl.BlockSpec((blk, 2, a, LANES), lambda bi, j: (j, 0, 0, 0)),
        ],
        out_specs=pl.BlockSpec((None, blk, 2 * a, w), lambda bi, j: (bi, j, 0, 0)),
        compiler_params=_cparams("parallel", "parallel"),
        name="fourier_stage1",
    )(x2, fa, tw)

    nblk = a // blk
    y = pl.pallas_call(
        functools.partial(_fft2_kernel, scale=1.0 / math.sqrt(seq * FOURIER_GROUP_DIM)),
        out_shape=jax.ShapeDtypeStruct((b, FFT_B, a, w), F32),
        grid=(b, nblk),
        in_specs=[
            pl.BlockSpec((None, FFT_B, blk, w), lambda bi, j: (bi, 0, j, 0)),
            pl.BlockSpec((None, FFT_B, blk, w), lambda bi, j: (bi, 0, nblk + j, 0)),
            pl.BlockSpec((2, 2 * FFT_B, 2 * FFT_B), lambda bi, j: (0, 0, 0)),
            pl.BlockSpec((2, 2 * LANES, LANES), lambda bi, j: (0, 0, 0)),
        ],
        out_specs=pl.BlockSpec((None, FFT_B, blk, w), lambda bi, j: (bi, 0, j, 0)),
        compiler_params=_cparams("parallel", "parallel"),
        name="fourier_stage2",
    )(z, z, m2, mc)
    return y.reshape(b * seq, w)


def _dft_ctx_kernel(x_ref, cn_ref, sn_ref, cc_ref, sc_ref, o_ref, *, scale):
    x = x_ref[...]
    outs = []
    for g in range(x.shape[1] // LANES):
        xg = x[:, g * LANES:(g + 1) * LANES]
        outs.append(_dot_hi(cn_ref[...], _dot_hi(xg, cc_ref[...])) - _dot_hi(sn_ref[...], _dot_hi(xg, sc_ref[...])))
    o_ref[...] = jnp.concatenate(outs, axis=1) * scale


def _fourier_ctx(f_all, rows):
    b, lc = rows.batch, rows.ctx_len
    w = FOURIER_WIDTH
    cn, sn = _dft_cs(lc)
    cc, sc = _dft_cs(FOURIER_GROUP_DIM)
    blk0 = rows.n_lat // lc
    mat = lambda m: pl.BlockSpec(m.shape, lambda bi: (0, 0))
    consts = [jnp.asarray(m, dtype=F32) for m in (cn, sn, cc, sc)]
    return pl.pallas_call(
        functools.partial(_dft_ctx_kernel, scale=1.0 / math.sqrt(lc * FOURIER_GROUP_DIM)),
        out_shape=jax.ShapeDtypeStruct((b * lc, w), F32),
        grid=(b,),
        in_specs=[pl.BlockSpec((lc, w), lambda bi: (blk0 + bi, 0))] + [mat(m) for m in consts],
        out_specs=pl.BlockSpec((lc, w), lambda bi: (bi, 0)),
        compiler_params=_cparams("parallel"),
        name="fourier_ctx",
    )(f_all, *consts)


def _ab_out_kernel(x1l_ref, x1c_ref, x2l_ref, x2c_ref, w_ref, h_ref, mod_ref, o_ref, *, lat_tiles):
    is_lat = pl.program_id(0) < lat_tiles
    x1 = jnp.where(is_lat, x1l_ref[...], x1c_ref[...])
    x2 = jnp.where(is_lat, x2l_ref[...], x2c_ref[...])
    x = jnp.concatenate([x1.astype(BF16), x2], axis=1)
    o_ref[...] = h_ref[...] + mod_ref[2:3, :] * _dot(x, w_ref[...])


def _ab_out(x1_lat, x1_ctx, x2_lat, x2_ctx, w_out, h, mods, rows):
    d = h.shape[1]
    tm = rows.tile
    lt = rows.lat_tiles
    lat_blk = lambda i: (jnp.minimum(i, lt - 1), 0)
    ctx_blk = lambda i: (jnp.maximum(i - lt, 0), 0)
    return pl.pallas_call(
        functools.partial(_ab_out_kernel, lat_tiles=lt),
        out_shape=jax.ShapeDtypeStruct((rows.n, d), F32),
        grid=(rows.tiles,),
        in_specs=[
            pl.BlockSpec((tm, x1_lat.shape[1]), lat_blk),
            pl.BlockSpec((tm, x1_ctx.shape[1]), ctx_blk),
            pl.BlockSpec((tm, x2_lat.shape[1]), lat_blk),
            pl.BlockSpec((tm, x2_ctx.shape[1]), ctx_blk),
            pl.BlockSpec(w_out.shape, lambda i: (0, 0)),
            pl.BlockSpec((tm, d), lambda i: (i, 0)),
            pl.BlockSpec((None, None, 3, d), lambda i: (rows.group(i), 1, 0, 0)),
        ],
        out_specs=pl.BlockSpec((tm, d), lambda i: (i, 0)),
        compiler_params=_cparams("parallel"),
        name="ab_out",
    )(x1_lat, x1_ctx, x2_lat, x2_ctx, w_out, h, mods)


def _hgrn_out_kernel(osum_ref, g_ref, gain_ref, w_ref, h_ref, mod_ref, o_ref):
    o = osum_ref[...]
    parts = []
    for hh in range(o.shape[1] // HEAD_DIM):
        oh = o[:, hh * HEAD_DIM:(hh + 1) * HEAD_DIM]
        parts.append(oh * lax.rsqrt(jnp.mean(oh * oh, axis=-1, keepdims=True) + EPS))
    on = jnp.concatenate(parts, axis=1) * gain_ref[...]
    y = (on * jax.nn.sigmoid(g_ref[...])).astype(BF16)
    o_ref[...] = h_ref[...] + mod_ref[2:3, :] * _dot(y, w_ref[...])


def _hgrn_out(o_sum, p, gain, w_out, h, mods, rows_r, n_tiles):
    d = h.shape[1]
    tm = rows_r.tile
    g_blk = (p.shape[1] - d) // d
    return pl.pallas_call(
        _hgrn_out_kernel,
        out_shape=jax.ShapeDtypeStruct((n_tiles * tm, d), F32),
        grid=(n_tiles,),
        in_specs=[
            pl.BlockSpec((tm, d), lambda i: (i, 0)),
            pl.BlockSpec((tm, d), lambda i: (i, g_blk)),
            pl.BlockSpec((1, d), lambda i: (0, 0)),
            pl.BlockSpec(w_out.shape, lambda i: (0, 0)),
            pl.BlockSpec((tm, d), lambda i: (i, 0)),
            pl.BlockSpec((None, None, 3, d), lambda i: (rows_r.group(i), 1, 0, 0)),
        ],
        out_specs=pl.BlockSpec((tm, d), lambda i: (i, 0)),
        compiler_params=_cparams("parallel"),
        name="hgrn_out",
    )(o_sum, p, gain, w_out, h, mods)


def _hgrn_proj_kernel(h_ref, mod_ref, gain_ref, w_ref, o_ref, xn_ref, *, n_silu):
    j = pl.program_id(1)
    r = pl.program_id(2)

    @pl.when(j == 0)
    def _():
        xn_ref[r] = _ada(h_ref[...], gain_ref[...], mod_ref[0:1, :], mod_ref[1:2, :]).astype(BF16)

    acc = _dot(xn_ref[r], w_ref[...])
    o_ref[...] = jnp.where(j < n_silu, _silu(acc), acc)


def _hgrn_proj(h, mods, gain, w_in, rows):
    d = h.shape[1]
    tm, tn = rows.tile, PROJ_COL_TILE
    nw = w_in.shape[1]
    gs = next(k for k in (PROJ_ROW_GROUP, 2, 1) if rows.tiles % k == 0)
    tile = lambda g, r: g * gs + r
    return pl.pallas_call(
        functools.partial(_hgrn_proj_kernel, n_silu=d // tn),
        out_shape=jax.ShapeDtypeStruct((rows.n, nw), F32),
        grid=(rows.tiles // gs, nw // tn, gs),
        in_specs=[
            pl.BlockSpec((tm, d), lambda g, j, r: (jnp.where(j == 0, tile(g, r), tile(g, gs - 1)), 0)),
            pl.BlockSpec((None, None, 3, d), lambda g, j, r: (rows.group(tile(g, r)), 1, 0, 0)),
            pl.BlockSpec((1, d), lambda g, j, r: (0, 0)),
            pl.BlockSpec((d, tn), lambda g, j, r: (0, j)),
        ],
        out_specs=pl.BlockSpec((tm, tn), lambda g, j, r: (tile(g, r), j)),
        scratch_shapes=[pltpu.VMEM((gs, tm, d), BF16)],
        compiler_params=_cparams("arbitrary", "arbitrary", "arbitrary"),
        name="hgrn_proj",
    )(h, mods, gain, w_in)


def _scan_levels(chunk):
    return [chunk >> (i + 1) for i in range(int(math.log2(chunk)))]


def _scan_consts(chunk, reverse):
    t = np.arange(chunk)[:, None]
    s = np.arange(chunk)[None, :]
    tri = (s >= t) if reverse else (s <= t)
    wide, narrow = [], []
    for h in _scan_levels(chunk):
        same = (t // (2 * h)) == (s // (2 * h))
        t_up = (t // h) % 2 == 1
        s_up = (s // h) % 2 == 1
        mask = same & ((~t_up & s_up) if reverse else (t_up & ~s_up))
        if h >= SUBLANES:
            wide.append(mask[np.nonzero(~t_up[:, 0] if reverse else t_up[:, 0])[0]])
        else:
            narrow.append(mask)
    narrow.append(t == s)
    return (jnp.asarray(tri, dtype=BF16), jnp.asarray(np.stack(wide), dtype=F32),
            jnp.asarray(np.stack(narrow), dtype=F32))


def _seg_bcast(x, h, reverse):
    c, w = x.shape
    off = h if reverse else h - 1
    if 2 * h >= 2 * SUBLANES:
        pieces = [jnp.broadcast_to(x[g * 2 * h + off:g * 2 * h + off + 1, :], (2 * h, w)) for g in range(c // (2 * h))]
        return pieces[0] if len(pieces) == 1 else jnp.concatenate(pieces, axis=0)
    x3 = x.reshape(c // SUBLANES, SUBLANES, w)
    sub = lax.broadcasted_iota(jnp.int32, x3.shape, 1)
    y = None
    for g in range(SUBLANES // (2 * h)):
        piece = jnp.broadcast_to(x3[:, g * 2 * h + off:g * 2 * h + off + 1, :], x3.shape)
        y = piece if y is None else jnp.where(sub >= g * 2 * h, piece, y)
    return y.reshape(c, w)


def _scan_kernel(*refs, reverse, chunk, accumulate):
    if accumulate:
        q_ref, f_ref, v_ref, lb_ref, tri_ref, wmask_ref, nmask_ref, other_ref, o_ref, st_ref = refs
    else:
        q_ref, f_ref, v_ref, lb_ref, tri_ref, wmask_ref, nmask_ref, o_ref, st_ref = refs

    @pl.when(pl.program_id(2) == 0)
    def _():
        st_ref[...] = jnp.zeros_like(st_ref)

    rows, width = q_ref.shape
    levels = _scan_levels(chunk)
    spans = [slice(ci * chunk, (ci + 1) * chunk) for ci in range(rows // chunk)]
    lanes = [slice(hh * HEAD_DIM, (hh + 1) * HEAD_DIM) for hh in range(width // HEAD_DIM)]
    q = q_ref[...]
    fl = f_ref[...]
    lb = lb_ref[...]
    u = jnp.exp(-jnp.abs(fl))
    key = (1.0 - lb) * (jnp.where(fl > 0.0, u, 1.0) / (1.0 + u))
    log_sig = jnp.minimum(fl, 0.0) - jnp.log(1.0 + u)
    x1 = jnp.log(lb)
    x2 = jnp.log1p(-lb) + log_sig
    delta = x1 - x2
    lf = jnp.where(jnp.isnan(delta), x1 + x2, jnp.maximum(x1, x2) + jnp.log(1.0 + jnp.exp(-jnp.abs(delta))))
    lf = lf * LOG2_E
    l1 = lf.astype(BF16)
    r1 = lf - l1.astype(F32)
    l2 = r1.astype(BF16)
    l3 = (r1 - l2.astype(F32)).astype(BF16)
    parts = jnp.concatenate([l1, l2, l3], axis=1)
    tri = tri_ref[...]
    cs = jnp.concatenate([_dot(tri, parts[sp]) for sp in spans], axis=0)
    b = cs[:, :width] + cs[:, width:2 * width] + cs[:, 2 * width:]
    totals = [b[sp.start:sp.start + 1, :] if reverse else b[sp.stop - 1:sp.stop, :] for sp in spans]
    total_rows = jnp.concatenate([jnp.broadcast_to(t, (chunk, width)) for t in totals], axis=0)

    q_in = (q * jnp.exp2(b)).astype(BF16)
    k_out = (key * jnp.exp2(total_rows - b)).astype(BF16)
    v = v_ref[...].astype(BF16)

    order = list(range(len(spans)))[::-1] if reverse else list(range(len(spans)))
    kv = [[_dot_tn(v[sp, sl], k_out[sp, sl]) for sl in lanes] for sp in spans]
    o_state = [[None] * len(lanes) for _ in spans]
    for hh, sl in enumerate(lanes):
        st = st_ref[hh]
        for ci in order:
            o_state[ci][hh] = _dot_nt(q_in[spans[ci], sl], st.astype(BF16))
            st = st * jnp.exp2(totals[ci][:, sl]) + kv[ci][hh]
        st_ref[hh] = st

    n_blk = chunk // SUBLANES
    wide_levels = [h for h in levels if h >= SUBLANES]
    narrow_levels = [h for h in levels if h < SUBLANES]

    def narrow_scores(li, ql, kl):
        out = []
        for sp in spans:
            row = []
            for sl in lanes:
                s = nmask_ref[li] * _dot_nt(ql[sp, sl], kl[sp, sl])
                row.append([s[k * SUBLANES:(k + 1) * SUBLANES, :] for k in range(n_blk)])
            out.append(row)
        return out

    a = narrow_scores(len(narrow_levels), q.astype(BF16), key.astype(BF16))
    row_id = lax.broadcasted_iota(jnp.int32, (rows, width), 0)
    for li, h in enumerate(narrow_levels):
        if h == 1:
            e = jnp.where((row_id & 1) == (0 if reverse else 1), jnp.exp2(lf), 1.0)
        else:
            e = jnp.exp2(-jnp.abs(b - _seg_bcast(b, h, reverse)))
        new = narrow_scores(li, (q * e).astype(BF16), (key * e).astype(BF16))
        a = [[[x + y for x, y in zip(xb, yb)] for xb, yb in zip(xa, ya)] for xa, ya in zip(a, new)]

    for li, h in enumerate(wide_levels):
        q_parts, k_parts = [], []
        for g in range(rows // (2 * h)):
            lo = slice(g * 2 * h, g * 2 * h + h)
            hi = slice(g * 2 * h + h, (g + 1) * 2 * h)
            r = g * 2 * h + (h if reverse else h - 1)
            b_ref = jnp.broadcast_to(b[r:r + 1, :], (h, width))
            q_half, k_half = (lo, hi) if reverse else (hi, lo)
            q_parts.append(q[q_half] * jnp.exp2(b[q_half] - b_ref))
            k_part = key[k_half] * jnp.exp2(b_ref - b[k_half])
            zeros = jnp.zeros((h, width), F32)
            k_parts += [zeros, k_part] if reverse else [k_part, zeros]
        q_sel = jnp.concatenate(q_parts, axis=0).astype(BF16)
        k_hat = jnp.concatenate(k_parts, axis=0).astype(BF16)
        half = chunk // 2
        for ci, sp in enumerate(spans):
            for hh, sl in enumerate(lanes):
                s = wmask_ref[li] * _dot_nt(q_sel[ci * half:(ci + 1) * half, sl], k_hat[sp, sl])
                for j in range(chunk // (2 * h)):
                    first = (j * 2 * h + (0 if reverse else h)) // SUBLANES
                    for k in range(h // SUBLANES):
                        r0 = j * h + k * SUBLANES
                        a[ci][hh][first + k] = a[ci][hh][first + k] + s[r0:r0 + SUBLANES, :]

    for ci, sp in enumerate(spans):
        for hh, sl in enumerate(lanes):
            pairs = jnp.concatenate(a[ci][hh], axis=0).astype(BF16)
            o = o_state[ci][hh] + _dot(pairs, v[sp, sl])
            o_ref[sp, sl] = o + other_ref[sp, sl] if accumulate else o


def _hgrn_scan(p, lower_bound, rows, reverse, other=None):
    b, seq, lc = rows.batch, rows.seq, rows.ctx_len
    d = lower_bound.shape[0]
    c, hb = SCAN_CHUNK, SCAN_HEADS_PER_BLOCK
    r = c * SCAN_CHUNKS_PER_STEP
    wb = hb * HEAD_DIM
    ncb = d // wb
    assert lc % r == 0 and seq % r == 0
    nctx, nlat = lc // r, seq // r
    ctx0 = rows.n_lat // r
    f_blk = (2 if reverse else 1) * ncb
    v_blk = 3 * ncb

    def row(bi, s):
        if reverse:
            return jnp.where(s < nctx, ctx0 + bi * nctx + (nctx - 1 - s), bi * nlat + (nlat - 1 - (s - nctx)))
        return jnp.where(s < nctx, ctx0 + bi * nctx + s, bi * nlat + (s - nctx))

    tri, wmasks, nmasks = _scan_consts(c, reverse)
    o_spec = pl.BlockSpec((r, wb), lambda bi, hi, s: (row(bi, s), hi))
    extra = () if other is None else (other,)
    return pl.pallas_call(
        functools.partial(_scan_kernel, reverse=reverse, chunk=c, accumulate=other is not None),
        out_shape=jax.ShapeDtypeStruct((rows.n, d), F32),
        grid=(b, ncb, nctx + nlat),
        in_specs=[
            pl.BlockSpec((r, wb), lambda bi, hi, s: (row(bi, s), hi)),
            pl.BlockSpec((r, wb), lambda bi, hi, s: (row(bi, s), f_blk + hi)),
            pl.BlockSpec((r, wb), lambda bi, hi, s: (row(bi, s), v_blk + hi)),
            pl.BlockSpec((1, wb), lambda bi, hi, s: (0, hi)),
            pl.BlockSpec(tri.shape, lambda bi, hi, s: (0, 0)),
            pl.BlockSpec(wmasks.shape, lambda bi, hi, s: (0, 0, 0)),
            pl.BlockSpec(nmasks.shape, lambda bi, hi, s: (0, 0, 0)),
        ] + [o_spec] * len(extra),
        out_specs=o_spec,
        scratch_shapes=[pltpu.VMEM((hb, HEAD_DIM, HEAD_DIM), F32)],
        compiler_params=_cparams("parallel", "parallel", "arbitrary"),
        name="hgrn_scan_bw" if reverse else "hgrn_scan_fw",
    )(p, p, p, lower_bound.reshape(1, d), tri, wmasks, nmasks, *extra)


def kernel(x, c, ctx, c_ctx, w_mod, b_mod, norm_gains, ffn_w_in, ffn_w_out, ab_w_in, qk_norm, ab_w_out,
           hgrn_w_in, hgrn_lb_logits, hgrn_o_norm, hgrn_w_out, final_norm):
    batch, seq, d = x.shape
    lc = ctx.shape[1]
    depth = w_mod.shape[0]
    rows = _Rows(batch, seq, lc, ROW_TILE)
    rows_r = _Rows(batch, seq, lc, READOUT_ROW_TILE)
    assert seq % GRID_W == 0 and batch + 1 <= SUBLANES

    c_rows = jnp.concatenate([c_ctx[None, :], c, jnp.zeros((SUBLANES - 1 - batch, d), F32)], axis=0)
    mods_all = _modulation(c_rows, w_mod, b_mod).reshape(depth, SUBLANES, 3, 3, d)

    lb_cum = jnp.cumsum(jax.nn.softmax(hgrn_lb_logits.astype(F32), axis=0), axis=0)
    lower_bounds = lb_cum - lb_cum[0]

    ffn_w_in_b = ffn_w_in.astype(BF16)
    ffn_w_out_b = ffn_w_out.astype(BF16)
    fin = final_norm.reshape(1, d)
    rope = _rope_tables(seq, ROW_TILE)

    h = (x.reshape(batch * seq, d), ctx.reshape(batch * lc, d))
    for layer in range(depth):
        last = layer == depth - 1
        mods = mods_all[layer]
        gains = norm_gains[layer].reshape(3, 1, d)
        h = _ffn(h, mods, 0, gains[0], ffn_w_in_b, ffn_w_out_b, layer, 0, fin, rows, rows.tiles, False)
        if layer % 2 == 0:
            e = layer // 2
            f_all, q_all, k_all, vt_all = _ab_proj(h, mods, gains[1], ab_w_in[e].astype(BF16), qk_norm[e], rope, rows)
            h = _ab_out(_fourier_latent(f_all, rows), _fourier_ctx(f_all, rows),
                        _attention(q_all, k_all, vt_all, rows, True), _attention(q_all, k_all, vt_all, rows, False),
                        ab_w_out[e].astype(BF16), h, mods, rows)
        else:
            o = layer // 2
            p = _hgrn_proj(h, mods, gains[1], hgrn_w_in[o].astype(BF16), rows)
            o_fw = _hgrn_scan(p, lower_bounds[layer], rows, False)
            o_sum = _hgrn_scan(p, lower_bounds[layer], rows, True, other=o_fw)
            gain_o = jnp.tile(hgrn_o_norm[o], d // HEAD_DIM).reshape(1, d)
            n_t = rows_r.lat_tiles if last else rows_r.tiles
            h = _hgrn_out(o_sum, p, gain_o, hgrn_w_out[o].astype(BF16), h, mods, rows_r, n_t)
        n_t = rows.lat_tiles if last else rows.tiles
        h = _ffn(h, mods, 2, gains[2], ffn_w_in_b, ffn_w_out_b, layer, 1, fin, rows, n_t, last)
    return h[:batch * seq].reshape(batch, seq, d)
```
